```python
import jax, jax.numpy as jnp
from jax import lax
import numpy as np

D_MODEL = 2048
BATCH = 8
SEQ = 2048
DEPTH = 1

MIX_WIDTH = D_MODEL
CONV_WIDTH = MIX_WIDTH // 2
CONV_GROUPS = 8
CONV_K = 3
FOX_WIDTH = MIX_WIDTH - CONV_WIDTH
HEAD_DIM = 128
FOX_HEADS = FOX_WIDTH // HEAD_DIM
Q_BLOCK = 128
MIX_IN = 3 * CONV_WIDTH + 3 * FOX_WIDTH + FOX_HEADS
MEM_TOKENS = 256
XATTN_HEADS = 4
XATTN_WIDTH = XATTN_HEADS * HEAD_DIM
N_EXPERTS = 32
TOP_K = 4
D_FF = D_MODEL
SWIGLU_LIMIT = 7.0
SWIGLU_ALPHA = 1.702
MOE_BLOCK = 128
EPS = 1e-6

kernel_name = "hybrid_conv_fox_memxattn_moe"


def rms_norm(x, g):
    xf = x.astype(jnp.float32)
    y = xf * lax.rsqrt(jnp.mean(xf * xf, axis=-1, keepdims=True) + EPS)
    return (y * g.astype(jnp.float32)).astype(x.dtype)


def causal_dwconv(u, w):
    c = u.shape[-1]
    return lax.conv_general_dilated(
        u, w[:, None, :].astype(u.dtype), window_strides=(1,),
        padding=[(CONV_K - 1, 0)], dimension_numbers=("NWC", "WIO", "NWC"),
        feature_group_count=c)


def fox_attention(q, k, v, log_f):
    b, s, h, dh = q.shape
    scale = dh ** -0.5
    c = jnp.cumsum(log_f, axis=1).transpose(0, 2, 1)
    nb = s // Q_BLOCK
    qb = q.reshape(b, nb, Q_BLOCK, h, dh).transpose(1, 0, 2, 3, 4)
    cq = c.reshape(b, h, nb, Q_BLOCK).transpose(2, 0, 1, 3)
    pos_k = jnp.arange(s)

    def one_block(args):
        qi, ci, bi = args
        pos_q = bi * Q_BLOCK + jnp.arange(Q_BLOCK)
        sc = jnp.einsum("bqhd,bkhd->bhqk", qi, k).astype(jnp.float32) * scale
        sc = sc + (ci[..., :, None] - c[..., None, :])
        mask = pos_k[None, :] <= pos_q[:, None]
        sc = jnp.where(mask, sc, jnp.finfo(jnp.float32).min)
        p = jax.nn.softmax(sc, axis=-1).astype(v.dtype)
        return jnp.einsum("bhqk,bkhd->bqhd", p, v)

    out = lax.map(one_block, (qb, cq, jnp.arange(nb)))
    return out.transpose(1, 0, 2, 3, 4).reshape(b, s, h * dh)


def memory_cross_attention(h, hm, w_q, w_kv, g_q, g_k, w_o):
    b, s, _ = h.shape
    m = hm.shape[1]
    q = (h @ w_q).reshape(b, s, XATTN_HEADS, HEAD_DIM)
    k, v = jnp.split(hm @ w_kv, 2, axis=-1)
    k = k.reshape(b, m, XATTN_HEADS, HEAD_DIM)
    v = v.reshape(b, m, XATTN_HEADS, HEAD_DIM)
    q = rms_norm(q, g_q)
    k = rms_norm(k, g_k)
    sc = jnp.einsum("bqhd,bkhd->bhqk", q, k).astype(jnp.float32) * HEAD_DIM ** -0.5
    p = jax.nn.softmax(sc, axis=-1).astype(v.dtype)
    o = jnp.einsum("bhqk,bkhd->bqhd", p, v).reshape(b, s, XATTN_WIDTH)
    return o @ w_o


def clamped_swiglu_expert(xb, w_gu, b_gu, w_dn, b_dn):
    gu = xb @ w_gu + b_gu
    gate, up = jnp.split(gu, 2, axis=-1)
    gate = jnp.minimum(gate, SWIGLU_LIMIT)
    up = jnp.clip(up, -SWIGLU_LIMIT, SWIGLU_LIMIT)
    glu = gate * jax.nn.sigmoid(SWIGLU_ALPHA * gate)
    return ((up + 1.0) * glu) @ w_dn + b_dn


def moe(h, w_router, b_router, w_gate_up, b_gate_up, w_down, b_down):
    t, d = h.shape
    logits = h.astype(jnp.float32) @ w_router.astype(jnp.float32) + b_router.astype(jnp.float32)
    top_val, top_idx = lax.top_k(logits, TOP_K)
    gates = jax.nn.softmax(top_val, axis=-1)
    n_assign = t * TOP_K
    flat_e = top_idx.reshape(-1)
    order = jnp.argsort(flat_e)
    sorted_e = flat_e[order]
    tok = order // TOP_K
    counts = jnp.bincount(flat_e, length=N_EXPERTS)
    padded = (counts + MOE_BLOCK - 1) // MOE_BLOCK * MOE_BLOCK
    pad_end = jnp.cumsum(padded)
    pad_start = pad_end - padded
    start = jnp.cumsum(counts) - counts
    dest = pad_start[sorted_e] + (jnp.arange(n_assign) - start[sorted_e])
    n_blocks = -(-n_assign // MOE_BLOCK) + N_EXPERTS
    n_rows = n_blocks * MOE_BLOCK
    row_tok = jnp.zeros((n_rows,), jnp.int32).at[dest].set(tok.astype(jnp.int32))
    row_gate = jnp.zeros((n_rows,), h.dtype).at[dest].set(gates.reshape(-1)[order].astype(h.dtype))
    block_start = jnp.arange(n_blocks) * MOE_BLOCK
    block_expert = jnp.minimum(jnp.searchsorted(pad_end, block_start, side="right"), N_EXPERTS - 1)
    xs = h[row_tok].reshape(n_blocks, MOE_BLOCK, d)

    def expert_block(args):
        xb, e = args
        return clamped_swiglu_expert(xb, w_gate_up[e], b_gate_up[e], w_down[e], b_down[e])

    ys = lax.map(expert_block, (xs, block_expert)).reshape(n_rows, d)
    return jnp.zeros((t, d), h.dtype).at[row_tok].add(ys * row_gate[:, None])


def setup_inputs(seed: int = 0) -> dict:
    key = jax.random.key(seed)
    ks = jax.random.split(key, 26)
    f32 = jnp.float32
    L = DEPTH

    def nrm(k, shape, scale):
        return jax.random.normal(k, shape, f32) * scale

    def gain(k, shape):
        return 1.0 + 0.05 * jax.random.normal(k, shape, f32)

    return {
        "x": nrm(ks[0], (BATCH, SEQ, D_MODEL), 1.0),
        "mem": nrm(ks[1], (BATCH, MEM_TOKENS, D_MODEL), 1.0),
        "g_mix": gain(ks[2], (L, D_MODEL)),
        "w_mix_in": nrm(ks[3], (L, D_MODEL, MIX_IN), D_MODEL ** -0.5),
        "b_forget": jax.random.uniform(ks[4], (L, FOX_HEADS), f32, 1.0, 5.0),
        "conv_w": nrm(ks[5], (L, CONV_K, CONV_WIDTH), CONV_K ** -0.5),
        "g_q": gain(ks[6], (L, HEAD_DIM)),
        "g_k": gain(ks[7], (L, HEAD_DIM)),
        "g_conv_out": gain(ks[8], (L, CONV_WIDTH)),
        "g_fox_out": gain(ks[9], (L, FOX_WIDTH)),
        "w_mix_out": nrm(ks[10], (L, MIX_WIDTH, D_MODEL), MIX_WIDTH ** -0.5),
        "g_xattn": gain(ks[11], (L, D_MODEL)),
        "g_mem": gain(ks[12], (L, D_MODEL)),
        "w_xq": nrm(ks[13], (L, D_MODEL, XATTN_WIDTH), D_MODEL ** -0.5),
        "w_xkv": nrm(ks[14], (L, D_MODEL, 2 * XATTN_WIDTH), D_MODEL ** -0.5),
        "g_xq": gain(ks[15], (L, HEAD_DIM)),
        "g_xk": gain(ks[16], (L, HEAD_DIM)),
        "w_xo": nrm(ks[17], (L, XATTN_WIDTH, D_MODEL), XATTN_WIDTH ** -0.5),
        "g_moe": gain(ks[18], (L, D_MODEL)),
        "w_router": nrm(ks[19], (L, D_MODEL, N_EXPERTS), D_MODEL ** -0.5),
        "b_router": nrm(ks[20], (L, N_EXPERTS), 0.01),
        "w_gate_up": nrm(ks[21], (L, N_EXPERTS, D_MODEL, 2 * D_FF), D_MODEL ** -0.5),
        "b_gate_up": nrm(ks[22], (L, N_EXPERTS, 2 * D_FF), 0.01),
        "w_down": nrm(ks[23], (L, N_EXPERTS, D_FF, D_MODEL), D_FF ** -0.5),
        "b_down": nrm(ks[24], (L, N_EXPERTS, D_MODEL), 0.01),
    }


def reference(x, mem, g_mix, w_mix_in, b_forget, conv_w, g_q, g_k, g_conv_out,
              g_fox_out, w_mix_out, g_xattn, g_mem, w_xq, w_xkv, g_xq, g_xk, w_xo,
              g_moe, w_router, b_router, w_gate_up, b_gate_up, w_down, b_down):
    b, s, d = x.shape
    splits = [CONV_WIDTH, 2 * CONV_WIDTH, 3 * CONV_WIDTH,
              3 * CONV_WIDTH + FOX_WIDTH, 3 * CONV_WIDTH + 2 * FOX_WIDTH,
              3 * CONV_WIDTH + 3 * FOX_WIDTH]
    for l in range(DEPTH):
        h = rms_norm(x, g_mix[l])
        proj = h @ w_mix_in[l]
        u_b, u_c, u_x, q, k, v, f_logit = jnp.split(proj, splits, axis=-1)
        y_conv = u_b * causal_dwconv(u_c * u_x, conv_w[l])
        q = rms_norm(q.reshape(b, s, FOX_HEADS, HEAD_DIM), g_q[l])
        k = rms_norm(k.reshape(b, s, FOX_HEADS, HEAD_DIM), g_k[l])
        v = v.reshape(b, s, FOX_HEADS, HEAD_DIM)
        log_f = jax.nn.log_sigmoid((f_logit + b_forget[l]).astype(jnp.float32))
        y_fox = fox_attention(q, k, v, log_f)
        y_mix = jnp.concatenate([rms_norm(y_conv, g_conv_out[l]),
                                 rms_norm(y_fox, g_fox_out[l])], axis=-1)
        x = x + y_mix @ w_mix_out[l]
        x = x + memory_cross_attention(rms_norm(x, g_xattn[l]), rms_norm(mem, g_mem[l]),
                                       w_xq[l], w_xkv[l], g_xq[l], g_xk[l], w_xo[l])
        h = rms_norm(x, g_moe[l]).reshape(b * s, d)
        x = x + moe(h, w_router[l], b_router[l], w_gate_up[l], b_gate_up[l],
                    w_down[l], b_down[l]).reshape(b, s, d)
    return x
```

```python
import functools

import jax
import jax.numpy as jnp
from jax import lax
from jax.experimental import pallas as pl
from jax.experimental.pallas import tpu as pltpu

F32 = jnp.float32
BF16 = jnp.bfloat16
I32 = jnp.int32
SDS = jax.ShapeDtypeStruct

EPS = 1e-6
HEAD_DIM = 128
TOP_K = 4
SWIGLU_LIMIT = 7.0
SWIGLU_ALPHA = 1.702
MASK_VALUE = float(jnp.finfo(jnp.float32).min)
V7X_VMEM_LIMIT_BYTES = 56 * 1024 * 1024
F_ROWS = 16


def _cparams(n_axes):
    return pltpu.CompilerParams(
        dimension_semantics=("arbitrary",) * n_axes,
        vmem_limit_bytes=V7X_VMEM_LIMIT_BYTES)


def _rms(x, g):
    return x * lax.rsqrt(jnp.mean(x * x, axis=-1, keepdims=True) + EPS) * g


def _dot(a, b):
    return jnp.dot(a, b, preferred_element_type=F32)


def _dot_nt(a, b):
    return lax.dot_general(a, b, (((1,), (1,)), ((), ())), preferred_element_type=F32)


def _tile(n, pref):
    return pref if n % pref == 0 else n


def _prenorm_kernel(x_ref, g_ref, wf_ref, h_ref, ft_ref):
    hb = _rms(x_ref[...], g_ref[...]).astype(BF16)
    h_ref[...] = hb
    ft_ref[...] = _dot_nt(wf_ref[...], hb)


def _prenorm(x2d, g, wf_t):
    t, d = x2d.shape
    tm = _tile(t, 512)
    return pl.pallas_call(
        _prenorm_kernel,
        grid=(t // tm,),
        in_specs=[pl.BlockSpec((tm, d), lambda i: (i, 0)),
                  pl.BlockSpec((1, d), lambda i: (0, 0)),
                  pl.BlockSpec((F_ROWS, d), lambda i: (0, 0))],
        out_specs=[pl.BlockSpec((tm, d), lambda i: (i, 0)),
                   pl.BlockSpec((F_ROWS, tm), lambda i: (0, i))],
        out_shape=[SDS((t, d), BF16), SDS((F_ROWS, t), F32)],
        compiler_params=_cparams(1), name="prenorm")(x2d, g, wf_t)


def _decay_kernel(ft_ref, b_ref, c_ref):
    z = ft_ref[...] + b_ref[...]
    lf = jnp.minimum(z, 0.0) - jnp.log(1.0 + jnp.exp(-jnp.abs(z)))
    s = lf.shape[1]
    lane = lax.broadcasted_iota(I32, lf.shape, 1)
    d = 1
    while d < s:
        lf = lf + jnp.where(lane >= d, pltpu.roll(lf, d, 1), 0.0)
        d *= 2
    c_ref[...] = lf


def _decay(ft, b_col, seq):
    rows, t = ft.shape
    return pl.pallas_call(
        _decay_kernel,
        grid=(t // seq,),
        in_specs=[pl.BlockSpec((rows, seq), lambda b: (0, b)),
                  pl.BlockSpec((rows, 1), lambda b: (0, 0))],
        out_specs=pl.BlockSpec((rows, seq), lambda b: (0, b)),
        out_shape=SDS((rows, t), F32),
        compiler_params=_cparams(1), name="decay")(ft, b_col)


def _matmul_kernel(a_ref, b_ref, o_ref):
    o_ref[...] = _dot(a_ref[...], b_ref[...]).astype(o_ref.dtype)


def _matmul_res_kernel(a_ref, b_ref, r_ref, o_ref):
    o_ref[...] = r_ref[...] + _dot(a_ref[...], b_ref[...])


def _matmul(a, b, n_out, out_dtype, res=None, name="matmul"):
    t, k = a.shape
    tm = _tile(t, 512)
    tn = _tile(n_out, 1024)
    in_specs = [pl.BlockSpec((tm, k), lambda j, i: (i, 0)),
                pl.BlockSpec((k, tn), lambda j, i: (0, j))]
    args = [a, b]
    kern = _matmul_kernel
    if res is not None:
        in_specs.append(pl.BlockSpec((tm, tn), lambda j, i: (i, j)))
        args.append(res)
        kern = _matmul_res_kernel
    return pl.pallas_call(
        kern,
        grid=(n_out // tn, t // tm),
        in_specs=in_specs,
        out_specs=pl.BlockSpec((tm, tn), lambda j, i: (i, j)),
        out_shape=SDS((t, n_out), out_dtype),
        compiler_params=_cparams(2), name=name)(*args)


def _fox_kernel(q_ref, k_ref, v_ref, c_ref, gq_ref, gk_ref, o_ref,
                qn_ref, kn_ref, bias_ref, m_ref, l_ref, acc_ref, *, tq):
    h = pl.program_id(1)
    seq = q_ref.shape[0]
    nq = seq // tq
    scale = HEAD_DIM ** -0.5
    qn_ref[...] = (_rms(q_ref[...].astype(F32), gq_ref[...]) * scale).astype(BF16)
    kn_ref[...] = _rms(k_ref[...].astype(F32), gk_ref[...]).astype(BF16)
    neg_c = -c_ref[pl.ds(h, 1), :]
    for j in range(nq):
        bias_ref[j] = neg_c[:, j * tq:(j + 1) * tq]

    row = lax.broadcasted_iota(I32, (tq, tq), 0)
    col = lax.broadcasted_iota(I32, (tq, tq), 1)

    def block(q, ki, masked):
        ks = pl.multiple_of(ki * tq, tq)
        s = _dot_nt(q, kn_ref[pl.ds(ks, tq), :]) + bias_ref[ki]
        if masked:
            s = jnp.where(col <= row, s, MASK_VALUE)
        m_old = m_ref[...]
        m_new = jnp.maximum(m_old, jnp.max(s, axis=-1, keepdims=True))
        alpha = jnp.exp(m_old - m_new)
        p = jnp.exp(s - m_new)
        l_ref[...] = alpha * l_ref[...] + jnp.sum(p, axis=-1, keepdims=True)
        acc_ref[...] = alpha * acc_ref[...] + _dot(p.astype(BF16), v_ref[pl.ds(ks, tq), :])
        m_ref[...] = m_new

    def q_block(qi, carry):
        qs = pl.multiple_of(qi * tq, tq)
        q = qn_ref[pl.ds(qs, tq), :]
        m_ref[...] = jnp.full(m_ref.shape, MASK_VALUE, F32)
        l_ref[...] = jnp.zeros(l_ref.shape, F32)
        acc_ref[...] = jnp.zeros(acc_ref.shape, F32)

        def k_block(ki, c):
            block(q, ki, False)
            return c

        lax.fori_loop(0, qi, k_block, 0)
        block(q, qi, True)
        o_ref[pl.ds(qs, tq), :] = (acc_ref[...] / l_ref[...]).astype(o_ref.dtype)
        return carry

    lax.fori_loop(0, nq, q_block, 0)


def _fox_attention(proj, c, g_q, g_k, batch, seq, n_heads, col0, width):
    t = proj.shape[0]
    tq = _tile(seq, 256)
    cb = col0 // HEAD_DIM
    wb = width // HEAD_DIM
    hd = HEAD_DIM
    kern = functools.partial(_fox_kernel, tq=tq)
    return pl.pallas_call(
        kern,
        grid=(batch, n_heads),
        in_specs=[pl.BlockSpec((seq, hd), lambda b, h: (b, cb + h)),
                  pl.BlockSpec((seq, hd), lambda b, h: (b, cb + wb + h)),
                  pl.BlockSpec((seq, hd), lambda b, h: (b, cb + 2 * wb + h)),
                  pl.BlockSpec((F_ROWS, seq), lambda b, h: (0, b)),
                  pl.BlockSpec((1, hd), lambda b, h: (0, 0)),
                  pl.BlockSpec((1, hd), lambda b, h: (0, 0))],
        out_specs=pl.BlockSpec((seq, hd), lambda b, h: (b, h)),
        out_shape=SDS((t, width), BF16),
        scratch_shapes=[pltpu.VMEM((seq, hd), BF16),
                        pltpu.VMEM((seq, hd), BF16),
                        pltpu.VMEM((seq // tq, 1, tq), F32),
                        pltpu.VMEM((tq, 1), F32),
                        pltpu.VMEM((tq, 1), F32),
                        pltpu.VMEM((tq, hd), F32)],
        compiler_params=_cparams(2), name="fox_attention")(proj, proj, proj, c, g_q, g_k)


def _mixprep_kernel(ub_ref, uc_ref, ux_ref, hc_ref, hx_ref, yf_ref, cw_ref, gc_ref, gf_ref,
                    o_ref):
    si = pl.program_id(1)
    p = uc_ref[...].astype(F32) * ux_ref[...].astype(F32)
    halo = hc_ref[...].astype(F32) * hx_ref[...].astype(F32)
    halo = jnp.where(si > 0, halo, 0.0)
    row = lax.broadcasted_iota(I32, p.shape, 0)
    p1 = jnp.where(row == 0, halo[7:8, :], pltpu.roll(p, 1, 0))
    p2 = jnp.where(row == 0, halo[6:7, :],
                   jnp.where(row == 1, halo[7:8, :], pltpu.roll(p, 2, 0)))
    w = cw_ref[...]
    y = ub_ref[...].astype(F32) * (w[0:1, :] * p2 + w[1:2, :] * p1 + w[2:3, :] * p)
    c = y.shape[1]
    o_ref[:, :c] = _rms(y, gc_ref[...]).astype(o_ref.dtype)
    o_ref[:, c:] = _rms(yf_ref[...].astype(F32), gf_ref[...]).astype(o_ref.dtype)


def _mixprep(proj, y_fox, conv_w, g_conv, g_fox, batch, seq):
    t = proj.shape[0]
    c = conv_w.shape[1]
    w = y_fox.shape[1]
    ts = _tile(seq, 512)
    ns = seq // ts
    hb = ts // 8

    def halo(col):
        return lambda b, s: (jnp.maximum((b * ns + s) * hb - 1, 0), col)

    return pl.pallas_call(
        _mixprep_kernel,
        grid=(batch, ns),
        in_specs=[pl.BlockSpec((ts, c), lambda b, s: (b * ns + s, 0)),
                  pl.BlockSpec((ts, c), lambda b, s: (b * ns + s, 1)),
                  pl.BlockSpec((ts, c), lambda b, s: (b * ns + s, 2)),
                  pl.BlockSpec((8, c), halo(1)),
                  pl.BlockSpec((8, c), halo(2)),
                  pl.BlockSpec((ts, w), lambda b, s: (b * ns + s, 0)),
                  pl.BlockSpec((3, c), lambda b, s: (0, 0)),
                  pl.BlockSpec((1, c), lambda b, s: (0, 0)),
                  pl.BlockSpec((1, w), lambda b, s: (0, 0))],
        out_specs=pl.BlockSpec((ts, c + w), lambda b, s: (b * ns + s, 0)),
        out_shape=SDS((t, c + w), BF16),
        compiler_params=_cparams(2), name="mixprep")(
            proj, proj, proj, proj, proj, y_fox, conv_w, g_conv, g_fox)


def _memkv_kernel(mem_ref, g_ref, w_ref, gk_ref, k_ref, v_ref):
    hm = _rms(mem_ref[...], g_ref[...]).astype(BF16)
    kv = _dot(hm, w_ref[...])
    xw = k_ref.shape[1]
    for h in range(xw // HEAD_DIM):
        sl = slice(h * HEAD_DIM, (h + 1) * HEAD_DIM)
        k_ref[:, sl] = _rms(kv[:, sl], gk_ref[...]).astype(BF16)
    v_ref[...] = kv[:, xw:].astype(BF16)


def _memkv(mem2d, g_mem, w_xkv, g_xk, n_mem):
    tm_, d = mem2d.shape
    xw = w_xkv.shape[1] // 2
    return pl.pallas_call(
        _memkv_kernel,
        grid=(tm_ // n_mem,),
        in_specs=[pl.BlockSpec((n_mem, d), lambda b: (b, 0)),
                  pl.BlockSpec((1, d), lambda b: (0, 0)),
                  pl.BlockSpec((d, 2 * xw), lambda b: (0, 0)),
                  pl.BlockSpec((1, HEAD_DIM), lambda b: (0, 0))],
        out_specs=[pl.BlockSpec((n_mem, xw), lambda b: (b, 0)),
                   pl.BlockSpec((n_mem, xw), lambda b: (b, 0))],
        out_shape=[SDS((tm_, xw), BF16), SDS((tm_, xw), BF16)],
        compiler_params=_cparams(1), name="memkv")(mem2d, g_mem, w_xkv, g_xk)


def _xattn_kernel(x_ref, gx_ref, wq_ref, gq_ref, kn_ref, v_ref, wo_ref, gm_ref, wr_ref, br_ref,
                  x2_ref, hm_ref, ti_ref, tg_ref):
    x = x_ref[...]
    hb = _rms(x, gx_ref[...]).astype(BF16)
    q = _dot(hb, wq_ref[...])
    scale = HEAD_DIM ** -0.5
    outs = []
    for h in range(q.shape[1] // HEAD_DIM):
        sl = slice(h * HEAD_DIM, (h + 1) * HEAD_DIM)
        qh = (_rms(q[:, sl], gq_ref[...]) * scale).astype(BF16)
        s = _dot_nt(qh, kn_ref[:, sl])
        p = jnp.exp(s - jnp.max(s, axis=-1, keepdims=True))
        oh = _dot(p.astype(BF16), v_ref[:, sl]) / jnp.sum(p, axis=-1, keepdims=True)
        outs.append(oh.astype(BF16))
    x2 = x + _dot(jnp.concatenate(outs, axis=-1), wo_ref[...])
    x2_ref[...] = x2
    hm = _rms(x2, gm_ref[...])
    hm_ref[...] = hm
    lg = _dot_nt(wr_ref[...], hm.astype(BF16)) + br_ref[...]
    n_exp = lg.shape[0]
    eidx = lax.broadcasted_iota(I32, lg.shape, 0).astype(F32)
    vals = []
    for r in range(TOP_K):
        mx = jnp.max(lg, axis=0, keepdims=True)
        am = jnp.min(jnp.where(lg == mx, eidx, float(n_exp)), axis=0, keepdims=True)
        ti_ref[r:r + 1, :] = am.astype(I32)
        vals.append(mx)
        lg = jnp.where(eidx == am, -jnp.inf, lg)
    ex = [jnp.exp(v - vals[0]) for v in vals]
    den = ex[0]
    for e in ex[1:]:
        den = den + e
    for r in range(TOP_K):
        tg_ref[r:r + 1, :] = ex[r] / den


def _xattn(x1, g_xattn, w_xq, g_xq, kn, vm, w_xo, g_moe, w_rt, b_r, seq, n_mem):
    t, d = x1.shape
    xw = w_xq.shape[1]
    n_exp = w_rt.shape[0]
    tm = _tile(seq, 512)
    per = seq // tm
    const = lambda i: (0, 0)
    return pl.pallas_call(
        _xattn_kernel,
        grid=(t // tm,),
        in_specs=[pl.BlockSpec((tm, d), lambda i: (i, 0)),
                  pl.BlockSpec((1, d), const),
                  pl.BlockSpec((d, xw), const),
                  pl.BlockSpec((1, HEAD_DIM), const),
                  pl.BlockSpec((n_mem, xw), lambda i: (i // per, 0)),
                  pl.BlockSpec((n_mem, xw), lambda i: (i // per, 0)),
                  pl.BlockSpec((xw, d), const),
                  pl.BlockSpec((1, d), const),
                  pl.BlockSpec((n_exp, d), const),
                  pl.BlockSpec((n_exp, 1), const)],
        out_specs=[pl.BlockSpec((tm, d), lambda i: (i, 0)),
                   pl.BlockSpec((tm, d), lambda i: (i, 0)),
                   pl.BlockSpec((TOP_K, tm), lambda i: (0, i)),
                   pl.BlockSpec((TOP_K, tm), lambda i: (0, i))],
        out_shape=[SDS((t, d), F32), SDS((t, d), F32),
                   SDS((TOP_K, t), I32), SDS((TOP_K, t), F32)],
        compiler_params=_cparams(1), name="xattn_router")(
            x1, g_xattn, w_xq, g_xq, kn, vm, w_xo, g_moe, w_rt, b_r)


def _routing(ti, n_exp, tm):
    k, t = ti.shape
    onehot = (ti[:, :, None] == jnp.arange(n_exp, dtype=I32)[None, None, :]).astype(I32).sum(0)
    incl = jnp.cumsum(onehot, axis=0)
    counts = incl[-1]
    padded = (counts + tm - 1) // tm * tm
    pad_end = jnp.cumsum(padded)
    base = (pad_end - padded)[None, :] + incl - onehot
    dest = jnp.take_along_axis(base, ti.T, axis=1).T.astype(I32)
    n_blk = (k * t) // tm + n_exp
    tok = jnp.broadcast_to(jnp.arange(t, dtype=I32)[None, :], (k, t))
    row_tok = jnp.zeros((n_blk * tm,), I32).at[dest.reshape(-1)].set(tok.reshape(-1))
    n_used = (pad_end[-1] // tm).astype(I32)
    blk = jnp.minimum(jnp.arange(n_blk, dtype=I32), n_used - 1)
    blk_exp = jnp.minimum(jnp.searchsorted(pad_end, blk * tm, side="right"), n_exp - 1).astype(I32)
    return dest, row_tok.reshape(n_blk, 1, tm), blk, blk_exp, n_used.reshape(1)


def _gather_kernel(nu_ref, tok_ref, nxt_ref, src_ref, o_ref, buf, sem, *, tm):
    i = pl.program_id(0)
    n_used = nu_ref[0]

    def row_copy(t, r, slot):
        return pltpu.make_async_copy(src_ref.at[pl.ds(t, 1)], buf.at[slot, pl.ds(r, 1)],
                                     sem.at[slot])

    def start_block(ref, slot):
        def body(r, c):
            row_copy(ref[0, 0, r], r, slot).start()
            return c
        lax.fori_loop(0, tm, body, 0)

    @pl.when(i == 0)
    def _():
        start_block(tok_ref, 0)

    @pl.when(i + 1 < n_used)
    def _():
        start_block(nxt_ref, (i + 1) % 2)

    @pl.when(i < n_used)
    def _():
        slot = i % 2

        def body(r, c):
            row_copy(0, r, slot).wait()
            return c
        lax.fori_loop(0, tm, body, 0)
        o_ref[...] = buf[slot].astype(o_ref.dtype)

    @pl.when(i >= n_used)
    def _():
        o_ref[...] = jnp.zeros(o_ref.shape, o_ref.dtype)


def _gather_rows(src, row_tok, n_used, tm):
    n_blk = row_tok.shape[0]
    d = src.shape[1]
    kern = functools.partial(_gather_kernel, tm=tm)
    grid_spec = pltpu.PrefetchScalarGridSpec(
        num_scalar_prefetch=1,
        grid=(n_blk,),
        in_specs=[pl.BlockSpec((1, 1, tm), lambda i, nu: (i, 0, 0), memory_space=pltpu.SMEM),
                  pl.BlockSpec((1, 1, tm), lambda i, nu: (jnp.minimum(i + 1, n_blk - 1), 0, 0),
                               memory_space=pltpu.SMEM),
                  pl.BlockSpec(memory_space=pl.ANY)],
        out_specs=pl.BlockSpec((tm, d), lambda i, nu: (i, 0)),
        scratch_shapes=[pltpu.VMEM((2, tm, d), src.dtype),
                        pltpu.SemaphoreType.DMA((2,))])
    return pl.pallas_call(
        kern, grid_spec=grid_spec,
        out_shape=SDS((n_blk * tm, d), BF16),
        compiler_params=_cparams(1), name="moe_gather")(n_used, row_tok, row_tok, src)


def _moe_up_kernel(be_ref, bi_ref, nu_ref, x_ref, wg_ref, wu_ref, bg_ref, bu_ref, o_ref,
                   wgb_ref, wub_ref):
    i = pl.program_id(1)
    e = be_ref[i]
    e_prev = be_ref[jnp.maximum(i - 1, 0)]

    @pl.when((i == 0) | (e != e_prev))
    def _():
        wgb_ref[...] = wg_ref[...].astype(BF16)
        wub_ref[...] = wu_ref[...].astype(BF16)

    @pl.when(i < nu_ref[0])
    def _():
        x = x_ref[...]
        gate = jnp.minimum(_dot(x, wgb_ref[...]) + bg_ref[...], SWIGLU_LIMIT)
        up = jnp.clip(_dot(x, wub_ref[...]) + bu_ref[...], -SWIGLU_LIMIT, SWIGLU_LIMIT)
        glu = gate * jax.nn.sigmoid(SWIGLU_ALPHA * gate)
        o_ref[...] = ((up + 1.0) * glu).astype(o_ref.dtype)

    @pl.when(i >= nu_ref[0])
    def _():
        o_ref[...] = jnp.zeros(o_ref.shape, o_ref.dtype)


def _moe_up(xs, w_gu, b_gu, blk, blk_exp, n_used, tm):
    n_rows, d = xs.shape
    f = w_gu.shape[2] // 2
    tn = _tile(f, 1024)
    nj = f // tn
    grid_spec = pltpu.PrefetchScalarGridSpec(
        num_scalar_prefetch=3,
        grid=(nj, n_rows // tm),
        in_specs=[pl.BlockSpec((tm, d), lambda j, i, be, bi, nu: (bi[i], 0)),
                  pl.BlockSpec((None, d, tn), lambda j, i, be, bi, nu: (be[i], 0, j)),
                  pl.BlockSpec((None, d, tn), lambda j, i, be, bi, nu: (be[i], 0, j + nj)),
                  pl.BlockSpec((None, 1, tn), lambda j, i, be, bi, nu: (be[i], 0, j)),
                  pl.BlockSpec((None, 1, tn), lambda j, i, be, bi, nu: (be[i], 0, j + nj))],
        out_specs=pl.BlockSpec((tm, tn), lambda j, i, be, bi, nu: (i, j)),
        scratch_shapes=[pltpu.VMEM((d, tn), BF16), pltpu.VMEM((d, tn), BF16)])
    return pl.pallas_call(
        _moe_up_kernel, grid_spec=grid_spec,
        out_shape=SDS((n_rows, f), BF16),
        compiler_params=_cparams(2), name="moe_up")(
            blk_exp, blk, n_used, xs, w_gu, w_gu, b_gu, b_gu)


def _moe_down_kernel(be_ref, bi_ref, nu_ref, a_ref, w_ref, b_ref, o_ref, wb_ref):
    i = pl.program_id(0)
    e = be_ref[i]
    e_prev = be_ref[jnp.maximum(i - 1, 0)]

    @pl.when((i == 0) | (e != e_prev))
    def _():
        wb_ref[...] = w_ref[...].astype(BF16)

    @pl.when(i < nu_ref[0])
    def _():
        o_ref[...] = _dot(a_ref[...], wb_ref[...]) + b_ref[...]

    @pl.when(i >= nu_ref[0])
    def _():
        o_ref[...] = jnp.zeros(o_ref.shape, o_ref.dtype)


def _moe_down(act, w_dn, b_dn, blk, blk_exp, n_used, tm):
    n_rows, f = act.shape
    d = w_dn.shape[2]
    grid_spec = pltpu.PrefetchScalarGridSpec(
        num_scalar_prefetch=3,
        grid=(n_rows // tm,),
        in_specs=[pl.BlockSpec((tm, f), lambda i, be, bi, nu: (bi[i], 0)),
                  pl.BlockSpec((None, f, d), lambda i, be, bi, nu: (be[i], 0, 0)),
                  pl.BlockSpec((None, 1, d), lambda i, be, bi, nu: (be[i], 0, 0))],
        out_specs=pl.BlockSpec((tm, d), lambda i, be, bi, nu: (i, 0)),
        scratch_shapes=[pltpu.VMEM((f, d), BF16)])
    return pl.pallas_call(
        _moe_down_kernel, grid_spec=grid_spec,
        out_shape=SDS((n_rows, d), F32),
        compiler_params=_cparams(1), name="moe_down")(blk_exp, blk, n_used, act, w_dn, b_dn)


def _combine_kernel(dst_ref, nxt_ref, x_ref, g_ref, ys_ref, o_ref, buf, sem, *, tc):
    i = pl.program_id(0)
    n = pl.num_programs(0)

    def row_copy(row, k, r, slot):
        return pltpu.make_async_copy(ys_ref.at[pl.ds(row, 1)], buf.at[slot, k, pl.ds(r, 1)],
                                     sem.at[slot])

    def start_block(ref, slot):
        for k in range(TOP_K):
            def body(r, c, k=k):
                row_copy(ref[0, k, r], k, r, slot).start()
                return c
            lax.fori_loop(0, tc, body, 0)

    @pl.when(i == 0)
    def _():
        start_block(dst_ref, 0)

    @pl.when(i + 1 < n)
    def _():
        start_block(nxt_ref, (i + 1) % 2)

    slot = i % 2
    for k in range(TOP_K):
        def body(r, c, k=k):
            row_copy(0, k, r, slot).wait()
            return c
        lax.fori_loop(0, tc, body, 0)
    acc = x_ref[...]
    g = g_ref[...]
    for k in range(TOP_K):
        acc = acc + g[:, k:k + 1] * buf[slot, k]
    o_ref[...] = acc


def _combine(x2, gates_t, dest, ys):
    t, d = x2.shape
    tc = _tile(t, 128)
    n = t // tc
    dest3 = dest.reshape(TOP_K, n, tc).transpose(1, 0, 2)
    kern = functools.partial(_combine_kernel, tc=tc)
    return pl.pallas_call(
        kern,
        grid=(n,),
        in_specs=[pl.BlockSpec((1, TOP_K, tc), lambda i: (i, 0, 0), memory_space=pltpu.SMEM),
                  pl.BlockSpec((1, TOP_K, tc), lambda i: (jnp.minimum(i + 1, n - 1), 0, 0),
                               memory_space=pltpu.SMEM),
                  pl.BlockSpec((tc, d), lambda i: (i, 0)),
                  pl.BlockSpec((tc, TOP_K), lambda i: (i, 0)),
                  pl.BlockSpec(memory_space=pl.ANY)],
        out_specs=pl.BlockSpec((tc, d), lambda i: (i, 0)),
        out_shape=SDS((t, d), F32),
        scratch_shapes=[pltpu.VMEM((2, TOP_K, tc, d), F32),
                        pltpu.SemaphoreType.DMA((2,))],
        compiler_params=_cparams(1), name="moe_combine")(dest3, dest3, x2, gates_t, ys)


def _layer(x2d, mem2d, batch, seq, n_mem, p):
    t, d = x2d.shape
    c = p["conv_w"].shape[1]
    fw = p["g_fox_out"].shape[0]
    n_heads = fw // HEAD_DIM
    n_main = 3 * c + 3 * fw
    n_exp = p["w_router"].shape[1]

    w_in = p["w_mix_in"].astype(BF16)
    wf_t = jnp.zeros((F_ROWS, d), BF16).at[:n_heads].set(w_in[:, n_main:].T)
    b_col = jnp.zeros((F_ROWS, 1), F32).at[:n_heads, 0].set(p["b_forget"])

    h, ft = _prenorm(x2d, p["g_mix"][None, :], wf_t)
    cdec = _decay(ft, b_col, seq)
    proj = _matmul(h, w_in, n_main, BF16, name="mix_in")
    y_fox = _fox_attention(proj, cdec, p["g_q"][None, :], p["g_k"][None, :],
                           batch, seq, n_heads, 3 * c, fw)
    y_mix = _mixprep(proj, y_fox, p["conv_w"], p["g_conv_out"][None, :],
                     p["g_fox_out"][None, :], batch, seq)
    x1 = _matmul(y_mix, p["w_mix_out"].astype(BF16), d, F32, res=x2d, name="mix_out")

    kn, vm = _memkv(mem2d, p["g_mem"][None, :], p["w_xkv"].astype(BF16),
                    p["g_xk"][None, :], n_mem)
    x2, hm, ti, tg = _xattn(x1, p["g_xattn"][None, :], p["w_xq"].astype(BF16),
                            p["g_xq"][None, :], kn, vm, p["w_xo"].astype(BF16),
                            p["g_moe"][None, :], p["w_router"].T.astype(BF16),
                            p["b_router"][:, None], seq, n_mem)

    tm = _tile(TOP_K * t, 256)
    dest, row_tok, blk, blk_exp, n_used = _routing(ti, n_exp, tm)
    xs = _gather_rows(hm, row_tok, n_used, tm)
    act = _moe_up(xs, p["w_gate_up"], p["b_gate_up"][:, None, :], blk, blk_exp, n_used, tm)
    ys = _moe_down(act, p["w_down"], p["b_down"][:, None, :], blk, blk_exp, n_used, tm)
    return _combine(x2, tg.T, dest, ys)


def kernel(x, mem, g_mix, w_mix_in, b_forget, conv_w, g_q, g_k, g_conv_out, g_fox_out, w_mix_out, g_xattn, g_mem, w_xq, w_xkv, g_xq, g_xk, w_xo, g_moe, w_router, b_router, w_gate_up, b_gate_up, w_down, b_down):
    batch, seq, d = x.shape
    n_mem = mem.shape[1]
    params = dict(g_mix=g_mix, w_mix_in=w_mix_in, b_forget=b_forget, conv_w=conv_w, g_q=g_q,
                  g_k=g_k, g_conv_out=g_conv_out, g_fox_out=g_fox_out, w_mix_out=w_mix_out,
                  g_xattn=g_xattn, g_mem=g_mem, w_xq=w_xq, w_xkv=w_xkv, g_xq=g_xq, g_xk=g_xk,
                  w_xo=w_xo, g_moe=g_moe, w_router=w_router, b_router=b_router,
                  w_gate_up=w_gate_up, b_gate_up=b_gate_up, w_down=w_down, b_down=b_down)
    x2d = x.reshape(batch * seq, d)
    mem2d = mem.reshape(batch * n_mem, d)
    for l in range(g_mix.shape[0]):
        x2d = _layer(x2d, mem2d, batch, seq, n_mem, {k: v[l] for k, v in params.items()})
    return x2d.reshape(batch, seq, d)
```

```python
import functools

import jax
import jax.numpy as jnp
from jax import lax
from jax.experimental import pallas as pl
from jax.experimental.pallas import tpu as pltpu

F32 = jnp.float32
BF16 = jnp.bfloat16
I32 = jnp.int32
SDS = jax.ShapeDtypeStruct

EPS = 1e-6
HEAD_DIM = 128
TOP_K = 4
SWIGLU_LIMIT = 7.0
SWIGLU_ALPHA = 1.702
MASK_VALUE = float(jnp.finfo(jnp.float32).min)
V7X_VMEM_LIMIT_BYTES = 56 * 1024 * 1024
F_ROWS = 16


def _cparams(n_axes):
    return pltpu.CompilerParams(
        dimension_semantics=("arbitrary",) * n_axes,
        vmem_limit_bytes=V7X_VMEM_LIMIT_BYTES)


def _rms(x, g):
    return x * lax.rsqrt(jnp.mean(x * x, axis=-1, keepdims=True) + EPS) * g


def _dot(a, b):
    return jnp.dot(a, b, preferred_element_type=F32)


def _dot_nt(a, b):
    return lax.dot_general(a, b, (((1,), (1,)), ((), ())), preferred_element_type=F32)


def _tile(n, pref):
    return pref if n % pref == 0 else n


def _prenorm_kernel(x_ref, g_ref, wf_ref, h_ref, ft_ref):
    hb = _rms(x_ref[...], g_ref[...]).astype(BF16)
    h_ref[...] = hb
    ft_ref[...] = _dot_nt(wf_ref[...], hb)


def _prenorm(x2d, g, wf_t):
    t, d = x2d.shape
    tm = _tile(t, 512)
    return pl.pallas_call(
        _prenorm_kernel,
        grid=(t // tm,),
        in_specs=[pl.BlockSpec((tm, d), lambda i: (i, 0)),
                  pl.BlockSpec((1, d), lambda i: (0, 0)),
                  pl.BlockSpec((F_ROWS, d), lambda i: (0, 0))],
        out_specs=[pl.BlockSpec((tm, d), lambda i: (i, 0)),
                   pl.BlockSpec((F_ROWS, tm), lambda i: (0, i))],
        out_shape=[SDS((t, d), BF16), SDS((F_ROWS, t), F32)],
        compiler_params=_cparams(1), name="prenorm")(x2d, g, wf_t)


def _decay_kernel(ft_ref, b_ref, c_ref):
    z = ft_ref[...] + b_ref[...]
    lf = jnp.minimum(z, 0.0) - jnp.log(1.0 + jnp.exp(-jnp.abs(z)))
    s = lf.shape[1]
    lane = lax.broadcasted_iota(I32, lf.shape, 1)
    d = 1
    while d < s:
        lf = lf + jnp.where(lane >= d, pltpu.roll(lf, d, 1), 0.0)
        d *= 2
    c_ref[...] = lf


def _decay(ft, b_col, seq):
    rows, t = ft.shape
    return pl.pallas_call(
        _decay_kernel,
        grid=(t // seq,),
        in_specs=[pl.BlockSpec((rows, seq), lambda b: (0, b)),
                  pl.BlockSpec((rows, 1), lambda b: (0, 0))],
        out_specs=pl.BlockSpec((rows, seq), lambda b: (0, b)),
        out_shape=SDS((rows, t), F32),
        compiler_params=_cparams(1), name="decay")(ft, b_col)


def _matmul_kernel(a_ref, b_ref, o_ref):
    o_ref[...] = _dot(a_ref[...], b_ref[...]).astype(o_ref.dtype)


def _matmul_res_kernel(a_ref, b_ref, r_ref, o_ref):
    o_ref[...] = r_ref[...] + _dot(a_ref[...], b_ref[...])


def _matmul(a, b, n_out, out_dtype, res=None, name="matmul"):
    t, k = a.shape
    tm = _tile(t, 512)
    tn = _tile(n_out, 1024)
    in_specs = [pl.BlockSpec((tm, k), lambda j, i: (i, 0)),
                pl.BlockSpec((k, tn), lambda j, i: (0, j))]
    args = [a, b]
    kern = _matmul_kernel
    if res is not None:
        in_specs.append(pl.BlockSpec((tm, tn), lambda j, i: (i, j)))
        args.append(res)
        kern = _matmul_res_kernel
    return pl.pallas_call(
        kern,
        grid=(n_out // tn, t // tm),
        in_specs=in_specs,
        out_specs=pl.BlockSpec((tm, tn), lambda j, i: (i, j)),
        out_shape=SDS((t, n_out), out_dtype),
        compiler_params=_cparams(2), name=name)(*args)


def _fox_kernel(q_ref, k_ref, v_ref, c_ref, gq_ref, gk_ref, o_ref, kn_ref, nc_ref, *, tq):
    h = pl.program_id(1)
    seq = q_ref.shape[0]
    scale = HEAD_DIM ** -0.5
    kn_ref[...] = _rms(k_ref[...].astype(F32), gk_ref[...]).astype(BF16)
    nc_ref[...] = -c_ref[pl.ds(h, 1), :]
    row = lax.broadcasted_iota(I32, (tq, tq), 0)
    col = lax.broadcasted_iota(I32, (tq, tq), 1)
    causal = col <= row
    for qi in range(seq // tq):
        lo, hi = qi * tq, (qi + 1) * tq
        q = (_rms(q_ref[lo:hi, :].astype(F32), gq_ref[...]) * scale).astype(BF16)
        s_diag = jnp.where(causal, _dot_nt(q, kn_ref[lo:hi, :]) + nc_ref[:, lo:hi], MASK_VALUE)
        m = jnp.max(s_diag, axis=-1, keepdims=True)
        if qi > 0:
            s_past = _dot_nt(q, kn_ref[:lo, :]) + nc_ref[:, :lo]
            m = jnp.maximum(m, jnp.max(s_past, axis=-1, keepdims=True))
        p_diag = jnp.exp(s_diag - m)
        l = jnp.sum(p_diag, axis=-1, keepdims=True)
        acc = _dot(p_diag.astype(BF16), v_ref[lo:hi, :])
        if qi > 0:
            p_past = jnp.exp(s_past - m)
            l = l + jnp.sum(p_past, axis=-1, keepdims=True)
            acc = acc + _dot(p_past.astype(BF16), v_ref[:lo, :])
        o_ref[lo:hi, :] = (acc / l).astype(o_ref.dtype)


def _fox_attention(proj, c, g_q, g_k, batch, seq, n_heads, col0, width):
    t = proj.shape[0]
    tq = _tile(seq, 256)
    cb = col0 // HEAD_DIM
    wb = width // HEAD_DIM
    hd = HEAD_DIM
    kern = functools.partial(_fox_kernel, tq=tq)
    return pl.pallas_call(
        kern,
        grid=(batch, n_heads),
        in_specs=[pl.BlockSpec((seq, hd), lambda b, h: (b, cb + h)),
                  pl.BlockSpec((seq, hd), lambda b, h: (b, cb + wb + h)),
                  pl.BlockSpec((seq, hd), lambda b, h: (b, cb + 2 * wb + h)),
                  pl.BlockSpec((F_ROWS, seq), lambda b, h: (0, b)),
                  pl.BlockSpec((1, hd), lambda b, h: (0, 0)),
                  pl.BlockSpec((1, hd), lambda b, h: (0, 0))],
        out_specs=pl.BlockSpec((seq, hd), lambda b, h: (b, h)),
        out_shape=SDS((t, width), BF16),
        scratch_shapes=[pltpu.VMEM((seq, hd), BF16), pltpu.VMEM((1, seq), F32)],
        compiler_params=_cparams(2), name="fox_attention")(proj, proj, proj, c, g_q, g_k)


def _mixprep_kernel(ub_ref, uc_ref, ux_ref, hc_ref, hx_ref, yf_ref, cw_ref, gc_ref, gf_ref,
                    o_ref):
    si = pl.program_id(1)
    p = uc_ref[...].astype(F32) * ux_ref[...].astype(F32)
    halo = hc_ref[...].astype(F32) * hx_ref[...].astype(F32)
    halo = jnp.where(si > 0, halo, 0.0)
    row = lax.broadcasted_iota(I32, p.shape, 0)
    p1 = jnp.where(row == 0, halo[7:8, :], pltpu.roll(p, 1, 0))
    p2 = jnp.where(row == 0, halo[6:7, :],
                   jnp.where(row == 1, halo[7:8, :], pltpu.roll(p, 2, 0)))
    w = cw_ref[...]
    y = ub_ref[...].astype(F32) * (w[0:1, :] * p2 + w[1:2, :] * p1 + w[2:3, :] * p)
    c = y.shape[1]
    o_ref[:, :c] = _rms(y, gc_ref[...]).astype(o_ref.dtype)
    o_ref[:, c:] = _rms(yf_ref[...].astype(F32), gf_ref[...]).astype(o_ref.dtype)


def _mixprep(proj, y_fox, conv_w, g_conv, g_fox, batch, seq):
    t = proj.shape[0]
    c = conv_w.shape[1]
    w = y_fox.shape[1]
    ts = _tile(seq, 512)
    ns = seq // ts
    hb = ts // 8

    def halo(col):
        return lambda b, s: (jnp.maximum((b * ns + s) * hb - 1, 0), col)

    return pl.pallas_call(
        _mixprep_kernel,
        grid=(batch, ns),
        in_specs=[pl.BlockSpec((ts, c), lambda b, s: (b * ns + s, 0)),
                  pl.BlockSpec((ts, c), lambda b, s: (b * ns + s, 1)),
                  pl.BlockSpec((ts, c), lambda b, s: (b * ns + s, 2)),
                  pl.BlockSpec((8, c), halo(1)),
                  pl.BlockSpec((8, c), halo(2)),
                  pl.BlockSpec((ts, w), lambda b, s: (b * ns + s, 0)),
                  pl.BlockSpec((3, c), lambda b, s: (0, 0)),
                  pl.BlockSpec((1, c), lambda b, s: (0, 0)),
                  pl.BlockSpec((1, w), lambda b, s: (0, 0))],
        out_specs=pl.BlockSpec((ts, c + w), lambda b, s: (b * ns + s, 0)),
        out_shape=SDS((t, c + w), BF16),
        compiler_params=_cparams(2), name="mixprep")(
            proj, proj, proj, proj, proj, y_fox, conv_w, g_conv, g_fox)


def _memkv_kernel(mem_ref, g_ref, w_ref, gk_ref, k_ref, v_ref):
    hm = _rms(mem_ref[...], g_ref[...]).astype(BF16)
    kv = _dot(hm, w_ref[...])
    xw = k_ref.shape[1]
    for h in range(xw // HEAD_DIM):
        sl = slice(h * HEAD_DIM, (h + 1) * HEAD_DIM)
        k_ref[:, sl] = _rms(kv[:, sl], gk_ref[...]).astype(BF16)
    v_ref[...] = kv[:, xw:].astype(BF16)


def _memkv(mem2d, g_mem, w_xkv, g_xk, n_mem):
    tm_, d = mem2d.shape
    xw = w_xkv.shape[1] // 2
    return pl.pallas_call(
        _memkv_kernel,
        grid=(tm_ // n_mem,),
        in_specs=[pl.BlockSpec((n_mem, d), lambda b: (b, 0)),
                  pl.BlockSpec((1, d), lambda b: (0, 0)),
                  pl.BlockSpec((d, 2 * xw), lambda b: (0, 0)),
                  pl.BlockSpec((1, HEAD_DIM), lambda b: (0, 0))],
        out_specs=[pl.BlockSpec((n_mem, xw), lambda b: (b, 0)),
                   pl.BlockSpec((n_mem, xw), lambda b: (b, 0))],
        out_shape=[SDS((tm_, xw), BF16), SDS((tm_, xw), BF16)],
        compiler_params=_cparams(1), name="memkv")(mem2d, g_mem, w_xkv, g_xk)


def _xattn_kernel(x_ref, gx_ref, wq_ref, gq_ref, kn_ref, v_ref, wo_ref, gm_ref, wr_ref, br_ref,
                  x2_ref, hm_ref, ti_ref, tg_ref):
    x = x_ref[...]
    hb = _rms(x, gx_ref[...]).astype(BF16)
    q = _dot(hb, wq_ref[...])
    scale = HEAD_DIM ** -0.5
    outs = []
    for h in range(q.shape[1] // HEAD_DIM):
        sl = slice(h * HEAD_DIM, (h + 1) * HEAD_DIM)
        qh = (_rms(q[:, sl], gq_ref[...]) * scale).astype(BF16)
        s = _dot_nt(qh, kn_ref[:, sl])
        p = jnp.exp(s - jnp.max(s, axis=-1, keepdims=True))
        oh = _dot(p.astype(BF16), v_ref[:, sl]) / jnp.sum(p, axis=-1, keepdims=True)
        outs.append(oh.astype(BF16))
    x2 = x + _dot(jnp.concatenate(outs, axis=-1), wo_ref[...])
    x2_ref[...] = x2
    hm = _rms(x2, gm_ref[...])
    hm_ref[...] = hm
    lg = _dot_nt(wr_ref[...], hm.astype(BF16)) + br_ref[...]
    n_exp = lg.shape[0]
    eidx = lax.broadcasted_iota(I32, lg.shape, 0).astype(F32)
    vals = []
    for r in range(TOP_K):
        mx = jnp.max(lg, axis=0, keepdims=True)
        am = jnp.min(jnp.where(lg == mx, eidx, float(n_exp)), axis=0, keepdims=True)
        ti_ref[r:r + 1, :] = am.astype(I32)
        vals.append(mx)
        lg = jnp.where(eidx == am, -jnp.inf, lg)
    ex = [jnp.exp(v - vals[0]) for v in vals]
    den = ex[0]
    for e in ex[1:]:
        den = den + e
    for r in range(TOP_K):
        tg_ref[r:r + 1, :] = ex[r] / den


def _xattn(x1, g_xattn, w_xq, g_xq, kn, vm, w_xo, g_moe, w_rt, b_r, seq, n_mem):
    t, d = x1.shape
    xw = w_xq.shape[1]
    n_exp = w_rt.shape[0]
    tm = _tile(seq, 512)
    per = seq // tm
    const = lambda i: (0, 0)
    return pl.pallas_call(
        _xattn_kernel,
        grid=(t // tm,),
        in_specs=[pl.BlockSpec((tm, d), lambda i: (i, 0)),
                  pl.BlockSpec((1, d), const),
                  pl.BlockSpec((d, xw), const),
                  pl.BlockSpec((1, HEAD_DIM), const),
                  pl.BlockSpec((n_mem, xw), lambda i: (i // per, 0)),
                  pl.BlockSpec((n_mem, xw), lambda i: (i // per, 0)),
                  pl.BlockSpec((xw, d), const),
                  pl.BlockSpec((1, d), const),
                  pl.BlockSpec((n_exp, d), const),
                  pl.BlockSpec((n_exp, 1), const)],
        out_specs=[pl.BlockSpec((tm, d), lambda i: (i, 0)),
                   pl.BlockSpec((tm, d), lambda i: (i, 0)),
                   pl.BlockSpec((TOP_K, tm), lambda i: (0, i)),
                   pl.BlockSpec((TOP_K, tm), lambda i: (0, i))],
        out_shape=[SDS((t, d), F32), SDS((t, d), F32),
                   SDS((TOP_K, t), I32), SDS((TOP_K, t), F32)],
        compiler_params=_cparams(1), name="xattn_router")(
            x1, g_xattn, w_xq, g_xq, kn, vm, w_xo, g_moe, w_rt, b_r)


def _routing(ti, n_exp, tm):
    k, t = ti.shape
    onehot = (ti[:, :, None] == jnp.arange(n_exp, dtype=I32)[None, None, :]).astype(I32).sum(0)
    incl = jnp.cumsum(onehot, axis=0)
    counts = incl[-1]
    padded = (counts + tm - 1) // tm * tm
    pad_end = jnp.cumsum(padded)
    base = (pad_end - padded)[None, :] + incl - onehot
    dest = jnp.take_along_axis(base, ti.T, axis=1).T.astype(I32)
    n_blk = (k * t) // tm + n_exp
    tok = jnp.broadcast_to(jnp.arange(t, dtype=I32)[None, :], (k, t))
    row_tok = jnp.zeros((n_blk * tm,), I32).at[dest.reshape(-1)].set(tok.reshape(-1))
    n_used = (pad_end[-1] // tm).astype(I32)
    blk = jnp.minimum(jnp.arange(n_blk, dtype=I32), n_used - 1)
    blk_exp = jnp.minimum(jnp.searchsorted(pad_end, blk * tm, side="right"), n_exp - 1).astype(I32)
    return dest, row_tok.reshape(n_blk, 1, tm), blk, blk_exp, n_used.reshape(1)


def _gather_kernel(nu_ref, tok_ref, nxt_ref, src_ref, o_ref, buf, sem, *, tm):
    i = pl.program_id(0)
    n_used = nu_ref[0]

    def row_copy(t, r, slot):
        return pltpu.make_async_copy(src_ref.at[pl.ds(t, 1)], buf.at[slot, pl.ds(r, 1)],
                                     sem.at[slot])

    def start_block(ref, slot):
        def body(r, c):
            row_copy(ref[0, 0, r], r, slot).start()
            return c
        lax.fori_loop(0, tm, body, 0, unroll=8)

    @pl.when(i == 0)
    def _():
        start_block(tok_ref, 0)

    @pl.when(i + 1 < n_used)
    def _():
        start_block(nxt_ref, (i + 1) % 2)

    @pl.when(i < n_used)
    def _():
        slot = i % 2
        pltpu.make_async_copy(src_ref.at[pl.ds(0, tm)], buf.at[slot], sem.at[slot]).wait()
        o_ref[...] = buf[slot].astype(o_ref.dtype)

    @pl.when(i >= n_used)
    def _():
        o_ref[...] = jnp.zeros(o_ref.shape, o_ref.dtype)


def _gather_rows(src, row_tok, n_used, tm):
    n_blk = row_tok.shape[0]
    d = src.shape[1]
    kern = functools.partial(_gather_kernel, tm=tm)
    grid_spec = pltpu.PrefetchScalarGridSpec(
        num_scalar_prefetch=1,
        grid=(n_blk,),
        in_specs=[pl.BlockSpec((1, 1, tm), lambda i, nu: (i, 0, 0), memory_space=pltpu.SMEM),
                  pl.BlockSpec((1, 1, tm), lambda i, nu: (jnp.minimum(i + 1, n_blk - 1), 0, 0),
                               memory_space=pltpu.SMEM),
                  pl.BlockSpec(memory_space=pl.ANY)],
        out_specs=pl.BlockSpec((tm, d), lambda i, nu: (i, 0)),
        scratch_shapes=[pltpu.VMEM((2, tm, d), src.dtype),
                        pltpu.SemaphoreType.DMA((2,))])
    return pl.pallas_call(
        kern, grid_spec=grid_spec,
        out_shape=SDS((n_blk * tm, d), BF16),
        compiler_params=_cparams(1), name="moe_gather")(n_used, row_tok, row_tok, src)


def _moe_up_kernel(be_ref, bi_ref, nu_ref, x_ref, wg_ref, wu_ref, bg_ref, bu_ref, o_ref,
                   wgb_ref, wub_ref):
    i = pl.program_id(1)
    e = be_ref[i]
    e_prev = be_ref[jnp.maximum(i - 1, 0)]

    @pl.when((i == 0) | (e != e_prev))
    def _():
        wgb_ref[...] = wg_ref[...].astype(BF16)
        wub_ref[...] = wu_ref[...].astype(BF16)

    @pl.when(i < nu_ref[0])
    def _():
        x = x_ref[...]
        gate = jnp.minimum(_dot(x, wgb_ref[...]) + bg_ref[...], SWIGLU_LIMIT)
        up = jnp.clip(_dot(x, wub_ref[...]) + bu_ref[...], -SWIGLU_LIMIT, SWIGLU_LIMIT)
        glu = gate * jax.nn.sigmoid(SWIGLU_ALPHA * gate)
        o_ref[...] = ((up + 1.0) * glu).astype(o_ref.dtype)

    @pl.when(i >= nu_ref[0])
    def _():
        o_ref[...] = jnp.zeros(o_ref.shape, o_ref.dtype)


def _moe_up(xs, w_gu, b_gu, blk, blk_exp, n_used, tm):
    n_rows, d = xs.shape
    f = w_gu.shape[2] // 2
    tn = _tile(f, 1024)
    nj = f // tn
    grid_spec = pltpu.PrefetchScalarGridSpec(
        num_scalar_prefetch=3,
        grid=(nj, n_rows // tm),
        in_specs=[pl.BlockSpec((tm, d), lambda j, i, be, bi, nu: (bi[i], 0)),
                  pl.BlockSpec((None, d, tn), lambda j, i, be, bi, nu: (be[i], 0, j)),
                  pl.BlockSpec((None, d, tn), lambda j, i, be, bi, nu: (be[i], 0, j + nj)),
                  pl.BlockSpec((None, 1, tn), lambda j, i, be, bi, nu: (be[i], 0, j)),
                  pl.BlockSpec((None, 1, tn), lambda j, i, be, bi, nu: (be[i], 0, j + nj))],
        out_specs=pl.BlockSpec((tm, tn), lambda j, i, be, bi, nu: (i, j)),
        scratch_shapes=[pltpu.VMEM((d, tn), BF16), pltpu.VMEM((d, tn), BF16)])
    return pl.pallas_call(
        _moe_up_kernel, grid_spec=grid_spec,
        out_shape=SDS((n_rows, f), BF16),
        compiler_params=_cparams(2), name="moe_up")(
            blk_exp, blk, n_used, xs, w_gu, w_gu, b_gu, b_gu)


def _moe_down_kernel(be_ref, bi_ref, nu_ref, a_ref, w_ref, b_ref, o_ref, wb_ref):
    i = pl.program_id(0)
    e = be_ref[i]
    e_prev = be_ref[jnp.maximum(i - 1, 0)]

    @pl.when((i == 0) | (e != e_prev))
    def _():
        wb_ref[...] = w_ref[...].astype(BF16)

    @pl.when(i < nu_ref[0])
    def _():
        o_ref[...] = _dot(a_ref[...], wb_ref[...]) + b_ref[...]

    @pl.when(i >= nu_ref[0])
    def _():
        o_ref[...] = jnp.zeros(o_ref.shape, o_ref.dtype)


def _moe_down(act, w_dn, b_dn, blk, blk_exp, n_used, tm):
    n_rows, f = act.shape
    d = w_dn.shape[2]
    grid_spec = pltpu.PrefetchScalarGridSpec(
        num_scalar_prefetch=3,
        grid=(n_rows // tm,),
        in_specs=[pl.BlockSpec((tm, f), lambda i, be, bi, nu: (bi[i], 0)),
                  pl.BlockSpec((None, f, d), lambda i, be, bi, nu: (be[i], 0, 0)),
                  pl.BlockSpec((None, 1, d), lambda i, be, bi, nu: (be[i], 0, 0))],
        out_specs=pl.BlockSpec((tm, d), lambda i, be, bi, nu: (i, 0)),
        scratch_shapes=[pltpu.VMEM((f, d), BF16)])
    return pl.pallas_call(
        _moe_down_kernel, grid_spec=grid_spec,
        out_shape=SDS((n_rows, d), F32),
        compiler_params=_cparams(1), name="moe_down")(blk_exp, blk, n_used, act, w_dn, b_dn)


def _combine_kernel(dst_ref, nxt_ref, x_ref, g_ref, ys_ref, o_ref, buf, sem, *, tc):
    i = pl.program_id(0)
    n = pl.num_programs(0)

    def row_copy(row, k, r, slot):
        return pltpu.make_async_copy(ys_ref.at[pl.ds(row, 1)], buf.at[slot, k, pl.ds(r, 1)],
                                     sem.at[slot])

    def start_block(ref, slot):
        for k in range(TOP_K):
            def body(r, c, k=k):
                row_copy(ref[0, k, r], k, r, slot).start()
                return c
            lax.fori_loop(0, tc, body, 0, unroll=8)

    @pl.when(i == 0)
    def _():
        start_block(dst_ref, 0)

    @pl.when(i + 1 < n)
    def _():
        start_block(nxt_ref, (i + 1) % 2)

    slot = i % 2
    for k in range(TOP_K):
        pltpu.make_async_copy(ys_ref.at[pl.ds(0, tc)], buf.at[slot, k], sem.at[slot]).wait()
    acc = x_ref[...]
    g = g_ref[...]
    for k in range(TOP_K):
        acc = acc + g[:, k:k + 1] * buf[slot, k]
    o_ref[...] = acc


def _combine(x2, gates_t, dest, ys):
    t, d = x2.shape
    tc = _tile(t, 128)
    n = t // tc
    dest3 = dest.reshape(TOP_K, n, tc).transpose(1, 0, 2)
    kern = functools.partial(_combine_kernel, tc=tc)
    return pl.pallas_call(
        kern,
        grid=(n,),
        in_specs=[pl.BlockSpec((1, TOP_K, tc), lambda i: (i, 0, 0), memory_space=pltpu.SMEM),
                  pl.BlockSpec((1, TOP_K, tc), lambda i: (jnp.minimum(i + 1, n - 1), 0, 0),
                               memory_space=pltpu.SMEM),
                  pl.BlockSpec((tc, d), lambda i: (i, 0)),
                  pl.BlockSpec((tc, TOP_K), lambda i: (i, 0)),
                  pl.BlockSpec(memory_space=pl.ANY)],
        out_specs=pl.BlockSpec((tc, d), lambda i: (i, 0)),
        out_shape=SDS((t, d), F32),
        scratch_shapes=[pltpu.VMEM((2, TOP_K, tc, d), F32),
                        pltpu.SemaphoreType.DMA((2,))],
        compiler_params=_cparams(1), name="moe_combine")(dest3, dest3, x2, gates_t, ys)


def _layer(x2d, mem2d, batch, seq, n_mem, p):
    t, d = x2d.shape
    c = p["conv_w"].shape[1]
    fw = p["g_fox_out"].shape[0]
    n_heads = fw // HEAD_DIM
    n_main = 3 * c + 3 * fw
    n_exp = p["w_router"].shape[1]

    w_in = p["w_mix_in"].astype(BF16)
    wf_t = jnp.zeros((F_ROWS, d), BF16).at[:n_heads].set(w_in[:, n_main:].T)
    b_col = jnp.zeros((F_ROWS, 1), F32).at[:n_heads, 0].set(p["b_forget"])

    h, ft = _prenorm(x2d, p["g_mix"][None, :], wf_t)
    cdec = _decay(ft, b_col, seq)
    proj = _matmul(h, w_in, n_main, BF16, name="mix_in")
    y_fox = _fox_attention(proj, cdec, p["g_q"][None, :], p["g_k"][None, :],
                           batch, seq, n_heads, 3 * c, fw)
    y_mix = _mixprep(proj, y_fox, p["conv_w"], p["g_conv_out"][None, :],
                     p["g_fox_out"][None, :], batch, seq)
    x1 = _matmul(y_mix, p["w_mix_out"].astype(BF16), d, F32, res=x2d, name="mix_out")

    kn, vm = _memkv(mem2d, p["g_mem"][None, :], p["w_xkv"].astype(BF16),
                    p["g_xk"][None, :], n_mem)
    x2, hm, ti, tg = _xattn(x1, p["g_xattn"][None, :], p["w_xq"].astype(BF16),
                            p["g_xq"][None, :], kn, vm, p["w_xo"].astype(BF16),
                            p["g_moe"][None, :], p["w_router"].T.astype(BF16),
                            p["b_router"][:, None], seq, n_mem)

    tm = _tile(TOP_K * t, 256)
    dest, row_tok, blk, blk_exp, n_used = _routing(ti, n_exp, tm)
    xs = _gather_rows(hm, row_tok, n_used, tm)
    act = _moe_up(xs, p["w_gate_up"], p["b_gate_up"][:, None, :], blk, blk_exp, n_used, tm)
    ys = _moe_down(act, p["w_down"], p["b_down"][:, None, :], blk, blk_exp, n_used, tm)
    return _combine(x2, tg.T, dest, ys)


def kernel(x, mem, g_mix, w_mix_in, b_forget, conv_w, g_q, g_k, g_conv_out, g_fox_out, w_mix_out, g_xattn, g_mem, w_xq, w_xkv, g_xq, g_xk, w_xo, g_moe, w_router, b_router, w_gate_up, b_gate_up, w_down, b_down):
    batch, seq, d = x.shape
    n_mem = mem.shape[1]
    params = dict(g_mix=g_mix, w_mix_in=w_mix_in, b_forget=b_forget, conv_w=conv_w, g_q=g_q,
                  g_k=g_k, g_conv_out=g_conv_out, g_fox_out=g_fox_out, w_mix_out=w_mix_out,
                  g_xattn=g_xattn, g_mem=g_mem, w_xq=w_xq, w_xkv=w_xkv, g_xq=g_xq, g_xk=g_xk,
                  w_xo=w_xo, g_moe=g_moe, w_router=w_router, b_router=b_router,
                  w_gate_up=w_gate_up, b_gate_up=b_gate_up, w_down=w_down, b_down=b_down)
    x2d = x.reshape(batch * seq, d)
    mem2d = mem.reshape(batch * n_mem, d)
    for l in range(g_mix.shape[0]):
        x2d = _layer(x2d, mem2d, batch, seq, n_mem, {k: v[l] for k, v in params.items()})
    return x2d.reshape(batch, seq, d)
```

```python
import functools

import jax
import jax.numpy as jnp
from jax import lax
from jax.experimental import pallas as pl
from jax.experimental.pallas import tpu as pltpu

F32 = jnp.float32
BF16 = jnp.bfloat16
I32 = jnp.int32
SDS = jax.ShapeDtypeStruct

EPS = 1e-6
HEAD_DIM = 128
TOP_K = 4
SWIGLU_LIMIT = 7.0
SWIGLU_ALPHA = 1.702
MASK_VALUE = float(jnp.finfo(jnp.float32).min)
V7X_VMEM_LIMIT_BYTES = 56 * 1024 * 1024
F_ROWS = 16


def _cparams(n_axes):
    return pltpu.CompilerParams(
        dimension_semantics=("arbitrary",) * n_axes,
        vmem_limit_bytes=V7X_VMEM_LIMIT_BYTES)


def _rms(x, g):
    return x * lax.rsqrt(jnp.mean(x * x, axis=-1, keepdims=True) + EPS) * g


def _dot(a, b):
    return jnp.dot(a, b, preferred_element_type=F32)


def _dot_nt(a, b):
    return lax.dot_general(a, b, (((1,), (1,)), ((), ())), preferred_element_type=F32)


def _tile(n, pref):
    return pref if n % pref == 0 else n


def _prenorm_kernel(x_ref, g_ref, wf_ref, h_ref, ft_ref):
    hb = _rms(x_ref[...], g_ref[...]).astype(BF16)
    h_ref[...] = hb
    ft_ref[...] = _dot_nt(wf_ref[...], hb)


def _prenorm(x2d, g, wf_t):
    t, d = x2d.shape
    tm = _tile(t, 512)
    return pl.pallas_call(
        _prenorm_kernel,
        grid=(t // tm,),
        in_specs=[pl.BlockSpec((tm, d), lambda i: (i, 0)),
                  pl.BlockSpec((1, d), lambda i: (0, 0)),
                  pl.BlockSpec((F_ROWS, d), lambda i: (0, 0))],
        out_specs=[pl.BlockSpec((tm, d), lambda i: (i, 0)),
                   pl.BlockSpec((F_ROWS, tm), lambda i: (0, i))],
        out_shape=[SDS((t, d), BF16), SDS((F_ROWS, t), F32)],
        compiler_params=_cparams(1), name="prenorm")(x2d, g, wf_t)


def _decay_kernel(ft_ref, b_ref, c_ref):
    z = ft_ref[...] + b_ref[...]
    lf = jnp.minimum(z, 0.0) - jnp.log(1.0 + jnp.exp(-jnp.abs(z)))
    s = lf.shape[1]
    lane = lax.broadcasted_iota(I32, lf.shape, 1)
    d = 1
    while d < s:
        lf = lf + jnp.where(lane >= d, pltpu.roll(lf, d, 1), 0.0)
        d *= 2
    c_ref[...] = lf


def _decay(ft, b_col, seq):
    rows, t = ft.shape
    return pl.pallas_call(
        _decay_kernel,
        grid=(t // seq,),
        in_specs=[pl.BlockSpec((rows, seq), lambda b: (0, b)),
                  pl.BlockSpec((rows, 1), lambda b: (0, 0))],
        out_specs=pl.BlockSpec((rows, seq), lambda b: (0, b)),
        out_shape=SDS((rows, t), F32),
        compiler_params=_cparams(1), name="decay")(ft, b_col)


def _matmul_kernel(a_ref, b_ref, o_ref):
    o_ref[...] = _dot(a_ref[...], b_ref[...]).astype(o_ref.dtype)


def _matmul_res_kernel(a_ref, b_ref, r_ref, o_ref):
    o_ref[...] = r_ref[...] + _dot(a_ref[...], b_ref[...])


def _matmul(a, b, n_out, out_dtype, res=None, name="matmul"):
    t, k = a.shape
    tm = _tile(t, 512)
    tn = _tile(n_out, 1024)
    in_specs = [pl.BlockSpec((tm, k), lambda j, i: (i, 0)),
                pl.BlockSpec((k, tn), lambda j, i: (0, j))]
    args = [a, b]
    kern = _matmul_kernel
    if res is not None:
        in_specs.append(pl.BlockSpec((tm, tn), lambda j, i: (i, j)))
        args.append(res)
        kern = _matmul_res_kernel
    return pl.pallas_call(
        kern,
        grid=(n_out // tn, t // tm),
        in_specs=in_specs,
        out_specs=pl.BlockSpec((tm, tn), lambda j, i: (i, j)),
        out_shape=SDS((t, n_out), out_dtype),
        compiler_params=_cparams(2), name=name)(*args)


def _fox_kernel(q_ref, k_ref, v_ref, c_ref, gq_ref, gk_ref, o_ref, kn_ref, nc_ref, *, tq):
    h = pl.program_id(1)
    seq = q_ref.shape[0]
    scale = HEAD_DIM ** -0.5
    kn_ref[...] = _rms(k_ref[...].astype(F32), gk_ref[...]).astype(BF16)
    nc_ref[...] = -c_ref[pl.ds(h, 1), :]
    row = lax.broadcasted_iota(I32, (tq, tq), 0)
    col = lax.broadcasted_iota(I32, (tq, tq), 1)
    causal = col <= row
    for qi in range(seq // tq):
        lo, hi = qi * tq, (qi + 1) * tq
        q = (_rms(q_ref[lo:hi, :].astype(F32), gq_ref[...]) * scale).astype(BF16)
        s_diag = jnp.where(causal, _dot_nt(q, kn_ref[lo:hi, :]) + nc_ref[:, lo:hi], MASK_VALUE)
        m = jnp.max(s_diag, axis=-1, keepdims=True)
        if qi > 0:
            s_past = _dot_nt(q, kn_ref[:lo, :]) + nc_ref[:, :lo]
            m = jnp.maximum(m, jnp.max(s_past, axis=-1, keepdims=True))
        p_diag = jnp.exp(s_diag - m)
        l = jnp.sum(p_diag, axis=-1, keepdims=True)
        acc = _dot(p_diag.astype(BF16), v_ref[lo:hi, :])
        if qi > 0:
            p_past = jnp.exp(s_past - m)
            l = l + jnp.sum(p_past, axis=-1, keepdims=True)
            acc = acc + _dot(p_past.astype(BF16), v_ref[:lo, :])
        o_ref[lo:hi, :] = (acc / l).astype(o_ref.dtype)


def _fox_attention(proj, c, g_q, g_k, batch, seq, n_heads, col0, width):
    t = proj.shape[0]
    tq = _tile(seq, 256)
    cb = col0 // HEAD_DIM
    wb = width // HEAD_DIM
    hd = HEAD_DIM
    kern = functools.partial(_fox_kernel, tq=tq)
    return pl.pallas_call(
        kern,
        grid=(batch, n_heads),
        in_specs=[pl.BlockSpec((seq, hd), lambda b, h: (b, cb + h)),
                  pl.BlockSpec((seq, hd), lambda b, h: (b, cb + wb + h)),
                  pl.BlockSpec((seq, hd), lambda b, h: (b, cb + 2 * wb + h)),
                  pl.BlockSpec((F_ROWS, seq), lambda b, h: (0, b)),
                  pl.BlockSpec((1, hd), lambda b, h: (0, 0)),
                  pl.BlockSpec((1, hd), lambda b, h: (0, 0))],
        out_specs=pl.BlockSpec((seq, hd), lambda b, h: (b, h)),
        out_shape=SDS((t, width), BF16),
        scratch_shapes=[pltpu.VMEM((seq, hd), BF16), pltpu.VMEM((1, seq), F32)],
        compiler_params=_cparams(2), name="fox_attention")(proj, proj, proj, c, g_q, g_k)


def _mixprep_kernel(ub_ref, uc_ref, ux_ref, hc_ref, hx_ref, yf_ref, cw_ref, gc_ref, gf_ref,
                    o_ref):
    si = pl.program_id(1)
    p = uc_ref[...].astype(F32) * ux_ref[...].astype(F32)
    halo = hc_ref[...].astype(F32) * hx_ref[...].astype(F32)
    halo = jnp.where(si > 0, halo, 0.0)
    row = lax.broadcasted_iota(I32, p.shape, 0)
    p1 = jnp.where(row == 0, halo[7:8, :], pltpu.roll(p, 1, 0))
    p2 = jnp.where(row == 0, halo[6:7, :],
                   jnp.where(row == 1, halo[7:8, :], pltpu.roll(p, 2, 0)))
    w = cw_ref[...]
    y = ub_ref[...].astype(F32) * (w[0:1, :] * p2 + w[1:2, :] * p1 + w[2:3, :] * p)
    c = y.shape[1]
    o_ref[:, :c] = _rms(y, gc_ref[...]).astype(o_ref.dtype)
    o_ref[:, c:] = _rms(yf_ref[...].astype(F32), gf_ref[...]).astype(o_ref.dtype)


def _mixprep(proj, y_fox, conv_w, g_conv, g_fox, batch, seq):
    t = proj.shape[0]
    c = conv_w.shape[1]
    w = y_fox.shape[1]
    ts = _tile(seq, 512)
    ns = seq // ts
    hb = ts // 8

    def halo(col):
        return lambda b, s: (jnp.maximum((b * ns + s) * hb - 1, 0), col)

    return pl.pallas_call(
        _mixprep_kernel,
        grid=(batch, ns),
        in_specs=[pl.BlockSpec((ts, c), lambda b, s: (b * ns + s, 0)),
                  pl.BlockSpec((ts, c), lambda b, s: (b * ns + s, 1)),
                  pl.BlockSpec((ts, c), lambda b, s: (b * ns + s, 2)),
                  pl.BlockSpec((8, c), halo(1)),
                  pl.BlockSpec((8, c), halo(2)),
                  pl.BlockSpec((ts, w), lambda b, s: (b * ns + s, 0)),
                  pl.BlockSpec((3, c), lambda b, s: (0, 0)),
                  pl.BlockSpec((1, c), lambda b, s: (0, 0)),
                  pl.BlockSpec((1, w), lambda b, s: (0, 0))],
        out_specs=pl.BlockSpec((ts, c + w), lambda b, s: (b * ns + s, 0)),
        out_shape=SDS((t, c + w), BF16),
        compiler_params=_cparams(2), name="mixprep")(
            proj, proj, proj, proj, proj, y_fox, conv_w, g_conv, g_fox)


def _memkv_kernel(mem_ref, g_ref, w_ref, gk_ref, k_ref, v_ref):
    hm = _rms(mem_ref[...], g_ref[...]).astype(BF16)
    kv = _dot(hm, w_ref[...])
    xw = k_ref.shape[1]
    for h in range(xw // HEAD_DIM):
        sl = slice(h * HEAD_DIM, (h + 1) * HEAD_DIM)
        k_ref[:, sl] = _rms(kv[:, sl], gk_ref[...]).astype(BF16)
    v_ref[...] = kv[:, xw:].astype(BF16)


def _memkv(mem2d, g_mem, w_xkv, g_xk, n_mem):
    tm_, d = mem2d.shape
    xw = w_xkv.shape[1] // 2
    return pl.pallas_call(
        _memkv_kernel,
        grid=(tm_ // n_mem,),
        in_specs=[pl.BlockSpec((n_mem, d), lambda b: (b, 0)),
                  pl.BlockSpec((1, d), lambda b: (0, 0)),
                  pl.BlockSpec((d, 2 * xw), lambda b: (0, 0)),
                  pl.BlockSpec((1, HEAD_DIM), lambda b: (0, 0))],
        out_specs=[pl.BlockSpec((n_mem, xw), lambda b: (b, 0)),
                   pl.BlockSpec((n_mem, xw), lambda b: (b, 0))],
        out_shape=[SDS((tm_, xw), BF16), SDS((tm_, xw), BF16)],
        compiler_params=_cparams(1), name="memkv")(mem2d, g_mem, w_xkv, g_xk)


def _xattn_kernel(x_ref, gx_ref, wq_ref, gq_ref, kn_ref, v_ref, wo_ref, gm_ref, wr_ref, br_ref,
                  x2_ref, hm_ref, ti_ref, tg_ref, pos_ref, cnt_ref, run_ref, tri_ref):
    i = pl.program_id(0)
    tm = x_ref.shape[0]

    @pl.when(i == 0)
    def _():
        run_ref[...] = jnp.zeros(run_ref.shape, F32)
        r = lax.broadcasted_iota(I32, (tm, tm), 0)
        c = lax.broadcasted_iota(I32, (tm, tm), 1)
        tri_ref[...] = jnp.where(r <= c, 1.0, 0.0).astype(BF16)

    x = x_ref[...]
    hb = _rms(x, gx_ref[...]).astype(BF16)
    q = _dot(hb, wq_ref[...])
    scale = HEAD_DIM ** -0.5
    outs = []
    for h in range(q.shape[1] // HEAD_DIM):
        sl = slice(h * HEAD_DIM, (h + 1) * HEAD_DIM)
        qh = (_rms(q[:, sl], gq_ref[...]) * scale).astype(BF16)
        s = _dot_nt(qh, kn_ref[:, sl])
        p = jnp.exp(s - jnp.max(s, axis=-1, keepdims=True))
        oh = _dot(p.astype(BF16), v_ref[:, sl]) / jnp.sum(p, axis=-1, keepdims=True)
        outs.append(oh.astype(BF16))
    x2 = x + _dot(jnp.concatenate(outs, axis=-1), wo_ref[...])
    x2_ref[...] = x2
    hm = _rms(x2, gm_ref[...])
    hm_ref[...] = hm
    lg = _dot_nt(wr_ref[...], hm.astype(BF16)) + br_ref[...]
    n_exp = lg.shape[0]
    eidx = lax.broadcasted_iota(I32, lg.shape, 0).astype(F32)
    vals, sel = [], []
    for r in range(TOP_K):
        mx = jnp.max(lg, axis=0, keepdims=True)
        am = jnp.min(jnp.where(lg == mx, eidx, float(n_exp)), axis=0, keepdims=True)
        ti_ref[r:r + 1, :] = am.astype(I32)
        vals.append(mx)
        sel.append(eidx == am)
        lg = jnp.where(sel[r], -jnp.inf, lg)
    ex = [jnp.exp(v - vals[0]) for v in vals]
    den = ex[0]
    for e in ex[1:]:
        den = den + e
    for r in range(TOP_K):
        tg_ref[r:r + 1, :] = ex[r] / den
    onehot = jnp.zeros(lg.shape, F32)
    for r in range(TOP_K):
        onehot = onehot + jnp.where(sel[r], 1.0, 0.0)
    incl = _dot(onehot.astype(BF16), tri_ref[...])
    excl = incl - onehot + run_ref[...]
    for r in range(TOP_K):
        pos_ref[r:r + 1, :] = jnp.sum(jnp.where(sel[r], excl, 0.0), axis=0,
                                      keepdims=True).astype(I32)
    run = run_ref[...] + incl[:, tm - 1:tm]
    run_ref[...] = run
    cnt_ref[...] = jnp.broadcast_to(run, cnt_ref.shape)


def _xattn(x1, g_xattn, w_xq, g_xq, kn, vm, w_xo, g_moe, w_rt, b_r, seq, n_mem):
    t, d = x1.shape
    xw = w_xq.shape[1]
    n_exp = w_rt.shape[0]
    tm = _tile(seq, 512)
    per = seq // tm
    const = lambda i: (0, 0)
    return pl.pallas_call(
        _xattn_kernel,
        grid=(t // tm,),
        in_specs=[pl.BlockSpec((tm, d), lambda i: (i, 0)),
                  pl.BlockSpec((1, d), const),
                  pl.BlockSpec((d, xw), const),
                  pl.BlockSpec((1, HEAD_DIM), const),
                  pl.BlockSpec((n_mem, xw), lambda i: (i // per, 0)),
                  pl.BlockSpec((n_mem, xw), lambda i: (i // per, 0)),
                  pl.BlockSpec((xw, d), const),
                  pl.BlockSpec((1, d), const),
                  pl.BlockSpec((n_exp, d), const),
                  pl.BlockSpec((n_exp, 1), const)],
        out_specs=[pl.BlockSpec((tm, d), lambda i: (i, 0)),
                   pl.BlockSpec((tm, d), lambda i: (i, 0)),
                   pl.BlockSpec((TOP_K, tm), lambda i: (0, i)),
                   pl.BlockSpec((TOP_K, tm), lambda i: (0, i)),
                   pl.BlockSpec((TOP_K, tm), lambda i: (0, i)),
                   pl.BlockSpec((n_exp, 128), const)],
        out_shape=[SDS((t, d), F32), SDS((t, d), F32),
                   SDS((TOP_K, t), I32), SDS((TOP_K, t), F32), SDS((TOP_K, t), I32),
                   SDS((n_exp, 128), F32)],
        scratch_shapes=[pltpu.VMEM((n_exp, 1), F32), pltpu.VMEM((tm, tm), BF16)],
        compiler_params=_cparams(1), name="xattn_router")(
            x1, g_xattn, w_xq, g_xq, kn, vm, w_xo, g_moe, w_rt, b_r)


def _route_tables(counts, n_assign, tm):
    n_exp = counts.shape[0]
    padded = (counts + tm - 1) // tm * tm
    pad_end = jnp.cumsum(padded).astype(I32)
    pad_start = pad_end - padded
    n_blk = n_assign // tm + n_exp
    n_used = pad_end[-1] // tm
    blk = jnp.minimum(jnp.arange(n_blk, dtype=I32), n_used - 1)
    blk_exp = jnp.minimum(jnp.searchsorted(pad_end, blk * tm, side="right"), n_exp - 1).astype(I32)
    return pad_start, pad_end, padded.astype(I32), blk, blk_exp, n_used.reshape(1).astype(I32)


def _dest_kernel(ps_ref, ti_ref, pos_ref, o_ref, *, n_exp):
    ti = ti_ref[...]
    acc = pos_ref[...]
    for e in range(n_exp):
        acc = acc + jnp.where(ti == e, ps_ref[e], 0)
    o_ref[...] = acc


def _dest_rows(pad_start, ti, pos):
    k, t = ti.shape
    kern = functools.partial(_dest_kernel, n_exp=pad_start.shape[0])
    grid_spec = pltpu.PrefetchScalarGridSpec(
        num_scalar_prefetch=1, grid=(1,),
        in_specs=[pl.BlockSpec((k, t), lambda i, ps: (0, 0)),
                  pl.BlockSpec((k, t), lambda i, ps: (0, 0))],
        out_specs=pl.BlockSpec((k, t), lambda i, ps: (0, 0)))
    return pl.pallas_call(kern, grid_spec=grid_spec, out_shape=SDS((k, t), I32),
                          compiler_params=_cparams(1), name="moe_dest")(pad_start, ti, pos)


def _dispatch_kernel(pe_ref, pc_ref, nu_ref, dst_ref, x_ref, xs_ref, buf, zbuf, sem, zsem,
                     *, tc, tm, n_exp, n_blk):
    i = pl.program_id(0)
    n = pl.num_programs(0)

    def zero_copy(row0):
        return pltpu.make_async_copy(zbuf, xs_ref.at[pl.ds(pl.multiple_of(row0, tm), tm)], zsem)

    @pl.when(i == 0)
    def _():
        zbuf[...] = jnp.zeros(zbuf.shape, zbuf.dtype)
        for start in (True, False):
            for e in range(n_exp):
                @pl.when(pc_ref[e] > 0)
                def _(e=e):
                    cp = zero_copy(pe_ref[e] - tm)
                    cp.start() if start else cp.wait()

            def tail(b, c):
                cp = zero_copy(b * tm)
                cp.start() if start else cp.wait()
                return c
            lax.fori_loop(nu_ref[0], n_blk, tail, 0)

    slot = i % 2
    buf[slot] = x_ref[...]
    for k in range(TOP_K):
        for r in range(tc):
            pltpu.make_async_copy(buf.at[slot, pl.ds(r, 1)],
                                  xs_ref.at[pl.ds(dst_ref[0, k, r], 1)], sem.at[slot]).start()

    def wait_slot(s):
        for k in range(TOP_K):
            pltpu.make_async_copy(buf.at[s], xs_ref.at[pl.ds(0, tc)], sem.at[s]).wait()

    @pl.when(i >= 1)
    def _():
        wait_slot(1 - slot)

    @pl.when(i == n - 1)
    def _():
        wait_slot(slot)


def _dispatch(hm, dest, pad_end, padded, n_used, tm, n_blk):
    t, d = hm.shape
    tc = _tile(t, 128)
    n = t // tc
    n_exp = pad_end.shape[0]
    dest3 = dest.reshape(TOP_K, n, tc).transpose(1, 0, 2)
    kern = functools.partial(_dispatch_kernel, tc=tc, tm=tm, n_exp=n_exp, n_blk=n_blk)
    grid_spec = pltpu.PrefetchScalarGridSpec(
        num_scalar_prefetch=3, grid=(n,),
        in_specs=[pl.BlockSpec((1, TOP_K, tc), lambda i, *_: (i, 0, 0), memory_space=pltpu.SMEM),
                  pl.BlockSpec((tc, d), lambda i, *_: (i, 0))],
        out_specs=pl.BlockSpec(memory_space=pl.ANY),
        scratch_shapes=[pltpu.VMEM((2, tc, d), hm.dtype),
                        pltpu.VMEM((tm, d), hm.dtype),
                        pltpu.SemaphoreType.DMA((2,)),
                        pltpu.SemaphoreType.DMA(())])
    return pl.pallas_call(
        kern, grid_spec=grid_spec,
        out_shape=SDS((n_blk * tm, d), hm.dtype),
        compiler_params=_cparams(1), name="moe_dispatch")(pad_end, padded, n_used, dest3, hm)


def _moe_up_kernel(be_ref, bi_ref, nu_ref, x_ref, wg_ref, wu_ref, bg_ref, bu_ref, o_ref,
                   wgb_ref, wub_ref):
    i = pl.program_id(1)
    e = be_ref[i]
    e_prev = be_ref[jnp.maximum(i - 1, 0)]

    @pl.when((i == 0) | (e != e_prev))
    def _():
        wgb_ref[...] = wg_ref[...].astype(BF16)
        wub_ref[...] = wu_ref[...].astype(BF16)

    @pl.when(i < nu_ref[0])
    def _():
        x = x_ref[...].astype(BF16)
        gate = jnp.minimum(_dot(x, wgb_ref[...]) + bg_ref[...], SWIGLU_LIMIT)
        up = jnp.clip(_dot(x, wub_ref[...]) + bu_ref[...], -SWIGLU_LIMIT, SWIGLU_LIMIT)
        glu = gate * jax.nn.sigmoid(SWIGLU_ALPHA * gate)
        o_ref[...] = ((up + 1.0) * glu).astype(o_ref.dtype)

    @pl.when(i >= nu_ref[0])
    def _():
        o_ref[...] = jnp.zeros(o_ref.shape, o_ref.dtype)


def _moe_up(xs, w_gu, b_gu, blk, blk_exp, n_used, tm):
    n_rows, d = xs.shape
    f = w_gu.shape[2] // 2
    tn = _tile(f, 1024)
    nj = f // tn
    grid_spec = pltpu.PrefetchScalarGridSpec(
        num_scalar_prefetch=3,
        grid=(nj, n_rows // tm),
        in_specs=[pl.BlockSpec((tm, d), lambda j, i, be, bi, nu: (bi[i], 0)),
                  pl.BlockSpec((None, d, tn), lambda j, i, be, bi, nu: (be[i], 0, j)),
                  pl.BlockSpec((None, d, tn), lambda j, i, be, bi, nu: (be[i], 0, j + nj)),
                  pl.BlockSpec((None, 1, tn), lambda j, i, be, bi, nu: (be[i], 0, j)),
                  pl.BlockSpec((None, 1, tn), lambda j, i, be, bi, nu: (be[i], 0, j + nj))],
        out_specs=pl.BlockSpec((tm, tn), lambda j, i, be, bi, nu: (i, j)),
        scratch_shapes=[pltpu.VMEM((d, tn), BF16), pltpu.VMEM((d, tn), BF16)])
    return pl.pallas_call(
        _moe_up_kernel, grid_spec=grid_spec,
        out_shape=SDS((n_rows, f), BF16),
        compiler_params=_cparams(2), name="moe_up")(
            blk_exp, blk, n_used, xs, w_gu, w_gu, b_gu, b_gu)


def _moe_down_kernel(be_ref, bi_ref, nu_ref, a_ref, w_ref, b_ref, o_ref, wb_ref):
    i = pl.program_id(0)
    e = be_ref[i]
    e_prev = be_ref[jnp.maximum(i - 1, 0)]

    @pl.when((i == 0) | (e != e_prev))
    def _():
        wb_ref[...] = w_ref[...].astype(BF16)

    @pl.when(i < nu_ref[0])
    def _():
        o_ref[...] = _dot(a_ref[...], wb_ref[...]) + b_ref[...]

    @pl.when(i >= nu_ref[0])
    def _():
        o_ref[...] = jnp.zeros(o_ref.shape, o_ref.dtype)


def _moe_down(act, w_dn, b_dn, blk, blk_exp, n_used, tm):
    n_rows, f = act.shape
    d = w_dn.shape[2]
    grid_spec = pltpu.PrefetchScalarGridSpec(
        num_scalar_prefetch=3,
        grid=(n_rows // tm,),
        in_specs=[pl.BlockSpec((tm, f), lambda i, be, bi, nu: (bi[i], 0)),
                  pl.BlockSpec((None, f, d), lambda i, be, bi, nu: (be[i], 0, 0)),
                  pl.BlockSpec((None, 1, d), lambda i, be, bi, nu: (be[i], 0, 0))],
        out_specs=pl.BlockSpec((tm, d), lambda i, be, bi, nu: (i, 0)),
        scratch_shapes=[pltpu.VMEM((f, d), BF16)])
    return pl.pallas_call(
        _moe_down_kernel, grid_spec=grid_spec,
        out_shape=SDS((n_rows, d), F32),
        compiler_params=_cparams(1), name="moe_down")(blk_exp, blk, n_used, act, w_dn, b_dn)


def _combine_kernel(dst_ref, x_ref, g_ref, ys_ref, o_ref, buf, sem, *, tc):
    i = pl.program_id(0)
    n = pl.num_programs(0) - 1

    @pl.when(i < n)
    def _():
        slot = i % 2
        for k in range(TOP_K):
            for r in range(tc):
                pltpu.make_async_copy(ys_ref.at[pl.ds(dst_ref[0, k, r], 1)],
                                      buf.at[slot, k, pl.ds(r, 1)], sem.at[slot]).start()

    @pl.when(i >= 1)
    def _():
        slot = (i - 1) % 2
        for k in range(TOP_K):
            pltpu.make_async_copy(ys_ref.at[pl.ds(0, tc)], buf.at[slot, k], sem.at[slot]).wait()
        acc = x_ref[...]
        g = g_ref[...]
        for k in range(TOP_K):
            acc = acc + g[:, k:k + 1] * buf[slot, k]
        o_ref[...] = acc


def _combine(x2, gates_t, dest, ys):
    t, d = x2.shape
    tc = _tile(t, 128)
    n = t // tc
    dest3 = dest.reshape(TOP_K, n, tc).transpose(1, 0, 2)
    kern = functools.partial(_combine_kernel, tc=tc)
    prev = lambda i: (jnp.maximum(i - 1, 0), 0)
    return pl.pallas_call(
        kern,
        grid=(n + 1,),
        in_specs=[pl.BlockSpec((1, TOP_K, tc), lambda i: (jnp.minimum(i, n - 1), 0, 0),
                               memory_space=pltpu.SMEM),
                  pl.BlockSpec((tc, d), prev),
                  pl.BlockSpec((tc, TOP_K), prev),
                  pl.BlockSpec(memory_space=pl.ANY)],
        out_specs=pl.BlockSpec((tc, d), prev),
        out_shape=SDS((t, d), F32),
        scratch_shapes=[pltpu.VMEM((2, TOP_K, tc, d), F32),
                        pltpu.SemaphoreType.DMA((2,))],
        compiler_params=_cparams(1), name="moe_combine")(dest3, x2, gates_t, ys)


def _layer(x2d, mem2d, batch, seq, n_mem, p):
    t, d = x2d.shape
    c = p["conv_w"].shape[1]
    fw = p["g_fox_out"].shape[0]
    n_heads = fw // HEAD_DIM
    n_main = 3 * c + 3 * fw
    n_exp = p["w_router"].shape[1]

    w_in = p["w_mix_in"].astype(BF16)
    wf_t = jnp.zeros((F_ROWS, d), BF16).at[:n_heads].set(w_in[:, n_main:].T)
    b_col = jnp.zeros((F_ROWS, 1), F32).at[:n_heads, 0].set(p["b_forget"])

    h, ft = _prenorm(x2d, p["g_mix"][None, :], wf_t)
    cdec = _decay(ft, b_col, seq)
    proj = _matmul(h, w_in, n_main, BF16, name="mix_in")
    y_fox = _fox_attention(proj, cdec, p["g_q"][None, :], p["g_k"][None, :],
                           batch, seq, n_heads, 3 * c, fw)
    y_mix = _mixprep(proj, y_fox, p["conv_w"], p["g_conv_out"][None, :],
                     p["g_fox_out"][None, :], batch, seq)
    x1 = _matmul(y_mix, p["w_mix_out"].astype(BF16), d, F32, res=x2d, name="mix_out")

    kn, vm = _memkv(mem2d, p["g_mem"][None, :], p["w_xkv"].astype(BF16),
                    p["g_xk"][None, :], n_mem)
    x2, hm, ti, tg, pos, cnt = _xattn(x1, p["g_xattn"][None, :], p["w_xq"].astype(BF16),
                            p["g_xq"][None, :], kn, vm, p["w_xo"].astype(BF16),
                            p["g_moe"][None, :], p["w_router"].T.astype(BF16),
                            p["b_router"][:, None], seq, n_mem)

    tm = _tile(TOP_K * t, 256)
    pad_start, pad_end, padded, blk, blk_exp, n_used = _route_tables(
        cnt[:, 0].astype(I32), TOP_K * t, tm)
    dest = _dest_rows(pad_start, ti, pos)
    xs = _dispatch(hm, dest, pad_end, padded, n_used, tm, blk.shape[0])
    act = _moe_up(xs, p["w_gate_up"], p["b_gate_up"][:, None, :], blk, blk_exp, n_used, tm)
    ys = _moe_down(act, p["w_down"], p["b_down"][:, None, :], blk, blk_exp, n_used, tm)
    return _combine(x2, tg.T, dest, ys)


def kernel(x, mem, g_mix, w_mix_in, b_forget, conv_w, g_q, g_k, g_conv_out, g_fox_out, w_mix_out, g_xattn, g_mem, w_xq, w_xkv, g_xq, g_xk, w_xo, g_moe, w_router, b_router, w_gate_up, b_gate_up, w_down, b_down):
    batch, seq, d = x.shape
    n_mem = mem.shape[1]
    params = dict(g_mix=g_mix, w_mix_in=w_mix_in, b_forget=b_forget, conv_w=conv_w, g_q=g_q,
                  g_k=g_k, g_conv_out=g_conv_out, g_fox_out=g_fox_out, w_mix_out=w_mix_out,
                  g_xattn=g_xattn, g_mem=g_mem, w_xq=w_xq, w_xkv=w_xkv, g_xq=g_xq, g_xk=g_xk,
                  w_xo=w_xo, g_moe=g_moe, w_router=w_router, b_router=b_router,
                  w_gate_up=w_gate_up, b_gate_up=b_gate_up, w_down=w_down, b_down=b_down)
    x2d = x.reshape(batch * seq, d)
    mem2d = mem.reshape(batch * n_mem, d)
    for l in range(g_mix.shape[0]):
        x2d = _layer(x2d, mem2d, batch, seq, n_mem, {k: v[l] for k, v in params.items()})
    return x2d.reshape(batch, seq, d)
```

```python
import functools

import jax
import jax.numpy as jnp
from jax import lax
from jax.experimental import pallas as pl
from jax.experimental.pallas import tpu as pltpu

F32 = jnp.float32
BF16 = jnp.bfloat16
I32 = jnp.int32
SDS = jax.ShapeDtypeStruct

EPS = 1e-6
HEAD_DIM = 128
TOP_K = 4
SWIGLU_LIMIT = 7.0
SWIGLU_ALPHA = 1.702
MASK_VALUE = float(jnp.finfo(jnp.float32).min)
V7X_VMEM_LIMIT_BYTES = 56 * 1024 * 1024
F_ROWS = 16


def _cparams(n_axes):
    return pltpu.CompilerParams(
        dimension_semantics=("arbitrary",) * n_axes,
        vmem_limit_bytes=V7X_VMEM_LIMIT_BYTES)


def _rms(x, g):
    return x * lax.rsqrt(jnp.mean(x * x, axis=-1, keepdims=True) + EPS) * g


def _dot(a, b):
    return jnp.dot(a, b, preferred_element_type=F32)


def _dot_nt(a, b):
    return lax.dot_general(a, b, (((1,), (1,)), ((), ())), preferred_element_type=F32)


def _tile(n, pref):
    return pref if n % pref == 0 else n


def _prenorm_kernel(x_ref, g_ref, wf_ref, h_ref, ft_ref):
    hb = _rms(x_ref[...], g_ref[...]).astype(BF16)
    h_ref[...] = hb
    ft_ref[...] = _dot_nt(wf_ref[...], hb)


def _prenorm(x2d, g, wf_t):
    t, d = x2d.shape
    tm = _tile(t, 512)
    return pl.pallas_call(
        _prenorm_kernel,
        grid=(t // tm,),
        in_specs=[pl.BlockSpec((tm, d), lambda i: (i, 0)),
                  pl.BlockSpec((1, d), lambda i: (0, 0)),
                  pl.BlockSpec((F_ROWS, d), lambda i: (0, 0))],
        out_specs=[pl.BlockSpec((tm, d), lambda i: (i, 0)),
                   pl.BlockSpec((F_ROWS, tm), lambda i: (0, i))],
        out_shape=[SDS((t, d), BF16), SDS((F_ROWS, t), F32)],
        compiler_params=_cparams(1), name="prenorm")(x2d, g, wf_t)


def _decay_kernel(ft_ref, b_ref, c_ref):
    z = ft_ref[...] + b_ref[...]
    lf = jnp.minimum(z, 0.0) - jnp.log(1.0 + jnp.exp(-jnp.abs(z)))
    s = lf.shape[1]
    lane = lax.broadcasted_iota(I32, lf.shape, 1)
    d = 1
    while d < s:
        lf = lf + jnp.where(lane >= d, pltpu.roll(lf, d, 1), 0.0)
        d *= 2
    c_ref[...] = lf


def _decay(ft, b_col, seq):
    rows, t = ft.shape
    return pl.pallas_call(
        _decay_kernel,
        grid=(t // seq,),
        in_specs=[pl.BlockSpec((rows, seq), lambda b: (0, b)),
                  pl.BlockSpec((rows, 1), lambda b: (0, 0))],
        out_specs=pl.BlockSpec((rows, seq), lambda b: (0, b)),
        out_shape=SDS((rows, t), F32),
        compiler_params=_cparams(1), name="decay")(ft, b_col)


def _matmul_kernel(a_ref, b_ref, o_ref):
    o_ref[...] = _dot(a_ref[...], b_ref[...]).astype(o_ref.dtype)


def _matmul_res_kernel(a_ref, b_ref, r_ref, o_ref):
    o_ref[...] = r_ref[...] + _dot(a_ref[...], b_ref[...])


def _matmul(a, b, n_out, out_dtype, res=None, name="matmul"):
    t, k = a.shape
    tm = _tile(t, 512)
    tn = _tile(n_out, 1024)
    in_specs = [pl.BlockSpec((tm, k), lambda j, i: (i, 0)),
                pl.BlockSpec((k, tn), lambda j, i: (0, j))]
    args = [a, b]
    kern = _matmul_kernel
    if res is not None:
        in_specs.append(pl.BlockSpec((tm, tn), lambda j, i: (i, j)))
        args.append(res)
        kern = _matmul_res_kernel
    return pl.pallas_call(
        kern,
        grid=(n_out // tn, t // tm),
        in_specs=in_specs,
        out_specs=pl.BlockSpec((tm, tn), lambda j, i: (i, j)),
        out_shape=SDS((t, n_out), out_dtype),
        compiler_params=_cparams(2), name=name)(*args)


def _fox_kernel(q_ref, k_ref, v_ref, c_ref, gq_ref, gk_ref, o_ref, kn_ref, nc_ref, *, tq):
    h = pl.program_id(1)
    seq = q_ref.shape[0]
    scale = HEAD_DIM ** -0.5
    kn_ref[...] = _rms(k_ref[...].astype(F32), gk_ref[...]).astype(BF16)
    nc_ref[...] = -c_ref[pl.ds(h, 1), :]
    row = lax.broadcasted_iota(I32, (tq, tq), 0)
    col = lax.broadcasted_iota(I32, (tq, tq), 1)
    causal = col <= row
    for qi in range(seq // tq):
        lo, hi = qi * tq, (qi + 1) * tq
        q = (_rms(q_ref[lo:hi, :].astype(F32), gq_ref[...]) * scale).astype(BF16)
        s_diag = jnp.where(causal, _dot_nt(q, kn_ref[lo:hi, :]) + nc_ref[:, lo:hi], MASK_VALUE)
        m = jnp.max(s_diag, axis=-1, keepdims=True)
        if qi > 0:
            s_past = _dot_nt(q, kn_ref[:lo, :]) + nc_ref[:, :lo]
            m = jnp.maximum(m, jnp.max(s_past, axis=-1, keepdims=True))
        p_diag = jnp.exp(s_diag - m)
        l = jnp.sum(p_diag, axis=-1, keepdims=True)
        acc = _dot(p_diag.astype(BF16), v_ref[lo:hi, :])
        if qi > 0:
            p_past = jnp.exp(s_past - m)
            l = l + jnp.sum(p_past, axis=-1, keepdims=True)
            acc = acc + _dot(p_past.astype(BF16), v_ref[:lo, :])
        o_ref[lo:hi, :] = (acc / l).astype(o_ref.dtype)


def _fox_attention(proj, c, g_q, g_k, batch, seq, n_heads, col0, width):
    t = proj.shape[0]
    tq = _tile(seq, 256)
    cb = col0 // HEAD_DIM
    wb = width // HEAD_DIM
    hd = HEAD_DIM
    kern = functools.partial(_fox_kernel, tq=tq)
    return pl.pallas_call(
        kern,
        grid=(batch, n_heads),
        in_specs=[pl.BlockSpec((seq, hd), lambda b, h: (b, cb + h)),
                  pl.BlockSpec((seq, hd), lambda b, h: (b, cb + wb + h)),
                  pl.BlockSpec((seq, hd), lambda b, h: (b, cb + 2 * wb + h)),
                  pl.BlockSpec((F_ROWS, seq), lambda b, h: (0, b)),
                  pl.BlockSpec((1, hd), lambda b, h: (0, 0)),
                  pl.BlockSpec((1, hd), lambda b, h: (0, 0))],
        out_specs=pl.BlockSpec((seq, hd), lambda b, h: (b, h)),
        out_shape=SDS((t, width), BF16),
        scratch_shapes=[pltpu.VMEM((seq, hd), BF16), pltpu.VMEM((1, seq), F32)],
        compiler_params=_cparams(2), name="fox_attention")(proj, proj, proj, c, g_q, g_k)


def _mixprep_kernel(ub_ref, uc_ref, ux_ref, hc_ref, hx_ref, yf_ref, cw_ref, gc_ref, gf_ref,
                    o_ref):
    si = pl.program_id(1)
    p = uc_ref[...].astype(F32) * ux_ref[...].astype(F32)
    halo = hc_ref[...].astype(F32) * hx_ref[...].astype(F32)
    halo = jnp.where(si > 0, halo, 0.0)
    row = lax.broadcasted_iota(I32, p.shape, 0)
    p1 = jnp.where(row == 0, halo[7:8, :], pltpu.roll(p, 1, 0))
    p2 = jnp.where(row == 0, halo[6:7, :],
                   jnp.where(row == 1, halo[7:8, :], pltpu.roll(p, 2, 0)))
    w = cw_ref[...]
    y = ub_ref[...].astype(F32) * (w[0:1, :] * p2 + w[1:2, :] * p1 + w[2:3, :] * p)
    c = y.shape[1]
    o_ref[:, :c] = _rms(y, gc_ref[...]).astype(o_ref.dtype)
    o_ref[:, c:] = _rms(yf_ref[...].astype(F32), gf_ref[...]).astype(o_ref.dtype)


def _mixprep(proj, y_fox, conv_w, g_conv, g_fox, batch, seq):
    t = proj.shape[0]
    c = conv_w.shape[1]
    w = y_fox.shape[1]
    ts = _tile(seq, 512)
    ns = seq // ts
    hb = ts // 8

    def halo(col):
        return lambda b, s: (jnp.maximum((b * ns + s) * hb - 1, 0), col)

    return pl.pallas_call(
        _mixprep_kernel,
        grid=(batch, ns),
        in_specs=[pl.BlockSpec((ts, c), lambda b, s: (b * ns + s, 0)),
                  pl.BlockSpec((ts, c), lambda b, s: (b * ns + s, 1)),
                  pl.BlockSpec((ts, c), lambda b, s: (b * ns + s, 2)),
                  pl.BlockSpec((8, c), halo(1)),
                  pl.BlockSpec((8, c), halo(2)),
                  pl.BlockSpec((ts, w), lambda b, s: (b * ns + s, 0)),
                  pl.BlockSpec((3, c), lambda b, s: (0, 0)),
                  pl.BlockSpec((1, c), lambda b, s: (0, 0)),
                  pl.BlockSpec((1, w), lambda b, s: (0, 0))],
        out_specs=pl.BlockSpec((ts, c + w), lambda b, s: (b * ns + s, 0)),
        out_shape=SDS((t, c + w), BF16),
        compiler_params=_cparams(2), name="mixprep")(
            proj, proj, proj, proj, proj, y_fox, conv_w, g_conv, g_fox)


def _memkv_kernel(mem_ref, g_ref, w_ref, gk_ref, k_ref, v_ref):
    hm = _rms(mem_ref[...], g_ref[...]).astype(BF16)
    kv = _dot(hm, w_ref[...])
    xw = k_ref.shape[1]
    for h in range(xw // HEAD_DIM):
        sl = slice(h * HEAD_DIM, (h + 1) * HEAD_DIM)
        k_ref[:, sl] = _rms(kv[:, sl], gk_ref[...]).astype(BF16)
    v_ref[...] = kv[:, xw:].astype(BF16)


def _memkv(mem2d, g_mem, w_xkv, g_xk, n_mem):
    tm_, d = mem2d.shape
    xw = w_xkv.shape[1] // 2
    return pl.pallas_call(
        _memkv_kernel,
        grid=(tm_ // n_mem,),
        in_specs=[pl.BlockSpec((n_mem, d), lambda b: (b, 0)),
                  pl.BlockSpec((1, d), lambda b: (0, 0)),
                  pl.BlockSpec((d, 2 * xw), lambda b: (0, 0)),
                  pl.BlockSpec((1, HEAD_DIM), lambda b: (0, 0))],
        out_specs=[pl.BlockSpec((n_mem, xw), lambda b: (b, 0)),
                   pl.BlockSpec((n_mem, xw), lambda b: (b, 0))],
        out_shape=[SDS((tm_, xw), BF16), SDS((tm_, xw), BF16)],
        compiler_params=_cparams(1), name="memkv")(mem2d, g_mem, w_xkv, g_xk)


def _xattn_kernel(x_ref, gx_ref, wq_ref, gq_ref, kn_ref, v_ref, wo_ref, gm_ref, wr_ref, br_ref,
                  x2_ref, hm_ref, ti_ref, tg_ref, pos_ref, cnt_ref, run_ref, tri_ref):
    i = pl.program_id(0)
    tm = x_ref.shape[0]

    @pl.when(i == 0)
    def _():
        run_ref[...] = jnp.zeros(run_ref.shape, F32)
        r = lax.broadcasted_iota(I32, (tm, tm), 0)
        c = lax.broadcasted_iota(I32, (tm, tm), 1)
        tri_ref[...] = jnp.where(r <= c, 1.0, 0.0).astype(BF16)

    x = x_ref[...]
    hb = _rms(x, gx_ref[...]).astype(BF16)
    q = _dot(hb, wq_ref[...])
    scale = HEAD_DIM ** -0.5
    outs = []
    for h in range(q.shape[1] // HEAD_DIM):
        sl = slice(h * HEAD_DIM, (h + 1) * HEAD_DIM)
        qh = (_rms(q[:, sl], gq_ref[...]) * scale).astype(BF16)
        s = _dot_nt(qh, kn_ref[:, sl])
        p = jnp.exp(s - jnp.max(s, axis=-1, keepdims=True))
        oh = _dot(p.astype(BF16), v_ref[:, sl]) / jnp.sum(p, axis=-1, keepdims=True)
        outs.append(oh.astype(BF16))
    x2 = x + _dot(jnp.concatenate(outs, axis=-1), wo_ref[...])
    x2_ref[...] = x2
    hm = _rms(x2, gm_ref[...])
    hm_ref[...] = hm
    lg = _dot_nt(wr_ref[...], hm.astype(BF16)) + br_ref[...]
    n_exp = lg.shape[0]
    eidx = lax.broadcasted_iota(I32, lg.shape, 0).astype(F32)
    vals, sel = [], []
    for r in range(TOP_K):
        mx = jnp.max(lg, axis=0, keepdims=True)
        am = jnp.min(jnp.where(lg == mx, eidx, float(n_exp)), axis=0, keepdims=True)
        ti_ref[r:r + 1, :] = am.astype(I32)
        vals.append(mx)
        sel.append(eidx == am)
        lg = jnp.where(sel[r], -jnp.inf, lg)
    ex = [jnp.exp(v - vals[0]) for v in vals]
    den = ex[0]
    for e in ex[1:]:
        den = den + e
    for r in range(TOP_K):
        tg_ref[r:r + 1, :] = ex[r] / den
    onehot = jnp.zeros(lg.shape, F32)
    for r in range(TOP_K):
        onehot = onehot + jnp.where(sel[r], 1.0, 0.0)
    incl = _dot(onehot.astype(BF16), tri_ref[...])
    excl = incl - onehot + run_ref[...]
    for r in range(TOP_K):
        pos_ref[r:r + 1, :] = jnp.sum(jnp.where(sel[r], excl, 0.0), axis=0,
                                      keepdims=True).astype(I32)
    run = run_ref[...] + incl[:, tm - 1:tm]
    run_ref[...] = run
    cnt_ref[...] = jnp.broadcast_to(run, cnt_ref.shape)


def _xattn(x1, g_xattn, w_xq, g_xq, kn, vm, w_xo, g_moe, w_rt, b_r, seq, n_mem):
    t, d = x1.shape
    xw = w_xq.shape[1]
    n_exp = w_rt.shape[0]
    tm = _tile(seq, 512)
    per = seq // tm
    const = lambda i: (0, 0)
    return pl.pallas_call(
        _xattn_kernel,
        grid=(t // tm,),
        in_specs=[pl.BlockSpec((tm, d), lambda i: (i, 0)),
                  pl.BlockSpec((1, d), const),
                  pl.BlockSpec((d, xw), const),
                  pl.BlockSpec((1, HEAD_DIM), const),
                  pl.BlockSpec((n_mem, xw), lambda i: (i // per, 0)),
                  pl.BlockSpec((n_mem, xw), lambda i: (i // per, 0)),
                  pl.BlockSpec((xw, d), const),
                  pl.BlockSpec((1, d), const),
                  pl.BlockSpec((n_exp, d), const),
                  pl.BlockSpec((n_exp, 1), const)],
        out_specs=[pl.BlockSpec((tm, d), lambda i: (i, 0)),
                   pl.BlockSpec((tm, d), lambda i: (i, 0)),
                   pl.BlockSpec((TOP_K, tm), lambda i: (0, i)),
                   pl.BlockSpec((TOP_K, tm), lambda i: (0, i)),
                   pl.BlockSpec((TOP_K, tm), lambda i: (0, i)),
                   pl.BlockSpec((n_exp, 128), const)],
        out_shape=[SDS((t, d), F32), SDS((t, d), F32),
                   SDS((TOP_K, t), I32), SDS((TOP_K, t), F32), SDS((TOP_K, t), I32),
                   SDS((n_exp, 128), F32)],
        scratch_shapes=[pltpu.VMEM((n_exp, 1), F32), pltpu.VMEM((tm, tm), BF16)],
        compiler_params=_cparams(1), name="xattn_router")(
            x1, g_xattn, w_xq, g_xq, kn, vm, w_xo, g_moe, w_rt, b_r)


def _route_tables(counts, n_assign, tm):
    n_exp = counts.shape[0]
    padded = (counts + tm - 1) // tm * tm
    pad_end = jnp.cumsum(padded).astype(I32)
    pad_start = pad_end - padded
    n_blk = n_assign // tm + n_exp
    n_used = pad_end[-1] // tm
    blk = jnp.minimum(jnp.arange(n_blk, dtype=I32), n_used - 1)
    blk_exp = jnp.sum(((blk * tm)[:, None] >= pad_end[None, :]).astype(I32), axis=1)
    blk_exp = jnp.minimum(blk_exp, n_exp - 1)
    return pad_start, pad_end, padded.astype(I32), blk, blk_exp, n_used.reshape(1).astype(I32)


def _dest_kernel(ps_ref, ti_ref, pos_ref, o_ref, *, n_exp):
    ti = ti_ref[...]
    acc = pos_ref[...]
    for e in range(n_exp):
        acc = acc + jnp.where(ti == e, ps_ref[e], 0)
    o_ref[...] = acc


def _dest_rows(pad_start, ti, pos):
    k, t = ti.shape
    kern = functools.partial(_dest_kernel, n_exp=pad_start.shape[0])
    grid_spec = pltpu.PrefetchScalarGridSpec(
        num_scalar_prefetch=1, grid=(1,),
        in_specs=[pl.BlockSpec((k, t), lambda i, ps: (0, 0)),
                  pl.BlockSpec((k, t), lambda i, ps: (0, 0))],
        out_specs=pl.BlockSpec((k, t), lambda i, ps: (0, 0)))
    return pl.pallas_call(kern, grid_spec=grid_spec, out_shape=SDS((k, t), I32),
                          compiler_params=_cparams(1), name="moe_dest")(pad_start, ti, pos)


def _dispatch_kernel(pe_ref, pc_ref, nu_ref, dst_ref, x_ref, xs_ref, buf, zbuf, sem, zsem,
                     *, tc, tm, n_exp, n_blk):
    i = pl.program_id(0)
    n = pl.num_programs(0)

    def zero_copy(row0):
        return pltpu.make_async_copy(zbuf, xs_ref.at[pl.ds(pl.multiple_of(row0, tm), tm)], zsem)

    @pl.when(i == 0)
    def _():
        zbuf[...] = jnp.zeros(zbuf.shape, zbuf.dtype)
        for start in (True, False):
            for e in range(n_exp):
                @pl.when(pc_ref[e] > 0)
                def _(e=e):
                    cp = zero_copy(pe_ref[e] - tm)
                    cp.start() if start else cp.wait()

            def tail(b, c):
                cp = zero_copy(b * tm)
                cp.start() if start else cp.wait()
                return c
            lax.fori_loop(nu_ref[0], n_blk, tail, 0)

    slot = i % 2
    buf[slot] = x_ref[...]
    for k in range(TOP_K):
        for r in range(tc):
            pltpu.make_async_copy(buf.at[slot, pl.ds(r, 1)],
                                  xs_ref.at[pl.ds(dst_ref[0, k, r], 1)], sem.at[slot]).start()

    def wait_slot(s):
        for k in range(TOP_K):
            pltpu.make_async_copy(buf.at[s], xs_ref.at[pl.ds(0, tc)], sem.at[s]).wait()

    @pl.when(i >= 1)
    def _():
        wait_slot(1 - slot)

    @pl.when(i == n - 1)
    def _():
        wait_slot(slot)


def _dispatch(hm, dest, pad_end, padded, n_used, tm, n_blk):
    t, d = hm.shape
    tc = _tile(t, 128)
    n = t // tc
    n_exp = pad_end.shape[0]
    dest3 = dest.reshape(TOP_K, n, tc).transpose(1, 0, 2)
    kern = functools.partial(_dispatch_kernel, tc=tc, tm=tm, n_exp=n_exp, n_blk=n_blk)
    grid_spec = pltpu.PrefetchScalarGridSpec(
        num_scalar_prefetch=3, grid=(n,),
        in_specs=[pl.BlockSpec((1, TOP_K, tc), lambda i, *_: (i, 0, 0), memory_space=pltpu.SMEM),
                  pl.BlockSpec((tc, d), lambda i, *_: (i, 0))],
        out_specs=pl.BlockSpec(memory_space=pl.ANY),
        scratch_shapes=[pltpu.VMEM((2, tc, d), hm.dtype),
                        pltpu.VMEM((tm, d), hm.dtype),
                        pltpu.SemaphoreType.DMA((2,)),
                        pltpu.SemaphoreType.DMA(())])
    return pl.pallas_call(
        kern, grid_spec=grid_spec,
        out_shape=SDS((n_blk * tm, d), hm.dtype),
        compiler_params=_cparams(1), name="moe_dispatch")(pad_end, padded, n_used, dest3, hm)


W_CHUNK = 512


def _stream_tables(blk_exp, pad_end, counts, tm):
    n_blk = blk_exp.shape[0]
    n_exp = counts.shape[0]
    prev = jnp.concatenate([jnp.full((1,), -1, I32), blk_exp[:-1]])
    first = (blk_exp != prev).astype(I32)
    rem = pad_end[blk_exp] // tm - jnp.arange(n_blk, dtype=I32)
    cand = jnp.where(counts > 0, jnp.arange(n_exp, dtype=I32), n_exp)
    after = jnp.concatenate([lax.cummin(cand, reverse=True)[1:], jnp.full((1,), n_exp, I32)])
    nxt = jnp.where(after >= n_exp, -1, after)[blk_exp]
    return first, rem.astype(I32), nxt.astype(I32)


def _stream_weights(i, be_ref, first_ref, rem_ref, nxt_ref, w_hbm, wb_ref, stage_ref, st_ref, sem):
    n_chunks, _, chunk = wb_ref.shape[1:]

    def chunk_copy(e, c, slot):
        cols = pl.ds(pl.multiple_of(c * chunk, chunk), chunk)
        return pltpu.make_async_copy(w_hbm.at[e, :, cols], stage_ref.at[slot], sem.at[slot])

    def start_first_two(e):
        chunk_copy(e, 0, 0).start()
        chunk_copy(e, 1, 1).start()

    def cast_chunks(e, dst, done, issued, k):
        def body(_, carry):
            done, issued = carry
            slot = done % 2
            chunk_copy(e, done, slot).wait()
            wb_ref[dst, done] = stage_ref[slot].astype(BF16)

            @pl.when(issued < n_chunks)
            def _():
                chunk_copy(e, issued, slot).start()
            return done + 1, jnp.minimum(issued + 1, n_chunks)
        return lax.fori_loop(0, k, body, (done, issued))

    @pl.when(i == 0)
    def _():
        start_first_two(be_ref[0])
        cast_chunks(be_ref[0], 0, 0, 2, n_chunks)
        st_ref[0] = 1

    @pl.when(first_ref[i] == 1)
    def _():
        st_ref[0] = 1 - st_ref[0]
        st_ref[1] = 0
        st_ref[2] = 0

        @pl.when(nxt_ref[i] >= 0)
        def _():
            start_first_two(nxt_ref[i])
            st_ref[2] = 2

    cur = st_ref[0]

    @pl.when(nxt_ref[i] >= 0)
    def _():
        done = st_ref[1]
        rem = rem_ref[i]
        k = lax.div(n_chunks - done + rem - 1, rem)
        done, issued = cast_chunks(nxt_ref[i], 1 - cur, done, st_ref[2], k)
        st_ref[1] = done
        st_ref[2] = issued

    return cur


def _moe_up_kernel(be_ref, first_ref, rem_ref, nxt_ref, nu_ref, x_ref, b_ref, w_hbm, o_ref,
                   wb_ref, stage_ref, st_ref, sem):
    i = pl.program_id(0)

    @pl.when(i < nu_ref[0])
    def _():
        cur = _stream_weights(i, be_ref, first_ref, rem_ref, nxt_ref, w_hbm, wb_ref, stage_ref,
                              st_ref, sem)
        x = x_ref[...].astype(BF16)
        n_half = wb_ref.shape[1] // 2
        chunk = wb_ref.shape[3]
        f = n_half * chunk
        for n in range(n_half):
            lo, hi = n * chunk, (n + 1) * chunk
            gate = jnp.minimum(_dot(x, wb_ref[cur, n]) + b_ref[:, lo:hi], SWIGLU_LIMIT)
            up = jnp.clip(_dot(x, wb_ref[cur, n_half + n]) + b_ref[:, f + lo:f + hi],
                          -SWIGLU_LIMIT, SWIGLU_LIMIT)
            glu = gate * jax.nn.sigmoid(SWIGLU_ALPHA * gate)
            o_ref[:, lo:hi] = ((up + 1.0) * glu).astype(o_ref.dtype)

    @pl.when(i >= nu_ref[0])
    def _():
        o_ref[...] = jnp.zeros(o_ref.shape, o_ref.dtype)


def _moe_up(xs, w_gu, b_gu, tables, n_used, tm):
    n_rows, d = xs.shape
    two_f = w_gu.shape[2]
    chunk = min(W_CHUNK, two_f // 2)
    grid_spec = pltpu.PrefetchScalarGridSpec(
        num_scalar_prefetch=5,
        grid=(n_rows // tm,),
        in_specs=[pl.BlockSpec((tm, d), lambda i, be, fi, re, nx, nu: (jnp.minimum(i, nu[0] - 1), 0)),
                  pl.BlockSpec((None, 1, two_f), lambda i, be, fi, re, nx, nu: (be[i], 0, 0)),
                  pl.BlockSpec(memory_space=pl.ANY)],
        out_specs=pl.BlockSpec((tm, two_f // 2), lambda i, *_: (i, 0)),
        scratch_shapes=[pltpu.VMEM((2, two_f // chunk, d, chunk), BF16),
                        pltpu.VMEM((2, d, chunk), w_gu.dtype),
                        pltpu.SMEM((3,), I32),
                        pltpu.SemaphoreType.DMA((2,))])
    return pl.pallas_call(
        _moe_up_kernel, grid_spec=grid_spec,
        out_shape=SDS((n_rows, two_f // 2), BF16),
        compiler_params=_cparams(1), name="moe_up")(*tables, n_used, xs, b_gu, w_gu)


def _moe_down_kernel(be_ref, first_ref, rem_ref, nxt_ref, nu_ref, a_ref, b_ref, w_hbm, o_ref,
                     wb_ref, stage_ref, st_ref, sem):
    i = pl.program_id(0)

    @pl.when(i < nu_ref[0])
    def _():
        cur = _stream_weights(i, be_ref, first_ref, rem_ref, nxt_ref, w_hbm, wb_ref, stage_ref,
                              st_ref, sem)
        a = a_ref[...]
        chunk = wb_ref.shape[3]
        for n in range(wb_ref.shape[1]):
            lo, hi = n * chunk, (n + 1) * chunk
            o_ref[:, lo:hi] = _dot(a, wb_ref[cur, n]) + b_ref[:, lo:hi]

    @pl.when(i >= nu_ref[0])
    def _():
        o_ref[...] = jnp.zeros(o_ref.shape, o_ref.dtype)


def _moe_down(act, w_dn, b_dn, tables, n_used, tm):
    n_rows, f = act.shape
    d = w_dn.shape[2]
    chunk = min(W_CHUNK, d // 2)
    grid_spec = pltpu.PrefetchScalarGridSpec(
        num_scalar_prefetch=5,
        grid=(n_rows // tm,),
        in_specs=[pl.BlockSpec((tm, f), lambda i, be, fi, re, nx, nu: (jnp.minimum(i, nu[0] - 1), 0)),
                  pl.BlockSpec((None, 1, d), lambda i, be, fi, re, nx, nu: (be[i], 0, 0)),
                  pl.BlockSpec(memory_space=pl.ANY)],
        out_specs=pl.BlockSpec((tm, d), lambda i, *_: (i, 0)),
        scratch_shapes=[pltpu.VMEM((2, d // chunk, f, chunk), BF16),
                        pltpu.VMEM((2, f, chunk), w_dn.dtype),
                        pltpu.SMEM((3,), I32),
                        pltpu.SemaphoreType.DMA((2,))])
    return pl.pallas_call(
        _moe_down_kernel, grid_spec=grid_spec,
        out_shape=SDS((n_rows, d), F32),
        compiler_params=_cparams(1), name="moe_down")(*tables, n_used, act, b_dn, w_dn)


def _combine_kernel(dst_ref, x_ref, g_ref, ys_ref, o_ref, buf, sem, *, tc):
    i = pl.program_id(0)
    n = pl.num_programs(0) - 1

    @pl.when(i < n)
    def _():
        slot = i % 2
        for k in range(TOP_K):
            for r in range(tc):
                pltpu.make_async_copy(ys_ref.at[pl.ds(dst_ref[0, k, r], 1)],
                                      buf.at[slot, k, pl.ds(r, 1)], sem.at[slot]).start()

    @pl.when(i >= 1)
    def _():
        slot = (i - 1) % 2
        for k in range(TOP_K):
            pltpu.make_async_copy(ys_ref.at[pl.ds(0, tc)], buf.at[slot, k], sem.at[slot]).wait()
        acc = x_ref[...]
        g = g_ref[...]
        for k in range(TOP_K):
            acc = acc + g[:, k:k + 1] * buf[slot, k]
        o_ref[...] = acc


def _combine(x2, gates_t, dest, ys):
    t, d = x2.shape
    tc = _tile(t, 128)
    n = t // tc
    dest3 = dest.reshape(TOP_K, n, tc).transpose(1, 0, 2)
    kern = functools.partial(_combine_kernel, tc=tc)
    prev = lambda i: (jnp.maximum(i - 1, 0), 0)
    return pl.pallas_call(
        kern,
        grid=(n + 1,),
        in_specs=[pl.BlockSpec((1, TOP_K, tc), lambda i: (jnp.minimum(i, n - 1), 0, 0),
                               memory_space=pltpu.SMEM),
                  pl.BlockSpec((tc, d), prev),
                  pl.BlockSpec((tc, TOP_K), prev),
                  pl.BlockSpec(memory_space=pl.ANY)],
        out_specs=pl.BlockSpec((tc, d), prev),
        out_shape=SDS((t, d), F32),
        scratch_shapes=[pltpu.VMEM((2, TOP_K, tc, d), F32),
                        pltpu.SemaphoreType.DMA((2,))],
        compiler_params=_cparams(1), name="moe_combine")(dest3, x2, gates_t, ys)


def _layer(x2d, mem2d, batch, seq, n_mem, p):
    t, d = x2d.shape
    c = p["conv_w"].shape[1]
    fw = p["g_fox_out"].shape[0]
    n_heads = fw // HEAD_DIM
    n_main = 3 * c + 3 * fw
    n_exp = p["w_router"].shape[1]

    w_in = p["w_mix_in"].astype(BF16)
    wf_t = jnp.zeros((F_ROWS, d), BF16).at[:n_heads].set(w_in[:, n_main:].T)
    b_col = jnp.zeros((F_ROWS, 1), F32).at[:n_heads, 0].set(p["b_forget"])

    h, ft = _prenorm(x2d, p["g_mix"][None, :], wf_t)
    cdec = _decay(ft, b_col, seq)
    proj = _matmul(h, w_in, n_main, BF16, name="mix_in")
    y_fox = _fox_attention(proj, cdec, p["g_q"][None, :], p["g_k"][None, :],
                           batch, seq, n_heads, 3 * c, fw)
    y_mix = _mixprep(proj, y_fox, p["conv_w"], p["g_conv_out"][None, :],
                     p["g_fox_out"][None, :], batch, seq)
    x1 = _matmul(y_mix, p["w_mix_out"].astype(BF16), d, F32, res=x2d, name="mix_out")

    kn, vm = _memkv(mem2d, p["g_mem"][None, :], p["w_xkv"].astype(BF16),
                    p["g_xk"][None, :], n_mem)
    x2, hm, ti, tg, pos, cnt = _xattn(x1, p["g_xattn"][None, :], p["w_xq"].astype(BF16),
                            p["g_xq"][None, :], kn, vm, p["w_xo"].astype(BF16),
                            p["g_moe"][None, :], p["w_router"].T.astype(BF16),
                            p["b_router"][:, None], seq, n_mem)

    tm = _tile(TOP_K * t, 256)
    counts = cnt[:, 0].astype(I32)
    pad_start, pad_end, padded, blk, blk_exp, n_used = _route_tables(counts, TOP_K * t, tm)
    tables = (blk_exp,) + _stream_tables(blk_exp, pad_end, counts, tm)
    dest = _dest_rows(pad_start, ti, pos)
    xs = _dispatch(hm, dest, pad_end, padded, n_used, tm, blk.shape[0])
    act = _moe_up(xs, p["w_gate_up"], p["b_gate_up"][:, None, :], tables, n_used, tm)
    ys = _moe_down(act, p["w_down"], p["b_down"][:, None, :], tables, n_used, tm)
    return _combine(x2, tg.T, dest, ys)


def kernel(x, mem, g_mix, w_mix_in, b_forget, conv_w, g_q, g_k, g_conv_out, g_fox_out, w_mix_out, g_xattn, g_mem, w_xq, w_xkv, g_xq, g_xk, w_xo, g_moe, w_router, b_router, w_gate_up, b_gate_up, w_down, b_down):
    batch, seq, d = x.shape
    n_mem = mem.shape[1]
    params = dict(g_mix=g_mix, w_mix_in=w_mix_in, b_forget=b_forget, conv_w=conv_w, g_q=g_q,
                  g_k=g_k, g_conv_out=g_conv_out, g_fox_out=g_fox_out, w_mix_out=w_mix_out,
                  g_xattn=g_xattn, g_mem=g_mem, w_xq=w_xq, w_xkv=w_xkv, g_xq=g_xq, g_xk=g_xk,
                  w_xo=w_xo, g_moe=g_moe, w_router=w_router, b_router=b_router,
                  w_gate_up=w_gate_up, b_gate_up=b_gate_up, w_down=w_down, b_down=b_down)
    x2d = x.reshape(batch * seq, d)
    mem2d = mem.reshape(batch * n_mem, d)
    for l in range(g_mix.shape[0]):
        x2d = _layer(x2d, mem2d, batch, seq, n_mem, {k: v[l] for k, v in params.items()})
    return x2d.reshape(batch, seq, d)
```

```python
import functools

import jax
import jax.numpy as jnp
from jax import lax
from jax.experimental import pallas as pl
from jax.experimental.pallas import tpu as pltpu

F32 = jnp.float32
BF16 = jnp.bfloat16
I32 = jnp.int32
SDS = jax.ShapeDtypeStruct

EPS = 1e-6
HEAD_DIM = 128
TOP_K = 4
SWIGLU_LIMIT = 7.0
SWIGLU_ALPHA = 1.702
MASK_VALUE = float(jnp.finfo(jnp.float32).min)
V7X_VMEM_LIMIT_BYTES = 56 * 1024 * 1024
F_ROWS = 16


def _cparams(n_axes):
    return pltpu.CompilerParams(
        dimension_semantics=("arbitrary",) * n_axes,
        vmem_limit_bytes=V7X_VMEM_LIMIT_BYTES)


def _rms(x, g):
    return x * lax.rsqrt(jnp.mean(x * x, axis=-1, keepdims=True) + EPS) * g


def _dot(a, b):
    return jnp.dot(a, b, preferred_element_type=F32)


def _dot_nt(a, b):
    return lax.dot_general(a, b, (((1,), (1,)), ((), ())), preferred_element_type=F32)


def _tile(n, pref):
    return pref if n % pref == 0 else n


def _mix_in_kernel(x_ref, g_ref, wf_ref, w_ref, o_ref, ft_ref, h_ref):
    @pl.when(pl.program_id(1) == 0)
    def _():
        hb = _rms(x_ref[...], g_ref[...]).astype(BF16)
        h_ref[...] = hb
        ft_ref[...] = _dot_nt(wf_ref[...], hb)

    o_ref[...] = _dot(h_ref[...], w_ref[...]).astype(o_ref.dtype)


def _mix_in(x2d, g, wf_t, w_in, n_out):
    t, d = x2d.shape
    tm = _tile(t, 1024)
    tn = _tile(n_out, 1024)
    return pl.pallas_call(
        _mix_in_kernel,
        grid=(t // tm, n_out // tn),
        in_specs=[pl.BlockSpec((tm, d), lambda i, j: (i, 0)),
                  pl.BlockSpec((1, d), lambda i, j: (0, 0)),
                  pl.BlockSpec((F_ROWS, d), lambda i, j: (0, 0)),
                  pl.BlockSpec((d, tn), lambda i, j: (0, j))],
        out_specs=[pl.BlockSpec((tm, tn), lambda i, j: (i, j)),
                   pl.BlockSpec((F_ROWS, tm), lambda i, j: (0, i))],
        out_shape=[SDS((t, n_out), BF16), SDS((F_ROWS, t), F32)],
        scratch_shapes=[pltpu.VMEM((tm, d), BF16)],
        compiler_params=_cparams(2), name="mix_in")(x2d, g, wf_t, w_in)


def _decay_kernel(ft_ref, b_ref, c_ref):
    z = ft_ref[...] + b_ref[...]
    lf = jnp.minimum(z, 0.0) - jnp.log(1.0 + jnp.exp(-jnp.abs(z)))
    s = lf.shape[1]
    lane = lax.broadcasted_iota(I32, lf.shape, 1)
    d = 1
    while d < s:
        lf = lf + jnp.where(lane >= d, pltpu.roll(lf, d, 1), 0.0)
        d *= 2
    c_ref[...] = lf


def _decay(ft, b_col, seq):
    rows, t = ft.shape
    return pl.pallas_call(
        _decay_kernel,
        grid=(t // seq,),
        in_specs=[pl.BlockSpec((rows, seq), lambda b: (0, b)),
                  pl.BlockSpec((rows, 1), lambda b: (0, 0))],
        out_specs=pl.BlockSpec((rows, seq), lambda b: (0, b)),
        out_shape=SDS((rows, t), F32),
        compiler_params=_cparams(1), name="decay")(ft, b_col)


def _fox_kernel(q_ref, k_ref, v_ref, c_ref, gq_ref, gk_ref, o_ref, kn_ref, nc_ref, *, tq):
    h = pl.program_id(1)
    seq = q_ref.shape[0]
    scale = HEAD_DIM ** -0.5
    kn_ref[...] = _rms(k_ref[...].astype(F32), gk_ref[...]).astype(BF16)
    nc_ref[...] = -c_ref[pl.ds(h, 1), :]
    row = lax.broadcasted_iota(I32, (tq, tq), 0)
    col = lax.broadcasted_iota(I32, (tq, tq), 1)
    causal = col <= row
    for qi in range(seq // tq):
        lo, hi = qi * tq, (qi + 1) * tq
        q = (_rms(q_ref[lo:hi, :].astype(F32), gq_ref[...]) * scale).astype(BF16)
        s_diag = jnp.where(causal, _dot_nt(q, kn_ref[lo:hi, :]) + nc_ref[:, lo:hi], MASK_VALUE)
        m = jnp.max(s_diag, axis=-1, keepdims=True)
        if qi > 0:
            s_past = _dot_nt(q, kn_ref[:lo, :]) + nc_ref[:, :lo]
            m = jnp.maximum(m, jnp.max(s_past, axis=-1, keepdims=True))
        p_diag = jnp.exp(s_diag - m)
        l = jnp.sum(p_diag, axis=-1, keepdims=True)
        acc = _dot(p_diag.astype(BF16), v_ref[lo:hi, :])
        if qi > 0:
            p_past = jnp.exp(s_past - m)
            l = l + jnp.sum(p_past, axis=-1, keepdims=True)
            acc = acc + _dot(p_past.astype(BF16), v_ref[:lo, :])
        o_ref[lo:hi, :] = (acc / l).astype(o_ref.dtype)


def _fox_attention(proj, c, g_q, g_k, batch, seq, n_heads, col0, width):
    t = proj.shape[0]
    tq = _tile(seq, 256)
    cb = col0 // HEAD_DIM
    wb = width // HEAD_DIM
    hd = HEAD_DIM
    kern = functools.partial(_fox_kernel, tq=tq)
    return pl.pallas_call(
        kern,
        grid=(batch, n_heads),
        in_specs=[pl.BlockSpec((seq, hd), lambda b, h: (b, cb + h)),
                  pl.BlockSpec((seq, hd), lambda b, h: (b, cb + wb + h)),
                  pl.BlockSpec((seq, hd), lambda b, h: (b, cb + 2 * wb + h)),
                  pl.BlockSpec((F_ROWS, seq), lambda b, h: (0, b)),
                  pl.BlockSpec((1, hd), lambda b, h: (0, 0)),
                  pl.BlockSpec((1, hd), lambda b, h: (0, 0))],
        out_specs=pl.BlockSpec((seq, hd), lambda b, h: (b, h)),
        out_shape=SDS((t, width), BF16),
        scratch_shapes=[pltpu.VMEM((seq, hd), BF16), pltpu.VMEM((1, seq), F32)],
        compiler_params=_cparams(2), name="fox_attention")(proj, proj, proj, c, g_q, g_k)


def _mix_out_kernel(ub_ref, uc_ref, ux_ref, hc_ref, hx_ref, yf_ref, cw_ref, gc_ref, gf_ref,
                    w_ref, x_ref, o_ref):
    si = pl.program_id(1)
    p = uc_ref[...].astype(F32) * ux_ref[...].astype(F32)
    halo = hc_ref[...].astype(F32) * hx_ref[...].astype(F32)
    halo = jnp.where(si > 0, halo, 0.0)
    row = lax.broadcasted_iota(I32, p.shape, 0)
    p1 = jnp.where(row == 0, halo[7:8, :], pltpu.roll(p, 1, 0))
    p2 = jnp.where(row == 0, halo[6:7, :],
                   jnp.where(row == 1, halo[7:8, :], pltpu.roll(p, 2, 0)))
    w = cw_ref[...]
    y = ub_ref[...].astype(F32) * (w[0:1, :] * p2 + w[1:2, :] * p1 + w[2:3, :] * p)
    c = y.shape[1]
    y_conv = _rms(y, gc_ref[...]).astype(BF16)
    y_fox = _rms(yf_ref[...].astype(F32), gf_ref[...]).astype(BF16)
    o_ref[...] = x_ref[...] + _dot(y_conv, w_ref[:c, :]) + _dot(y_fox, w_ref[c:, :])


def _mix_out(proj, y_fox, conv_w, g_conv, g_fox, w_out, x2d, batch, seq):
    t, d = x2d.shape
    c = conv_w.shape[1]
    w = y_fox.shape[1]
    ts = _tile(seq, 512)
    ns = seq // ts
    hb = ts // 8
    tile = lambda col: (lambda b, s: (b * ns + s, col))
    halo = lambda col: (lambda b, s: (jnp.maximum((b * ns + s) * hb - 1, 0), col))
    const = lambda b, s: (0, 0)
    return pl.pallas_call(
        _mix_out_kernel,
        grid=(batch, ns),
        in_specs=[pl.BlockSpec((ts, c), tile(0)),
                  pl.BlockSpec((ts, c), tile(1)),
                  pl.BlockSpec((ts, c), tile(2)),
                  pl.BlockSpec((8, c), halo(1)),
                  pl.BlockSpec((8, c), halo(2)),
                  pl.BlockSpec((ts, w), tile(0)),
                  pl.BlockSpec((3, c), const),
                  pl.BlockSpec((1, c), const),
                  pl.BlockSpec((1, w), const),
                  pl.BlockSpec((c + w, d), const, pipeline_mode=pl.Buffered(1)),
                  pl.BlockSpec((ts, d), tile(0))],
        out_specs=pl.BlockSpec((ts, d), tile(0)),
        out_shape=SDS((t, d), F32),
        compiler_params=_cparams(2), name="mix_out")(
            proj, proj, proj, proj, proj, y_fox, conv_w, g_conv, g_fox, w_out, x2d)


def _memkv_kernel(mem_ref, g_ref, w_ref, gk_ref, k_ref, v_ref):
    hm = _rms(mem_ref[...], g_ref[...]).astype(BF16)
    kv = _dot(hm, w_ref[...])
    xw = k_ref.shape[1]
    for h in range(xw // HEAD_DIM):
        sl = slice(h * HEAD_DIM, (h + 1) * HEAD_DIM)
        k_ref[:, sl] = _rms(kv[:, sl], gk_ref[...]).astype(BF16)
    v_ref[...] = kv[:, xw:].astype(BF16)


def _memkv(mem2d, g_mem, w_xkv, g_xk, n_mem):
    tm_, d = mem2d.shape
    xw = w_xkv.shape[1] // 2
    return pl.pallas_call(
        _memkv_kernel,
        grid=(tm_ // n_mem,),
        in_specs=[pl.BlockSpec((n_mem, d), lambda b: (b, 0)),
                  pl.BlockSpec((1, d), lambda b: (0, 0)),
                  pl.BlockSpec((d, 2 * xw), lambda b: (0, 0)),
                  pl.BlockSpec((1, HEAD_DIM), lambda b: (0, 0))],
        out_specs=[pl.BlockSpec((n_mem, xw), lambda b: (b, 0)),
                   pl.BlockSpec((n_mem, xw), lambda b: (b, 0))],
        out_shape=[SDS((tm_, xw), BF16), SDS((tm_, xw), BF16)],
        compiler_params=_cparams(1), name="memkv")(mem2d, g_mem, w_xkv, g_xk)


def _xattn_kernel(x_ref, gx_ref, wq_ref, gq_ref, kn_ref, v_ref, wo_ref, gm_ref, wr_ref, br_ref,
                  x2_ref, hm_ref, ti_ref, tg_ref, pos_ref, cnt_ref, run_ref, tri_ref):
    i = pl.program_id(0)
    tm = x_ref.shape[0]

    @pl.when(i == 0)
    def _():
        run_ref[...] = jnp.zeros(run_ref.shape, F32)
        r = lax.broadcasted_iota(I32, (tm, tm), 0)
        c = lax.broadcasted_iota(I32, (tm, tm), 1)
        tri_ref[...] = jnp.where(r <= c, 1.0, 0.0).astype(BF16)

    x = x_ref[...]
    hb = _rms(x, gx_ref[...]).astype(BF16)
    q = _dot(hb, wq_ref[...])
    scale = HEAD_DIM ** -0.5
    outs = []
    for h in range(q.shape[1] // HEAD_DIM):
        sl = slice(h * HEAD_DIM, (h + 1) * HEAD_DIM)
        qh = (_rms(q[:, sl], gq_ref[...]) * scale).astype(BF16)
        s = _dot_nt(qh, kn_ref[:, sl])
        p = jnp.exp(s - jnp.max(s, axis=-1, keepdims=True))
        oh = _dot(p.astype(BF16), v_ref[:, sl]) / jnp.sum(p, axis=-1, keepdims=True)
        outs.append(oh.astype(BF16))
    x2 = x + _dot(jnp.concatenate(outs, axis=-1), wo_ref[...])
    x2_ref[...] = x2
    hm = _rms(x2, gm_ref[...])
    hm_ref[...] = hm
    lg = _dot_nt(wr_ref[...], hm.astype(BF16)) + br_ref[...]
    n_exp = lg.shape[0]
    eidx = lax.broadcasted_iota(I32, lg.shape, 0).astype(F32)
    vals, sel = [], []
    for r in range(TOP_K):
        mx = jnp.max(lg, axis=0, keepdims=True)
        am = jnp.min(jnp.where(lg == mx, eidx, float(n_exp)), axis=0, keepdims=True)
        ti_ref[r:r + 1, :] = am.astype(I32)
        vals.append(mx)
        sel.append(eidx == am)
        lg = jnp.where(sel[r], -jnp.inf, lg)
    ex = [jnp.exp(v - vals[0]) for v in vals]
    den = ex[0]
    for e in ex[1:]:
        den = den + e
    for r in range(TOP_K):
        tg_ref[r:r + 1, :] = ex[r] / den
    onehot = jnp.zeros(lg.shape, F32)
    for r in range(TOP_K):
        onehot = onehot + jnp.where(sel[r], 1.0, 0.0)
    incl = _dot(onehot.astype(BF16), tri_ref[...])
    excl = incl - onehot + run_ref[...]
    for r in range(TOP_K):
        pos_ref[r:r + 1, :] = jnp.sum(jnp.where(sel[r], excl, 0.0), axis=0,
                                      keepdims=True).astype(I32)
    run = run_ref[...] + incl[:, tm - 1:tm]
    run_ref[...] = run
    cnt_ref[...] = jnp.broadcast_to(run, cnt_ref.shape)


def _xattn(x1, g_xattn, w_xq, g_xq, kn, vm, w_xo, g_moe, w_rt, b_r, seq, n_mem):
    t, d = x1.shape
    xw = w_xq.shape[1]
    n_exp = w_rt.shape[0]
    tm = _tile(seq, 512)
    per = seq // tm
    const = lambda i: (0, 0)
    return pl.pallas_call(
        _xattn_kernel,
        grid=(t // tm,),
        in_specs=[pl.BlockSpec((tm, d), lambda i: (i, 0)),
                  pl.BlockSpec((1, d), const),
                  pl.BlockSpec((d, xw), const),
                  pl.BlockSpec((1, HEAD_DIM), const),
                  pl.BlockSpec((n_mem, xw), lambda i: (i // per, 0)),
                  pl.BlockSpec((n_mem, xw), lambda i: (i // per, 0)),
                  pl.BlockSpec((xw, d), const),
                  pl.BlockSpec((1, d), const),
                  pl.BlockSpec((n_exp, d), const),
                  pl.BlockSpec((n_exp, 1), const)],
        out_specs=[pl.BlockSpec((tm, d), lambda i: (i, 0)),
                   pl.BlockSpec((tm, d), lambda i: (i, 0)),
                   pl.BlockSpec((TOP_K, tm), lambda i: (0, i)),
                   pl.BlockSpec((TOP_K, tm), lambda i: (0, i)),
                   pl.BlockSpec((TOP_K, tm), lambda i: (0, i)),
                   pl.BlockSpec((n_exp, 128), const)],
        out_shape=[SDS((t, d), F32), SDS((t, d), F32),
                   SDS((TOP_K, t), I32), SDS((TOP_K, t), F32), SDS((TOP_K, t), I32),
                   SDS((n_exp, 128), F32)],
        scratch_shapes=[pltpu.VMEM((n_exp, 1), F32), pltpu.VMEM((tm, tm), BF16)],
        compiler_params=_cparams(1), name="xattn_router")(
            x1, g_xattn, w_xq, g_xq, kn, vm, w_xo, g_moe, w_rt, b_r)


def _route_tables(counts, n_assign, tm):
    n_exp = counts.shape[0]
    padded = (counts + tm - 1) // tm * tm
    pad_end = jnp.cumsum(padded).astype(I32)
    pad_start = pad_end - padded
    n_blk = n_assign // tm + n_exp
    n_used = pad_end[-1] // tm
    blk = jnp.minimum(jnp.arange(n_blk, dtype=I32), n_used - 1)
    blk_exp = jnp.sum(((blk * tm)[:, None] >= pad_end[None, :]).astype(I32), axis=1)
    blk_exp = jnp.minimum(blk_exp, n_exp - 1)
    return pad_start, pad_end, padded.astype(I32), blk, blk_exp, n_used.reshape(1).astype(I32)


def _dest_kernel(ps_ref, ti_ref, pos_ref, o_ref, *, n_exp):
    ti = ti_ref[...]
    acc = pos_ref[...]
    for e in range(n_exp):
        acc = acc + jnp.where(ti == e, ps_ref[e], 0)
    o_ref[...] = acc


def _dest_rows(pad_start, ti, pos):
    k, t = ti.shape
    kern = functools.partial(_dest_kernel, n_exp=pad_start.shape[0])
    grid_spec = pltpu.PrefetchScalarGridSpec(
        num_scalar_prefetch=1, grid=(1,),
        in_specs=[pl.BlockSpec((k, t), lambda i, ps: (0, 0)),
                  pl.BlockSpec((k, t), lambda i, ps: (0, 0))],
        out_specs=pl.BlockSpec((k, t), lambda i, ps: (0, 0)))
    return pl.pallas_call(kern, grid_spec=grid_spec, out_shape=SDS((k, t), I32),
                          compiler_params=_cparams(1), name="moe_dest")(pad_start, ti, pos)


def _dispatch_kernel(pe_ref, pc_ref, nu_ref, dst_ref, x_ref, xs_ref, buf, zbuf, sem, zsem,
                     *, tc, tm, n_exp, n_blk):
    i = pl.program_id(0)
    n = pl.num_programs(0)

    def zero_copy(row0):
        return pltpu.make_async_copy(zbuf, xs_ref.at[pl.ds(pl.multiple_of(row0, tm), tm)], zsem)

    @pl.when(i == 0)
    def _():
        zbuf[...] = jnp.zeros(zbuf.shape, zbuf.dtype)
        for start in (True, False):
            for e in range(n_exp):
                @pl.when(pc_ref[e] > 0)
                def _(e=e):
                    cp = zero_copy(pe_ref[e] - tm)
                    cp.start() if start else cp.wait()

            def tail(b, c):
                cp = zero_copy(b * tm)
                cp.start() if start else cp.wait()
                return c
            lax.fori_loop(nu_ref[0], n_blk, tail, 0)

    slot = i % 2
    buf[slot] = x_ref[...]
    for k in range(TOP_K):
        for r in range(tc):
            pltpu.make_async_copy(buf.at[slot, pl.ds(r, 1)],
                                  xs_ref.at[pl.ds(dst_ref[0, k, r], 1)],
                                  sem.at[slot]).start(priority=r % 2)

    def wait_slot(s):
        for k in range(TOP_K):
            pltpu.make_async_copy(buf.at[s], xs_ref.at[pl.ds(0, tc)], sem.at[s]).wait()

    @pl.when(i >= 1)
    def _():
        wait_slot(1 - slot)

    @pl.when(i == n - 1)
    def _():
        wait_slot(slot)


def _dispatch(hm, dest, pad_end, padded, n_used, tm, n_blk):
    t, d = hm.shape
    tc = _tile(t, 128)
    n = t // tc
    n_exp = pad_end.shape[0]
    dest3 = dest.reshape(TOP_K, n, tc).transpose(1, 0, 2)
    kern = functools.partial(_dispatch_kernel, tc=tc, tm=tm, n_exp=n_exp, n_blk=n_blk)
    grid_spec = pltpu.PrefetchScalarGridSpec(
        num_scalar_prefetch=3, grid=(n,),
        in_specs=[pl.BlockSpec((1, TOP_K, tc), lambda i, *_: (i, 0, 0), memory_space=pltpu.SMEM),
                  pl.BlockSpec((tc, d), lambda i, *_: (i, 0))],
        out_specs=pl.BlockSpec(memory_space=pl.ANY),
        scratch_shapes=[pltpu.VMEM((2, tc, d), hm.dtype),
                        pltpu.VMEM((tm, d), hm.dtype),
                        pltpu.SemaphoreType.DMA((2,)),
                        pltpu.SemaphoreType.DMA(())])
    return pl.pallas_call(
        kern, grid_spec=grid_spec,
        out_shape=SDS((n_blk * tm, d), hm.dtype),
        compiler_params=_cparams(1), name="moe_dispatch")(pad_end, padded, n_used, dest3, hm)


W_CHUNK = 512


def _stream_tables(blk_exp, pad_end, counts, tm):
    n_blk = blk_exp.shape[0]
    n_exp = counts.shape[0]
    prev = jnp.concatenate([jnp.full((1,), -1, I32), blk_exp[:-1]])
    first = (blk_exp != prev).astype(I32)
    rem = pad_end[blk_exp] // tm - jnp.arange(n_blk, dtype=I32)
    cand = jnp.where(counts > 0, jnp.arange(n_exp, dtype=I32), n_exp)
    after = jnp.concatenate([lax.cummin(cand, reverse=True)[1:], jnp.full((1,), n_exp, I32)])
    nxt = jnp.where(after >= n_exp, -1, after)[blk_exp]
    return first, rem.astype(I32), nxt.astype(I32)


def _stream_weights(i, be_ref, first_ref, rem_ref, nxt_ref, w_hbm, wb_ref, stage_ref, st_ref, sem):
    n_chunks, _, chunk = wb_ref.shape[1:]

    def chunk_copy(e, c, slot):
        cols = pl.ds(pl.multiple_of(c * chunk, chunk), chunk)
        return pltpu.make_async_copy(w_hbm.at[e, :, cols], stage_ref.at[slot], sem.at[slot])

    def start_first_two(e):
        chunk_copy(e, 0, 0).start()
        chunk_copy(e, 1, 1).start()

    def cast_chunks(e, dst, done, issued, k):
        def body(_, carry):
            done, issued = carry
            slot = done % 2
            chunk_copy(e, done, slot).wait()
            wb_ref[dst, done] = stage_ref[slot].astype(BF16)

            @pl.when(issued < n_chunks)
            def _():
                chunk_copy(e, issued, slot).start()
            return done + 1, jnp.minimum(issued + 1, n_chunks)
        return lax.fori_loop(0, k, body, (done, issued))

    @pl.when(i == 0)
    def _():
        start_first_two(be_ref[0])
        cast_chunks(be_ref[0], 0, 0, 2, n_chunks)
        st_ref[0] = 1

    @pl.when(first_ref[i] == 1)
    def _():
        st_ref[0] = 1 - st_ref[0]
        st_ref[1] = 0
        st_ref[2] = 0

        @pl.when(nxt_ref[i] >= 0)
        def _():
            start_first_two(nxt_ref[i])
            st_ref[2] = 2

    cur = st_ref[0]

    @pl.when(nxt_ref[i] >= 0)
    def _():
        done = st_ref[1]
        rem = rem_ref[i]
        k = lax.div(n_chunks - done + rem - 1, rem)
        done, issued = cast_chunks(nxt_ref[i], 1 - cur, done, st_ref[2], k)
        st_ref[1] = done
        st_ref[2] = issued

    return cur


def _moe_up_kernel(be_ref, first_ref, rem_ref, nxt_ref, nu_ref, x_ref, b_ref, w_hbm, o_ref,
                   wb_ref, stage_ref, st_ref, sem):
    i = pl.program_id(0)

    @pl.when(i < nu_ref[0])
    def _():
        cur = _stream_weights(i, be_ref, first_ref, rem_ref, nxt_ref, w_hbm, wb_ref, stage_ref,
                              st_ref, sem)
        x = x_ref[...].astype(BF16)
        n_half = wb_ref.shape[1] // 2
        chunk = wb_ref.shape[3]
        f = n_half * chunk
        for n in range(n_half):
            lo, hi = n * chunk, (n + 1) * chunk
            gate = jnp.minimum(_dot(x, wb_ref[cur, n]) + b_ref[:, lo:hi], SWIGLU_LIMIT)
            up = jnp.clip(_dot(x, wb_ref[cur, n_half + n]) + b_ref[:, f + lo:f + hi],
                          -SWIGLU_LIMIT, SWIGLU_LIMIT)
            glu = gate * jax.nn.sigmoid(SWIGLU_ALPHA * gate)
            o_ref[:, lo:hi] = ((up + 1.0) * glu).astype(o_ref.dtype)

    @pl.when(i >= nu_ref[0])
    def _():
        o_ref[...] = jnp.zeros(o_ref.shape, o_ref.dtype)


def _moe_up(xs, w_gu, b_gu, tables, n_used, tm):
    n_rows, d = xs.shape
    two_f = w_gu.shape[2]
    chunk = min(W_CHUNK, two_f // 2)
    grid_spec = pltpu.PrefetchScalarGridSpec(
        num_scalar_prefetch=5,
        grid=(n_rows // tm,),
        in_specs=[pl.BlockSpec((tm, d), lambda i, be, fi, re, nx, nu: (jnp.minimum(i, nu[0] - 1), 0)),
                  pl.BlockSpec((None, 1, two_f), lambda i, be, fi, re, nx, nu: (be[i], 0, 0)),
                  pl.BlockSpec(memory_space=pl.ANY)],
        out_specs=pl.BlockSpec((tm, two_f // 2), lambda i, *_: (i, 0)),
        scratch_shapes=[pltpu.VMEM((2, two_f // chunk, d, chunk), BF16),
                        pltpu.VMEM((2, d, chunk), w_gu.dtype),
                        pltpu.SMEM((3,), I32),
                        pltpu.SemaphoreType.DMA((2,))])
    return pl.pallas_call(
        _moe_up_kernel, grid_spec=grid_spec,
        out_shape=SDS((n_rows, two_f // 2), BF16),
        compiler_params=_cparams(1), name="moe_up")(*tables, n_used, xs, b_gu, w_gu)


def _moe_down_kernel(be_ref, first_ref, rem_ref, nxt_ref, nu_ref, a_ref, b_ref, w_hbm, o_ref,
                     wb_ref, stage_ref, st_ref, sem):
    i = pl.program_id(0)

    @pl.when(i < nu_ref[0])
    def _():
        cur = _stream_weights(i, be_ref, first_ref, rem_ref, nxt_ref, w_hbm, wb_ref, stage_ref,
                              st_ref, sem)
        a = a_ref[...]
        chunk = wb_ref.shape[3]
        for n in range(wb_ref.shape[1]):
            lo, hi = n * chunk, (n + 1) * chunk
            o_ref[:, lo:hi] = _dot(a, wb_ref[cur, n]) + b_ref[:, lo:hi]

    @pl.when(i >= nu_ref[0])
    def _():
        o_ref[...] = jnp.zeros(o_ref.shape, o_ref.dtype)


def _moe_down(act, w_dn, b_dn, tables, n_used, tm):
    n_rows, f = act.shape
    d = w_dn.shape[2]
    chunk = min(W_CHUNK, d // 2)
    grid_spec = pltpu.PrefetchScalarGridSpec(
        num_scalar_prefetch=5,
        grid=(n_rows // tm,),
        in_specs=[pl.BlockSpec((tm, f), lambda i, be, fi, re, nx, nu: (jnp.minimum(i, nu[0] - 1), 0)),
                  pl.BlockSpec((None, 1, d), lambda i, be, fi, re, nx, nu: (be[i], 0, 0)),
                  pl.BlockSpec(memory_space=pl.ANY)],
        out_specs=pl.BlockSpec((tm, d), lambda i, *_: (i, 0)),
        scratch_shapes=[pltpu.VMEM((2, d // chunk, f, chunk), BF16),
                        pltpu.VMEM((2, f, chunk), w_dn.dtype),
                        pltpu.SMEM((3,), I32),
                        pltpu.SemaphoreType.DMA((2,))])
    return pl.pallas_call(
        _moe_down_kernel, grid_spec=grid_spec,
        out_shape=SDS((n_rows, d), F32),
        compiler_params=_cparams(1), name="moe_down")(*tables, n_used, act, b_dn, w_dn)


def _combine_kernel(dst_ref, x_ref, g_ref, ys_ref, o_ref, buf, sem, *, tc):
    i = pl.program_id(0)
    n = pl.num_programs(0) - 1

    @pl.when(i < n)
    def _():
        slot = i % 2
        for k in range(TOP_K):
            for r in range(tc):
                pltpu.make_async_copy(ys_ref.at[pl.ds(dst_ref[0, k, r], 1)],
                                      buf.at[slot, k, pl.ds(r, 1)],
                                      sem.at[slot]).start(priority=r % 2)

    @pl.when(i >= 1)
    def _():
        slot = (i - 1) % 2
        for k in range(TOP_K):
            pltpu.make_async_copy(ys_ref.at[pl.ds(0, tc)], buf.at[slot, k], sem.at[slot]).wait()
        acc = x_ref[...]
        g = g_ref[...]
        g = jnp.concatenate([g, jnp.zeros((8 - TOP_K, tc), F32)], axis=0).T
        for k in range(TOP_K):
            acc = acc + g[:, k:k + 1] * buf[slot, k]
        o_ref[...] = acc


def _combine(x2, gates, dest, ys):
    t, d = x2.shape
    tc = _tile(t, 128)
    n = t // tc
    dest3 = dest.reshape(TOP_K, n, tc).transpose(1, 0, 2)
    kern = functools.partial(_combine_kernel, tc=tc)
    prev = lambda i: (jnp.maximum(i - 1, 0), 0)
    return pl.pallas_call(
        kern,
        grid=(n + 1,),
        in_specs=[pl.BlockSpec((1, TOP_K, tc), lambda i: (jnp.minimum(i, n - 1), 0, 0),
                               memory_space=pltpu.SMEM),
                  pl.BlockSpec((tc, d), prev),
                  pl.BlockSpec((TOP_K, tc), lambda i: (0, jnp.maximum(i - 1, 0))),
                  pl.BlockSpec(memory_space=pl.ANY)],
        out_specs=pl.BlockSpec((tc, d), prev),
        out_shape=SDS((t, d), F32),
        scratch_shapes=[pltpu.VMEM((2, TOP_K, tc, d), F32),
                        pltpu.SemaphoreType.DMA((2,))],
        compiler_params=_cparams(1), name="moe_combine")(dest3, x2, gates, ys)


def _layer(x2d, mem2d, batch, seq, n_mem, p):
    t, d = x2d.shape
    c = p["conv_w"].shape[1]
    fw = p["g_fox_out"].shape[0]
    n_heads = fw // HEAD_DIM
    n_main = 3 * c + 3 * fw
    n_exp = p["w_router"].shape[1]

    w_in = p["w_mix_in"].astype(BF16)
    wf_t = jnp.zeros((F_ROWS, d), BF16).at[:n_heads].set(w_in[:, n_main:].T)
    b_col = jnp.zeros((F_ROWS, 1), F32).at[:n_heads, 0].set(p["b_forget"])

    proj, ft = _mix_in(x2d, p["g_mix"][None, :], wf_t, w_in, n_main)
    cdec = _decay(ft, b_col, seq)
    y_fox = _fox_attention(proj, cdec, p["g_q"][None, :], p["g_k"][None, :],
                           batch, seq, n_heads, 3 * c, fw)
    x1 = _mix_out(proj, y_fox, p["conv_w"], p["g_conv_out"][None, :], p["g_fox_out"][None, :],
                  p["w_mix_out"].astype(BF16), x2d, batch, seq)

    kn, vm = _memkv(mem2d, p["g_mem"][None, :], p["w_xkv"].astype(BF16),
                    p["g_xk"][None, :], n_mem)
    x2, hm, ti, tg, pos, cnt = _xattn(x1, p["g_xattn"][None, :], p["w_xq"].astype(BF16),
                            p["g_xq"][None, :], kn, vm, p["w_xo"].astype(BF16),
                            p["g_moe"][None, :], p["w_router"].T.astype(BF16),
                            p["b_router"][:, None], seq, n_mem)

    tm = _tile(TOP_K * t, 256)
    counts = cnt[:, 0].astype(I32)
    pad_start, pad_end, padded, blk, blk_exp, n_used = _route_tables(counts, TOP_K * t, tm)
    tables = (blk_exp,) + _stream_tables(blk_exp, pad_end, counts, tm)
    dest = _dest_rows(pad_start, ti, pos)
    xs = _dispatch(hm, dest, pad_end, padded, n_used, tm, blk.shape[0])
    act = _moe_up(xs, p["w_gate_up"], p["b_gate_up"][:, None, :], tables, n_used, tm)
    ys = _moe_down(act, p["w_down"], p["b_down"][:, None, :], tables, n_used, tm)
    return _combine(x2, tg, dest, ys)


def kernel(x, mem, g_mix, w_mix_in, b_forget, conv_w, g_q, g_k, g_conv_out, g_fox_out, w_mix_out, g_xattn, g_mem, w_xq, w_xkv, g_xq, g_xk, w_xo, g_moe, w_router, b_router, w_gate_up, b_gate_up, w_down, b_down):
    batch, seq, d = x.shape
    n_mem = mem.shape[1]
    params = dict(g_mix=g_mix, w_mix_in=w_mix_in, b_forget=b_forget, conv_w=conv_w, g_q=g_q,
                  g_k=g_k, g_conv_out=g_conv_out, g_fox_out=g_fox_out, w_mix_out=w_mix_out,
                  g_xattn=g_xattn, g_mem=g_mem, w_xq=w_xq, w_xkv=w_xkv, g_xq=g_xq, g_xk=g_xk,
                  w_xo=w_xo, g_moe=g_moe, w_router=w_router, b_router=b_router,
                  w_gate_up=w_gate_up, b_gate_up=b_gate_up, w_down=w_down, b_down=b_down)
    x2d = x.reshape(batch * seq, d)
    mem2d = mem.reshape(batch * n_mem, d)
    for l in range(g_mix.shape[0]):
        x2d = _layer(x2d, mem2d, batch, seq, n_mem, {k: v[l] for k, v in params.items()})
    return x2d.reshape(batch, seq, d)
```

```python
import functools

import jax
import jax.numpy as jnp
from jax import lax
from jax.experimental import pallas as pl
from jax.experimental.pallas import tpu as pltpu

F32 = jnp.float32
BF16 = jnp.bfloat16
I32 = jnp.int32
U32 = jnp.uint32
SDS = jax.ShapeDtypeStruct

EPS = 1e-6
HEAD_DIM = 128
TOP_K = 4
SWIGLU_LIMIT = 7.0
SWIGLU_ALPHA = 1.702
MASK_VALUE = float(jnp.finfo(jnp.float32).min)
V7X_VMEM_LIMIT_BYTES = 56 * 1024 * 1024
F_ROWS = 16
LANES = 128


def _cparams(n_axes):
    return pltpu.CompilerParams(
        dimension_semantics=("arbitrary",) * n_axes,
        vmem_limit_bytes=V7X_VMEM_LIMIT_BYTES)


def _rms(x, g):
    return x * lax.rsqrt(jnp.mean(x * x, axis=-1, keepdims=True) + EPS) * g


def _dot(a, b):
    return jnp.dot(a, b, preferred_element_type=F32)


def _dot_nt(a, b):
    return lax.dot_general(a, b, (((1,), (1,)), ((), ())), preferred_element_type=F32)


def _tile(n, pref):
    return pref if n % pref == 0 else n


def _pack_pair(lo, hi):
    lo = lax.bitcast_convert_type(lo.astype(BF16).astype(F32), U32) >> 16
    hi = lax.bitcast_convert_type(hi.astype(BF16).astype(F32), U32) & jnp.uint32(0xFFFF0000)
    return hi | lo


def _unpack_pair(w):
    lo = lax.bitcast_convert_type(w << 16, F32)
    hi = lax.bitcast_convert_type(w & jnp.uint32(0xFFFF0000), F32)
    return lo, hi


def _mix_in_kernel(x_ref, g_ref, wf_ref, w_ref, o_ref, ft_ref, h_ref):
    @pl.when(pl.program_id(1) == 0)
    def _():
        hb = _rms(x_ref[...], g_ref[...]).astype(BF16)
        h_ref[...] = hb
        ft_ref[...] = _dot_nt(wf_ref[...], hb)

    o_ref[...] = _dot(h_ref[...], w_ref[...]).astype(o_ref.dtype)


def _mix_in(x2d, g, wf_t, w_in, n_out):
    t, d = x2d.shape
    tm = _tile(t, 1024)
    tn = _tile(n_out, 1024)
    return pl.pallas_call(
        _mix_in_kernel,
        grid=(t // tm, n_out // tn),
        in_specs=[pl.BlockSpec((tm, d), lambda i, j: (i, 0)),
                  pl.BlockSpec((1, d), lambda i, j: (0, 0)),
                  pl.BlockSpec((F_ROWS, d), lambda i, j: (0, 0)),
                  pl.BlockSpec((d, tn), lambda i, j: (0, j))],
        out_specs=[pl.BlockSpec((tm, tn), lambda i, j: (i, j)),
                   pl.BlockSpec((F_ROWS, tm), lambda i, j: (0, i))],
        out_shape=[SDS((t, n_out), BF16), SDS((F_ROWS, t), F32)],
        scratch_shapes=[pltpu.VMEM((tm, d), BF16)],
        compiler_params=_cparams(2), name="mix_in")(x2d, g, wf_t, w_in)


def _decay_kernel(ft_ref, b_ref, c_ref):
    z = ft_ref[...] + b_ref[...]
    lf = jnp.minimum(z, 0.0) - jnp.log(1.0 + jnp.exp(-jnp.abs(z)))
    s = lf.shape[1]
    lane = lax.broadcasted_iota(I32, lf.shape, 1)
    d = 1
    while d < s:
        lf = lf + jnp.where(lane >= d, pltpu.roll(lf, d, 1), 0.0)
        d *= 2
    c_ref[...] = lf


def _decay(ft, b_col, seq):
    rows, t = ft.shape
    return pl.pallas_call(
        _decay_kernel,
        grid=(t // seq,),
        in_specs=[pl.BlockSpec((rows, seq), lambda b: (0, b)),
                  pl.BlockSpec((rows, 1), lambda b: (0, 0))],
        out_specs=pl.BlockSpec((rows, seq), lambda b: (0, b)),
        out_shape=SDS((rows, t), F32),
        compiler_params=_cparams(1), name="decay")(ft, b_col)


def _fox_kernel(q_ref, k_ref, v_ref, c_ref, gq_ref, gk_ref, o_ref, kn_ref, nc_ref, *, tq):
    h = pl.program_id(1)
    seq = q_ref.shape[0]
    scale = HEAD_DIM ** -0.5
    kn_ref[...] = _rms(k_ref[...].astype(F32), gk_ref[...]).astype(BF16)
    nc_ref[...] = -c_ref[pl.ds(h, 1), :]
    row = lax.broadcasted_iota(I32, (tq, tq), 0)
    col = lax.broadcasted_iota(I32, (tq, tq), 1)
    causal = col <= row
    for qi in range(seq // tq):
        lo, hi = qi * tq, (qi + 1) * tq
        q = (_rms(q_ref[lo:hi, :].astype(F32), gq_ref[...]) * scale).astype(BF16)
        s_diag = jnp.where(causal, _dot_nt(q, kn_ref[lo:hi, :]) + nc_ref[:, lo:hi], MASK_VALUE)
        m = jnp.max(s_diag, axis=-1, keepdims=True)
        if qi > 0:
            s_past = _dot_nt(q, kn_ref[:lo, :]) + nc_ref[:, :lo]
            m = jnp.maximum(m, jnp.max(s_past, axis=-1, keepdims=True))
        p_diag = jnp.exp(s_diag - m)
        l = jnp.sum(p_diag, axis=-1, keepdims=True)
        acc = _dot(p_diag.astype(BF16), v_ref[lo:hi, :])
        if qi > 0:
            p_past = jnp.exp(s_past - m)
            l = l + jnp.sum(p_past, axis=-1, keepdims=True)
            acc = acc + _dot(p_past.astype(BF16), v_ref[:lo, :])
        o_ref[lo:hi, :] = (acc / l).astype(o_ref.dtype)


def _fox_attention(proj, c, g_q, g_k, batch, seq, n_heads, col0, width):
    t = proj.shape[0]
    tq = _tile(seq, 256)
    cb = col0 // HEAD_DIM
    wb = width // HEAD_DIM
    hd = HEAD_DIM
    kern = functools.partial(_fox_kernel, tq=tq)
    return pl.pallas_call(
        kern,
        grid=(batch, n_heads),
        in_specs=[pl.BlockSpec((seq, hd), lambda b, h: (b, cb + h)),
                  pl.BlockSpec((seq, hd), lambda b, h: (b, cb + wb + h)),
                  pl.BlockSpec((seq, hd), lambda b, h: (b, cb + 2 * wb + h)),
                  pl.BlockSpec((F_ROWS, seq), lambda b, h: (0, b)),
                  pl.BlockSpec((1, hd), lambda b, h: (0, 0)),
                  pl.BlockSpec((1, hd), lambda b, h: (0, 0))],
        out_specs=pl.BlockSpec((seq, hd), lambda b, h: (b, h)),
        out_shape=SDS((t, width), BF16),
        scratch_shapes=[pltpu.VMEM((seq, hd), BF16), pltpu.VMEM((1, seq), F32)],
        compiler_params=_cparams(2), name="fox_attention")(proj, proj, proj, c, g_q, g_k)


def _mix_out_kernel(ub_ref, uc_ref, ux_ref, hc_ref, hx_ref, yf_ref, cw_ref, gc_ref, gf_ref,
                    w_ref, x_ref, o_ref):
    si = pl.program_id(1)
    p = uc_ref[...].astype(F32) * ux_ref[...].astype(F32)
    halo = hc_ref[...].astype(F32) * hx_ref[...].astype(F32)
    halo = jnp.where(si > 0, halo, 0.0)
    row = lax.broadcasted_iota(I32, p.shape, 0)
    p1 = jnp.where(row == 0, halo[7:8, :], pltpu.roll(p, 1, 0))
    p2 = jnp.where(row == 0, halo[6:7, :],
                   jnp.where(row == 1, halo[7:8, :], pltpu.roll(p, 2, 0)))
    w = cw_ref[...]
    y = ub_ref[...].astype(F32) * (w[0:1, :] * p2 + w[1:2, :] * p1 + w[2:3, :] * p)
    c = y.shape[1]
    y_conv = _rms(y, gc_ref[...]).astype(BF16)
    y_fox = _rms(yf_ref[...].astype(F32), gf_ref[...]).astype(BF16)
    o_ref[...] = x_ref[...] + _dot(y_conv, w_ref[:c, :]) + _dot(y_fox, w_ref[c:, :])


def _mix_out(proj, y_fox, conv_w, g_conv, g_fox, w_out, x2d, batch, seq):
    t, d = x2d.shape
    c = conv_w.shape[1]
    w = y_fox.shape[1]
    ts = _tile(seq, 512)
    ns = seq // ts
    hb = ts // 8
    tile = lambda col: (lambda b, s: (b * ns + s, col))
    halo = lambda col: (lambda b, s: (jnp.maximum((b * ns + s) * hb - 1, 0), col))
    const = lambda b, s: (0, 0)
    return pl.pallas_call(
        _mix_out_kernel,
        grid=(batch, ns),
        in_specs=[pl.BlockSpec((ts, c), tile(0)),
                  pl.BlockSpec((ts, c), tile(1)),
                  pl.BlockSpec((ts, c), tile(2)),
                  pl.BlockSpec((8, c), halo(1)),
                  pl.BlockSpec((8, c), halo(2)),
                  pl.BlockSpec((ts, w), tile(0)),
                  pl.BlockSpec((3, c), const),
                  pl.BlockSpec((1, c), const),
                  pl.BlockSpec((1, w), const),
                  pl.BlockSpec((c + w, d), const, pipeline_mode=pl.Buffered(1)),
                  pl.BlockSpec((ts, d), tile(0))],
        out_specs=pl.BlockSpec((ts, d), tile(0)),
        out_shape=SDS((t, d), F32),
        compiler_params=_cparams(2), name="mix_out")(
            proj, proj, proj, proj, proj, y_fox, conv_w, g_conv, g_fox, w_out, x2d)


def _memkv_kernel(mem_ref, g_ref, w_ref, gk_ref, k_ref, v_ref):
    hm = _rms(mem_ref[...], g_ref[...]).astype(BF16)
    kv = _dot(hm, w_ref[...])
    xw = k_ref.shape[1]
    for h in range(xw // HEAD_DIM):
        sl = slice(h * HEAD_DIM, (h + 1) * HEAD_DIM)
        k_ref[:, sl] = _rms(kv[:, sl], gk_ref[...]).astype(BF16)
    v_ref[...] = kv[:, xw:].astype(BF16)


def _memkv(mem2d, g_mem, w_xkv, g_xk, n_mem):
    tm_, d = mem2d.shape
    xw = w_xkv.shape[1] // 2
    return pl.pallas_call(
        _memkv_kernel,
        grid=(tm_ // n_mem,),
        in_specs=[pl.BlockSpec((n_mem, d), lambda b: (b, 0)),
                  pl.BlockSpec((1, d), lambda b: (0, 0)),
                  pl.BlockSpec((d, 2 * xw), lambda b: (0, 0)),
                  pl.BlockSpec((1, HEAD_DIM), lambda b: (0, 0))],
        out_specs=[pl.BlockSpec((n_mem, xw), lambda b: (b, 0)),
                   pl.BlockSpec((n_mem, xw), lambda b: (b, 0))],
        out_shape=[SDS((tm_, xw), BF16), SDS((tm_, xw), BF16)],
        compiler_params=_cparams(1), name="memkv")(mem2d, g_mem, w_xkv, g_xk)


def _xattn_kernel(x_ref, gx_ref, wq_ref, gq_ref, kn_ref, v_ref, wo_ref, gm_ref, wr_ref, br_ref,
                  x2_ref, hm_ref, ti_ref, tg_ref, pos_ref, cnt_ref, run_ref, tri_ref):
    i = pl.program_id(0)
    tm = x_ref.shape[0]

    @pl.when(i == 0)
    def _():
        run_ref[...] = jnp.zeros(run_ref.shape, F32)
        r = lax.broadcasted_iota(I32, (tm, tm), 0)
        c = lax.broadcasted_iota(I32, (tm, tm), 1)
        tri_ref[...] = jnp.where(r <= c, 1.0, 0.0).astype(BF16)

    x = x_ref[...]
    hb = _rms(x, gx_ref[...]).astype(BF16)
    q = _dot(hb, wq_ref[...])
    scale = HEAD_DIM ** -0.5
    outs = []
    for h in range(q.shape[1] // HEAD_DIM):
        sl = slice(h * HEAD_DIM, (h + 1) * HEAD_DIM)
        qh = (_rms(q[:, sl], gq_ref[...]) * scale).astype(BF16)
        s = _dot_nt(qh, kn_ref[:, sl])
        p = jnp.exp(s - jnp.max(s, axis=-1, keepdims=True))
        oh = _dot(p.astype(BF16), v_ref[:, sl]) / jnp.sum(p, axis=-1, keepdims=True)
        outs.append(oh.astype(BF16))
    x2 = x + _dot(jnp.concatenate(outs, axis=-1), wo_ref[...])
    x2_ref[...] = x2
    hm = _rms(x2, gm_ref[...])
    hm_ref[...] = hm
    lg = _dot_nt(wr_ref[...], hm.astype(BF16)) + br_ref[...]
    n_exp = lg.shape[0]
    eidx = lax.broadcasted_iota(I32, lg.shape, 0).astype(F32)
    vals, sel = [], []
    for r in range(TOP_K):
        mx = jnp.max(lg, axis=0, keepdims=True)
        am = jnp.min(jnp.where(lg == mx, eidx, float(n_exp)), axis=0, keepdims=True)
        ti_ref[r:r + 1, :] = am.astype(I32)
        vals.append(mx)
        sel.append(eidx == am)
        lg = jnp.where(sel[r], -jnp.inf, lg)
    ex = [jnp.exp(v - vals[0]) for v in vals]
    den = ex[0]
    for e in ex[1:]:
        den = den + e
    for r in range(TOP_K):
        tg_ref[r:r + 1, :] = ex[r] / den
    onehot = jnp.zeros(lg.shape, F32)
    for r in range(TOP_K):
        onehot = onehot + jnp.where(sel[r], 1.0, 0.0)
    incl = _dot(onehot.astype(BF16), tri_ref[...])
    excl = incl - onehot + run_ref[...]
    for r in range(TOP_K):
        pos_ref[r:r + 1, :] = jnp.sum(jnp.where(sel[r], excl, 0.0), axis=0,
                                      keepdims=True).astype(I32)
    run = run_ref[...] + incl[:, tm - 1:tm]
    run_ref[...] = run
    cnt_ref[...] = jnp.broadcast_to(run, cnt_ref.shape)


def _xattn(x1, g_xattn, w_xq, g_xq, kn, vm, w_xo, g_moe, w_rt, b_r, seq, n_mem):
    t, d = x1.shape
    xw = w_xq.shape[1]
    n_exp = w_rt.shape[0]
    tm = _tile(seq, 512)
    per = seq // tm
    const = lambda i: (0, 0)
    return pl.pallas_call(
        _xattn_kernel,
        grid=(t // tm,),
        in_specs=[pl.BlockSpec((tm, d), lambda i: (i, 0)),
                  pl.BlockSpec((1, d), const),
                  pl.BlockSpec((d, xw), const),
                  pl.BlockSpec((1, HEAD_DIM), const),
                  pl.BlockSpec((n_mem, xw), lambda i: (i // per, 0)),
                  pl.BlockSpec((n_mem, xw), lambda i: (i // per, 0)),
                  pl.BlockSpec((xw, d), const),
                  pl.BlockSpec((1, d), const),
                  pl.BlockSpec((n_exp, d), const),
                  pl.BlockSpec((n_exp, 1), const)],
        out_specs=[pl.BlockSpec((tm, d), lambda i: (i, 0)),
                   pl.BlockSpec((tm, d), lambda i: (i, 0)),
                   pl.BlockSpec((TOP_K, tm), lambda i: (0, i)),
                   pl.BlockSpec((TOP_K, tm), lambda i: (0, i)),
                   pl.BlockSpec((TOP_K, tm), lambda i: (0, i)),
                   pl.BlockSpec((n_exp, 128), const)],
        out_shape=[SDS((t, d), F32), SDS((t, d), F32),
                   SDS((TOP_K, t), I32), SDS((TOP_K, t), F32), SDS((TOP_K, t), I32),
                   SDS((n_exp, 128), F32)],
        scratch_shapes=[pltpu.VMEM((n_exp, 1), F32), pltpu.VMEM((tm, tm), BF16)],
        compiler_params=_cparams(1), name="xattn_router")(
            x1, g_xattn, w_xq, g_xq, kn, vm, w_xo, g_moe, w_rt, b_r)


def _route_tables(counts, n_assign, tm):
    n_exp = counts.shape[0]
    padded = (counts + tm - 1) // tm * tm
    pad_end = jnp.cumsum(padded).astype(I32)
    pad_start = pad_end - padded
    n_blk = n_assign // tm + n_exp
    n_used = pad_end[-1] // tm
    blk = jnp.minimum(jnp.arange(n_blk, dtype=I32), n_used - 1)
    blk_exp = jnp.sum(((blk * tm)[:, None] >= pad_end[None, :]).astype(I32), axis=1)
    blk_exp = jnp.minimum(blk_exp, n_exp - 1)
    return pad_start, pad_end, padded.astype(I32), blk, blk_exp, n_used.reshape(1).astype(I32)


def _dest_kernel(ps_ref, ti_ref, pos_ref, o_ref, *, n_exp):
    ti = ti_ref[...]
    acc = pos_ref[...]
    for e in range(n_exp):
        acc = acc + jnp.where(ti == e, ps_ref[e], 0)
    o_ref[...] = acc


def _dest_rows(pad_start, ti, pos):
    k, t = ti.shape
    kern = functools.partial(_dest_kernel, n_exp=pad_start.shape[0])
    grid_spec = pltpu.PrefetchScalarGridSpec(
        num_scalar_prefetch=1, grid=(1,),
        in_specs=[pl.BlockSpec((k, t), lambda i, ps: (0, 0)),
                  pl.BlockSpec((k, t), lambda i, ps: (0, 0))],
        out_specs=pl.BlockSpec((k, t), lambda i, ps: (0, 0)))
    return pl.pallas_call(kern, grid_spec=grid_spec, out_shape=SDS((k, t), I32),
                          compiler_params=_cparams(1), name="moe_dest")(pad_start, ti, pos)


def _dispatch_kernel(pe_ref, pc_ref, nu_ref, dst_ref, x_ref, xs_ref, buf, zbuf, sem, zsem,
                     *, tc, tm, n_exp, n_blk):
    i = pl.program_id(0)
    n = pl.num_programs(0)
    ns = x_ref.shape[1] // (2 * LANES)

    def zero_copy(row0):
        rows = pl.ds(pl.multiple_of(row0 * ns, tm * ns), tm * ns)
        return pltpu.make_async_copy(zbuf, xs_ref.at[rows], zsem)

    @pl.when(i == 0)
    def _():
        zbuf[...] = jnp.zeros(zbuf.shape, zbuf.dtype)
        for start in (True, False):
            for e in range(n_exp):
                @pl.when(pc_ref[e] > 0)
                def _(e=e):
                    cp = zero_copy(pe_ref[e] - tm)
                    cp.start() if start else cp.wait()

            def tail(b, c):
                cp = zero_copy(b * tm)
                cp.start() if start else cp.wait()
                return c
            lax.fori_loop(nu_ref[0], n_blk, tail, 0)

    slot = i % 2
    x = x_ref[...]
    for s in range(ns):
        lo = x[:, (2 * s) * LANES:(2 * s + 1) * LANES]
        hi = x[:, (2 * s + 1) * LANES:(2 * s + 2) * LANES]
        buf[slot, pl.ds(s, tc, stride=ns), :] = _pack_pair(lo, hi)
    for k in range(TOP_K):
        for r in range(tc):
            dst = pl.ds(pl.multiple_of(dst_ref[0, k, r] * ns, ns), ns)
            pltpu.make_async_copy(buf.at[slot, pl.ds(r * ns, ns)], xs_ref.at[dst],
                                  sem.at[slot]).start(priority=r % 2)

    def wait_slot(s):
        for k in range(TOP_K):
            pltpu.make_async_copy(buf.at[s], xs_ref.at[pl.ds(0, tc * ns)], sem.at[s]).wait()

    @pl.when(i >= 1)
    def _():
        wait_slot(1 - slot)

    @pl.when(i == n - 1)
    def _():
        wait_slot(slot)


def _dispatch(hm, dest, pad_end, padded, n_used, tm, n_blk):
    t, d = hm.shape
    tc = _tile(t, 128)
    n = t // tc
    ns = d // (2 * LANES)
    n_exp = pad_end.shape[0]
    dest3 = dest.reshape(TOP_K, n, tc).transpose(1, 0, 2)
    kern = functools.partial(_dispatch_kernel, tc=tc, tm=tm, n_exp=n_exp, n_blk=n_blk)
    grid_spec = pltpu.PrefetchScalarGridSpec(
        num_scalar_prefetch=3, grid=(n,),
        in_specs=[pl.BlockSpec((1, TOP_K, tc), lambda i, *_: (i, 0, 0), memory_space=pltpu.SMEM),
                  pl.BlockSpec((tc, d), lambda i, *_: (i, 0))],
        out_specs=pl.BlockSpec(memory_space=pl.ANY),
        scratch_shapes=[pltpu.VMEM((2, tc * ns, LANES), U32),
                        pltpu.VMEM((tm * ns, LANES), U32),
                        pltpu.SemaphoreType.DMA((2,)),
                        pltpu.SemaphoreType.DMA(())])
    return pl.pallas_call(
        kern, grid_spec=grid_spec,
        out_shape=SDS((n_blk * tm * ns, LANES), U32),
        compiler_params=_cparams(1), name="moe_dispatch")(pad_end, padded, n_used, dest3, hm)


W_CHUNK = 512


def _stream_tables(blk_exp, pad_end, counts, tm):
    n_blk = blk_exp.shape[0]
    n_exp = counts.shape[0]
    prev = jnp.concatenate([jnp.full((1,), -1, I32), blk_exp[:-1]])
    first = (blk_exp != prev).astype(I32)
    rem = pad_end[blk_exp] // tm - jnp.arange(n_blk, dtype=I32)
    cand = jnp.where(counts > 0, jnp.arange(n_exp, dtype=I32), n_exp)
    after = jnp.concatenate([lax.cummin(cand, reverse=True)[1:], jnp.full((1,), n_exp, I32)])
    nxt = jnp.where(after >= n_exp, -1, after)[blk_exp]
    return first, rem.astype(I32), nxt.astype(I32)


def _stream_weights(i, be_ref, first_ref, rem_ref, nxt_ref, w_hbm, wb_ref, stage_ref, st_ref, sem):
    n_chunks, _, chunk = wb_ref.shape[1:]

    def chunk_copy(e, c, slot):
        cols = pl.ds(pl.multiple_of(c * chunk, chunk), chunk)
        return pltpu.make_async_copy(w_hbm.at[e, :, cols], stage_ref.at[slot], sem.at[slot])

    def start_first_two(e):
        chunk_copy(e, 0, 0).start()
        chunk_copy(e, 1, 1).start()

    def cast_chunks(e, dst, done, issued, k):
        def body(_, carry):
            done, issued = carry
            slot = done % 2
            chunk_copy(e, done, slot).wait()
            wb_ref[dst, done] = stage_ref[slot].astype(BF16)

            @pl.when(issued < n_chunks)
            def _():
                chunk_copy(e, issued, slot).start()
            return done + 1, jnp.minimum(issued + 1, n_chunks)
        return lax.fori_loop(0, k, body, (done, issued))

    @pl.when(i == 0)
    def _():
        start_first_two(be_ref[0])
        cast_chunks(be_ref[0], 0, 0, 2, n_chunks)
        st_ref[0] = 1

    @pl.when(first_ref[i] == 1)
    def _():
        st_ref[0] = 1 - st_ref[0]
        st_ref[1] = 0
        st_ref[2] = 0

        @pl.when(nxt_ref[i] >= 0)
        def _():
            start_first_two(nxt_ref[i])
            st_ref[2] = 2

    cur = st_ref[0]

    @pl.when(nxt_ref[i] >= 0)
    def _():
        done = st_ref[1]
        rem = rem_ref[i]
        k = lax.div(n_chunks - done + rem - 1, rem)
        done, issued = cast_chunks(nxt_ref[i], 1 - cur, done, st_ref[2], k)
        st_ref[1] = done
        st_ref[2] = issued

    return cur


def _moe_up_kernel(be_ref, first_ref, rem_ref, nxt_ref, nu_ref, x_ref, b_ref, w_hbm, o_ref,
                   wb_ref, stage_ref, st_ref, sem):
    i = pl.program_id(0)

    @pl.when(i < nu_ref[0])
    def _():
        cur = _stream_weights(i, be_ref, first_ref, rem_ref, nxt_ref, w_hbm, wb_ref, stage_ref,
                              st_ref, sem)
        tm = o_ref.shape[0]
        ns = x_ref.shape[0] // tm
        cols = []
        for s in range(ns):
            lo, hi = _unpack_pair(x_ref[pl.ds(s, tm, stride=ns), :])
            cols += [lo.astype(BF16), hi.astype(BF16)]
        x = jnp.concatenate(cols, axis=1)
        n_half = wb_ref.shape[1] // 2
        chunk = wb_ref.shape[3]
        f = n_half * chunk
        for n in range(n_half):
            lo, hi = n * chunk, (n + 1) * chunk
            gate = jnp.minimum(_dot(x, wb_ref[cur, n]) + b_ref[:, lo:hi], SWIGLU_LIMIT)
            up = jnp.clip(_dot(x, wb_ref[cur, n_half + n]) + b_ref[:, f + lo:f + hi],
                          -SWIGLU_LIMIT, SWIGLU_LIMIT)
            glu = gate * jax.nn.sigmoid(SWIGLU_ALPHA * gate)
            o_ref[:, lo:hi] = ((up + 1.0) * glu).astype(o_ref.dtype)

    @pl.when(i >= nu_ref[0])
    def _():
        o_ref[...] = jnp.zeros(o_ref.shape, o_ref.dtype)


def _moe_up(xs, w_gu, b_gu, tables, n_used, tm):
    d = w_gu.shape[1]
    ns = d // (2 * LANES)
    n_rows = xs.shape[0] // ns
    two_f = w_gu.shape[2]
    chunk = min(W_CHUNK, two_f // 2)
    grid_spec = pltpu.PrefetchScalarGridSpec(
        num_scalar_prefetch=5,
        grid=(n_rows // tm,),
        in_specs=[pl.BlockSpec((tm * ns, LANES),
                               lambda i, be, fi, re, nx, nu: (jnp.minimum(i, nu[0] - 1), 0)),
                  pl.BlockSpec((None, 1, two_f), lambda i, be, fi, re, nx, nu: (be[i], 0, 0)),
                  pl.BlockSpec(memory_space=pl.ANY)],
        out_specs=pl.BlockSpec((tm, two_f // 2), lambda i, *_: (i, 0)),
        scratch_shapes=[pltpu.VMEM((2, two_f // chunk, d, chunk), BF16),
                        pltpu.VMEM((2, d, chunk), w_gu.dtype),
                        pltpu.SMEM((3,), I32),
                        pltpu.SemaphoreType.DMA((2,))])
    return pl.pallas_call(
        _moe_up_kernel, grid_spec=grid_spec,
        out_shape=SDS((n_rows, two_f // 2), BF16),
        compiler_params=_cparams(1), name="moe_up")(*tables, n_used, xs, b_gu, w_gu)


def _moe_down_kernel(be_ref, first_ref, rem_ref, nxt_ref, nu_ref, a_ref, b_ref, w_hbm, o_ref,
                     wb_ref, stage_ref, st_ref, sem):
    i = pl.program_id(0)

    @pl.when(i < nu_ref[0])
    def _():
        cur = _stream_weights(i, be_ref, first_ref, rem_ref, nxt_ref, w_hbm, wb_ref, stage_ref,
                              st_ref, sem)
        a = a_ref[...]
        tm = a.shape[0]
        ns = o_ref.shape[0] // tm
        chunk = wb_ref.shape[3]
        for n in range(wb_ref.shape[1]):
            lo, hi = n * chunk, (n + 1) * chunk
            y = _dot(a, wb_ref[cur, n]) + b_ref[:, lo:hi]
            for j in range(chunk // (2 * LANES)):
                s = lo // (2 * LANES) + j
                w = _pack_pair(y[:, (2 * j) * LANES:(2 * j + 1) * LANES],
                               y[:, (2 * j + 1) * LANES:(2 * j + 2) * LANES])
                o_ref[pl.ds(s, tm, stride=ns), :] = w

    @pl.when(i >= nu_ref[0])
    def _():
        o_ref[...] = jnp.zeros(o_ref.shape, o_ref.dtype)


def _moe_down(act, w_dn, b_dn, tables, n_used, tm):
    n_rows, f = act.shape
    d = w_dn.shape[2]
    chunk = max(min(W_CHUNK, d // 2), 2 * LANES)
    grid_spec = pltpu.PrefetchScalarGridSpec(
        num_scalar_prefetch=5,
        grid=(n_rows // tm,),
        in_specs=[pl.BlockSpec((tm, f), lambda i, be, fi, re, nx, nu: (jnp.minimum(i, nu[0] - 1), 0)),
                  pl.BlockSpec((None, 1, d), lambda i, be, fi, re, nx, nu: (be[i], 0, 0)),
                  pl.BlockSpec(memory_space=pl.ANY)],
        out_specs=pl.BlockSpec((tm * (d // (2 * LANES)), LANES), lambda i, *_: (i, 0)),
        scratch_shapes=[pltpu.VMEM((2, d // chunk, f, chunk), BF16),
                        pltpu.VMEM((2, f, chunk), w_dn.dtype),
                        pltpu.SMEM((3,), I32),
                        pltpu.SemaphoreType.DMA((2,))])
    return pl.pallas_call(
        _moe_down_kernel, grid_spec=grid_spec,
        out_shape=SDS((n_rows * (d // (2 * LANES)), LANES), U32),
        compiler_params=_cparams(1), name="moe_down")(*tables, n_used, act, b_dn, w_dn)


def _combine_kernel(dst_ref, x_ref, g_ref, ys_ref, o_ref, buf, sem, *, tc):
    i = pl.program_id(0)
    n = pl.num_programs(0) - 1
    ns = x_ref.shape[1] // (2 * LANES)

    @pl.when(i < n)
    def _():
        slot = i % 2
        for k in range(TOP_K):
            for r in range(tc):
                src = pl.ds(pl.multiple_of(dst_ref[0, k, r] * ns, ns), ns)
                pltpu.make_async_copy(ys_ref.at[src], buf.at[slot, k, pl.ds(r * ns, ns)],
                                      sem.at[slot]).start(priority=r % 2)

    @pl.when(i >= 1)
    def _():
        slot = (i - 1) % 2
        for k in range(TOP_K):
            pltpu.make_async_copy(ys_ref.at[pl.ds(0, tc * ns)], buf.at[slot, k],
                                  sem.at[slot]).wait()
        g = g_ref[...]
        g = jnp.concatenate([g, jnp.zeros((8 - TOP_K, tc), F32)], axis=0).T
        for s in range(ns):
            c0, c1, c2 = (2 * s) * LANES, (2 * s + 1) * LANES, (2 * s + 2) * LANES
            acc_lo = x_ref[:, c0:c1]
            acc_hi = x_ref[:, c1:c2]
            for k in range(TOP_K):
                lo, hi = _unpack_pair(buf[slot, k, pl.ds(s, tc, stride=ns), :])
                acc_lo = acc_lo + g[:, k:k + 1] * lo
                acc_hi = acc_hi + g[:, k:k + 1] * hi
            o_ref[:, c0:c1] = acc_lo
            o_ref[:, c1:c2] = acc_hi


def _combine(x2, gates, dest, ys):
    t, d = x2.shape
    tc = _tile(t, 128)
    n = t // tc
    ns = d // (2 * LANES)
    dest3 = dest.reshape(TOP_K, n, tc).transpose(1, 0, 2)
    kern = functools.partial(_combine_kernel, tc=tc)
    prev = lambda i: (jnp.maximum(i - 1, 0), 0)
    return pl.pallas_call(
        kern,
        grid=(n + 1,),
        in_specs=[pl.BlockSpec((1, TOP_K, tc), lambda i: (jnp.minimum(i, n - 1), 0, 0),
                               memory_space=pltpu.SMEM),
                  pl.BlockSpec((tc, d), prev),
                  pl.BlockSpec((TOP_K, tc), lambda i: (0, jnp.maximum(i - 1, 0))),
                  pl.BlockSpec(memory_space=pl.ANY)],
        out_specs=pl.BlockSpec((tc, d), prev),
        out_shape=SDS((t, d), F32),
        scratch_shapes=[pltpu.VMEM((2, TOP_K, tc * ns, LANES), U32),
                        pltpu.SemaphoreType.DMA((2,))],
        compiler_params=_cparams(1), name="moe_combine")(dest3, x2, gates, ys)


def _layer(x2d, mem2d, batch, seq, n_mem, p):
    t, d = x2d.shape
    c = p["conv_w"].shape[1]
    fw = p["g_fox_out"].shape[0]
    n_heads = fw // HEAD_DIM
    n_main = 3 * c + 3 * fw
    n_exp = p["w_router"].shape[1]

    w_in = p["w_mix_in"].astype(BF16)
    wf_t = jnp.zeros((F_ROWS, d), BF16).at[:n_heads].set(w_in[:, n_main:].T)
    b_col = jnp.zeros((F_ROWS, 1), F32).at[:n_heads, 0].set(p["b_forget"])

    proj, ft = _mix_in(x2d, p["g_mix"][None, :], wf_t, w_in, n_main)
    cdec = _decay(ft, b_col, seq)
    y_fox = _fox_attention(proj, cdec, p["g_q"][None, :], p["g_k"][None, :],
                           batch, seq, n_heads, 3 * c, fw)
    x1 = _mix_out(proj, y_fox, p["conv_w"], p["g_conv_out"][None, :], p["g_fox_out"][None, :],
                  p["w_mix_out"].astype(BF16), x2d, batch, seq)

    kn, vm = _memkv(mem2d, p["g_mem"][None, :], p["w_xkv"].astype(BF16),
                    p["g_xk"][None, :], n_mem)
    x2, hm, ti, tg, pos, cnt = _xattn(x1, p["g_xattn"][None, :], p["w_xq"].astype(BF16),
                            p["g_xq"][None, :], kn, vm, p["w_xo"].astype(BF16),
                            p["g_moe"][None, :], p["w_router"].T.astype(BF16),
                            p["b_router"][:, None], seq, n_mem)

    tm = _tile(TOP_K * t, 256)
    counts = cnt[:, 0].astype(I32)
    pad_start, pad_end, padded, blk, blk_exp, n_used = _route_tables(counts, TOP_K * t, tm)
    tables = (blk_exp,) + _stream_tables(blk_exp, pad_end, counts, tm)
    dest = _dest_rows(pad_start, ti, pos)
    xs = _dispatch(hm, dest, pad_end, padded, n_used, tm, blk.shape[0])
    act = _moe_up(xs, p["w_gate_up"], p["b_gate_up"][:, None, :], tables, n_used, tm)
    ys = _moe_down(act, p["w_down"], p["b_down"][:, None, :], tables, n_used, tm)
    return _combine(x2, tg, dest, ys)


def kernel(x, mem, g_mix, w_mix_in, b_forget, conv_w, g_q, g_k, g_conv_out, g_fox_out, w_mix_out, g_xattn, g_mem, w_xq, w_xkv, g_xq, g_xk, w_xo, g_moe, w_router, b_router, w_gate_up, b_gate_up, w_down, b_down):
    batch, seq, d = x.shape
    n_mem = mem.shape[1]
    params = dict(g_mix=g_mix, w_mix_in=w_mix_in, b_forget=b_forget, conv_w=conv_w, g_q=g_q,
                  g_k=g_k, g_conv_out=g_conv_out, g_fox_out=g_fox_out, w_mix_out=w_mix_out,
                  g_xattn=g_xattn, g_mem=g_mem, w_xq=w_xq, w_xkv=w_xkv, g_xq=g_xq, g_xk=g_xk,
                  w_xo=w_xo, g_moe=g_moe, w_router=w_router, b_router=b_router,
                  w_gate_up=w_gate_up, b_gate_up=b_gate_up, w_down=w_down, b_down=b_down)
    x2d = x.reshape(batch * seq, d)
    mem2d = mem.reshape(batch * n_mem, d)
    for l in range(g_mix.shape[0]):
        x2d = _layer(x2d, mem2d, batch, seq, n_mem, {k: v[l] for k, v in params.items()})
    return x2d.reshape(batch, seq, d)
```

```python
import functools

import jax
import jax.numpy as jnp
from jax import lax
from jax.experimental import pallas as pl
from jax.experimental.pallas import tpu as pltpu

F32 = jnp.float32
BF16 = jnp.bfloat16
I32 = jnp.int32
U32 = jnp.uint32
SDS = jax.ShapeDtypeStruct

EPS = 1e-6
HEAD_DIM = 128
TOP_K = 4
SWIGLU_LIMIT = 7.0
SWIGLU_ALPHA = 1.702
MASK_VALUE = float(jnp.finfo(jnp.float32).min)
V7X_VMEM_LIMIT_BYTES = 56 * 1024 * 1024
F_ROWS = 16
LANES = 128


def _cparams(n_axes):
    return pltpu.CompilerParams(
        dimension_semantics=("arbitrary",) * n_axes,
        vmem_limit_bytes=V7X_VMEM_LIMIT_BYTES)


def _rms(x, g):
    return x * lax.rsqrt(jnp.mean(x * x, axis=-1, keepdims=True) + EPS) * g


def _dot(a, b):
    return jnp.dot(a, b, preferred_element_type=F32)


def _dot_nt(a, b):
    return lax.dot_general(a, b, (((1,), (1,)), ((), ())), preferred_element_type=F32)


def _tile(n, pref):
    return pref if n % pref == 0 else n


def _pack_pair(lo, hi):
    lo = lax.bitcast_convert_type(lo.astype(BF16).astype(F32), U32) >> 16
    hi = lax.bitcast_convert_type(hi.astype(BF16).astype(F32), U32) & jnp.uint32(0xFFFF0000)
    return hi | lo


def _unpack_pair(w):
    lo = lax.bitcast_convert_type(w << 16, F32)
    hi = lax.bitcast_convert_type(w & jnp.uint32(0xFFFF0000), F32)
    return lo, hi


def _mix_in_kernel(x_ref, g_ref, wf_ref, w_ref, o_ref, ft_ref, h_ref):
    @pl.when(pl.program_id(1) == 0)
    def _():
        hb = _rms(x_ref[...], g_ref[...]).astype(BF16)
        h_ref[...] = hb
        ft_ref[...] = _dot_nt(wf_ref[...], hb)

    o_ref[...] = _dot(h_ref[...], w_ref[...]).astype(o_ref.dtype)


def _mix_in(x2d, g, wf_t, w_in, n_out):
    t, d = x2d.shape
    tm = _tile(t, 1024)
    tn = _tile(n_out, 1024)
    return pl.pallas_call(
        _mix_in_kernel,
        grid=(t // tm, n_out // tn),
        in_specs=[pl.BlockSpec((tm, d), lambda i, j: (i, 0)),
                  pl.BlockSpec((1, d), lambda i, j: (0, 0)),
                  pl.BlockSpec((F_ROWS, d), lambda i, j: (0, 0)),
                  pl.BlockSpec((d, tn), lambda i, j: (0, j))],
        out_specs=[pl.BlockSpec((tm, tn), lambda i, j: (i, j)),
                   pl.BlockSpec((F_ROWS, tm), lambda i, j: (0, i))],
        out_shape=[SDS((t, n_out), BF16), SDS((F_ROWS, t), F32)],
        scratch_shapes=[pltpu.VMEM((tm, d), BF16)],
        compiler_params=_cparams(2), name="mix_in")(x2d, g, wf_t, w_in)


def _decay_kernel(ft_ref, b_ref, c_ref):
    z = ft_ref[...] + b_ref[...]
    lf = jnp.minimum(z, 0.0) - jnp.log(1.0 + jnp.exp(-jnp.abs(z)))
    s = lf.shape[1]
    lane = lax.broadcasted_iota(I32, lf.shape, 1)
    d = 1
    while d < s:
        lf = lf + jnp.where(lane >= d, pltpu.roll(lf, d, 1), 0.0)
        d *= 2
    c_ref[...] = lf


def _decay(ft, b_col, seq):
    rows, t = ft.shape
    return pl.pallas_call(
        _decay_kernel,
        grid=(t // seq,),
        in_specs=[pl.BlockSpec((rows, seq), lambda b: (0, b)),
                  pl.BlockSpec((rows, 1), lambda b: (0, 0))],
        out_specs=pl.BlockSpec((rows, seq), lambda b: (0, b)),
        out_shape=SDS((rows, t), F32),
        compiler_params=_cparams(1), name="decay")(ft, b_col)


def _fox_kernel(q_ref, k_ref, v_ref, c_ref, gq_ref, gk_ref, o_ref, kn_ref, nc_ref, *, tq):
    h = pl.program_id(1)
    seq = q_ref.shape[0]
    scale = HEAD_DIM ** -0.5
    kn_ref[...] = _rms(k_ref[...].astype(F32), gk_ref[...]).astype(BF16)
    nc_ref[...] = -c_ref[pl.ds(h, 1), :]
    row = lax.broadcasted_iota(I32, (tq, tq), 0)
    col = lax.broadcasted_iota(I32, (tq, tq), 1)
    causal = col <= row
    for qi in range(seq // tq):
        lo, hi = qi * tq, (qi + 1) * tq
        q = (_rms(q_ref[lo:hi, :].astype(F32), gq_ref[...]) * scale).astype(BF16)
        s_diag = jnp.where(causal, _dot_nt(q, kn_ref[lo:hi, :]) + nc_ref[:, lo:hi], MASK_VALUE)
        m = jnp.max(s_diag, axis=-1, keepdims=True)
        if qi > 0:
            s_past = _dot_nt(q, kn_ref[:lo, :]) + nc_ref[:, :lo]
            m = jnp.maximum(m, jnp.max(s_past, axis=-1, keepdims=True))
        p_diag = jnp.exp(s_diag - m)
        l = jnp.sum(p_diag, axis=-1, keepdims=True)
        acc = _dot(p_diag.astype(BF16), v_ref[lo:hi, :])
        if qi > 0:
            p_past = jnp.exp(s_past - m)
            l = l + jnp.sum(p_past, axis=-1, keepdims=True)
            acc = acc + _dot(p_past.astype(BF16), v_ref[:lo, :])
        o_ref[lo:hi, :] = (acc / l).astype(o_ref.dtype)


def _fox_attention(proj, c, g_q, g_k, batch, seq, n_heads, col0, width):
    t = proj.shape[0]
    tq = _tile(seq, 256)
    cb = col0 // HEAD_DIM
    wb = width // HEAD_DIM
    hd = HEAD_DIM
    kern = functools.partial(_fox_kernel, tq=tq)
    return pl.pallas_call(
        kern,
        grid=(batch, n_heads),
        in_specs=[pl.BlockSpec((seq, hd), lambda b, h: (b, cb + h)),
                  pl.BlockSpec((seq, hd), lambda b, h: (b, cb + wb + h)),
                  pl.BlockSpec((seq, hd), lambda b, h: (b, cb + 2 * wb + h)),
                  pl.BlockSpec((F_ROWS, seq), lambda b, h: (0, b)),
                  pl.BlockSpec((1, hd), lambda b, h: (0, 0)),
                  pl.BlockSpec((1, hd), lambda b, h: (0, 0))],
        out_specs=pl.BlockSpec((seq, hd), lambda b, h: (b, h)),
        out_shape=SDS((t, width), BF16),
        scratch_shapes=[pltpu.VMEM((seq, hd), BF16), pltpu.VMEM((1, seq), F32)],
        compiler_params=_cparams(2), name="fox_attention")(proj, proj, proj, c, g_q, g_k)


def _mix_out_kernel(ub_ref, uc_ref, ux_ref, hc_ref, hx_ref, yf_ref, cw_ref, gc_ref, gf_ref,
                    w_ref, x_ref, o_ref):
    si = pl.program_id(1)
    p = uc_ref[...].astype(F32) * ux_ref[...].astype(F32)
    halo = hc_ref[...].astype(F32) * hx_ref[...].astype(F32)
    halo = jnp.where(si > 0, halo, 0.0)
    row = lax.broadcasted_iota(I32, p.shape, 0)
    p1 = jnp.where(row == 0, halo[7:8, :], pltpu.roll(p, 1, 0))
    p2 = jnp.where(row == 0, halo[6:7, :],
                   jnp.where(row == 1, halo[7:8, :], pltpu.roll(p, 2, 0)))
    w = cw_ref[...]
    y = ub_ref[...].astype(F32) * (w[0:1, :] * p2 + w[1:2, :] * p1 + w[2:3, :] * p)
    c = y.shape[1]
    y_conv = _rms(y, gc_ref[...]).astype(BF16)
    y_fox = _rms(yf_ref[...].astype(F32), gf_ref[...]).astype(BF16)
    o_ref[...] = x_ref[...] + _dot(y_conv, w_ref[:c, :]) + _dot(y_fox, w_ref[c:, :])


def _mix_out(proj, y_fox, conv_w, g_conv, g_fox, w_out, x2d, batch, seq):
    t, d = x2d.shape
    c = conv_w.shape[1]
    w = y_fox.shape[1]
    ts = _tile(seq, 512)
    ns = seq // ts
    hb = ts // 8
    tile = lambda col: (lambda b, s: (b * ns + s, col))
    halo = lambda col: (lambda b, s: (jnp.maximum((b * ns + s) * hb - 1, 0), col))
    const = lambda b, s: (0, 0)
    return pl.pallas_call(
        _mix_out_kernel,
        grid=(batch, ns),
        in_specs=[pl.BlockSpec((ts, c), tile(0)),
                  pl.BlockSpec((ts, c), tile(1)),
                  pl.BlockSpec((ts, c), tile(2)),
                  pl.BlockSpec((8, c), halo(1)),
                  pl.BlockSpec((8, c), halo(2)),
                  pl.BlockSpec((ts, w), tile(0)),
                  pl.BlockSpec((3, c), const),
                  pl.BlockSpec((1, c), const),
                  pl.BlockSpec((1, w), const),
                  pl.BlockSpec((c + w, d), const, pipeline_mode=pl.Buffered(1)),
                  pl.BlockSpec((ts, d), tile(0))],
        out_specs=pl.BlockSpec((ts, d), tile(0)),
        out_shape=SDS((t, d), F32),
        compiler_params=_cparams(2), name="mix_out")(
            proj, proj, proj, proj, proj, y_fox, conv_w, g_conv, g_fox, w_out, x2d)


def _memkv_kernel(mem_ref, g_ref, w_ref, gk_ref, k_ref, v_ref):
    hm = _rms(mem_ref[...], g_ref[...]).astype(BF16)
    kv = _dot(hm, w_ref[...])
    xw = k_ref.shape[1]
    for h in range(xw // HEAD_DIM):
        sl = slice(h * HEAD_DIM, (h + 1) * HEAD_DIM)
        k_ref[:, sl] = _rms(kv[:, sl], gk_ref[...]).astype(BF16)
    v_ref[...] = kv[:, xw:].astype(BF16)


def _memkv(mem2d, g_mem, w_xkv, g_xk, n_mem):
    tm_, d = mem2d.shape
    xw = w_xkv.shape[1] // 2
    return pl.pallas_call(
        _memkv_kernel,
        grid=(tm_ // n_mem,),
        in_specs=[pl.BlockSpec((n_mem, d), lambda b: (b, 0)),
                  pl.BlockSpec((1, d), lambda b: (0, 0)),
                  pl.BlockSpec((d, 2 * xw), lambda b: (0, 0)),
                  pl.BlockSpec((1, HEAD_DIM), lambda b: (0, 0))],
        out_specs=[pl.BlockSpec((n_mem, xw), lambda b: (b, 0)),
                   pl.BlockSpec((n_mem, xw), lambda b: (b, 0))],
        out_shape=[SDS((tm_, xw), BF16), SDS((tm_, xw), BF16)],
        compiler_params=_cparams(1), name="memkv")(mem2d, g_mem, w_xkv, g_xk)


def _xattn_kernel(x_ref, gx_ref, wq_ref, gq_ref, kn_ref, v_ref, wo_ref, gm_ref, wr_ref, br_ref,
                  x2_ref, hm_ref, ti_ref, tg_ref, pos_ref, cnt_ref, run_ref, tri_ref):
    i = pl.program_id(0)
    tm = x_ref.shape[0]

    @pl.when(i == 0)
    def _():
        run_ref[...] = jnp.zeros(run_ref.shape, F32)
        r = lax.broadcasted_iota(I32, (tm, tm), 0)
        c = lax.broadcasted_iota(I32, (tm, tm), 1)
        tri_ref[...] = jnp.where(r <= c, 1.0, 0.0).astype(BF16)

    x = x_ref[...]
    hb = _rms(x, gx_ref[...]).astype(BF16)
    q = _dot(hb, wq_ref[...])
    scale = HEAD_DIM ** -0.5
    outs = []
    for h in range(q.shape[1] // HEAD_DIM):
        sl = slice(h * HEAD_DIM, (h + 1) * HEAD_DIM)
        qh = (_rms(q[:, sl], gq_ref[...]) * scale).astype(BF16)
        s = _dot_nt(qh, kn_ref[:, sl])
        p = jnp.exp(s - jnp.max(s, axis=-1, keepdims=True))
        oh = _dot(p.astype(BF16), v_ref[:, sl]) / jnp.sum(p, axis=-1, keepdims=True)
        outs.append(oh.astype(BF16))
    x2 = x + _dot(jnp.concatenate(outs, axis=-1), wo_ref[...])
    x2_ref[...] = x2
    hm = _rms(x2, gm_ref[...])
    hm_ref[...] = hm
    lg = _dot_nt(wr_ref[...], hm.astype(BF16)) + br_ref[...]
    n_exp = lg.shape[0]
    eidx = lax.broadcasted_iota(I32, lg.shape, 0).astype(F32)
    vals, sel = [], []
    for r in range(TOP_K):
        mx = jnp.max(lg, axis=0, keepdims=True)
        am = jnp.min(jnp.where(lg == mx, eidx, float(n_exp)), axis=0, keepdims=True)
        ti_ref[r:r + 1, :] = am.astype(I32)
        vals.append(mx)
        sel.append(eidx == am)
        lg = jnp.where(sel[r], -jnp.inf, lg)
    ex = [jnp.exp(v - vals[0]) for v in vals]
    den = ex[0]
    for e in ex[1:]:
        den = den + e
    for r in range(TOP_K):
        tg_ref[r:r + 1, :] = ex[r] / den
    onehot = jnp.zeros(lg.shape, F32)
    for r in range(TOP_K):
        onehot = onehot + jnp.where(sel[r], 1.0, 0.0)
    incl = _dot(onehot.astype(BF16), tri_ref[...])
    excl = incl - onehot + run_ref[...]
    for r in range(TOP_K):
        pos_ref[r:r + 1, :] = jnp.sum(jnp.where(sel[r], excl, 0.0), axis=0,
                                      keepdims=True).astype(I32)
    run = run_ref[...] + incl[:, tm - 1:tm]
    run_ref[...] = run
    cnt_ref[...] = jnp.broadcast_to(run, cnt_ref.shape)


def _xattn(x1, g_xattn, w_xq, g_xq, kn, vm, w_xo, g_moe, w_rt, b_r, seq, n_mem):
    t, d = x1.shape
    xw = w_xq.shape[1]
    n_exp = w_rt.shape[0]
    tm = _tile(seq, 512)
    per = seq // tm
    const = lambda i: (0, 0)
    return pl.pallas_call(
        _xattn_kernel,
        grid=(t // tm,),
        in_specs=[pl.BlockSpec((tm, d), lambda i: (i, 0)),
                  pl.BlockSpec((1, d), const),
                  pl.BlockSpec((d, xw), const),
                  pl.BlockSpec((1, HEAD_DIM), const),
                  pl.BlockSpec((n_mem, xw), lambda i: (i // per, 0)),
                  pl.BlockSpec((n_mem, xw), lambda i: (i // per, 0)),
                  pl.BlockSpec((xw, d), const),
                  pl.BlockSpec((1, d), const),
                  pl.BlockSpec((n_exp, d), const),
                  pl.BlockSpec((n_exp, 1), const)],
        out_specs=[pl.BlockSpec((tm, d), lambda i: (i, 0)),
                   pl.BlockSpec((tm, d), lambda i: (i, 0)),
                   pl.BlockSpec((TOP_K, tm), lambda i: (0, i)),
                   pl.BlockSpec((TOP_K, tm), lambda i: (0, i)),
                   pl.BlockSpec((TOP_K, tm), lambda i: (0, i)),
                   pl.BlockSpec((n_exp, 128), const)],
        out_shape=[SDS((t, d), F32), SDS((t, d), F32),
                   SDS((TOP_K, t), I32), SDS((TOP_K, t), F32), SDS((TOP_K, t), I32),
                   SDS((n_exp, 128), F32)],
        scratch_shapes=[pltpu.VMEM((n_exp, 1), F32), pltpu.VMEM((tm, tm), BF16)],
        compiler_params=_cparams(1), name="xattn_router")(
            x1, g_xattn, w_xq, g_xq, kn, vm, w_xo, g_moe, w_rt, b_r)


def _take(values, idx):
    pick = idx[:, None] == jnp.arange(values.shape[0], dtype=I32)[None, :]
    return jnp.sum(jnp.where(pick, values[None, :], 0), axis=1).astype(I32)


def _route_tables(counts, n_assign, tm):
    n_exp = counts.shape[0]
    ts = 2 * tm
    valid = (counts + tm - 1) // tm * tm
    seg = ((counts + ts - 1) // ts * ts).astype(I32)
    seg_end = jnp.cumsum(seg).astype(I32)
    seg_start = seg_end - seg
    n_steps = n_assign // ts + n_exp
    n_used = seg_end[-1] // ts
    idx = jnp.arange(n_steps, dtype=I32)
    row0 = jnp.minimum(idx, n_used - 1) * ts
    exp = jnp.sum((row0[:, None] >= seg_end[None, :]).astype(I32), axis=1)
    exp = jnp.minimum(exp, n_exp - 1)
    n_valid = jnp.clip((_take(seg_start + valid, exp) - row0) // tm, 1, 2).astype(I32)
    prev = jnp.concatenate([jnp.full((1,), -1, I32), exp[:-1]])
    first = (exp != prev).astype(I32)
    rem = _take(seg_end, exp) // ts - idx
    cand = jnp.where(counts > 0, jnp.arange(n_exp, dtype=I32), n_exp)
    after = jnp.concatenate([lax.cummin(cand, reverse=True)[1:], jnp.full((1,), n_exp, I32)])
    nxt = _take(jnp.where(after >= n_exp, -1, after), exp)
    tables = (exp, first, rem.astype(I32), nxt, n_valid)
    return seg_start, seg_end, seg, tables, n_used.reshape(1).astype(I32)


def _dest_kernel(ps_ref, ti_ref, pos_ref, o_ref, *, n_exp):
    ti = ti_ref[...]
    acc = pos_ref[...]
    for e in range(n_exp):
        acc = acc + jnp.where(ti == e, ps_ref[e], 0)
    o_ref[...] = acc


def _dest_rows(pad_start, ti, pos):
    k, t = ti.shape
    kern = functools.partial(_dest_kernel, n_exp=pad_start.shape[0])
    grid_spec = pltpu.PrefetchScalarGridSpec(
        num_scalar_prefetch=1, grid=(1,),
        in_specs=[pl.BlockSpec((k, t), lambda i, ps: (0, 0)),
                  pl.BlockSpec((k, t), lambda i, ps: (0, 0))],
        out_specs=pl.BlockSpec((k, t), lambda i, ps: (0, 0)))
    return pl.pallas_call(kern, grid_spec=grid_spec, out_shape=SDS((k, t), I32),
                          compiler_params=_cparams(1), name="moe_dest")(pad_start, ti, pos)


def _dispatch_kernel(pe_ref, pc_ref, nu_ref, dst_ref, x_ref, xs_ref, buf, zbuf, sem, zsem,
                     *, tc, tm, n_exp, n_blk):
    i = pl.program_id(0)
    n = pl.num_programs(0)
    ns = x_ref.shape[1] // (2 * LANES)

    def zero_copy(row0):
        rows = pl.ds(pl.multiple_of(row0 * ns, tm * ns), tm * ns)
        return pltpu.make_async_copy(zbuf, xs_ref.at[rows], zsem)

    @pl.when(i == 0)
    def _():
        zbuf[...] = jnp.zeros(zbuf.shape, zbuf.dtype)
        for start in (True, False):
            for e in range(n_exp):
                @pl.when(pc_ref[e] > 0)
                def _(e=e):
                    cp = zero_copy(pe_ref[e] - tm)
                    cp.start() if start else cp.wait()

            def tail(b, c):
                cp = zero_copy(b * tm)
                cp.start() if start else cp.wait()
                return c
            lax.fori_loop(nu_ref[0], n_blk, tail, 0)

    slot = i % 2
    x = x_ref[...]
    for s in range(ns):
        lo = x[:, (2 * s) * LANES:(2 * s + 1) * LANES]
        hi = x[:, (2 * s + 1) * LANES:(2 * s + 2) * LANES]
        buf[slot, pl.ds(s, tc, stride=ns), :] = _pack_pair(lo, hi)
    for k in range(TOP_K):
        for r in range(tc):
            dst = pl.ds(pl.multiple_of(dst_ref[0, k, r] * ns, ns), ns)
            pltpu.make_async_copy(buf.at[slot, pl.ds(r * ns, ns)], xs_ref.at[dst],
                                  sem.at[slot]).start(priority=r % 2)

    def wait_slot(s):
        for k in range(TOP_K):
            pltpu.make_async_copy(buf.at[s], xs_ref.at[pl.ds(0, tc * ns)], sem.at[s]).wait()

    @pl.when(i >= 1)
    def _():
        wait_slot(1 - slot)

    @pl.when(i == n - 1)
    def _():
        wait_slot(slot)


def _dispatch(hm, dest, seg_end, seg, n_used, ts, n_steps):
    t, d = hm.shape
    tc = _tile(t, 128)
    n = t // tc
    ns = d // (2 * LANES)
    n_exp = seg_end.shape[0]
    dest3 = dest.reshape(TOP_K, n, tc).transpose(1, 0, 2)
    kern = functools.partial(_dispatch_kernel, tc=tc, tm=ts, n_exp=n_exp, n_blk=n_steps)
    grid_spec = pltpu.PrefetchScalarGridSpec(
        num_scalar_prefetch=3, grid=(n,),
        in_specs=[pl.BlockSpec((1, TOP_K, tc), lambda i, *_: (i, 0, 0), memory_space=pltpu.SMEM),
                  pl.BlockSpec((tc, d), lambda i, *_: (i, 0))],
        out_specs=pl.BlockSpec(memory_space=pl.ANY),
        scratch_shapes=[pltpu.VMEM((2, tc * ns, LANES), U32),
                        pltpu.VMEM((ts * ns, LANES), U32),
                        pltpu.SemaphoreType.DMA((2,)),
                        pltpu.SemaphoreType.DMA(())])
    return pl.pallas_call(
        kern, grid_spec=grid_spec,
        out_shape=SDS((n_steps * ts * ns, LANES), U32),
        compiler_params=_cparams(1), name="moe_dispatch")(seg_end, seg, n_used, dest3, hm)


W_CHUNK = 512


def _stream_weights(i, be_ref, first_ref, rem_ref, nxt_ref, w_hbm, wb_ref, stage_ref, st_ref, sem):
    n_chunks, _, chunk = wb_ref.shape[1:]

    def chunk_copy(e, c, slot):
        cols = pl.ds(pl.multiple_of(c * chunk, chunk), chunk)
        return pltpu.make_async_copy(w_hbm.at[e, :, cols], stage_ref.at[slot], sem.at[slot])

    def start_first_two(e):
        chunk_copy(e, 0, 0).start()
        chunk_copy(e, 1, 1).start()

    def cast_chunks(e, dst, done, issued, k):
        def body(_, carry):
            done, issued = carry
            slot = done % 2
            chunk_copy(e, done, slot).wait()
            wb_ref[dst, done] = stage_ref[slot].astype(BF16)

            @pl.when(issued < n_chunks)
            def _():
                chunk_copy(e, issued, slot).start()
            return done + 1, jnp.minimum(issued + 1, n_chunks)
        return lax.fori_loop(0, k, body, (done, issued))

    @pl.when(i == 0)
    def _():
        start_first_two(be_ref[0])
        cast_chunks(be_ref[0], 0, 0, 2, n_chunks)
        st_ref[0] = 1

    @pl.when(first_ref[i] == 1)
    def _():
        st_ref[0] = 1 - st_ref[0]
        st_ref[1] = 0
        st_ref[2] = 0

        @pl.when(nxt_ref[i] >= 0)
        def _():
            start_first_two(nxt_ref[i])
            st_ref[2] = 2

    cur = st_ref[0]

    @pl.when(nxt_ref[i] >= 0)
    def _():
        done = st_ref[1]
        rem = rem_ref[i]
        k = lax.div(n_chunks - done + rem - 1, rem)
        done, issued = cast_chunks(nxt_ref[i], 1 - cur, done, st_ref[2], k)
        st_ref[1] = done
        st_ref[2] = issued

    return cur


def _moe_up_kernel(be_ref, first_ref, rem_ref, nxt_ref, nv_ref, nu_ref, x_ref, b_ref, w_hbm, o_ref,
                   wb_ref, stage_ref, st_ref, sem):
    i = pl.program_id(0)
    ts = o_ref.shape[0]
    ns = x_ref.shape[0] // ts
    n_half = wb_ref.shape[1] // 2
    chunk = wb_ref.shape[3]
    f = n_half * chunk

    def compute(cur, rows):
        cols = []
        for s in range(ns):
            lo, hi = _unpack_pair(x_ref[pl.ds(s, rows, stride=ns), :])
            cols += [lo.astype(BF16), hi.astype(BF16)]
        x = jnp.concatenate(cols, axis=1)
        for n in range(n_half):
            lo, hi = n * chunk, (n + 1) * chunk
            gate = jnp.minimum(_dot(x, wb_ref[cur, n]) + b_ref[:, lo:hi], SWIGLU_LIMIT)
            up = jnp.clip(_dot(x, wb_ref[cur, n_half + n]) + b_ref[:, f + lo:f + hi],
                          -SWIGLU_LIMIT, SWIGLU_LIMIT)
            glu = gate * jax.nn.sigmoid(SWIGLU_ALPHA * gate)
            o_ref[:rows, lo:hi] = ((up + 1.0) * glu).astype(o_ref.dtype)

    @pl.when(i < nu_ref[0])
    def _():
        cur = _stream_weights(i, be_ref, first_ref, rem_ref, nxt_ref, w_hbm, wb_ref, stage_ref,
                              st_ref, sem)

        @pl.when(nv_ref[i] == 2)
        def _():
            compute(cur, ts)

        @pl.when(nv_ref[i] == 1)
        def _():
            compute(cur, ts // 2)
            o_ref[ts // 2:, :] = jnp.zeros((ts // 2, o_ref.shape[1]), o_ref.dtype)

    @pl.when(i >= nu_ref[0])
    def _():
        o_ref[...] = jnp.zeros(o_ref.shape, o_ref.dtype)


def _moe_up(xs, w_gu, b_gu, tables, n_used, ts):
    d = w_gu.shape[1]
    ns = d // (2 * LANES)
    n_rows = xs.shape[0] // ns
    two_f = w_gu.shape[2]
    chunk = min(W_CHUNK, two_f // 2)
    grid_spec = pltpu.PrefetchScalarGridSpec(
        num_scalar_prefetch=6,
        grid=(n_rows // ts,),
        in_specs=[pl.BlockSpec((ts * ns, LANES),
                               lambda i, be, fi, re, nx, nv, nu: (jnp.minimum(i, nu[0] - 1), 0)),
                  pl.BlockSpec((None, 1, two_f), lambda i, be, fi, re, nx, nv, nu: (be[i], 0, 0)),
                  pl.BlockSpec(memory_space=pl.ANY)],
        out_specs=pl.BlockSpec((ts, two_f // 2), lambda i, *_: (i, 0)),
        scratch_shapes=[pltpu.VMEM((2, two_f // chunk, d, chunk), BF16),
                        pltpu.VMEM((2, d, chunk), w_gu.dtype),
                        pltpu.SMEM((3,), I32),
                        pltpu.SemaphoreType.DMA((2,))])
    return pl.pallas_call(
        _moe_up_kernel, grid_spec=grid_spec,
        out_shape=SDS((n_rows, two_f // 2), BF16),
        compiler_params=_cparams(1), name="moe_up")(*tables, n_used, xs, b_gu, w_gu)


def _moe_down_kernel(be_ref, first_ref, rem_ref, nxt_ref, nv_ref, nu_ref, a_ref, b_ref, w_hbm, o_ref,
                     wb_ref, stage_ref, st_ref, sem):
    i = pl.program_id(0)
    ts = a_ref.shape[0]
    ns = o_ref.shape[0] // ts
    chunk = wb_ref.shape[3]

    def compute(cur, rows):
        a = a_ref[:rows, :]
        for n in range(wb_ref.shape[1]):
            lo, hi = n * chunk, (n + 1) * chunk
            y = _dot(a, wb_ref[cur, n]) + b_ref[:, lo:hi]
            for j in range(chunk // (2 * LANES)):
                s = lo // (2 * LANES) + j
                w = _pack_pair(y[:, (2 * j) * LANES:(2 * j + 1) * LANES],
                               y[:, (2 * j + 1) * LANES:(2 * j + 2) * LANES])
                o_ref[pl.ds(s, rows, stride=ns), :] = w

    @pl.when(i < nu_ref[0])
    def _():
        cur = _stream_weights(i, be_ref, first_ref, rem_ref, nxt_ref, w_hbm, wb_ref, stage_ref,
                              st_ref, sem)

        @pl.when(nv_ref[i] == 2)
        def _():
            compute(cur, ts)

        @pl.when(nv_ref[i] == 1)
        def _():
            compute(cur, ts // 2)
            half = ts // 2 * ns
            o_ref[half:, :] = jnp.zeros((half, LANES), o_ref.dtype)

    @pl.when(i >= nu_ref[0])
    def _():
        o_ref[...] = jnp.zeros(o_ref.shape, o_ref.dtype)


def _moe_down(act, w_dn, b_dn, tables, n_used, ts):
    n_rows, f = act.shape
    d = w_dn.shape[2]
    ns = d // (2 * LANES)
    chunk = max(min(W_CHUNK, d // 2), 2 * LANES)
    grid_spec = pltpu.PrefetchScalarGridSpec(
        num_scalar_prefetch=6,
        grid=(n_rows // ts,),
        in_specs=[pl.BlockSpec((ts, f),
                               lambda i, be, fi, re, nx, nv, nu: (jnp.minimum(i, nu[0] - 1), 0)),
                  pl.BlockSpec((None, 1, d), lambda i, be, fi, re, nx, nv, nu: (be[i], 0, 0)),
                  pl.BlockSpec(memory_space=pl.ANY)],
        out_specs=pl.BlockSpec((ts * ns, LANES), lambda i, *_: (i, 0)),
        scratch_shapes=[pltpu.VMEM((2, d // chunk, f, chunk), BF16),
                        pltpu.VMEM((2, f, chunk), w_dn.dtype),
                        pltpu.SMEM((3,), I32),
                        pltpu.SemaphoreType.DMA((2,))])
    return pl.pallas_call(
        _moe_down_kernel, grid_spec=grid_spec,
        out_shape=SDS((n_rows * ns, LANES), U32),
        compiler_params=_cparams(1), name="moe_down")(*tables, n_used, act, b_dn, w_dn)


def _combine_kernel(dst_ref, x_ref, g_ref, ys_ref, o_ref, buf, sem, *, tc):
    i = pl.program_id(0)
    n = pl.num_programs(0) - 1
    ns = x_ref.shape[1] // (2 * LANES)

    @pl.when(i < n)
    def _():
        slot = i % 2
        for k in range(TOP_K):
            for r in range(tc):
                src = pl.ds(pl.multiple_of(dst_ref[0, k, r] * ns, ns), ns)
                pltpu.make_async_copy(ys_ref.at[src], buf.at[slot, k, pl.ds(r * ns, ns)],
                                      sem.at[slot]).start(priority=r % 2)

    @pl.when(i >= 1)
    def _():
        slot = (i - 1) % 2
        for k in range(TOP_K):
            pltpu.make_async_copy(ys_ref.at[pl.ds(0, tc * ns)], buf.at[slot, k],
                                  sem.at[slot]).wait()
        g = g_ref[...]
        g = jnp.concatenate([g, jnp.zeros((8 - TOP_K, tc), F32)], axis=0).T
        for s in range(ns):
            c0, c1, c2 = (2 * s) * LANES, (2 * s + 1) * LANES, (2 * s + 2) * LANES
            acc_lo = x_ref[:, c0:c1]
            acc_hi = x_ref[:, c1:c2]
            for k in range(TOP_K):
                lo, hi = _unpack_pair(buf[slot, k, pl.ds(s, tc, stride=ns), :])
                acc_lo = acc_lo + g[:, k:k + 1] * lo
                acc_hi = acc_hi + g[:, k:k + 1] * hi
            o_ref[:, c0:c1] = acc_lo
            o_ref[:, c1:c2] = acc_hi


def _combine(x2, gates, dest, ys):
    t, d = x2.shape
    tc = _tile(t, 128)
    n = t // tc
    ns = d // (2 * LANES)
    dest3 = dest.reshape(TOP_K, n, tc).transpose(1, 0, 2)
    kern = functools.partial(_combine_kernel, tc=tc)
    prev = lambda i: (jnp.maximum(i - 1, 0), 0)
    return pl.pallas_call(
        kern,
        grid=(n + 1,),
        in_specs=[pl.BlockSpec((1, TOP_K, tc), lambda i: (jnp.minimum(i, n - 1), 0, 0),
                               memory_space=pltpu.SMEM),
                  pl.BlockSpec((tc, d), prev),
                  pl.BlockSpec((TOP_K, tc), lambda i: (0, jnp.maximum(i - 1, 0))),
                  pl.BlockSpec(memory_space=pl.ANY)],
        out_specs=pl.BlockSpec((tc, d), prev),
        out_shape=SDS((t, d), F32),
        scratch_shapes=[pltpu.VMEM((2, TOP_K, tc * ns, LANES), U32),
                        pltpu.SemaphoreType.DMA((2,))],
        compiler_params=_cparams(1), name="moe_combine")(dest3, x2, gates, ys)


def _layer(x2d, mem2d, batch, seq, n_mem, p):
    t, d = x2d.shape
    c = p["conv_w"].shape[1]
    fw = p["g_fox_out"].shape[0]
    n_heads = fw // HEAD_DIM
    n_main = 3 * c + 3 * fw
    n_exp = p["w_router"].shape[1]

    w_in = p["w_mix_in"].astype(BF16)
    wf_t = jnp.zeros((F_ROWS, d), BF16).at[:n_heads].set(w_in[:, n_main:].T)
    b_col = jnp.zeros((F_ROWS, 1), F32).at[:n_heads, 0].set(p["b_forget"])

    proj, ft = _mix_in(x2d, p["g_mix"][None, :], wf_t, w_in, n_main)
    cdec = _decay(ft, b_col, seq)
    y_fox = _fox_attention(proj, cdec, p["g_q"][None, :], p["g_k"][None, :],
                           batch, seq, n_heads, 3 * c, fw)
    x1 = _mix_out(proj, y_fox, p["conv_w"], p["g_conv_out"][None, :], p["g_fox_out"][None, :],
                  p["w_mix_out"].astype(BF16), x2d, batch, seq)

    kn, vm = _memkv(mem2d, p["g_mem"][None, :], p["w_xkv"].astype(BF16),
                    p["g_xk"][None, :], n_mem)
    x2, hm, ti, tg, pos, cnt = _xattn(x1, p["g_xattn"][None, :], p["w_xq"].astype(BF16),
                            p["g_xq"][None, :], kn, vm, p["w_xo"].astype(BF16),
                            p["g_moe"][None, :], p["w_router"].T.astype(BF16),
                            p["b_router"][:, None], seq, n_mem)

    tm = _tile(TOP_K * t, 512) // 2
    counts = cnt[:, 0].astype(I32)
    seg_start, seg_end, seg, tables, n_used = _route_tables(counts, TOP_K * t, tm)
    dest = _dest_rows(seg_start, ti, pos)
    xs = _dispatch(hm, dest, seg_end, seg, n_used, 2 * tm, tables[0].shape[0])
    act = _moe_up(xs, p["w_gate_up"], p["b_gate_up"][:, None, :], tables, n_used, 2 * tm)
    ys = _moe_down(act, p["w_down"], p["b_down"][:, None, :], tables, n_used, 2 * tm)
    return _combine(x2, tg, dest, ys)


def kernel(x, mem, g_mix, w_mix_in, b_forget, conv_w, g_q, g_k, g_conv_out, g_fox_out, w_mix_out, g_xattn, g_mem, w_xq, w_xkv, g_xq, g_xk, w_xo, g_moe, w_router, b_router, w_gate_up, b_gate_up, w_down, b_down):
    batch, seq, d = x.shape
    n_mem = mem.shape[1]
    params = dict(g_mix=g_mix, w_mix_in=w_mix_in, b_forget=b_forget, conv_w=conv_w, g_q=g_q,
                  g_k=g_k, g_conv_out=g_conv_out, g_fox_out=g_fox_out, w_mix_out=w_mix_out,
                  g_xattn=g_xattn, g_mem=g_mem, w_xq=w_xq, w_xkv=w_xkv, g_xq=g_xq, g_xk=g_xk,
                  w_xo=w_xo, g_moe=g_moe, w_router=w_router, b_router=b_router,
                  w_gate_up=w_gate_up, b_gate_up=b_gate_up, w_down=w_down, b_down=b_down)
    x2d = x.reshape(batch * seq, d)
    mem2d = mem.reshape(batch * n_mem, d)
    for l in range(g_mix.shape[0]):
        x2d = _layer(x2d, mem2d, batch, seq, n_mem, {k: v[l] for k, v in params.items()})
    return x2d.reshape(batch, seq, d)
```

```python
import functools

import jax
import jax.numpy as jnp
from jax import lax
from jax.experimental import pallas as pl
from jax.experimental.pallas import tpu as pltpu

F32 = jnp.float32
BF16 = jnp.bfloat16
I32 = jnp.int32
U32 = jnp.uint32
SDS = jax.ShapeDtypeStruct

EPS = 1e-6
HEAD_DIM = 128
TOP_K = 4
SWIGLU_LIMIT = 7.0
SWIGLU_ALPHA = 1.702
MASK_VALUE = float(jnp.finfo(jnp.float32).min)
V7X_VMEM_LIMIT_BYTES = 56 * 1024 * 1024
F_ROWS = 16
LANES = 128


def _cparams(n_axes):
    return pltpu.CompilerParams(
        dimension_semantics=("arbitrary",) * n_axes,
        vmem_limit_bytes=V7X_VMEM_LIMIT_BYTES)


def _rms(x, g):
    return x * lax.rsqrt(jnp.mean(x * x, axis=-1, keepdims=True) + EPS) * g


def _dot(a, b):
    return jnp.dot(a, b, preferred_element_type=F32)


def _dot_nt(a, b):
    return lax.dot_general(a, b, (((1,), (1,)), ((), ())), preferred_element_type=F32)


def _tile(n, pref):
    return pref if n % pref == 0 else n


def _pack_pair(lo, hi):
    lo = lax.bitcast_convert_type(lo.astype(BF16).astype(F32), U32) >> 16
    hi = lax.bitcast_convert_type(hi.astype(BF16).astype(F32), U32) & jnp.uint32(0xFFFF0000)
    return hi | lo


def _unpack_pair(w):
    lo = lax.bitcast_convert_type(w << 16, F32)
    hi = lax.bitcast_convert_type(w & jnp.uint32(0xFFFF0000), F32)
    return lo, hi


def _mix_in_kernel(x_ref, g_ref, wf_ref, w_ref, o_ref, ft_ref, h_ref):
    @pl.when(pl.program_id(1) == 0)
    def _():
        hb = _rms(x_ref[...], g_ref[...]).astype(BF16)
        h_ref[...] = hb
        ft_ref[...] = _dot_nt(wf_ref[...], hb)

    o_ref[...] = _dot(h_ref[...], w_ref[...].astype(BF16)).astype(o_ref.dtype)


def _mix_in(x2d, g, wf_t, w_in, n_out):
    t, d = x2d.shape
    tm = _tile(t, 1024)
    tn = _tile(n_out, 1024)
    return pl.pallas_call(
        _mix_in_kernel,
        grid=(t // tm, n_out // tn),
        in_specs=[pl.BlockSpec((tm, d), lambda i, j: (i, 0)),
                  pl.BlockSpec((1, d), lambda i, j: (0, 0)),
                  pl.BlockSpec((F_ROWS, d), lambda i, j: (0, 0)),
                  pl.BlockSpec((d, tn), lambda i, j: (0, j))],
        out_specs=[pl.BlockSpec((tm, tn), lambda i, j: (i, j)),
                   pl.BlockSpec((F_ROWS, tm), lambda i, j: (0, i))],
        out_shape=[SDS((t, n_out), BF16), SDS((F_ROWS, t), F32)],
        scratch_shapes=[pltpu.VMEM((tm, d), BF16)],
        compiler_params=_cparams(2), name="mix_in")(x2d, g, wf_t, w_in)


def _decay_kernel(ft_ref, b_ref, c_ref):
    z = ft_ref[...] + b_ref[...]
    lf = jnp.minimum(z, 0.0) - jnp.log(1.0 + jnp.exp(-jnp.abs(z)))
    s = lf.shape[1]
    lane = lax.broadcasted_iota(I32, lf.shape, 1)
    d = 1
    while d < s:
        lf = lf + jnp.where(lane >= d, pltpu.roll(lf, d, 1), 0.0)
        d *= 2
    c_ref[...] = lf


def _decay(ft, b_col, seq):
    rows, t = ft.shape
    return pl.pallas_call(
        _decay_kernel,
        grid=(t // seq,),
        in_specs=[pl.BlockSpec((rows, seq), lambda b: (0, b)),
                  pl.BlockSpec((rows, 1), lambda b: (0, 0))],
        out_specs=pl.BlockSpec((rows, seq), lambda b: (0, b)),
        out_shape=SDS((rows, t), F32),
        compiler_params=_cparams(1), name="decay")(ft, b_col)


def _fox_kernel(q_ref, k_ref, v_ref, c_ref, gq_ref, gk_ref, o_ref, kn_ref, nc_ref, *, tq):
    h = pl.program_id(1)
    seq = q_ref.shape[0]
    scale = HEAD_DIM ** -0.5
    kn_ref[...] = _rms(k_ref[...].astype(F32), gk_ref[...]).astype(BF16)
    nc_ref[...] = -c_ref[pl.ds(h, 1), :]
    row = lax.broadcasted_iota(I32, (tq, tq), 0)
    col = lax.broadcasted_iota(I32, (tq, tq), 1)
    causal = col <= row
    for qi in range(seq // tq):
        lo, hi = qi * tq, (qi + 1) * tq
        q = (_rms(q_ref[lo:hi, :].astype(F32), gq_ref[...]) * scale).astype(BF16)
        s_diag = jnp.where(causal, _dot_nt(q, kn_ref[lo:hi, :]) + nc_ref[:, lo:hi], MASK_VALUE)
        m = jnp.max(s_diag, axis=-1, keepdims=True)
        if qi > 0:
            s_past = _dot_nt(q, kn_ref[:lo, :]) + nc_ref[:, :lo]
            m = jnp.maximum(m, jnp.max(s_past, axis=-1, keepdims=True))
        p_diag = jnp.exp(s_diag - m)
        l = jnp.sum(p_diag, axis=-1, keepdims=True)
        acc = _dot(p_diag.astype(BF16), v_ref[lo:hi, :])
        if qi > 0:
            p_past = jnp.exp(s_past - m)
            l = l + jnp.sum(p_past, axis=-1, keepdims=True)
            acc = acc + _dot(p_past.astype(BF16), v_ref[:lo, :])
        o_ref[lo:hi, :] = (acc / l).astype(o_ref.dtype)


def _fox_attention(proj, c, g_q, g_k, batch, seq, n_heads, col0, width):
    t = proj.shape[0]
    tq = _tile(seq, 256)
    cb = col0 // HEAD_DIM
    wb = width // HEAD_DIM
    hd = HEAD_DIM
    kern = functools.partial(_fox_kernel, tq=tq)
    return pl.pallas_call(
        kern,
        grid=(batch, n_heads),
        in_specs=[pl.BlockSpec((seq, hd), lambda b, h: (b, cb + h)),
                  pl.BlockSpec((seq, hd), lambda b, h: (b, cb + wb + h)),
                  pl.BlockSpec((seq, hd), lambda b, h: (b, cb + 2 * wb + h)),
                  pl.BlockSpec((F_ROWS, seq), lambda b, h: (0, b)),
                  pl.BlockSpec((1, hd), lambda b, h: (0, 0)),
                  pl.BlockSpec((1, hd), lambda b, h: (0, 0))],
        out_specs=pl.BlockSpec((seq, hd), lambda b, h: (b, h)),
        out_shape=SDS((t, width), BF16),
        scratch_shapes=[pltpu.VMEM((seq, hd), BF16), pltpu.VMEM((1, seq), F32)],
        compiler_params=_cparams(2), name="fox_attention")(proj, proj, proj, c, g_q, g_k)


def _mix_out_kernel(ub_ref, uc_ref, ux_ref, hc_ref, hx_ref, yf_ref, cw_ref, gc_ref, gf_ref,
                    w_ref, x_ref, o_ref):
    si = pl.program_id(1)
    p = uc_ref[...].astype(F32) * ux_ref[...].astype(F32)
    halo = hc_ref[...].astype(F32) * hx_ref[...].astype(F32)
    halo = jnp.where(si > 0, halo, 0.0)
    row = lax.broadcasted_iota(I32, p.shape, 0)
    p1 = jnp.where(row == 0, halo[7:8, :], pltpu.roll(p, 1, 0))
    p2 = jnp.where(row == 0, halo[6:7, :],
                   jnp.where(row == 1, halo[7:8, :], pltpu.roll(p, 2, 0)))
    w = cw_ref[...]
    y = ub_ref[...].astype(F32) * (w[0:1, :] * p2 + w[1:2, :] * p1 + w[2:3, :] * p)
    c = y.shape[1]
    y_conv = _rms(y, gc_ref[...]).astype(BF16)
    y_fox = _rms(yf_ref[...].astype(F32), gf_ref[...]).astype(BF16)
    o_ref[...] = x_ref[...] + _dot(y_conv, w_ref[:c, :]) + _dot(y_fox, w_ref[c:, :])


def _mix_out(proj, y_fox, conv_w, g_conv, g_fox, w_out, x2d, batch, seq):
    t, d = x2d.shape
    c = conv_w.shape[1]
    w = y_fox.shape[1]
    ts = _tile(seq, 512)
    ns = seq // ts
    hb = ts // 8
    tile = lambda col: (lambda b, s: (b * ns + s, col))
    halo = lambda col: (lambda b, s: (jnp.maximum((b * ns + s) * hb - 1, 0), col))
    const = lambda b, s: (0, 0)
    return pl.pallas_call(
        _mix_out_kernel,
        grid=(batch, ns),
        in_specs=[pl.BlockSpec((ts, c), tile(0)),
                  pl.BlockSpec((ts, c), tile(1)),
                  pl.BlockSpec((ts, c), tile(2)),
                  pl.BlockSpec((8, c), halo(1)),
                  pl.BlockSpec((8, c), halo(2)),
                  pl.BlockSpec((ts, w), tile(0)),
                  pl.BlockSpec((3, c), const),
                  pl.BlockSpec((1, c), const),
                  pl.BlockSpec((1, w), const),
                  pl.BlockSpec((c + w, d), const, pipeline_mode=pl.Buffered(1)),
                  pl.BlockSpec((ts, d), tile(0))],
        out_specs=pl.BlockSpec((ts, d), tile(0)),
        out_shape=SDS((t, d), F32),
        compiler_params=_cparams(2), name="mix_out")(
            proj, proj, proj, proj, proj, y_fox, conv_w, g_conv, g_fox, w_out, x2d)


def _memkv_kernel(mem_ref, g_ref, w_ref, gk_ref, k_ref, v_ref):
    hm = _rms(mem_ref[...], g_ref[...]).astype(BF16)
    kv = _dot(hm, w_ref[...])
    xw = k_ref.shape[1]
    for h in range(xw // HEAD_DIM):
        sl = slice(h * HEAD_DIM, (h + 1) * HEAD_DIM)
        k_ref[:, sl] = _rms(kv[:, sl], gk_ref[...]).astype(BF16)
    v_ref[...] = kv[:, xw:].astype(BF16)


def _memkv(mem2d, g_mem, w_xkv, g_xk, n_mem):
    tm_, d = mem2d.shape
    xw = w_xkv.shape[1] // 2
    return pl.pallas_call(
        _memkv_kernel,
        grid=(tm_ // n_mem,),
        in_specs=[pl.BlockSpec((n_mem, d), lambda b: (b, 0)),
                  pl.BlockSpec((1, d), lambda b: (0, 0)),
                  pl.BlockSpec((d, 2 * xw), lambda b: (0, 0)),
                  pl.BlockSpec((1, HEAD_DIM), lambda b: (0, 0))],
        out_specs=[pl.BlockSpec((n_mem, xw), lambda b: (b, 0)),
                   pl.BlockSpec((n_mem, xw), lambda b: (b, 0))],
        out_shape=[SDS((tm_, xw), BF16), SDS((tm_, xw), BF16)],
        compiler_params=_cparams(1), name="memkv")(mem2d, g_mem, w_xkv, g_xk)


def _xattn_kernel(x_ref, gx_ref, wq_ref, gq_ref, kn_ref, v_ref, wo_ref, gm_ref, wr_ref, br_ref,
                  x2_ref, hm_ref, ti_ref, tg_ref, pos_ref, cnt_ref, run_ref, tri_ref):
    i = pl.program_id(0)
    tm = x_ref.shape[0]

    @pl.when(i == 0)
    def _():
        run_ref[...] = jnp.zeros(run_ref.shape, F32)
        r = lax.broadcasted_iota(I32, (tm, tm), 0)
        c = lax.broadcasted_iota(I32, (tm, tm), 1)
        tri_ref[...] = jnp.where(r <= c, 1.0, 0.0).astype(BF16)

    x = x_ref[...]
    hb = _rms(x, gx_ref[...]).astype(BF16)
    q = _dot(hb, wq_ref[...])
    scale = HEAD_DIM ** -0.5
    outs = []
    for h in range(q.shape[1] // HEAD_DIM):
        sl = slice(h * HEAD_DIM, (h + 1) * HEAD_DIM)
        qh = (_rms(q[:, sl], gq_ref[...]) * scale).astype(BF16)
        s = _dot_nt(qh, kn_ref[:, sl])
        p = jnp.exp(s - jnp.max(s, axis=-1, keepdims=True))
        oh = _dot(p.astype(BF16), v_ref[:, sl]) / jnp.sum(p, axis=-1, keepdims=True)
        outs.append(oh.astype(BF16))
    x2 = x + _dot(jnp.concatenate(outs, axis=-1), wo_ref[...])
    x2_ref[...] = x2
    hm = _rms(x2, gm_ref[...])
    hm_ref[...] = hm
    lg = _dot_nt(wr_ref[...], hm.astype(BF16)) + br_ref[...]
    n_exp = lg.shape[0]
    eidx = lax.broadcasted_iota(I32, lg.shape, 0).astype(F32)
    vals, sel = [], []
    for r in range(TOP_K):
        mx = jnp.max(lg, axis=0, keepdims=True)
        am = jnp.min(jnp.where(lg == mx, eidx, float(n_exp)), axis=0, keepdims=True)
        ti_ref[r:r + 1, :] = am.astype(I32)
        vals.append(mx)
        sel.append(eidx == am)
        lg = jnp.where(sel[r], -jnp.inf, lg)
    ex = [jnp.exp(v - vals[0]) for v in vals]
    den = ex[0]
    for e in ex[1:]:
        den = den + e
    for r in range(TOP_K):
        tg_ref[r:r + 1, :] = ex[r] / den
    onehot = jnp.zeros(lg.shape, F32)
    for r in range(TOP_K):
        onehot = onehot + jnp.where(sel[r], 1.0, 0.0)
    incl = _dot(onehot.astype(BF16), tri_ref[...])
    excl = incl - onehot + run_ref[...]
    for r in range(TOP_K):
        pos_ref[r:r + 1, :] = jnp.sum(jnp.where(sel[r], excl, 0.0), axis=0,
                                      keepdims=True).astype(I32)
    run = run_ref[...] + incl[:, tm - 1:tm]
    run_ref[...] = run
    cnt_ref[...] = jnp.broadcast_to(run, cnt_ref.shape)


def _xattn(x1, g_xattn, w_xq, g_xq, kn, vm, w_xo, g_moe, w_rt, b_r, seq, n_mem):
    t, d = x1.shape
    xw = w_xq.shape[1]
    n_exp = w_rt.shape[0]
    tm = _tile(seq, 512)
    per = seq // tm
    const = lambda i: (0, 0)
    return pl.pallas_call(
        _xattn_kernel,
        grid=(t // tm,),
        in_specs=[pl.BlockSpec((tm, d), lambda i: (i, 0)),
                  pl.BlockSpec((1, d), const),
                  pl.BlockSpec((d, xw), const),
                  pl.BlockSpec((1, HEAD_DIM), const),
                  pl.BlockSpec((n_mem, xw), lambda i: (i // per, 0)),
                  pl.BlockSpec((n_mem, xw), lambda i: (i // per, 0)),
                  pl.BlockSpec((xw, d), const),
                  pl.BlockSpec((1, d), const),
                  pl.BlockSpec((n_exp, d), const),
                  pl.BlockSpec((n_exp, 1), const)],
        out_specs=[pl.BlockSpec((tm, d), lambda i: (i, 0)),
                   pl.BlockSpec((tm, d), lambda i: (i, 0)),
                   pl.BlockSpec((TOP_K, tm), lambda i: (0, i)),
                   pl.BlockSpec((TOP_K, tm), lambda i: (0, i)),
                   pl.BlockSpec((TOP_K, tm), lambda i: (0, i)),
                   pl.BlockSpec((n_exp, 128), const)],
        out_shape=[SDS((t, d), F32), SDS((t, d), F32),
                   SDS((TOP_K, t), I32), SDS((TOP_K, t), F32), SDS((TOP_K, t), I32),
                   SDS((n_exp, 128), F32)],
        scratch_shapes=[pltpu.VMEM((n_exp, 1), F32), pltpu.VMEM((tm, tm), BF16)],
        compiler_params=_cparams(1), name="xattn_router")(
            x1, g_xattn, w_xq, g_xq, kn, vm, w_xo, g_moe, w_rt, b_r)


def _take(values, idx):
    pick = idx[:, None] == jnp.arange(values.shape[0], dtype=I32)[None, :]
    return jnp.sum(jnp.where(pick, values[None, :], 0), axis=1).astype(I32)


def _route_tables(counts, n_assign, tm):
    n_exp = counts.shape[0]
    ts = 2 * tm
    valid = (counts + tm - 1) // tm * tm
    seg = ((counts + ts - 1) // ts * ts).astype(I32)
    seg_end = jnp.cumsum(seg).astype(I32)
    seg_start = seg_end - seg
    n_steps = n_assign // ts + n_exp
    n_used = seg_end[-1] // ts
    idx = jnp.arange(n_steps, dtype=I32)
    row0 = jnp.minimum(idx, n_used - 1) * ts
    exp = jnp.sum((row0[:, None] >= seg_end[None, :]).astype(I32), axis=1)
    exp = jnp.minimum(exp, n_exp - 1)
    n_valid = jnp.clip((_take(seg_start + valid, exp) - row0) // tm, 1, 2).astype(I32)
    prev = jnp.concatenate([jnp.full((1,), -1, I32), exp[:-1]])
    first = (exp != prev).astype(I32)
    rem = _take(seg_end, exp) // ts - idx
    cand = jnp.where(counts > 0, jnp.arange(n_exp, dtype=I32), n_exp)
    after = jnp.concatenate([lax.cummin(cand, reverse=True)[1:], jnp.full((1,), n_exp, I32)])
    nxt = _take(jnp.where(after >= n_exp, -1, after), exp)
    tables = (exp, first, rem.astype(I32), nxt, n_valid)
    return seg_start, seg_end, seg, tables, n_used.reshape(1).astype(I32)


def _dest_kernel(ps_ref, ti_ref, pos_ref, o_ref, *, n_exp):
    ti = ti_ref[...]
    acc = pos_ref[...]
    for e in range(n_exp):
        acc = acc + jnp.where(ti == e, ps_ref[e], 0)
    o_ref[...] = acc


def _dest_rows(pad_start, ti, pos):
    k, t = ti.shape
    kern = functools.partial(_dest_kernel, n_exp=pad_start.shape[0])
    grid_spec = pltpu.PrefetchScalarGridSpec(
        num_scalar_prefetch=1, grid=(1,),
        in_specs=[pl.BlockSpec((k, t), lambda i, ps: (0, 0)),
                  pl.BlockSpec((k, t), lambda i, ps: (0, 0))],
        out_specs=pl.BlockSpec((k, t), lambda i, ps: (0, 0)))
    return pl.pallas_call(kern, grid_spec=grid_spec, out_shape=SDS((k, t), I32),
                          compiler_params=_cparams(1), name="moe_dest")(pad_start, ti, pos)


def _dispatch_kernel(pe_ref, pc_ref, nu_ref, dst_ref, x_ref, xs_ref, buf, zbuf, sem, zsem,
                     *, tc, tm, n_exp, n_blk):
    i = pl.program_id(0)
    n = pl.num_programs(0)
    ns = x_ref.shape[1] // (2 * LANES)

    def zero_copy(row0):
        rows = pl.ds(pl.multiple_of(row0 * ns, tm * ns), tm * ns)
        return pltpu.make_async_copy(zbuf, xs_ref.at[rows], zsem)

    @pl.when(i == 0)
    def _():
        zbuf[...] = jnp.zeros(zbuf.shape, zbuf.dtype)
        for start in (True, False):
            for e in range(n_exp):
                @pl.when(pc_ref[e] > 0)
                def _(e=e):
                    cp = zero_copy(pe_ref[e] - tm)
                    cp.start() if start else cp.wait()

            def tail(b, c):
                cp = zero_copy(b * tm)
                cp.start() if start else cp.wait()
                return c
            lax.fori_loop(nu_ref[0], n_blk, tail, 0)

    slot = i % 2
    x = x_ref[...]
    for s in range(ns):
        lo = x[:, (2 * s) * LANES:(2 * s + 1) * LANES]
        hi = x[:, (2 * s + 1) * LANES:(2 * s + 2) * LANES]
        buf[slot, pl.ds(s, tc, stride=ns), :] = _pack_pair(lo, hi)
    for k in range(TOP_K):
        for r in range(tc):
            dst = pl.ds(pl.multiple_of(dst_ref[0, k, r] * ns, ns), ns)
            pltpu.make_async_copy(buf.at[slot, pl.ds(r * ns, ns)], xs_ref.at[dst],
                                  sem.at[slot]).start(priority=r % 2)

    def wait_slot(s):
        for k in range(TOP_K):
            pltpu.make_async_copy(buf.at[s], xs_ref.at[pl.ds(0, tc * ns)], sem.at[s]).wait()

    @pl.when(i >= 1)
    def _():
        wait_slot(1 - slot)

    @pl.when(i == n - 1)
    def _():
        wait_slot(slot)


def _dispatch(hm, dest, seg_end, seg, n_used, ts, n_steps):
    t, d = hm.shape
    tc = _tile(t, 128)
    n = t // tc
    ns = d // (2 * LANES)
    n_exp = seg_end.shape[0]
    dest3 = dest.reshape(TOP_K, n, tc).transpose(1, 0, 2)
    kern = functools.partial(_dispatch_kernel, tc=tc, tm=ts, n_exp=n_exp, n_blk=n_steps)
    grid_spec = pltpu.PrefetchScalarGridSpec(
        num_scalar_prefetch=3, grid=(n,),
        in_specs=[pl.BlockSpec((1, TOP_K, tc), lambda i, *_: (i, 0, 0), memory_space=pltpu.SMEM),
                  pl.BlockSpec((tc, d), lambda i, *_: (i, 0))],
        out_specs=pl.BlockSpec(memory_space=pl.ANY),
        scratch_shapes=[pltpu.VMEM((2, tc * ns, LANES), U32),
                        pltpu.VMEM((ts * ns, LANES), U32),
                        pltpu.SemaphoreType.DMA((2,)),
                        pltpu.SemaphoreType.DMA(())])
    return pl.pallas_call(
        kern, grid_spec=grid_spec,
        out_shape=SDS((n_steps * ts * ns, LANES), U32),
        compiler_params=_cparams(1), name="moe_dispatch")(seg_end, seg, n_used, dest3, hm)


W_CHUNK = 512


def _stream_weights(i, be_ref, first_ref, rem_ref, nxt_ref, w_hbm, wb_ref, stage_ref, st_ref, sem):
    n_chunks, _, chunk = wb_ref.shape[1:]

    def chunk_copy(e, c, slot):
        cols = pl.ds(pl.multiple_of(c * chunk, chunk), chunk)
        return pltpu.make_async_copy(w_hbm.at[e, :, cols], stage_ref.at[slot], sem.at[slot])

    def start_first_two(e):
        chunk_copy(e, 0, 0).start()
        chunk_copy(e, 1, 1).start()

    def cast_chunks(e, dst, done, issued, k):
        def body(_, carry):
            done, issued = carry
            slot = done % 2
            chunk_copy(e, done, slot).wait()
            wb_ref[dst, done] = stage_ref[slot].astype(BF16)

            @pl.when(issued < n_chunks)
            def _():
                chunk_copy(e, issued, slot).start()
            return done + 1, jnp.minimum(issued + 1, n_chunks)
        return lax.fori_loop(0, k, body, (done, issued))

    @pl.when(i == 0)
    def _():
        start_first_two(be_ref[0])
        cast_chunks(be_ref[0], 0, 0, 2, n_chunks)
        st_ref[0] = 1

    @pl.when(first_ref[i] == 1)
    def _():
        st_ref[0] = 1 - st_ref[0]
        st_ref[1] = 0
        st_ref[2] = 0

        @pl.when(nxt_ref[i] >= 0)
        def _():
            start_first_two(nxt_ref[i])
            st_ref[2] = 2

    cur = st_ref[0]

    @pl.when(nxt_ref[i] >= 0)
    def _():
        done = st_ref[1]
        rem = rem_ref[i]
        k = lax.div(n_chunks - done + rem - 1, rem)
        done, issued = cast_chunks(nxt_ref[i], 1 - cur, done, st_ref[2], k)
        st_ref[1] = done
        st_ref[2] = issued

    return cur


def _moe_up_kernel(be_ref, first_ref, rem_ref, nxt_ref, nv_ref, nu_ref, x_ref, b_ref, w_hbm, o_ref,
                   wb_ref, stage_ref, st_ref, sem):
    i = pl.program_id(0)
    ts = o_ref.shape[0]
    ns = x_ref.shape[0] // ts
    n_half = wb_ref.shape[1] // 2
    chunk = wb_ref.shape[3]
    f = n_half * chunk

    def compute(cur, rows):
        cols = []
        for s in range(ns):
            lo, hi = _unpack_pair(x_ref[pl.ds(s, rows, stride=ns), :])
            cols += [lo.astype(BF16), hi.astype(BF16)]
        x = jnp.concatenate(cols, axis=1)
        sub = min(chunk, 128 * 1024 // rows)
        for n in range(n_half):
            for c0 in range(0, chunk, sub):
                lo, hi = n * chunk + c0, n * chunk + c0 + sub
                gate = jnp.minimum(_dot(x, wb_ref[cur, n, :, c0:c0 + sub]) + b_ref[:, lo:hi],
                                   SWIGLU_LIMIT)
                up = jnp.clip(_dot(x, wb_ref[cur, n_half + n, :, c0:c0 + sub])
                              + b_ref[:, f + lo:f + hi], -SWIGLU_LIMIT, SWIGLU_LIMIT)
                glu = gate * jax.nn.sigmoid(SWIGLU_ALPHA * gate)
                o_ref[:rows, lo:hi] = ((up + 1.0) * glu).astype(o_ref.dtype)

    @pl.when(i < nu_ref[0])
    def _():
        cur = _stream_weights(i, be_ref, first_ref, rem_ref, nxt_ref, w_hbm, wb_ref, stage_ref,
                              st_ref, sem)

        @pl.when(nv_ref[i] == 2)
        def _():
            compute(cur, ts)

        @pl.when(nv_ref[i] == 1)
        def _():
            compute(cur, ts // 2)
            o_ref[ts // 2:, :] = jnp.zeros((ts // 2, o_ref.shape[1]), o_ref.dtype)

    @pl.when(i >= nu_ref[0])
    def _():
        o_ref[...] = jnp.zeros(o_ref.shape, o_ref.dtype)


def _moe_up(xs, w_gu, b_gu, tables, n_used, ts):
    d = w_gu.shape[1]
    ns = d // (2 * LANES)
    n_rows = xs.shape[0] // ns
    two_f = w_gu.shape[2]
    chunk = min(W_CHUNK, two_f // 2)
    grid_spec = pltpu.PrefetchScalarGridSpec(
        num_scalar_prefetch=6,
        grid=(n_rows // ts,),
        in_specs=[pl.BlockSpec((ts * ns, LANES),
                               lambda i, be, fi, re, nx, nv, nu: (jnp.minimum(i, nu[0] - 1), 0)),
                  pl.BlockSpec((None, 1, two_f), lambda i, be, fi, re, nx, nv, nu: (be[i], 0, 0)),
                  pl.BlockSpec(memory_space=pl.ANY)],
        out_specs=pl.BlockSpec((ts, two_f // 2), lambda i, *_: (i, 0)),
        scratch_shapes=[pltpu.VMEM((2, two_f // chunk, d, chunk), BF16),
                        pltpu.VMEM((2, d, chunk), w_gu.dtype),
                        pltpu.SMEM((3,), I32),
                        pltpu.SemaphoreType.DMA((2,))])
    return pl.pallas_call(
        _moe_up_kernel, grid_spec=grid_spec,
        out_shape=SDS((n_rows, two_f // 2), BF16),
        compiler_params=_cparams(1), name="moe_up")(*tables, n_used, xs, b_gu, w_gu)


def _moe_down_kernel(be_ref, first_ref, rem_ref, nxt_ref, nv_ref, nu_ref, a_ref, b_ref, w_hbm, o_ref,
                     wb_ref, stage_ref, st_ref, sem):
    i = pl.program_id(0)
    ts = a_ref.shape[0]
    ns = o_ref.shape[0] // ts
    chunk = wb_ref.shape[3]

    def compute(cur, rows):
        a = a_ref[:rows, :]
        for n in range(wb_ref.shape[1]):
            lo, hi = n * chunk, (n + 1) * chunk
            y = _dot(a, wb_ref[cur, n]) + b_ref[:, lo:hi]
            for j in range(chunk // (2 * LANES)):
                s = lo // (2 * LANES) + j
                w = _pack_pair(y[:, (2 * j) * LANES:(2 * j + 1) * LANES],
                               y[:, (2 * j + 1) * LANES:(2 * j + 2) * LANES])
                o_ref[pl.ds(s, rows, stride=ns), :] = w

    @pl.when(i < nu_ref[0])
    def _():
        cur = _stream_weights(i, be_ref, first_ref, rem_ref, nxt_ref, w_hbm, wb_ref, stage_ref,
                              st_ref, sem)

        @pl.when(nv_ref[i] == 2)
        def _():
            compute(cur, ts)

        @pl.when(nv_ref[i] == 1)
        def _():
            compute(cur, ts // 2)
            half = ts // 2 * ns
            o_ref[half:, :] = jnp.zeros((half, LANES), o_ref.dtype)

    @pl.when(i >= nu_ref[0])
    def _():
        o_ref[...] = jnp.zeros(o_ref.shape, o_ref.dtype)


def _moe_down(act, w_dn, b_dn, tables, n_used, ts):
    n_rows, f = act.shape
    d = w_dn.shape[2]
    ns = d // (2 * LANES)
    chunk = max(min(W_CHUNK, d // 2), 2 * LANES)
    grid_spec = pltpu.PrefetchScalarGridSpec(
        num_scalar_prefetch=6,
        grid=(n_rows // ts,),
        in_specs=[pl.BlockSpec((ts, f),
                               lambda i, be, fi, re, nx, nv, nu: (jnp.minimum(i, nu[0] - 1), 0)),
                  pl.BlockSpec((None, 1, d), lambda i, be, fi, re, nx, nv, nu: (be[i], 0, 0)),
                  pl.BlockSpec(memory_space=pl.ANY)],
        out_specs=pl.BlockSpec((ts * ns, LANES), lambda i, *_: (i, 0)),
        scratch_shapes=[pltpu.VMEM((2, d // chunk, f, chunk), BF16),
                        pltpu.VMEM((2, f, chunk), w_dn.dtype),
                        pltpu.SMEM((3,), I32),
                        pltpu.SemaphoreType.DMA((2,))])
    return pl.pallas_call(
        _moe_down_kernel, grid_spec=grid_spec,
        out_shape=SDS((n_rows * ns, LANES), U32),
        compiler_params=_cparams(1), name="moe_down")(*tables, n_used, act, b_dn, w_dn)


def _combine_kernel(dst_ref, x_ref, g_ref, ys_ref, o_ref, buf, sem, *, tc):
    i = pl.program_id(0)
    n = pl.num_programs(0) - 1
    ns = x_ref.shape[1] // (2 * LANES)

    @pl.when(i < n)
    def _():
        slot = i % 2
        for k in range(TOP_K):
            for r in range(tc):
                src = pl.ds(pl.multiple_of(dst_ref[0, k, r] * ns, ns), ns)
                pltpu.make_async_copy(ys_ref.at[src], buf.at[slot, k, pl.ds(r * ns, ns)],
                                      sem.at[slot]).start(priority=r % 2)

    @pl.when(i >= 1)
    def _():
        slot = (i - 1) % 2
        for k in range(TOP_K):
            pltpu.make_async_copy(ys_ref.at[pl.ds(0, tc * ns)], buf.at[slot, k],
                                  sem.at[slot]).wait()
        g = g_ref[...]
        g = jnp.concatenate([g, jnp.zeros((8 - TOP_K, tc), F32)], axis=0).T
        for s in range(ns):
            c0, c1, c2 = (2 * s) * LANES, (2 * s + 1) * LANES, (2 * s + 2) * LANES
            acc_lo = x_ref[:, c0:c1]
            acc_hi = x_ref[:, c1:c2]
            for k in range(TOP_K):
                lo, hi = _unpack_pair(buf[slot, k, pl.ds(s, tc, stride=ns), :])
                acc_lo = acc_lo + g[:, k:k + 1] * lo
                acc_hi = acc_hi + g[:, k:k + 1] * hi
            o_ref[:, c0:c1] = acc_lo
            o_ref[:, c1:c2] = acc_hi


def _combine(x2, gates, dest, ys):
    t, d = x2.shape
    tc = _tile(t, 128)
    n = t // tc
    ns = d // (2 * LANES)
    dest3 = dest.reshape(TOP_K, n, tc).transpose(1, 0, 2)
    kern = functools.partial(_combine_kernel, tc=tc)
    prev = lambda i: (jnp.maximum(i - 1, 0), 0)
    return pl.pallas_call(
        kern,
        grid=(n + 1,),
        in_specs=[pl.BlockSpec((1, TOP_K, tc), lambda i: (jnp.minimum(i, n - 1), 0, 0),
                               memory_space=pltpu.SMEM),
                  pl.BlockSpec((tc, d), prev),
                  pl.BlockSpec((TOP_K, tc), lambda i: (0, jnp.maximum(i - 1, 0))),
                  pl.BlockSpec(memory_space=pl.ANY)],
        out_specs=pl.BlockSpec((tc, d), prev),
        out_shape=SDS((t, d), F32),
        scratch_shapes=[pltpu.VMEM((2, TOP_K, tc * ns, LANES), U32),
                        pltpu.SemaphoreType.DMA((2,))],
        compiler_params=_cparams(1), name="moe_combine")(dest3, x2, gates, ys)


def _layer(x2d, mem2d, batch, seq, n_mem, p):
    t, d = x2d.shape
    c = p["conv_w"].shape[1]
    fw = p["g_fox_out"].shape[0]
    n_heads = fw // HEAD_DIM
    n_main = 3 * c + 3 * fw
    n_exp = p["w_router"].shape[1]

    w_in = p["w_mix_in"]
    wf_t = jnp.zeros((F_ROWS, d), BF16).at[:n_heads].set(w_in[:, n_main:].T.astype(BF16))
    b_col = jnp.zeros((F_ROWS, 1), F32).at[:n_heads, 0].set(p["b_forget"])

    proj, ft = _mix_in(x2d, p["g_mix"][None, :], wf_t, w_in, n_main)
    cdec = _decay(ft, b_col, seq)
    y_fox = _fox_attention(proj, cdec, p["g_q"][None, :], p["g_k"][None, :],
                           batch, seq, n_heads, 3 * c, fw)
    x1 = _mix_out(proj, y_fox, p["conv_w"], p["g_conv_out"][None, :], p["g_fox_out"][None, :],
                  p["w_mix_out"].astype(BF16), x2d, batch, seq)

    kn, vm = _memkv(mem2d, p["g_mem"][None, :], p["w_xkv"].astype(BF16),
                    p["g_xk"][None, :], n_mem)
    x2, hm, ti, tg, pos, cnt = _xattn(x1, p["g_xattn"][None, :], p["w_xq"].astype(BF16),
                            p["g_xq"][None, :], kn, vm, p["w_xo"].astype(BF16),
                            p["g_moe"][None, :], p["w_router"].T.astype(BF16),
                            p["b_router"][:, None], seq, n_mem)

    tm = _tile(TOP_K * t, 512) // 2
    counts = cnt[:, 0].astype(I32)
    seg_start, seg_end, seg, tables, n_used = _route_tables(counts, TOP_K * t, tm)
    dest = _dest_rows(seg_start, ti, pos)
    xs = _dispatch(hm, dest, seg_end, seg, n_used, 2 * tm, tables[0].shape[0])
    act = _moe_up(xs, p["w_gate_up"], p["b_gate_up"][:, None, :], tables, n_used, 2 * tm)
    ys = _moe_down(act, p["w_down"], p["b_down"][:, None, :], tables, n_used, 2 * tm)
    return _combine(x2, tg, dest, ys)


def kernel(x, mem, g_mix, w_mix_in, b_forget, conv_w, g_q, g_k, g_conv_out, g_fox_out, w_mix_out, g_xattn, g_mem, w_xq, w_xkv, g_xq, g_xk, w_xo, g_moe, w_router, b_router, w_gate_up, b_gate_up, w_down, b_down):
    batch, seq, d = x.shape
    n_mem = mem.shape[1]
    params = dict(g_mix=g_mix, w_mix_in=w_mix_in, b_forget=b_forget, conv_w=conv_w, g_q=g_q,
                  g_k=g_k, g_conv_out=g_conv_out, g_fox_out=g_fox_out, w_mix_out=w_mix_out,
                  g_xattn=g_xattn, g_mem=g_mem, w_xq=w_xq, w_xkv=w_xkv, g_xq=g_xq, g_xk=g_xk,
                  w_xo=w_xo, g_moe=g_moe, w_router=w_router, b_router=b_router,
                  w_gate_up=w_gate_up, b_gate_up=b_gate_up, w_down=w_down, b_down=b_down)
    x2d = x.reshape(batch * seq, d)
    mem2d = mem.reshape(batch * n_mem, d)
    for l in range(g_mix.shape[0]):
        x2d = _layer(x2d, mem2d, batch, seq, n_mem, {k: v[l] for k, v in params.items()})
    return x2d.reshape(batch, seq, d)
```

```python
import functools

import jax
import jax.numpy as jnp
from jax import lax
from jax.experimental import pallas as pl
from jax.experimental.pallas import tpu as pltpu

F32 = jnp.float32
BF16 = jnp.bfloat16
I32 = jnp.int32
U32 = jnp.uint32
SDS = jax.ShapeDtypeStruct

EPS = 1e-6
HEAD_DIM = 128
TOP_K = 4
SWIGLU_LIMIT = 7.0
SWIGLU_ALPHA = 1.702
MASK_VALUE = float(jnp.finfo(jnp.float32).min)
V7X_VMEM_LIMIT_BYTES = 56 * 1024 * 1024
F_ROWS = 16
LANES = 128


def _cparams(n_axes):
    return pltpu.CompilerParams(
        dimension_semantics=("arbitrary",) * n_axes,
        vmem_limit_bytes=V7X_VMEM_LIMIT_BYTES)


def _rms(x, g):
    return x * lax.rsqrt(jnp.mean(x * x, axis=-1, keepdims=True) + EPS) * g


def _dot(a, b):
    return jnp.dot(a, b, preferred_element_type=F32)


def _dot_nt(a, b):
    return lax.dot_general(a, b, (((1,), (1,)), ((), ())), preferred_element_type=F32)


def _tile(n, pref):
    return pref if n % pref == 0 else n


def _pack_pair(lo, hi):
    lo = lax.bitcast_convert_type(lo.astype(BF16).astype(F32), U32) >> 16
    hi = lax.bitcast_convert_type(hi.astype(BF16).astype(F32), U32) & jnp.uint32(0xFFFF0000)
    return hi | lo


def _unpack_pair(w):
    lo = lax.bitcast_convert_type(w << 16, F32)
    hi = lax.bitcast_convert_type(w & jnp.uint32(0xFFFF0000), F32)
    return lo, hi


def _mix_in_kernel(x_ref, g_ref, wf_ref, w_ref, o_ref, ft_ref, h_ref):
    @pl.when(pl.program_id(1) == 0)
    def _():
        hb = _rms(x_ref[...], g_ref[...]).astype(BF16)
        h_ref[...] = hb
        ft_ref[...] = _dot_nt(wf_ref[...], hb)

    o_ref[...] = _dot(h_ref[...], w_ref[...]).astype(o_ref.dtype)


def _mix_in(x2d, g, wf_t, w_in, n_out):
    t, d = x2d.shape
    tm = _tile(t, 1024)
    tn = _tile(n_out, 1024)
    return pl.pallas_call(
        _mix_in_kernel,
        grid=(t // tm, n_out // tn),
        in_specs=[pl.BlockSpec((tm, d), lambda i, j: (i, 0)),
                  pl.BlockSpec((1, d), lambda i, j: (0, 0)),
                  pl.BlockSpec((F_ROWS, d), lambda i, j: (0, 0)),
                  pl.BlockSpec((d, tn), lambda i, j: (0, j))],
        out_specs=[pl.BlockSpec((tm, tn), lambda i, j: (i, j)),
                   pl.BlockSpec((F_ROWS, tm), lambda i, j: (0, i))],
        out_shape=[SDS((t, n_out), BF16), SDS((F_ROWS, t), F32)],
        scratch_shapes=[pltpu.VMEM((tm, d), BF16)],
        compiler_params=_cparams(2), name="mix_in")(x2d, g, wf_t, w_in)


def _decay_kernel(ft_ref, b_ref, c_ref):
    z = ft_ref[...] + b_ref[...]
    lf = jnp.minimum(z, 0.0) - jnp.log(1.0 + jnp.exp(-jnp.abs(z)))
    s = lf.shape[1]
    lane = lax.broadcasted_iota(I32, lf.shape, 1)
    d = 1
    while d < s:
        lf = lf + jnp.where(lane >= d, pltpu.roll(lf, d, 1), 0.0)
        d *= 2
    c_ref[...] = lf


def _decay(ft, b_col, seq):
    rows, t = ft.shape
    return pl.pallas_call(
        _decay_kernel,
        grid=(t // seq,),
        in_specs=[pl.BlockSpec((rows, seq), lambda b: (0, b)),
                  pl.BlockSpec((rows, 1), lambda b: (0, 0))],
        out_specs=pl.BlockSpec((rows, seq), lambda b: (0, b)),
        out_shape=SDS((rows, t), F32),
        compiler_params=_cparams(1), name="decay")(ft, b_col)


def _fox_kernel(q_ref, k_ref, v_ref, c_ref, gq_ref, gk_ref, o_ref, kn_ref, nc_ref, *, tq):
    h = pl.program_id(1)
    seq = q_ref.shape[0]
    scale = HEAD_DIM ** -0.5
    kn_ref[...] = _rms(k_ref[...].astype(F32), gk_ref[...]).astype(BF16)
    nc_ref[...] = -c_ref[pl.ds(h, 1), :]
    row = lax.broadcasted_iota(I32, (tq, tq), 0)
    col = lax.broadcasted_iota(I32, (tq, tq), 1)
    causal = col <= row
    for qi in range(seq // tq):
        lo, hi = qi * tq, (qi + 1) * tq
        q = (_rms(q_ref[lo:hi, :].astype(F32), gq_ref[...]) * scale).astype(BF16)
        s_diag = jnp.where(causal, _dot_nt(q, kn_ref[lo:hi, :]) + nc_ref[:, lo:hi], MASK_VALUE)
        m = jnp.max(s_diag, axis=-1, keepdims=True)
        if qi > 0:
            s_past = _dot_nt(q, kn_ref[:lo, :]) + nc_ref[:, :lo]
            m = jnp.maximum(m, jnp.max(s_past, axis=-1, keepdims=True))
        p_diag = jnp.exp(s_diag - m)
        l = jnp.sum(p_diag, axis=-1, keepdims=True)
        acc = _dot(p_diag.astype(BF16), v_ref[lo:hi, :])
        if qi > 0:
            p_past = jnp.exp(s_past - m)
            l = l + jnp.sum(p_past, axis=-1, keepdims=True)
            acc = acc + _dot(p_past.astype(BF16), v_ref[:lo, :])
        o_ref[lo:hi, :] = (acc / l).astype(o_ref.dtype)


def _fox_attention(proj, c, g_q, g_k, batch, seq, n_heads, col0, width):
    t = proj.shape[0]
    tq = _tile(seq, 256)
    cb = col0 // HEAD_DIM
    wb = width // HEAD_DIM
    hd = HEAD_DIM
    kern = functools.partial(_fox_kernel, tq=tq)
    return pl.pallas_call(
        kern,
        grid=(batch, n_heads),
        in_specs=[pl.BlockSpec((seq, hd), lambda b, h: (b, cb + h)),
                  pl.BlockSpec((seq, hd), lambda b, h: (b, cb + wb + h)),
                  pl.BlockSpec((seq, hd), lambda b, h: (b, cb + 2 * wb + h)),
                  pl.BlockSpec((F_ROWS, seq), lambda b, h: (0, b)),
                  pl.BlockSpec((1, hd), lambda b, h: (0, 0)),
                  pl.BlockSpec((1, hd), lambda b, h: (0, 0))],
        out_specs=pl.BlockSpec((seq, hd), lambda b, h: (b, h)),
        out_shape=SDS((t, width), BF16),
        scratch_shapes=[pltpu.VMEM((seq, hd), BF16), pltpu.VMEM((1, seq), F32)],
        compiler_params=_cparams(2), name="fox_attention")(proj, proj, proj, c, g_q, g_k)


def _mix_out_kernel(ub_ref, uc_ref, ux_ref, hc_ref, hx_ref, yf_ref, cw_ref, gc_ref, gf_ref,
                    w_ref, x_ref, o_ref):
    si = pl.program_id(1)
    p = uc_ref[...].astype(F32) * ux_ref[...].astype(F32)
    halo = hc_ref[...].astype(F32) * hx_ref[...].astype(F32)
    halo = jnp.where(si > 0, halo, 0.0)
    row = lax.broadcasted_iota(I32, p.shape, 0)
    p1 = jnp.where(row == 0, halo[7:8, :], pltpu.roll(p, 1, 0))
    p2 = jnp.where(row == 0, halo[6:7, :],
                   jnp.where(row == 1, halo[7:8, :], pltpu.roll(p, 2, 0)))
    w = cw_ref[...]
    y = ub_ref[...].astype(F32) * (w[0:1, :] * p2 + w[1:2, :] * p1 + w[2:3, :] * p)
    c = y.shape[1]
    y_conv = _rms(y, gc_ref[...]).astype(BF16)
    y_fox = _rms(yf_ref[...].astype(F32), gf_ref[...]).astype(BF16)
    o_ref[...] = x_ref[...] + _dot(y_conv, w_ref[:c, :]) + _dot(y_fox, w_ref[c:, :])


def _mix_out(proj, y_fox, conv_w, g_conv, g_fox, w_out, x2d, batch, seq):
    t, d = x2d.shape
    c = conv_w.shape[1]
    w = y_fox.shape[1]
    ts = _tile(seq, 512)
    ns = seq // ts
    hb = ts // 8
    tile = lambda col: (lambda b, s: (b * ns + s, col))
    halo = lambda col: (lambda b, s: (jnp.maximum((b * ns + s) * hb - 1, 0), col))
    const = lambda b, s: (0, 0)
    return pl.pallas_call(
        _mix_out_kernel,
        grid=(batch, ns),
        in_specs=[pl.BlockSpec((ts, c), tile(0)),
                  pl.BlockSpec((ts, c), tile(1)),
                  pl.BlockSpec((ts, c), tile(2)),
                  pl.BlockSpec((8, c), halo(1)),
                  pl.BlockSpec((8, c), halo(2)),
                  pl.BlockSpec((ts, w), tile(0)),
                  pl.BlockSpec((3, c), const),
                  pl.BlockSpec((1, c), const),
                  pl.BlockSpec((1, w), const),
                  pl.BlockSpec((c + w, d), const, pipeline_mode=pl.Buffered(1)),
                  pl.BlockSpec((ts, d), tile(0))],
        out_specs=pl.BlockSpec((ts, d), tile(0)),
        out_shape=SDS((t, d), F32),
        compiler_params=_cparams(2), name="mix_out")(
            proj, proj, proj, proj, proj, y_fox, conv_w, g_conv, g_fox, w_out, x2d)


def _memkv_kernel(mem_ref, g_ref, w_ref, gk_ref, k_ref, v_ref):
    hm = _rms(mem_ref[...], g_ref[...]).astype(BF16)
    kv = _dot(hm, w_ref[...])
    xw = k_ref.shape[1]
    for h in range(xw // HEAD_DIM):
        sl = slice(h * HEAD_DIM, (h + 1) * HEAD_DIM)
        k_ref[:, sl] = _rms(kv[:, sl], gk_ref[...]).astype(BF16)
    v_ref[...] = kv[:, xw:].astype(BF16)


def _memkv(mem2d, g_mem, w_xkv, g_xk, n_mem):
    tm_, d = mem2d.shape
    xw = w_xkv.shape[1] // 2
    return pl.pallas_call(
        _memkv_kernel,
        grid=(tm_ // n_mem,),
        in_specs=[pl.BlockSpec((n_mem, d), lambda b: (b, 0)),
                  pl.BlockSpec((1, d), lambda b: (0, 0)),
                  pl.BlockSpec((d, 2 * xw), lambda b: (0, 0)),
                  pl.BlockSpec((1, HEAD_DIM), lambda b: (0, 0))],
        out_specs=[pl.BlockSpec((n_mem, xw), lambda b: (b, 0)),
                   pl.BlockSpec((n_mem, xw), lambda b: (b, 0))],
        out_shape=[SDS((tm_, xw), BF16), SDS((tm_, xw), BF16)],
        compiler_params=_cparams(1), name="memkv")(mem2d, g_mem, w_xkv, g_xk)


def _xattn_kernel(x_ref, gx_ref, wq_ref, gq_ref, kn_ref, v_ref, wo_ref, gm_ref, wr_ref, br_ref,
                  x2_ref, hm_ref, ti_ref, tg_ref, pos_ref, cnt_ref, run_ref, tri_ref):
    i = pl.program_id(0)
    tm = x_ref.shape[0]

    @pl.when(i == 0)
    def _():
        run_ref[...] = jnp.zeros(run_ref.shape, F32)
        r = lax.broadcasted_iota(I32, (tm, tm), 0)
        c = lax.broadcasted_iota(I32, (tm, tm), 1)
        tri_ref[...] = jnp.where(r <= c, 1.0, 0.0).astype(BF16)

    x = x_ref[...]
    hb = _rms(x, gx_ref[...]).astype(BF16)
    q = _dot(hb, wq_ref[...])
    scale = HEAD_DIM ** -0.5
    outs = []
    for h in range(q.shape[1] // HEAD_DIM):
        sl = slice(h * HEAD_DIM, (h + 1) * HEAD_DIM)
        qh = (_rms(q[:, sl], gq_ref[...]) * scale).astype(BF16)
        s = _dot_nt(qh, kn_ref[:, sl])
        p = jnp.exp(s - jnp.max(s, axis=-1, keepdims=True))
        oh = _dot(p.astype(BF16), v_ref[:, sl]) / jnp.sum(p, axis=-1, keepdims=True)
        outs.append(oh.astype(BF16))
    x2 = x + _dot(jnp.concatenate(outs, axis=-1), wo_ref[...])
    x2_ref[...] = x2
    hm = _rms(x2, gm_ref[...])
    hm_ref[...] = hm
    lg = _dot_nt(wr_ref[...], hm.astype(BF16)) + br_ref[...]
    n_exp = lg.shape[0]
    eidx = lax.broadcasted_iota(I32, lg.shape, 0).astype(F32)
    vals, sel = [], []
    for r in range(TOP_K):
        mx = jnp.max(lg, axis=0, keepdims=True)
        am = jnp.min(jnp.where(lg == mx, eidx, float(n_exp)), axis=0, keepdims=True)
        ti_ref[r:r + 1, :] = am.astype(I32)
        vals.append(mx)
        sel.append(eidx == am)
        lg = jnp.where(sel[r], -jnp.inf, lg)
    ex = [jnp.exp(v - vals[0]) for v in vals]
    den = ex[0]
    for e in ex[1:]:
        den = den + e
    for r in range(TOP_K):
        tg_ref[r:r + 1, :] = ex[r] / den
    onehot = jnp.zeros(lg.shape, F32)
    for r in range(TOP_K):
        onehot = onehot + jnp.where(sel[r], 1.0, 0.0)
    incl = _dot(onehot.astype(BF16), tri_ref[...])
    excl = incl - onehot + run_ref[...]
    for r in range(TOP_K):
        pos_ref[r:r + 1, :] = jnp.sum(jnp.where(sel[r], excl, 0.0), axis=0,
                                      keepdims=True).astype(I32)
    run = run_ref[...] + incl[:, tm - 1:tm]
    run_ref[...] = run
    cnt_ref[...] = jnp.broadcast_to(run, cnt_ref.shape)


def _xattn(x1, g_xattn, w_xq, g_xq, kn, vm, w_xo, g_moe, w_rt, b_r, seq, n_mem):
    t, d = x1.shape
    xw = w_xq.shape[1]
    n_exp = w_rt.shape[0]
    tm = _tile(seq, 512)
    per = seq // tm
    const = lambda i: (0, 0)
    return pl.pallas_call(
        _xattn_kernel,
        grid=(t // tm,),
        in_specs=[pl.BlockSpec((tm, d), lambda i: (i, 0)),
                  pl.BlockSpec((1, d), const),
                  pl.BlockSpec((d, xw), const),
                  pl.BlockSpec((1, HEAD_DIM), const),
                  pl.BlockSpec((n_mem, xw), lambda i: (i // per, 0)),
                  pl.BlockSpec((n_mem, xw), lambda i: (i // per, 0)),
                  pl.BlockSpec((xw, d), const),
                  pl.BlockSpec((1, d), const),
                  pl.BlockSpec((n_exp, d), const),
                  pl.BlockSpec((n_exp, 1), const)],
        out_specs=[pl.BlockSpec((tm, d), lambda i: (i, 0)),
                   pl.BlockSpec((tm, d), lambda i: (i, 0)),
                   pl.BlockSpec((TOP_K, tm), lambda i: (0, i)),
                   pl.BlockSpec((TOP_K, tm), lambda i: (0, i)),
                   pl.BlockSpec((TOP_K, tm), lambda i: (0, i)),
                   pl.BlockSpec((n_exp, 128), const)],
        out_shape=[SDS((t, d), F32), SDS((t, d), F32),
                   SDS((TOP_K, t), I32), SDS((TOP_K, t), F32), SDS((TOP_K, t), I32),
                   SDS((n_exp, 128), F32)],
        scratch_shapes=[pltpu.VMEM((n_exp, 1), F32), pltpu.VMEM((tm, tm), BF16)],
        compiler_params=_cparams(1), name="xattn_router")(
            x1, g_xattn, w_xq, g_xq, kn, vm, w_xo, g_moe, w_rt, b_r)


def _take(values, idx):
    pick = idx[:, None] == jnp.arange(values.shape[0], dtype=I32)[None, :]
    return jnp.sum(jnp.where(pick, values[None, :], 0), axis=1).astype(I32)


def _route_tables(counts, n_assign, tm):
    n_exp = counts.shape[0]
    ts = 2 * tm
    valid = (counts + tm - 1) // tm * tm
    seg = ((counts + ts - 1) // ts * ts).astype(I32)
    seg_end = jnp.cumsum(seg).astype(I32)
    seg_start = seg_end - seg
    n_steps = n_assign // ts + n_exp
    n_used = seg_end[-1] // ts
    idx = jnp.arange(n_steps, dtype=I32)
    row0 = jnp.minimum(idx, n_used - 1) * ts
    exp = jnp.sum((row0[:, None] >= seg_end[None, :]).astype(I32), axis=1)
    exp = jnp.minimum(exp, n_exp - 1)
    n_valid = jnp.clip((_take(seg_start + valid, exp) - row0) // tm, 1, 2).astype(I32)
    prev = jnp.concatenate([jnp.full((1,), -1, I32), exp[:-1]])
    first = (exp != prev).astype(I32)
    rem = _take(seg_end, exp) // ts - idx
    cand = jnp.where(counts > 0, jnp.arange(n_exp, dtype=I32), n_exp)
    after = jnp.concatenate([lax.cummin(cand, reverse=True)[1:], jnp.full((1,), n_exp, I32)])
    nxt = _take(jnp.where(after >= n_exp, -1, after), exp)
    tables = (exp, first, rem.astype(I32), nxt, n_valid)
    return seg_start, seg_end, seg, tables, n_used.reshape(1).astype(I32)


def _block_tables(tables, n_used):
    exp, first, rem, nxt, n_valid = tables
    half = jnp.tile(jnp.arange(2, dtype=I32), exp.shape[0])
    rep = lambda v: jnp.repeat(v, 2)
    return (rep(exp), rep(first) * (1 - half), 2 * rep(rem) - half, rep(nxt),
            (half < rep(n_valid)).astype(I32)), 2 * n_used


def _dest_kernel(ps_ref, ti_ref, pos_ref, o_ref, *, n_exp):
    ti = ti_ref[...]
    acc = pos_ref[...]
    for e in range(n_exp):
        acc = acc + jnp.where(ti == e, ps_ref[e], 0)
    o_ref[...] = acc


def _dest_rows(pad_start, ti, pos):
    k, t = ti.shape
    kern = functools.partial(_dest_kernel, n_exp=pad_start.shape[0])
    grid_spec = pltpu.PrefetchScalarGridSpec(
        num_scalar_prefetch=1, grid=(1,),
        in_specs=[pl.BlockSpec((k, t), lambda i, ps: (0, 0)),
                  pl.BlockSpec((k, t), lambda i, ps: (0, 0))],
        out_specs=pl.BlockSpec((k, t), lambda i, ps: (0, 0)))
    return pl.pallas_call(kern, grid_spec=grid_spec, out_shape=SDS((k, t), I32),
                          compiler_params=_cparams(1), name="moe_dest")(pad_start, ti, pos)


def _dispatch_kernel(pe_ref, pc_ref, nu_ref, dst_ref, x_ref, xs_ref, buf, zbuf, sem, zsem,
                     *, tc, tm, n_exp, n_blk):
    i = pl.program_id(0)
    n = pl.num_programs(0)
    ns = x_ref.shape[1] // (2 * LANES)

    def zero_copy(row0):
        rows = pl.ds(pl.multiple_of(row0 * ns, tm * ns), tm * ns)
        return pltpu.make_async_copy(zbuf, xs_ref.at[rows], zsem)

    @pl.when(i == 0)
    def _():
        zbuf[...] = jnp.zeros(zbuf.shape, zbuf.dtype)
        for start in (True, False):
            for e in range(n_exp):
                @pl.when(pc_ref[e] > 0)
                def _(e=e):
                    cp = zero_copy(pe_ref[e] - tm)
                    cp.start() if start else cp.wait()

            def tail(b, c):
                cp = zero_copy(b * tm)
                cp.start() if start else cp.wait()
                return c
            lax.fori_loop(nu_ref[0], n_blk, tail, 0)

    slot = i % 2
    x = x_ref[...]
    for s in range(ns):
        lo = x[:, (2 * s) * LANES:(2 * s + 1) * LANES]
        hi = x[:, (2 * s + 1) * LANES:(2 * s + 2) * LANES]
        buf[slot, pl.ds(s, tc, stride=ns), :] = _pack_pair(lo, hi)
    for k in range(TOP_K):
        for r in range(tc):
            dst = pl.ds(pl.multiple_of(dst_ref[0, k, r] * ns, ns), ns)
            pltpu.make_async_copy(buf.at[slot, pl.ds(r * ns, ns)], xs_ref.at[dst],
                                  sem.at[slot]).start(priority=r % 2)

    def wait_slot(s):
        for k in range(TOP_K):
            pltpu.make_async_copy(buf.at[s], xs_ref.at[pl.ds(0, tc * ns)], sem.at[s]).wait()

    @pl.when(i >= 1)
    def _():
        wait_slot(1 - slot)

    @pl.when(i == n - 1)
    def _():
        wait_slot(slot)


def _dispatch(hm, dest, seg_end, seg, n_used, ts, n_steps):
    t, d = hm.shape
    tc = _tile(t, 128)
    n = t // tc
    ns = d // (2 * LANES)
    n_exp = seg_end.shape[0]
    dest3 = dest.reshape(TOP_K, n, tc).transpose(1, 0, 2)
    kern = functools.partial(_dispatch_kernel, tc=tc, tm=ts, n_exp=n_exp, n_blk=n_steps)
    grid_spec = pltpu.PrefetchScalarGridSpec(
        num_scalar_prefetch=3, grid=(n,),
        in_specs=[pl.BlockSpec((1, TOP_K, tc), lambda i, *_: (i, 0, 0), memory_space=pltpu.SMEM),
                  pl.BlockSpec((tc, d), lambda i, *_: (i, 0))],
        out_specs=pl.BlockSpec(memory_space=pl.ANY),
        scratch_shapes=[pltpu.VMEM((2, tc * ns, LANES), U32),
                        pltpu.VMEM((ts * ns, LANES), U32),
                        pltpu.SemaphoreType.DMA((2,)),
                        pltpu.SemaphoreType.DMA(())])
    return pl.pallas_call(
        kern, grid_spec=grid_spec,
        out_shape=SDS((n_steps * ts * ns, LANES), U32),
        compiler_params=_cparams(1), name="moe_dispatch")(seg_end, seg, n_used, dest3, hm)


W_CHUNK = 512


def _stream_weights(i, be_ref, first_ref, rem_ref, nxt_ref, w_hbm, wb_ref, stage_ref, st_ref, sem):
    n_chunks, _, chunk = wb_ref.shape[1:]

    def chunk_copy(e, c, slot):
        cols = pl.ds(pl.multiple_of(c * chunk, chunk), chunk)
        return pltpu.make_async_copy(w_hbm.at[e, :, cols], stage_ref.at[slot], sem.at[slot])

    def start_first_two(e):
        chunk_copy(e, 0, 0).start()
        chunk_copy(e, 1, 1).start()

    def cast_chunks(e, dst, done, issued, k):
        def body(_, carry):
            done, issued = carry
            slot = done % 2
            chunk_copy(e, done, slot).wait()
            wb_ref[dst, done] = stage_ref[slot].astype(BF16)

            @pl.when(issued < n_chunks)
            def _():
                chunk_copy(e, issued, slot).start()
            return done + 1, jnp.minimum(issued + 1, n_chunks)
        return lax.fori_loop(0, k, body, (done, issued))

    @pl.when(i == 0)
    def _():
        start_first_two(be_ref[0])
        cast_chunks(be_ref[0], 0, 0, 2, n_chunks)
        st_ref[0] = 1

    @pl.when(first_ref[i] == 1)
    def _():
        st_ref[0] = 1 - st_ref[0]
        st_ref[1] = 0
        st_ref[2] = 0

        @pl.when(nxt_ref[i] >= 0)
        def _():
            start_first_two(nxt_ref[i])
            st_ref[2] = 2

    cur = st_ref[0]

    @pl.when(nxt_ref[i] >= 0)
    def _():
        done = st_ref[1]
        rem = rem_ref[i]
        lead = (first_ref[i] == 1) & (rem > 1)
        left = jnp.where(first_ref[i] == 1, jnp.maximum(rem - 1, 1), rem)
        k = jnp.where(lead, 0, lax.div(n_chunks - done + left - 1, left))
        done, issued = cast_chunks(nxt_ref[i], 1 - cur, done, st_ref[2], k)
        st_ref[1] = done
        st_ref[2] = issued

    return cur


def _moe_up_kernel(be_ref, first_ref, rem_ref, nxt_ref, ok_ref, nu_ref, x_ref, b_ref, w_hbm, o_ref,
                   wb_ref, stage_ref, st_ref, sem):
    i = pl.program_id(0)
    tm = o_ref.shape[0]
    ns = x_ref.shape[0] // tm
    n_half = wb_ref.shape[1] // 2
    chunk = wb_ref.shape[3]
    f = n_half * chunk

    @pl.when(i < nu_ref[0])
    def _():
        cur = _stream_weights(i, be_ref, first_ref, rem_ref, nxt_ref, w_hbm, wb_ref, stage_ref,
                              st_ref, sem)

        @pl.when(ok_ref[i] == 1)
        def _():
            cols = []
            for s in range(ns):
                lo, hi = _unpack_pair(x_ref[pl.ds(s, tm, stride=ns), :])
                cols += [lo.astype(BF16), hi.astype(BF16)]
            x = jnp.concatenate(cols, axis=1)
            for n in range(n_half):
                lo, hi = n * chunk, (n + 1) * chunk
                gate = jnp.minimum(_dot(x, wb_ref[cur, n]) + b_ref[:, lo:hi], SWIGLU_LIMIT)
                up = jnp.clip(_dot(x, wb_ref[cur, n_half + n]) + b_ref[:, f + lo:f + hi],
                              -SWIGLU_LIMIT, SWIGLU_LIMIT)
                glu = gate * jax.nn.sigmoid(SWIGLU_ALPHA * gate)
                o_ref[:, lo:hi] = ((up + 1.0) * glu).astype(o_ref.dtype)

    @pl.when((i >= nu_ref[0]) | (ok_ref[i] == 0))
    def _():
        o_ref[...] = jnp.zeros(o_ref.shape, o_ref.dtype)


def _moe_up(xs, w_gu, b_gu, tables, n_used, tm):
    d = w_gu.shape[1]
    ns = d // (2 * LANES)
    n_rows = xs.shape[0] // ns
    two_f = w_gu.shape[2]
    chunk = min(W_CHUNK, two_f // 2)
    grid_spec = pltpu.PrefetchScalarGridSpec(
        num_scalar_prefetch=6,
        grid=(n_rows // tm,),
        in_specs=[pl.BlockSpec((tm * ns, LANES),
                               lambda i, be, fi, re, nx, ok, nu: (jnp.minimum(i, nu[0] - 1), 0)),
                  pl.BlockSpec((None, 1, two_f), lambda i, be, fi, re, nx, ok, nu: (be[i], 0, 0)),
                  pl.BlockSpec(memory_space=pl.ANY)],
        out_specs=pl.BlockSpec((tm, two_f // 2), lambda i, *_: (i, 0)),
        scratch_shapes=[pltpu.VMEM((2, two_f // chunk, d, chunk), BF16),
                        pltpu.VMEM((2, d, chunk), w_gu.dtype),
                        pltpu.SMEM((3,), I32),
                        pltpu.SemaphoreType.DMA((2,))])
    return pl.pallas_call(
        _moe_up_kernel, grid_spec=grid_spec,
        out_shape=SDS((n_rows, two_f // 2), BF16),
        compiler_params=_cparams(1), name="moe_up")(*tables, n_used, xs, b_gu, w_gu)


def _moe_down_kernel(be_ref, first_ref, rem_ref, nxt_ref, nv_ref, nu_ref, a_ref, b_ref, w_hbm, o_ref,
                     wb_ref, stage_ref, st_ref, sem):
    i = pl.program_id(0)
    ts = a_ref.shape[0]
    ns = o_ref.shape[0] // ts
    chunk = wb_ref.shape[3]

    def compute(cur, rows):
        a = a_ref[:rows, :]
        for n in range(wb_ref.shape[1]):
            lo, hi = n * chunk, (n + 1) * chunk
            y = _dot(a, wb_ref[cur, n]) + b_ref[:, lo:hi]
            for j in range(chunk // (2 * LANES)):
                s = lo // (2 * LANES) + j
                w = _pack_pair(y[:, (2 * j) * LANES:(2 * j + 1) * LANES],
                               y[:, (2 * j + 1) * LANES:(2 * j + 2) * LANES])
                o_ref[pl.ds(s, rows, stride=ns), :] = w

    @pl.when(i < nu_ref[0])
    def _():
        cur = _stream_weights(i, be_ref, first_ref, rem_ref, nxt_ref, w_hbm, wb_ref, stage_ref,
                              st_ref, sem)

        @pl.when(nv_ref[i] == 2)
        def _():
            compute(cur, ts)

        @pl.when(nv_ref[i] == 1)
        def _():
            compute(cur, ts // 2)
            half = ts // 2 * ns
            o_ref[half:, :] = jnp.zeros((half, LANES), o_ref.dtype)

    @pl.when(i >= nu_ref[0])
    def _():
        o_ref[...] = jnp.zeros(o_ref.shape, o_ref.dtype)


def _moe_down(act, w_dn, b_dn, tables, n_used, ts):
    n_rows, f = act.shape
    d = w_dn.shape[2]
    ns = d // (2 * LANES)
    chunk = max(min(W_CHUNK, d // 2), 2 * LANES)
    grid_spec = pltpu.PrefetchScalarGridSpec(
        num_scalar_prefetch=6,
        grid=(n_rows // ts,),
        in_specs=[pl.BlockSpec((ts, f),
                               lambda i, be, fi, re, nx, nv, nu: (jnp.minimum(i, nu[0] - 1), 0)),
                  pl.BlockSpec((None, 1, d), lambda i, be, fi, re, nx, nv, nu: (be[i], 0, 0)),
                  pl.BlockSpec(memory_space=pl.ANY)],
        out_specs=pl.BlockSpec((ts * ns, LANES), lambda i, *_: (i, 0)),
        scratch_shapes=[pltpu.VMEM((2, d // chunk, f, chunk), BF16),
                        pltpu.VMEM((2, f, chunk), w_dn.dtype),
                        pltpu.SMEM((3,), I32),
                        pltpu.SemaphoreType.DMA((2,))])
    return pl.pallas_call(
        _moe_down_kernel, grid_spec=grid_spec,
        out_shape=SDS((n_rows * ns, LANES), U32),
        compiler_params=_cparams(1), name="moe_down")(*tables, n_used, act, b_dn, w_dn)


def _combine_kernel(dst_ref, x_ref, g_ref, ys_ref, o_ref, buf, sem, *, tc):
    i = pl.program_id(0)
    n = pl.num_programs(0) - 1
    ns = x_ref.shape[1] // (2 * LANES)

    @pl.when(i < n)
    def _():
        slot = i % 2
        for k in range(TOP_K):
            for r in range(tc):
                src = pl.ds(pl.multiple_of(dst_ref[0, k, r] * ns, ns), ns)
                pltpu.make_async_copy(ys_ref.at[src], buf.at[slot, k, pl.ds(r * ns, ns)],
                                      sem.at[slot]).start(priority=r % 2)

    @pl.when(i >= 1)
    def _():
        slot = (i - 1) % 2
        for k in range(TOP_K):
            pltpu.make_async_copy(ys_ref.at[pl.ds(0, tc * ns)], buf.at[slot, k],
                                  sem.at[slot]).wait()
        g = g_ref[...]
        g = jnp.concatenate([g, jnp.zeros((8 - TOP_K, tc), F32)], axis=0).T
        for s in range(ns):
            c0, c1, c2 = (2 * s) * LANES, (2 * s + 1) * LANES, (2 * s + 2) * LANES
            acc_lo = x_ref[:, c0:c1]
            acc_hi = x_ref[:, c1:c2]
            for k in range(TOP_K):
                lo, hi = _unpack_pair(buf[slot, k, pl.ds(s, tc, stride=ns), :])
                acc_lo = acc_lo + g[:, k:k + 1] * lo
                acc_hi = acc_hi + g[:, k:k + 1] * hi
            o_ref[:, c0:c1] = acc_lo
            o_ref[:, c1:c2] = acc_hi


def _combine(x2, gates, dest, ys):
    t, d = x2.shape
    tc = _tile(t, 128)
    n = t // tc
    ns = d // (2 * LANES)
    dest3 = dest.reshape(TOP_K, n, tc).transpose(1, 0, 2)
    kern = functools.partial(_combine_kernel, tc=tc)
    prev = lambda i: (jnp.maximum(i - 1, 0), 0)
    return pl.pallas_call(
        kern,
        grid=(n + 1,),
        in_specs=[pl.BlockSpec((1, TOP_K, tc), lambda i: (jnp.minimum(i, n - 1), 0, 0),
                               memory_space=pltpu.SMEM),
                  pl.BlockSpec((tc, d), prev),
                  pl.BlockSpec((TOP_K, tc), lambda i: (0, jnp.maximum(i - 1, 0))),
                  pl.BlockSpec(memory_space=pl.ANY)],
        out_specs=pl.BlockSpec((tc, d), prev),
        out_shape=SDS((t, d), F32),
        scratch_shapes=[pltpu.VMEM((2, TOP_K, tc * ns, LANES), U32),
                        pltpu.SemaphoreType.DMA((2,))],
        compiler_params=_cparams(1), name="moe_combine")(dest3, x2, gates, ys)


def _layer(x2d, mem2d, batch, seq, n_mem, p):
    t, d = x2d.shape
    c = p["conv_w"].shape[1]
    fw = p["g_fox_out"].shape[0]
    n_heads = fw // HEAD_DIM
    n_main = 3 * c + 3 * fw
    n_exp = p["w_router"].shape[1]

    w_in = p["w_mix_in"].astype(BF16)
    wf_t = jnp.zeros((F_ROWS, d), BF16).at[:n_heads].set(w_in[:, n_main:].T)
    b_col = jnp.zeros((F_ROWS, 1), F32).at[:n_heads, 0].set(p["b_forget"])

    proj, ft = _mix_in(x2d, p["g_mix"][None, :], wf_t, w_in, n_main)
    cdec = _decay(ft, b_col, seq)
    y_fox = _fox_attention(proj, cdec, p["g_q"][None, :], p["g_k"][None, :],
                           batch, seq, n_heads, 3 * c, fw)
    x1 = _mix_out(proj, y_fox, p["conv_w"], p["g_conv_out"][None, :], p["g_fox_out"][None, :],
                  p["w_mix_out"].astype(BF16), x2d, batch, seq)

    kn, vm = _memkv(mem2d, p["g_mem"][None, :], p["w_xkv"].astype(BF16),
                    p["g_xk"][None, :], n_mem)
    x2, hm, ti, tg, pos, cnt = _xattn(x1, p["g_xattn"][None, :], p["w_xq"].astype(BF16),
                            p["g_xq"][None, :], kn, vm, p["w_xo"].astype(BF16),
                            p["g_moe"][None, :], p["w_router"].T.astype(BF16),
                            p["b_router"][:, None], seq, n_mem)

    tm = _tile(TOP_K * t, 512) // 2
    counts = cnt[:, 0].astype(I32)
    seg_start, seg_end, seg, tables, n_used = _route_tables(counts, TOP_K * t, tm)
    dest = _dest_rows(seg_start, ti, pos)
    xs = _dispatch(hm, dest, seg_end, seg, n_used, 2 * tm, tables[0].shape[0])
    act = _moe_up(xs, p["w_gate_up"], p["b_gate_up"][:, None, :], *_block_tables(tables, n_used), tm)
    ys = _moe_down(act, p["w_down"], p["b_down"][:, None, :], tables, n_used, 2 * tm)
    return _combine(x2, tg, dest, ys)


def kernel(x, mem, g_mix, w_mix_in, b_forget, conv_w, g_q, g_k, g_conv_out, g_fox_out, w_mix_out, g_xattn, g_mem, w_xq, w_xkv, g_xq, g_xk, w_xo, g_moe, w_router, b_router, w_gate_up, b_gate_up, w_down, b_down):
    batch, seq, d = x.shape
    n_mem = mem.shape[1]
    params = dict(g_mix=g_mix, w_mix_in=w_mix_in, b_forget=b_forget, conv_w=conv_w, g_q=g_q,
                  g_k=g_k, g_conv_out=g_conv_out, g_fox_out=g_fox_out, w_mix_out=w_mix_out,
                  g_xattn=g_xattn, g_mem=g_mem, w_xq=w_xq, w_xkv=w_xkv, g_xq=g_xq, g_xk=g_xk,
                  w_xo=w_xo, g_moe=g_moe, w_router=w_router, b_router=b_router,
                  w_gate_up=w_gate_up, b_gate_up=b_gate_up, w_down=w_down, b_down=b_down)
    x2d = x.reshape(batch * seq, d)
    mem2d = mem.reshape(batch * n_mem, d)
    for l in range(g_mix.shape[0]):
        x2d = _layer(x2d, mem2d, batch, seq, n_mem, {k: v[l] for k, v in params.items()})
    return x2d.reshape(batch, seq, d)
```

```python
import functools

import jax
import jax.numpy as jnp
from jax import lax
from jax.experimental import pallas as pl
from jax.experimental.pallas import tpu as pltpu

F32 = jnp.float32
BF16 = jnp.bfloat16
I32 = jnp.int32
U32 = jnp.uint32
SDS = jax.ShapeDtypeStruct

EPS = 1e-6
HEAD_DIM = 128
TOP_K = 4
SWIGLU_LIMIT = 7.0
SWIGLU_ALPHA = 1.702
MASK_VALUE = float(jnp.finfo(jnp.float32).min)
V7X_VMEM_LIMIT_BYTES = 56 * 1024 * 1024
F_ROWS = 16
LANES = 128


def _cparams(n_axes):
    return pltpu.CompilerParams(
        dimension_semantics=("arbitrary",) * n_axes,
        vmem_limit_bytes=V7X_VMEM_LIMIT_BYTES)


def _rms(x, g):
    return x * lax.rsqrt(jnp.mean(x * x, axis=-1, keepdims=True) + EPS) * g


def _dot(a, b):
    return jnp.dot(a, b, preferred_element_type=F32)


def _dot_nt(a, b):
    return lax.dot_general(a, b, (((1,), (1,)), ((), ())), preferred_element_type=F32)


def _tile(n, pref):
    return pref if n % pref == 0 else n


def _pack_pair(lo, hi):
    lo = lax.bitcast_convert_type(lo.astype(BF16).astype(F32), U32) >> 16
    hi = lax.bitcast_convert_type(hi.astype(BF16).astype(F32), U32) & jnp.uint32(0xFFFF0000)
    return hi | lo


def _unpack_pair(w):
    lo = lax.bitcast_convert_type(w << 16, F32)
    hi = lax.bitcast_convert_type(w & jnp.uint32(0xFFFF0000), F32)
    return lo, hi


def _mix_in_kernel(x_ref, g_ref, wf_ref, w_ref, o_ref, ft_ref, h_ref):
    @pl.when(pl.program_id(1) == 0)
    def _():
        hb = _rms(x_ref[...], g_ref[...]).astype(BF16)
        h_ref[...] = hb
        ft_ref[...] = _dot_nt(wf_ref[...], hb)

    o_ref[...] = _dot(h_ref[...], w_ref[...]).astype(o_ref.dtype)


def _mix_in(x2d, g, wf_t, w_in, n_out):
    t, d = x2d.shape
    tm = _tile(t, 1024)
    tn = _tile(n_out, 1024)
    return pl.pallas_call(
        _mix_in_kernel,
        grid=(t // tm, n_out // tn),
        in_specs=[pl.BlockSpec((tm, d), lambda i, j: (i, 0)),
                  pl.BlockSpec((1, d), lambda i, j: (0, 0)),
                  pl.BlockSpec((F_ROWS, d), lambda i, j: (0, 0)),
                  pl.BlockSpec((d, tn), lambda i, j: (0, j))],
        out_specs=[pl.BlockSpec((tm, tn), lambda i, j: (i, j)),
                   pl.BlockSpec((F_ROWS, tm), lambda i, j: (0, i))],
        out_shape=[SDS((t, n_out), BF16), SDS((F_ROWS, t), F32)],
        scratch_shapes=[pltpu.VMEM((tm, d), BF16)],
        compiler_params=_cparams(2), name="mix_in")(x2d, g, wf_t, w_in)


def _decay_kernel(ft_ref, b_ref, c_ref):
    z = ft_ref[...] + b_ref[...]
    lf = jnp.minimum(z, 0.0) - jnp.log(1.0 + jnp.exp(-jnp.abs(z)))
    s = lf.shape[1]
    lane = lax.broadcasted_iota(I32, lf.shape, 1)
    d = 1
    while d < s:
        lf = lf + jnp.where(lane >= d, pltpu.roll(lf, d, 1), 0.0)
        d *= 2
    c_ref[...] = lf


def _decay(ft, b_col, seq):
    rows, t = ft.shape
    return pl.pallas_call(
        _decay_kernel,
        grid=(t // seq,),
        in_specs=[pl.BlockSpec((rows, seq), lambda b: (0, b)),
                  pl.BlockSpec((rows, 1), lambda b: (0, 0))],
        out_specs=pl.BlockSpec((rows, seq), lambda b: (0, b)),
        out_shape=SDS((rows, t), F32),
        compiler_params=_cparams(1), name="decay")(ft, b_col)


def _fox_kernel(q_ref, k_ref, v_ref, c_ref, gq_ref, gk_ref, o_ref, kn_ref, nc_ref, *, tq, hpb):
    g = pl.program_id(1)
    seq = q_ref.shape[0]
    hd = HEAD_DIM
    scale = HEAD_DIM ** -0.5
    for j in range(hpb):
        cs = slice(j * hd, (j + 1) * hd)
        kn_ref[:, cs] = _rms(k_ref[:, cs].astype(F32), gk_ref[...]).astype(BF16)
        nc_ref[j:j + 1, :] = -c_ref[pl.ds(g * hpb + j, 1), :]
    row = lax.broadcasted_iota(I32, (tq, tq), 0)
    col = lax.broadcasted_iota(I32, (tq, tq), 1)
    causal = col <= row
    for qi in range(seq // tq):
        lo, hi = qi * tq, (qi + 1) * tq
        for j in range(hpb):
            cs = slice(j * hd, (j + 1) * hd)
            q = (_rms(q_ref[lo:hi, cs].astype(F32), gq_ref[...]) * scale).astype(BF16)
            s_diag = jnp.where(causal, _dot_nt(q, kn_ref[lo:hi, cs]) + nc_ref[j:j + 1, lo:hi],
                               MASK_VALUE)
            m = jnp.max(s_diag, axis=-1, keepdims=True)
            if qi > 0:
                s_past = _dot_nt(q, kn_ref[:lo, cs]) + nc_ref[j:j + 1, :lo]
                m = jnp.maximum(m, jnp.max(s_past, axis=-1, keepdims=True))
            p_diag = jnp.exp(s_diag - m)
            l = jnp.sum(p_diag, axis=-1, keepdims=True)
            acc = _dot(p_diag.astype(BF16), v_ref[lo:hi, cs])
            if qi > 0:
                p_past = jnp.exp(s_past - m)
                l = l + jnp.sum(p_past, axis=-1, keepdims=True)
                acc = acc + _dot(p_past.astype(BF16), v_ref[:lo, cs])
            o_ref[lo:hi, cs] = (acc / l).astype(o_ref.dtype)


def _fox_attention(proj, c, g_q, g_k, batch, seq, n_heads, col0, width):
    t = proj.shape[0]
    tq = _tile(seq, 256)
    hpb = 2 if n_heads % 2 == 0 else 1
    bw = hpb * HEAD_DIM
    cb = col0 // bw
    wb = width // bw
    kern = functools.partial(_fox_kernel, tq=tq, hpb=hpb)
    return pl.pallas_call(
        kern,
        grid=(batch, n_heads // hpb),
        in_specs=[pl.BlockSpec((seq, bw), lambda b, h: (b, cb + h)),
                  pl.BlockSpec((seq, bw), lambda b, h: (b, cb + wb + h)),
                  pl.BlockSpec((seq, bw), lambda b, h: (b, cb + 2 * wb + h)),
                  pl.BlockSpec((F_ROWS, seq), lambda b, h: (0, b)),
                  pl.BlockSpec((1, HEAD_DIM), lambda b, h: (0, 0)),
                  pl.BlockSpec((1, HEAD_DIM), lambda b, h: (0, 0))],
        out_specs=pl.BlockSpec((seq, bw), lambda b, h: (b, h)),
        out_shape=SDS((t, width), BF16),
        scratch_shapes=[pltpu.VMEM((seq, bw), BF16), pltpu.VMEM((8, seq), F32)],
        compiler_params=_cparams(2), name="fox_attention")(proj, proj, proj, c, g_q, g_k)


def _mix_out_kernel(ub_ref, uc_ref, ux_ref, hc_ref, hx_ref, yf_ref, cw_ref, gc_ref, gf_ref,
                    w_ref, x_ref, o_ref):
    si = pl.program_id(1)
    p = uc_ref[...].astype(F32) * ux_ref[...].astype(F32)
    halo = hc_ref[...].astype(F32) * hx_ref[...].astype(F32)
    halo = jnp.where(si > 0, halo, 0.0)
    row = lax.broadcasted_iota(I32, p.shape, 0)
    p1 = jnp.where(row == 0, halo[7:8, :], pltpu.roll(p, 1, 0))
    p2 = jnp.where(row == 0, halo[6:7, :],
                   jnp.where(row == 1, halo[7:8, :], pltpu.roll(p, 2, 0)))
    w = cw_ref[...]
    y = ub_ref[...].astype(F32) * (w[0:1, :] * p2 + w[1:2, :] * p1 + w[2:3, :] * p)
    c = y.shape[1]
    y_conv = _rms(y, gc_ref[...]).astype(BF16)
    y_fox = _rms(yf_ref[...].astype(F32), gf_ref[...]).astype(BF16)
    o_ref[...] = x_ref[...] + _dot(y_conv, w_ref[:c, :]) + _dot(y_fox, w_ref[c:, :])


def _mix_out(proj, y_fox, conv_w, g_conv, g_fox, w_out, x2d, batch, seq):
    t, d = x2d.shape
    c = conv_w.shape[1]
    w = y_fox.shape[1]
    ts = _tile(seq, 512)
    ns = seq // ts
    hb = ts // 8
    tile = lambda col: (lambda b, s: (b * ns + s, col))
    halo = lambda col: (lambda b, s: (jnp.maximum((b * ns + s) * hb - 1, 0), col))
    const = lambda b, s: (0, 0)
    return pl.pallas_call(
        _mix_out_kernel,
        grid=(batch, ns),
        in_specs=[pl.BlockSpec((ts, c), tile(0)),
                  pl.BlockSpec((ts, c), tile(1)),
                  pl.BlockSpec((ts, c), tile(2)),
                  pl.BlockSpec((8, c), halo(1)),
                  pl.BlockSpec((8, c), halo(2)),
                  pl.BlockSpec((ts, w), tile(0)),
                  pl.BlockSpec((3, c), const),
                  pl.BlockSpec((1, c), const),
                  pl.BlockSpec((1, w), const),
                  pl.BlockSpec((c + w, d), const, pipeline_mode=pl.Buffered(1)),
                  pl.BlockSpec((ts, d), tile(0))],
        out_specs=pl.BlockSpec((ts, d), tile(0)),
        out_shape=SDS((t, d), F32),
        compiler_params=_cparams(2), name="mix_out")(
            proj, proj, proj, proj, proj, y_fox, conv_w, g_conv, g_fox, w_out, x2d)


def _memkv_kernel(mem_ref, g_ref, w_ref, gk_ref, k_ref, v_ref):
    hm = _rms(mem_ref[...], g_ref[...]).astype(BF16)
    kv = _dot(hm, w_ref[...])
    xw = k_ref.shape[1]
    for h in range(xw // HEAD_DIM):
        sl = slice(h * HEAD_DIM, (h + 1) * HEAD_DIM)
        k_ref[:, sl] = _rms(kv[:, sl], gk_ref[...]).astype(BF16)
    v_ref[...] = kv[:, xw:].astype(BF16)


def _memkv(mem2d, g_mem, w_xkv, g_xk, n_mem):
    tm_, d = mem2d.shape
    xw = w_xkv.shape[1] // 2
    return pl.pallas_call(
        _memkv_kernel,
        grid=(tm_ // n_mem,),
        in_specs=[pl.BlockSpec((n_mem, d), lambda b: (b, 0)),
                  pl.BlockSpec((1, d), lambda b: (0, 0)),
                  pl.BlockSpec((d, 2 * xw), lambda b: (0, 0)),
                  pl.BlockSpec((1, HEAD_DIM), lambda b: (0, 0))],
        out_specs=[pl.BlockSpec((n_mem, xw), lambda b: (b, 0)),
                   pl.BlockSpec((n_mem, xw), lambda b: (b, 0))],
        out_shape=[SDS((tm_, xw), BF16), SDS((tm_, xw), BF16)],
        compiler_params=_cparams(1), name="memkv")(mem2d, g_mem, w_xkv, g_xk)


def _xattn_kernel(x_ref, gx_ref, wq_ref, gq_ref, kn_ref, v_ref, wo_ref, gm_ref, wr_ref, br_ref,
                  x2_ref, hm_ref, ti_ref, tg_ref, pos_ref, cnt_ref, run_ref, tri_ref):
    i = pl.program_id(0)
    tm = x_ref.shape[0]

    @pl.when(i == 0)
    def _():
        run_ref[...] = jnp.zeros(run_ref.shape, F32)
        r = lax.broadcasted_iota(I32, (tm, tm), 0)
        c = lax.broadcasted_iota(I32, (tm, tm), 1)
        tri_ref[...] = jnp.where(r <= c, 1.0, 0.0).astype(BF16)

    x = x_ref[...]
    hb = _rms(x, gx_ref[...]).astype(BF16)
    q = _dot(hb, wq_ref[...])
    scale = HEAD_DIM ** -0.5
    outs = []
    for h in range(q.shape[1] // HEAD_DIM):
        sl = slice(h * HEAD_DIM, (h + 1) * HEAD_DIM)
        qh = (_rms(q[:, sl], gq_ref[...]) * scale).astype(BF16)
        s = _dot_nt(qh, kn_ref[:, sl])
        p = jnp.exp(s - jnp.max(s, axis=-1, keepdims=True))
        oh = _dot(p.astype(BF16), v_ref[:, sl]) / jnp.sum(p, axis=-1, keepdims=True)
        outs.append(oh.astype(BF16))
    x2 = x + _dot(jnp.concatenate(outs, axis=-1), wo_ref[...])
    x2_ref[...] = x2
    hm = _rms(x2, gm_ref[...])
    hm_ref[...] = hm
    lg = _dot_nt(wr_ref[...], hm.astype(BF16)) + br_ref[...]
    n_exp = lg.shape[0]
    eidx = lax.broadcasted_iota(I32, lg.shape, 0).astype(F32)
    vals, sel = [], []
    for r in range(TOP_K):
        mx = jnp.max(lg, axis=0, keepdims=True)
        am = jnp.min(jnp.where(lg == mx, eidx, float(n_exp)), axis=0, keepdims=True)
        ti_ref[r:r + 1, :] = am.astype(I32)
        vals.append(mx)
        sel.append(eidx == am)
        lg = jnp.where(sel[r], -jnp.inf, lg)
    ex = [jnp.exp(v - vals[0]) for v in vals]
    den = ex[0]
    for e in ex[1:]:
        den = den + e
    for r in range(TOP_K):
        tg_ref[r:r + 1, :] = ex[r] / den
    onehot = jnp.zeros(lg.shape, F32)
    for r in range(TOP_K):
        onehot = onehot + jnp.where(sel[r], 1.0, 0.0)
    incl = _dot(onehot.astype(BF16), tri_ref[...])
    excl = incl - onehot + run_ref[...]
    for r in range(TOP_K):
        pos_ref[r:r + 1, :] = jnp.sum(jnp.where(sel[r], excl, 0.0), axis=0,
                                      keepdims=True).astype(I32)
    run = run_ref[...] + incl[:, tm - 1:tm]
    run_ref[...] = run
    cnt_ref[...] = jnp.broadcast_to(run, cnt_ref.shape)


def _xattn(x1, g_xattn, w_xq, g_xq, kn, vm, w_xo, g_moe, w_rt, b_r, seq, n_mem):
    t, d = x1.shape
    xw = w_xq.shape[1]
    n_exp = w_rt.shape[0]
    tm = _tile(seq, 512)
    per = seq // tm
    const = lambda i: (0, 0)
    return pl.pallas_call(
        _xattn_kernel,
        grid=(t // tm,),
        in_specs=[pl.BlockSpec((tm, d), lambda i: (i, 0)),
                  pl.BlockSpec((1, d), const),
                  pl.BlockSpec((d, xw), const),
                  pl.BlockSpec((1, HEAD_DIM), const),
                  pl.BlockSpec((n_mem, xw), lambda i: (i // per, 0)),
                  pl.BlockSpec((n_mem, xw), lambda i: (i // per, 0)),
                  pl.BlockSpec((xw, d), const),
                  pl.BlockSpec((1, d), const),
                  pl.BlockSpec((n_exp, d), const),
                  pl.BlockSpec((n_exp, 1), const)],
        out_specs=[pl.BlockSpec((tm, d), lambda i: (i, 0)),
                   pl.BlockSpec((tm, d), lambda i: (i, 0)),
                   pl.BlockSpec((TOP_K, tm), lambda i: (0, i)),
                   pl.BlockSpec((TOP_K, tm), lambda i: (0, i)),
                   pl.BlockSpec((TOP_K, tm), lambda i: (0, i)),
                   pl.BlockSpec((n_exp, 128), const)],
        out_shape=[SDS((t, d), F32), SDS((t, d), F32),
                   SDS((TOP_K, t), I32), SDS((TOP_K, t), F32), SDS((TOP_K, t), I32),
                   SDS((n_exp, 128), F32)],
        scratch_shapes=[pltpu.VMEM((n_exp, 1), F32), pltpu.VMEM((tm, tm), BF16)],
        compiler_params=_cparams(1), name="xattn_router")(
            x1, g_xattn, w_xq, g_xq, kn, vm, w_xo, g_moe, w_rt, b_r)


def _take(values, idx):
    pick = idx[:, None] == jnp.arange(values.shape[0], dtype=I32)[None, :]
    return jnp.sum(jnp.where(pick, values[None, :], 0), axis=1).astype(I32)


def _route_tables(counts, n_assign, tm):
    n_exp = counts.shape[0]
    ts = 2 * tm
    valid = (counts + tm - 1) // tm * tm
    seg = ((counts + ts - 1) // ts * ts).astype(I32)
    seg_end = jnp.cumsum(seg).astype(I32)
    seg_start = seg_end - seg
    n_steps = n_assign // ts + n_exp
    n_used = seg_end[-1] // ts
    idx = jnp.arange(n_steps, dtype=I32)
    row0 = jnp.minimum(idx, n_used - 1) * ts
    exp = jnp.sum((row0[:, None] >= seg_end[None, :]).astype(I32), axis=1)
    exp = jnp.minimum(exp, n_exp - 1)
    n_valid = jnp.clip((_take(seg_start + valid, exp) - row0) // tm, 1, 2).astype(I32)
    prev = jnp.concatenate([jnp.full((1,), -1, I32), exp[:-1]])
    first = (exp != prev).astype(I32)
    rem = _take(seg_end, exp) // ts - idx
    cand = jnp.where(counts > 0, jnp.arange(n_exp, dtype=I32), n_exp)
    after = jnp.concatenate([lax.cummin(cand, reverse=True)[1:], jnp.full((1,), n_exp, I32)])
    nxt = _take(jnp.where(after >= n_exp, -1, after), exp)
    tables = (exp, first, rem.astype(I32), nxt, n_valid)
    return seg_start, seg_end, seg, tables, n_used.reshape(1).astype(I32)


def _block_tables(tables, n_used):
    exp, first, rem, nxt, n_valid = tables
    half = jnp.tile(jnp.arange(2, dtype=I32), exp.shape[0])
    rep = lambda v: jnp.repeat(v, 2)
    return (rep(exp), rep(first) * (1 - half), 2 * rep(rem) - half, rep(nxt),
            (half < rep(n_valid)).astype(I32)), 2 * n_used


def _dest_kernel(ps_ref, ti_ref, pos_ref, o_ref, *, n_exp):
    ti = ti_ref[...]
    acc = pos_ref[...]
    for e in range(n_exp):
        acc = acc + jnp.where(ti == e, ps_ref[e], 0)
    o_ref[...] = acc


def _dest_rows(pad_start, ti, pos):
    k, t = ti.shape
    kern = functools.partial(_dest_kernel, n_exp=pad_start.shape[0])
    grid_spec = pltpu.PrefetchScalarGridSpec(
        num_scalar_prefetch=1, grid=(1,),
        in_specs=[pl.BlockSpec((k, t), lambda i, ps: (0, 0)),
                  pl.BlockSpec((k, t), lambda i, ps: (0, 0))],
        out_specs=pl.BlockSpec((k, t), lambda i, ps: (0, 0)))
    return pl.pallas_call(kern, grid_spec=grid_spec, out_shape=SDS((k, t), I32),
                          compiler_params=_cparams(1), name="moe_dest")(pad_start, ti, pos)


def _dispatch_kernel(pe_ref, pc_ref, nu_ref, dst_ref, x_ref, xs_ref, buf, zbuf, sem, zsem,
                     *, tc, tm, n_exp, n_blk):
    i = pl.program_id(0)
    n = pl.num_programs(0)
    ns = x_ref.shape[1] // (2 * LANES)

    def zero_copy(row0):
        rows = pl.ds(pl.multiple_of(row0 * ns, tm * ns), tm * ns)
        return pltpu.make_async_copy(zbuf, xs_ref.at[rows], zsem)

    @pl.when(i == 0)
    def _():
        zbuf[...] = jnp.zeros(zbuf.shape, zbuf.dtype)
        for start in (True, False):
            for e in range(n_exp):
                @pl.when(pc_ref[e] > 0)
                def _(e=e):
                    cp = zero_copy(pe_ref[e] - tm)
                    cp.start() if start else cp.wait()

            def tail(b, c):
                cp = zero_copy(b * tm)
                cp.start() if start else cp.wait()
                return c
            lax.fori_loop(nu_ref[0], n_blk, tail, 0)

    slot = i % 2
    x = x_ref[...]
    for s in range(ns):
        lo = x[:, (2 * s) * LANES:(2 * s + 1) * LANES]
        hi = x[:, (2 * s + 1) * LANES:(2 * s + 2) * LANES]
        buf[slot, pl.ds(s, tc, stride=ns), :] = _pack_pair(lo, hi)
    for k in range(TOP_K):
        for r in range(tc):
            dst = pl.ds(pl.multiple_of(dst_ref[0, k, r] * ns, ns), ns)
            pltpu.make_async_copy(buf.at[slot, pl.ds(r * ns, ns)], xs_ref.at[dst],
                                  sem.at[slot]).start(priority=r % 2)

    def wait_slot(s):
        for k in range(TOP_K):
            pltpu.make_async_copy(buf.at[s], xs_ref.at[pl.ds(0, tc * ns)], sem.at[s]).wait()

    @pl.when(i >= 1)
    def _():
        wait_slot(1 - slot)

    @pl.when(i == n - 1)
    def _():
        wait_slot(slot)


def _dispatch(hm, dest, seg_end, seg, n_used, ts, n_steps):
    t, d = hm.shape
    tc = _tile(t, 128)
    n = t // tc
    ns = d // (2 * LANES)
    n_exp = seg_end.shape[0]
    dest3 = dest.reshape(TOP_K, n, tc).transpose(1, 0, 2)
    kern = functools.partial(_dispatch_kernel, tc=tc, tm=ts, n_exp=n_exp, n_blk=n_steps)
    grid_spec = pltpu.PrefetchScalarGridSpec(
        num_scalar_prefetch=3, grid=(n,),
        in_specs=[pl.BlockSpec((1, TOP_K, tc), lambda i, *_: (i, 0, 0), memory_space=pltpu.SMEM),
                  pl.BlockSpec((tc, d), lambda i, *_: (i, 0))],
        out_specs=pl.BlockSpec(memory_space=pl.ANY),
        scratch_shapes=[pltpu.VMEM((2, tc * ns, LANES), U32),
                        pltpu.VMEM((ts * ns, LANES), U32),
                        pltpu.SemaphoreType.DMA((2,)),
                        pltpu.SemaphoreType.DMA(())])
    return pl.pallas_call(
        kern, grid_spec=grid_spec,
        out_shape=SDS((n_steps * ts * ns, LANES), U32),
        compiler_params=_cparams(1), name="moe_dispatch")(seg_end, seg, n_used, dest3, hm)


W_CHUNK = 512


def _stream_weights(i, be_ref, first_ref, rem_ref, nxt_ref, w_hbm, wb_ref, stage_ref, st_ref, sem):
    n_chunks, _, chunk = wb_ref.shape[1:]

    def chunk_copy(e, c, slot):
        cols = pl.ds(pl.multiple_of(c * chunk, chunk), chunk)
        return pltpu.make_async_copy(w_hbm.at[e, :, cols], stage_ref.at[slot], sem.at[slot])

    def start_first_two(e):
        chunk_copy(e, 0, 0).start()
        chunk_copy(e, 1, 1).start()

    def cast_chunks(e, dst, done, issued, k):
        def body(_, carry):
            done, issued = carry
            slot = done % 2
            chunk_copy(e, done, slot).wait()
            wb_ref[dst, done] = stage_ref[slot].astype(BF16)

            @pl.when(issued < n_chunks)
            def _():
                chunk_copy(e, issued, slot).start()
            return done + 1, jnp.minimum(issued + 1, n_chunks)
        return lax.fori_loop(0, k, body, (done, issued))

    @pl.when(i == 0)
    def _():
        start_first_two(be_ref[0])
        cast_chunks(be_ref[0], 0, 0, 2, n_chunks)
        st_ref[0] = 1

    @pl.when(first_ref[i] == 1)
    def _():
        st_ref[0] = 1 - st_ref[0]
        st_ref[1] = 0
        st_ref[2] = 0

        @pl.when(nxt_ref[i] >= 0)
        def _():
            start_first_two(nxt_ref[i])
            st_ref[2] = 2

    cur = st_ref[0]

    @pl.when(nxt_ref[i] >= 0)
    def _():
        done = st_ref[1]
        rem = rem_ref[i]
        lead = (first_ref[i] == 1) & (rem > 1)
        left = jnp.where(first_ref[i] == 1, jnp.maximum(rem - 1, 1), rem)
        k = jnp.where(lead, 0, lax.div(n_chunks - done + left - 1, left))
        done, issued = cast_chunks(nxt_ref[i], 1 - cur, done, st_ref[2], k)
        st_ref[1] = done
        st_ref[2] = issued

    return cur


def _moe_up_kernel(be_ref, first_ref, rem_ref, nxt_ref, ok_ref, nu_ref, x_ref, b_ref, w_hbm, o_ref,
                   wb_ref, stage_ref, st_ref, sem):
    i = pl.program_id(0)
    tm = o_ref.shape[0]
    ns = x_ref.shape[0] // tm
    n_half = wb_ref.shape[1] // 2
    chunk = wb_ref.shape[3]
    f = n_half * chunk

    @pl.when(i < nu_ref[0])
    def _():
        cur = _stream_weights(i, be_ref, first_ref, rem_ref, nxt_ref, w_hbm, wb_ref, stage_ref,
                              st_ref, sem)

        @pl.when(ok_ref[i] == 1)
        def _():
            cols = []
            for s in range(ns):
                lo, hi = _unpack_pair(x_ref[pl.ds(s, tm, stride=ns), :])
                cols += [lo.astype(BF16), hi.astype(BF16)]
            x = jnp.concatenate(cols, axis=1)
            for n in range(n_half):
                lo, hi = n * chunk, (n + 1) * chunk
                gate = jnp.minimum(_dot(x, wb_ref[cur, n]) + b_ref[:, lo:hi], SWIGLU_LIMIT)
                up = jnp.clip(_dot(x, wb_ref[cur, n_half + n]) + b_ref[:, f + lo:f + hi],
                              -SWIGLU_LIMIT, SWIGLU_LIMIT)
                glu = gate * jax.nn.sigmoid(SWIGLU_ALPHA * gate)
                o_ref[:, lo:hi] = ((up + 1.0) * glu).astype(o_ref.dtype)

    @pl.when((i >= nu_ref[0]) | (ok_ref[i] == 0))
    def _():
        o_ref[...] = jnp.zeros(o_ref.shape, o_ref.dtype)


def _moe_up(xs, w_gu, b_gu, tables, n_used, tm):
    d = w_gu.shape[1]
    ns = d // (2 * LANES)
    n_rows = xs.shape[0] // ns
    two_f = w_gu.shape[2]
    chunk = min(W_CHUNK, two_f // 2)
    grid_spec = pltpu.PrefetchScalarGridSpec(
        num_scalar_prefetch=6,
        grid=(n_rows // tm,),
        in_specs=[pl.BlockSpec((tm * ns, LANES),
                               lambda i, be, fi, re, nx, ok, nu: (jnp.minimum(i, nu[0] - 1), 0)),
                  pl.BlockSpec((None, 1, two_f), lambda i, be, fi, re, nx, ok, nu: (be[i], 0, 0)),
                  pl.BlockSpec(memory_space=pl.ANY)],
        out_specs=pl.BlockSpec((tm, two_f // 2), lambda i, *_: (i, 0)),
        scratch_shapes=[pltpu.VMEM((2, two_f // chunk, d, chunk), BF16),
                        pltpu.VMEM((2, d, chunk), w_gu.dtype),
                        pltpu.SMEM((3,), I32),
                        pltpu.SemaphoreType.DMA((2,))])
    return pl.pallas_call(
        _moe_up_kernel, grid_spec=grid_spec,
        out_shape=SDS((n_rows, two_f // 2), BF16),
        compiler_params=_cparams(1), name="moe_up")(*tables, n_used, xs, b_gu, w_gu)


def _moe_down_kernel(be_ref, first_ref, rem_ref, nxt_ref, nv_ref, nu_ref, a_ref, b_ref, w_hbm, o_ref,
                     wb_ref, stage_ref, st_ref, sem):
    i = pl.program_id(0)
    ts = a_ref.shape[0]
    ns = o_ref.shape[0] // ts
    chunk = wb_ref.shape[3]

    def compute(cur, rows):
        a = a_ref[:rows, :]
        for n in range(wb_ref.shape[1]):
            lo, hi = n * chunk, (n + 1) * chunk
            y = _dot(a, wb_ref[cur, n]) + b_ref[:, lo:hi]
            for j in range(chunk // (2 * LANES)):
                s = lo // (2 * LANES) + j
                w = _pack_pair(y[:, (2 * j) * LANES:(2 * j + 1) * LANES],
                               y[:, (2 * j + 1) * LANES:(2 * j + 2) * LANES])
                o_ref[pl.ds(s, rows, stride=ns), :] = w

    @pl.when(i < nu_ref[0])
    def _():
        cur = _stream_weights(i, be_ref, first_ref, rem_ref, nxt_ref, w_hbm, wb_ref, stage_ref,
                              st_ref, sem)

        @pl.when(nv_ref[i] == 2)
        def _():
            compute(cur, ts)

        @pl.when(nv_ref[i] == 1)
        def _():
            compute(cur, ts // 2)
            half = ts // 2 * ns
            o_ref[half:, :] = jnp.zeros((half, LANES), o_ref.dtype)

    @pl.when(i >= nu_ref[0])
    def _():
        o_ref[...] = jnp.zeros(o_ref.shape, o_ref.dtype)


def _moe_down(act, w_dn, b_dn, tables, n_used, ts):
    n_rows, f = act.shape
    d = w_dn.shape[2]
    ns = d // (2 * LANES)
    chunk = max(min(W_CHUNK, d // 2), 2 * LANES)
    grid_spec = pltpu.PrefetchScalarGridSpec(
        num_scalar_prefetch=6,
        grid=(n_rows // ts,),
        in_specs=[pl.BlockSpec((ts, f),
                               lambda i, be, fi, re, nx, nv, nu: (jnp.minimum(i, nu[0] - 1), 0)),
                  pl.BlockSpec((None, 1, d), lambda i, be, fi, re, nx, nv, nu: (be[i], 0, 0)),
                  pl.BlockSpec(memory_space=pl.ANY)],
        out_specs=pl.BlockSpec((ts * ns, LANES), lambda i, *_: (i, 0)),
        scratch_shapes=[pltpu.VMEM((2, d // chunk, f, chunk), BF16),
                        pltpu.VMEM((2, f, chunk), w_dn.dtype),
                        pltpu.SMEM((3,), I32),
                        pltpu.SemaphoreType.DMA((2,))])
    return pl.pallas_call(
        _moe_down_kernel, grid_spec=grid_spec,
        out_shape=SDS((n_rows * ns, LANES), U32),
        compiler_params=_cparams(1), name="moe_down")(*tables, n_used, act, b_dn, w_dn)


def _combine_kernel(dst_ref, x_ref, g_ref, ys_ref, o_ref, buf, sem, *, tc):
    i = pl.program_id(0)
    n = pl.num_programs(0) - 1
    ns = x_ref.shape[1] // (2 * LANES)

    @pl.when(i < n)
    def _():
        slot = i % 2
        for k in range(TOP_K):
            for r in range(tc):
                src = pl.ds(pl.multiple_of(dst_ref[0, k, r] * ns, ns), ns)
                pltpu.make_async_copy(ys_ref.at[src], buf.at[slot, k, pl.ds(r * ns, ns)],
                                      sem.at[slot]).start(priority=r % 2)

    @pl.when(i >= 1)
    def _():
        slot = (i - 1) % 2
        for k in range(TOP_K):
            pltpu.make_async_copy(ys_ref.at[pl.ds(0, tc * ns)], buf.at[slot, k],
                                  sem.at[slot]).wait()
        g = g_ref[...]
        g = jnp.concatenate([g, jnp.zeros((8 - TOP_K, tc), F32)], axis=0).T
        for s in range(ns):
            c0, c1, c2 = (2 * s) * LANES, (2 * s + 1) * LANES, (2 * s + 2) * LANES
            acc_lo = x_ref[:, c0:c1]
            acc_hi = x_ref[:, c1:c2]
            for k in range(TOP_K):
                lo, hi = _unpack_pair(buf[slot, k, pl.ds(s, tc, stride=ns), :])
                acc_lo = acc_lo + g[:, k:k + 1] * lo
                acc_hi = acc_hi + g[:, k:k + 1] * hi
            o_ref[:, c0:c1] = acc_lo
            o_ref[:, c1:c2] = acc_hi


def _combine(x2, gates, dest, ys):
    t, d = x2.shape
    tc = _tile(t, 128)
    n = t // tc
    ns = d // (2 * LANES)
    dest3 = dest.reshape(TOP_K, n, tc).transpose(1, 0, 2)
    kern = functools.partial(_combine_kernel, tc=tc)
    prev = lambda i: (jnp.maximum(i - 1, 0), 0)
    return pl.pallas_call(
        kern,
        grid=(n + 1,),
        in_specs=[pl.BlockSpec((1, TOP_K, tc), lambda i: (jnp.minimum(i, n - 1), 0, 0),
                               memory_space=pltpu.SMEM),
                  pl.BlockSpec((tc, d), prev),
                  pl.BlockSpec((TOP_K, tc), lambda i: (0, jnp.maximum(i - 1, 0))),
                  pl.BlockSpec(memory_space=pl.ANY)],
        out_specs=pl.BlockSpec((tc, d), prev),
        out_shape=SDS((t, d), F32),
        scratch_shapes=[pltpu.VMEM((2, TOP_K, tc * ns, LANES), U32),
                        pltpu.SemaphoreType.DMA((2,))],
        compiler_params=_cparams(1), name="moe_combine")(dest3, x2, gates, ys)


def _layer(x2d, mem2d, batch, seq, n_mem, p):
    t, d = x2d.shape
    c = p["conv_w"].shape[1]
    fw = p["g_fox_out"].shape[0]
    n_heads = fw // HEAD_DIM
    n_main = 3 * c + 3 * fw
    n_exp = p["w_router"].shape[1]

    w_in = p["w_mix_in"].astype(BF16)
    wf_t = jnp.zeros((F_ROWS, d), BF16).at[:n_heads].set(w_in[:, n_main:].T)
    b_col = jnp.zeros((F_ROWS, 1), F32).at[:n_heads, 0].set(p["b_forget"])

    proj, ft = _mix_in(x2d, p["g_mix"][None, :], wf_t, w_in, n_main)
    cdec = _decay(ft, b_col, seq)
    y_fox = _fox_attention(proj, cdec, p["g_q"][None, :], p["g_k"][None, :],
                           batch, seq, n_heads, 3 * c, fw)
    x1 = _mix_out(proj, y_fox, p["conv_w"], p["g_conv_out"][None, :], p["g_fox_out"][None, :],
                  p["w_mix_out"].astype(BF16), x2d, batch, seq)

    kn, vm = _memkv(mem2d, p["g_mem"][None, :], p["w_xkv"].astype(BF16),
                    p["g_xk"][None, :], n_mem)
    x2, hm, ti, tg, pos, cnt = _xattn(x1, p["g_xattn"][None, :], p["w_xq"].astype(BF16),
                            p["g_xq"][None, :], kn, vm, p["w_xo"].astype(BF16),
                            p["g_moe"][None, :], p["w_router"].T.astype(BF16),
                            p["b_router"][:, None], seq, n_mem)

    tm = _tile(TOP_K * t, 512) // 2
    counts = cnt[:, 0].astype(I32)
    seg_start, seg_end, seg, tables, n_used = _route_tables(counts, TOP_K * t, tm)
    dest = _dest_rows(seg_start, ti, pos)
    xs = _dispatch(hm, dest, seg_end, seg, n_used, 2 * tm, tables[0].shape[0])
    act = _moe_up(xs, p["w_gate_up"], p["b_gate_up"][:, None, :], *_block_tables(tables, n_used), tm)
    ys = _moe_down(act, p["w_down"], p["b_down"][:, None, :], tables, n_used, 2 * tm)
    return _combine(x2, tg, dest, ys)


def kernel(x, mem, g_mix, w_mix_in, b_forget, conv_w, g_q, g_k, g_conv_out, g_fox_out, w_mix_out, g_xattn, g_mem, w_xq, w_xkv, g_xq, g_xk, w_xo, g_moe, w_router, b_router, w_gate_up, b_gate_up, w_down, b_down):
    batch, seq, d = x.shape
    n_mem = mem.shape[1]
    params = dict(g_mix=g_mix, w_mix_in=w_mix_in, b_forget=b_forget, conv_w=conv_w, g_q=g_q,
                  g_k=g_k, g_conv_out=g_conv_out, g_fox_out=g_fox_out, w_mix_out=w_mix_out,
                  g_xattn=g_xattn, g_mem=g_mem, w_xq=w_xq, w_xkv=w_xkv, g_xq=g_xq, g_xk=g_xk,
                  w_xo=w_xo, g_moe=g_moe, w_router=w_router, b_router=b_router,
                  w_gate_up=w_gate_up, b_gate_up=b_gate_up, w_down=w_down, b_down=b_down)
    x2d = x.reshape(batch * seq, d)
    mem2d = mem.reshape(batch * n_mem, d)
    for l in range(g_mix.shape[0]):
        x2d = _layer(x2d, mem2d, batch, seq, n_mem, {k: v[l] for k, v in params.items()})
    return x2d.reshape(batch, seq, d)
```

```python
import functools

import jax
import jax.numpy as jnp
from jax import lax
from jax.experimental import pallas as pl
from jax.experimental.pallas import tpu as pltpu

F32 = jnp.float32
BF16 = jnp.bfloat16
I32 = jnp.int32
U32 = jnp.uint32
SDS = jax.ShapeDtypeStruct

EPS = 1e-6
HEAD_DIM = 128
TOP_K = 4
SWIGLU_LIMIT = 7.0
SWIGLU_ALPHA = 1.702
MASK_VALUE = float(jnp.finfo(jnp.float32).min)
V7X_VMEM_LIMIT_BYTES = 56 * 1024 * 1024
F_ROWS = 16
LANES = 128
SUBLANES = 8


def _cparams(n_axes):
    return pltpu.CompilerParams(
        dimension_semantics=("arbitrary",) * n_axes,
        vmem_limit_bytes=V7X_VMEM_LIMIT_BYTES)


def _rms(x, g):
    return x * lax.rsqrt(jnp.mean(x * x, axis=-1, keepdims=True) + EPS) * g


def _dot(a, b):
    return jnp.dot(a, b, preferred_element_type=F32)


def _dot_nt(a, b):
    return lax.dot_general(a, b, (((1,), (1,)), ((), ())), preferred_element_type=F32)


def _tile(n, pref):
    return pref if n % pref == 0 else n


def _pack_pair(lo, hi):
    lo = lax.bitcast_convert_type(lo.astype(BF16).astype(F32), U32) >> 16
    hi = lax.bitcast_convert_type(hi.astype(BF16).astype(F32), U32) & jnp.uint32(0xFFFF0000)
    return hi | lo


def _unpack_pair(w):
    lo = lax.bitcast_convert_type(w << 16, F32)
    hi = lax.bitcast_convert_type(w & jnp.uint32(0xFFFF0000), F32)
    return lo, hi


def _mix_in_kernel(x_ref, g_ref, wf_ref, w_ref, o_ref, ft_ref, h_ref):
    @pl.when(pl.program_id(1) == 0)
    def _():
        hb = _rms(x_ref[...], g_ref[...]).astype(BF16)
        h_ref[...] = hb
        ft_ref[...] = _dot_nt(wf_ref[...], hb)

    o_ref[...] = _dot(h_ref[...], w_ref[...]).astype(o_ref.dtype)


def _mix_in(x2d, g, wf_t, w_in, n_out):
    t, d = x2d.shape
    tm = _tile(t, 1024)
    tn = _tile(n_out, 1024)
    return pl.pallas_call(
        _mix_in_kernel,
        grid=(t // tm, n_out // tn),
        in_specs=[pl.BlockSpec((tm, d), lambda i, j: (i, 0)),
                  pl.BlockSpec((1, d), lambda i, j: (0, 0)),
                  pl.BlockSpec((F_ROWS, d), lambda i, j: (0, 0)),
                  pl.BlockSpec((d, tn), lambda i, j: (0, j))],
        out_specs=[pl.BlockSpec((tm, tn), lambda i, j: (i, j)),
                   pl.BlockSpec((F_ROWS, tm), lambda i, j: (0, i))],
        out_shape=[SDS((t, n_out), BF16), SDS((F_ROWS, t), F32)],
        scratch_shapes=[pltpu.VMEM((tm, d), BF16)],
        compiler_params=_cparams(2), name="mix_in")(x2d, g, wf_t, w_in)


def _decay_kernel(ft_ref, b_ref, c_ref):
    z = ft_ref[...] + b_ref[...]
    lf = jnp.minimum(z, 0.0) - jnp.log(1.0 + jnp.exp(-jnp.abs(z)))
    s = lf.shape[1]
    lane = lax.broadcasted_iota(I32, lf.shape, 1)
    d = 1
    while d < s:
        lf = lf + jnp.where(lane >= d, pltpu.roll(lf, d, 1), 0.0)
        d *= 2
    c_ref[...] = lf


def _decay(ft, b_col, seq):
    rows, t = ft.shape
    return pl.pallas_call(
        _decay_kernel,
        grid=(t // seq,),
        in_specs=[pl.BlockSpec((rows, seq), lambda b: (0, b)),
                  pl.BlockSpec((rows, 1), lambda b: (0, 0))],
        out_specs=pl.BlockSpec((rows, seq), lambda b: (0, b)),
        out_shape=SDS((rows, t), F32),
        compiler_params=_cparams(1), name="decay")(ft, b_col)


def _fox_kernel(q_ref, k_ref, v_ref, c_ref, gq_ref, gk_ref, o_ref, kn_ref, nc_ref, *, tq, hpb):
    g = pl.program_id(1)
    seq = q_ref.shape[0]
    hd = HEAD_DIM
    scale = HEAD_DIM ** -0.5
    for j in range(hpb):
        cs = slice(j * hd, (j + 1) * hd)
        kn_ref[:, cs] = _rms(k_ref[:, cs].astype(F32), gk_ref[...]).astype(BF16)
        nc_ref[j:j + 1, :] = -c_ref[pl.ds(g * hpb + j, 1), :]
    row = lax.broadcasted_iota(I32, (tq, tq), 0)
    col = lax.broadcasted_iota(I32, (tq, tq), 1)
    causal = col <= row
    for qi in range(seq // tq):
        lo, hi = qi * tq, (qi + 1) * tq
        for j in range(hpb):
            cs = slice(j * hd, (j + 1) * hd)
            q = (_rms(q_ref[lo:hi, cs].astype(F32), gq_ref[...]) * scale).astype(BF16)
            s_diag = jnp.where(causal, _dot_nt(q, kn_ref[lo:hi, cs]) + nc_ref[j:j + 1, lo:hi],
                               MASK_VALUE)
            m = jnp.max(s_diag, axis=-1, keepdims=True)
            if qi > 0:
                s_past = _dot_nt(q, kn_ref[:lo, cs]) + nc_ref[j:j + 1, :lo]
                m = jnp.maximum(m, jnp.max(s_past, axis=-1, keepdims=True))
            p_diag = jnp.exp(s_diag - m)
            l = jnp.sum(p_diag, axis=-1, keepdims=True)
            acc = _dot(p_diag.astype(BF16), v_ref[lo:hi, cs])
            if qi > 0:
                p_past = jnp.exp(s_past - m)
                l = l + jnp.sum(p_past, axis=-1, keepdims=True)
                acc = acc + _dot(p_past.astype(BF16), v_ref[:lo, cs])
            o_ref[lo:hi, cs] = (acc / l).astype(o_ref.dtype)


def _fox_attention(proj, c, g_q, g_k, batch, seq, n_heads, col0, width):
    t = proj.shape[0]
    tq = _tile(seq, 256)
    hpb = 2 if n_heads % 2 == 0 else 1
    bw = hpb * HEAD_DIM
    cb = col0 // bw
    wb = width // bw
    kern = functools.partial(_fox_kernel, tq=tq, hpb=hpb)
    return pl.pallas_call(
        kern,
        grid=(batch, n_heads // hpb),
        in_specs=[pl.BlockSpec((seq, bw), lambda b, h: (b, cb + h)),
                  pl.BlockSpec((seq, bw), lambda b, h: (b, cb + wb + h)),
                  pl.BlockSpec((seq, bw), lambda b, h: (b, cb + 2 * wb + h)),
                  pl.BlockSpec((F_ROWS, seq), lambda b, h: (0, b)),
                  pl.BlockSpec((1, HEAD_DIM), lambda b, h: (0, 0)),
                  pl.BlockSpec((1, HEAD_DIM), lambda b, h: (0, 0))],
        out_specs=pl.BlockSpec((seq, bw), lambda b, h: (b, h)),
        out_shape=SDS((t, width), BF16),
        scratch_shapes=[pltpu.VMEM((seq, bw), BF16), pltpu.VMEM((SUBLANES, seq), F32)],
        compiler_params=_cparams(2), name="fox_attention")(proj, proj, proj, c, g_q, g_k)


def _mix_out_kernel(ub_ref, uc_ref, ux_ref, hc_ref, hx_ref, yf_ref, cw_ref, gc_ref, gf_ref,
                    w_ref, x_ref, o_ref):
    si = pl.program_id(1)
    p = uc_ref[...].astype(F32) * ux_ref[...].astype(F32)
    halo = hc_ref[...].astype(F32) * hx_ref[...].astype(F32)
    halo = jnp.where(si > 0, halo, 0.0)
    row = lax.broadcasted_iota(I32, p.shape, 0)
    last, last2 = halo[SUBLANES - 1:SUBLANES, :], halo[SUBLANES - 2:SUBLANES - 1, :]
    p1 = jnp.where(row == 0, last, pltpu.roll(p, 1, 0))
    p2 = jnp.where(row == 0, last2, jnp.where(row == 1, last, pltpu.roll(p, 2, 0)))
    w = cw_ref[...]
    y = ub_ref[...].astype(F32) * (w[0:1, :] * p2 + w[1:2, :] * p1 + w[2:3, :] * p)
    c = y.shape[1]
    y_conv = _rms(y, gc_ref[...]).astype(BF16)
    y_fox = _rms(yf_ref[...].astype(F32), gf_ref[...]).astype(BF16)
    o_ref[...] = x_ref[...] + _dot(y_conv, w_ref[:c, :]) + _dot(y_fox, w_ref[c:, :])


def _mix_out(proj, y_fox, conv_w, g_conv, g_fox, w_out, x2d, batch, seq):
    t, d = x2d.shape
    c = conv_w.shape[1]
    w = y_fox.shape[1]
    ts = _tile(seq, 512)
    ns = seq // ts
    hb = ts // SUBLANES
    tile = lambda col: (lambda b, s: (b * ns + s, col))
    halo = lambda col: (lambda b, s: (jnp.maximum((b * ns + s) * hb - 1, 0), col))
    const = lambda b, s: (0, 0)
    return pl.pallas_call(
        _mix_out_kernel,
        grid=(batch, ns),
        in_specs=[pl.BlockSpec((ts, c), tile(0)),
                  pl.BlockSpec((ts, c), tile(1)),
                  pl.BlockSpec((ts, c), tile(2)),
                  pl.BlockSpec((SUBLANES, c), halo(1)),
                  pl.BlockSpec((SUBLANES, c), halo(2)),
                  pl.BlockSpec((ts, w), tile(0)),
                  pl.BlockSpec((3, c), const),
                  pl.BlockSpec((1, c), const),
                  pl.BlockSpec((1, w), const),
                  pl.BlockSpec((c + w, d), const, pipeline_mode=pl.Buffered(1)),
                  pl.BlockSpec((ts, d), tile(0))],
        out_specs=pl.BlockSpec((ts, d), tile(0)),
        out_shape=SDS((t, d), F32),
        compiler_params=_cparams(2), name="mix_out")(
            proj, proj, proj, proj, proj, y_fox, conv_w, g_conv, g_fox, w_out, x2d)


def _memkv_kernel(mem_ref, g_ref, w_ref, gk_ref, k_ref, v_ref):
    hm = _rms(mem_ref[...], g_ref[...]).astype(BF16)
    kv = _dot(hm, w_ref[...])
    xw = k_ref.shape[1]
    for h in range(xw // HEAD_DIM):
        sl = slice(h * HEAD_DIM, (h + 1) * HEAD_DIM)
        k_ref[:, sl] = _rms(kv[:, sl], gk_ref[...]).astype(BF16)
    v_ref[...] = kv[:, xw:].astype(BF16)


def _memkv(mem2d, g_mem, w_xkv, g_xk, n_mem):
    tm_, d = mem2d.shape
    xw = w_xkv.shape[1] // 2
    return pl.pallas_call(
        _memkv_kernel,
        grid=(tm_ // n_mem,),
        in_specs=[pl.BlockSpec((n_mem, d), lambda b: (b, 0)),
                  pl.BlockSpec((1, d), lambda b: (0, 0)),
                  pl.BlockSpec((d, 2 * xw), lambda b: (0, 0)),
                  pl.BlockSpec((1, HEAD_DIM), lambda b: (0, 0))],
        out_specs=[pl.BlockSpec((n_mem, xw), lambda b: (b, 0)),
                   pl.BlockSpec((n_mem, xw), lambda b: (b, 0))],
        out_shape=[SDS((tm_, xw), BF16), SDS((tm_, xw), BF16)],
        compiler_params=_cparams(1), name="memkv")(mem2d, g_mem, w_xkv, g_xk)


def _xattn_kernel(x_ref, gx_ref, wq_ref, gq_ref, kn_ref, v_ref, wo_ref, gm_ref, wr_ref, br_ref,
                  x2_ref, hm_ref, ti_ref, tg_ref, pos_ref, cnt_ref, run_ref, tri_ref):
    i = pl.program_id(0)
    tm = x_ref.shape[0]

    @pl.when(i == 0)
    def _():
        run_ref[...] = jnp.zeros(run_ref.shape, F32)
        r = lax.broadcasted_iota(I32, (tm, tm), 0)
        c = lax.broadcasted_iota(I32, (tm, tm), 1)
        tri_ref[...] = jnp.where(r <= c, 1.0, 0.0).astype(BF16)

    x = x_ref[...]
    hb = _rms(x, gx_ref[...]).astype(BF16)
    q = _dot(hb, wq_ref[...])
    scale = HEAD_DIM ** -0.5
    outs = []
    for h in range(q.shape[1] // HEAD_DIM):
        sl = slice(h * HEAD_DIM, (h + 1) * HEAD_DIM)
        qh = (_rms(q[:, sl], gq_ref[...]) * scale).astype(BF16)
        s = _dot_nt(qh, kn_ref[:, sl])
        p = jnp.exp(s - jnp.max(s, axis=-1, keepdims=True))
        oh = _dot(p.astype(BF16), v_ref[:, sl]) / jnp.sum(p, axis=-1, keepdims=True)
        outs.append(oh.astype(BF16))
    x2 = x + _dot(jnp.concatenate(outs, axis=-1), wo_ref[...])
    x2_ref[...] = x2
    hm = _rms(x2, gm_ref[...]).astype(BF16)
    hm_ref[...] = hm
    lg = _dot_nt(wr_ref[...], hm) + br_ref[...]
    n_exp = lg.shape[0]
    eidx = lax.broadcasted_iota(I32, lg.shape, 0).astype(F32)
    vals, sel = [], []
    for r in range(TOP_K):
        mx = jnp.max(lg, axis=0, keepdims=True)
        am = jnp.min(jnp.where(lg == mx, eidx, float(n_exp)), axis=0, keepdims=True)
        ti_ref[r:r + 1, :] = am.astype(I32)
        vals.append(mx)
        sel.append(eidx == am)
        lg = jnp.where(sel[r], -jnp.inf, lg)
    ex = [jnp.exp(v - vals[0]) for v in vals]
    den = ex[0]
    for e in ex[1:]:
        den = den + e
    for r in range(TOP_K):
        tg_ref[r:r + 1, :] = ex[r] / den
    onehot = jnp.zeros(lg.shape, F32)
    for r in range(TOP_K):
        onehot = onehot + jnp.where(sel[r], 1.0, 0.0)
    incl = _dot(onehot.astype(BF16), tri_ref[...])
    excl = incl - onehot + run_ref[...]
    for r in range(TOP_K):
        pos_ref[r:r + 1, :] = jnp.sum(jnp.where(sel[r], excl, 0.0), axis=0,
                                      keepdims=True).astype(I32)
    run = run_ref[...] + incl[:, tm - 1:tm]
    run_ref[...] = run
    cnt_ref[...] = jnp.broadcast_to(run, cnt_ref.shape)


def _xattn(x1, g_xattn, w_xq, g_xq, kn, vm, w_xo, g_moe, w_rt, b_r, seq, n_mem):
    t, d = x1.shape
    xw = w_xq.shape[1]
    n_exp = w_rt.shape[0]
    tm = _tile(seq, 512)
    per = seq // tm
    const = lambda i: (0, 0)
    return pl.pallas_call(
        _xattn_kernel,
        grid=(t // tm,),
        in_specs=[pl.BlockSpec((tm, d), lambda i: (i, 0)),
                  pl.BlockSpec((1, d), const),
                  pl.BlockSpec((d, xw), const),
                  pl.BlockSpec((1, HEAD_DIM), const),
                  pl.BlockSpec((n_mem, xw), lambda i: (i // per, 0)),
                  pl.BlockSpec((n_mem, xw), lambda i: (i // per, 0)),
                  pl.BlockSpec((xw, d), const),
                  pl.BlockSpec((1, d), const),
                  pl.BlockSpec((n_exp, d), const),
                  pl.BlockSpec((n_exp, 1), const)],
        out_specs=[pl.BlockSpec((tm, d), lambda i: (i, 0)),
                   pl.BlockSpec((tm, d), lambda i: (i, 0)),
                   pl.BlockSpec((TOP_K, tm), lambda i: (0, i)),
                   pl.BlockSpec((TOP_K, tm), lambda i: (0, i)),
                   pl.BlockSpec((TOP_K, tm), lambda i: (0, i)),
                   pl.BlockSpec((n_exp, 128), const)],
        out_shape=[SDS((t, d), F32), SDS((t, d), BF16),
                   SDS((TOP_K, t), I32), SDS((TOP_K, t), F32), SDS((TOP_K, t), I32),
                   SDS((n_exp, 128), F32)],
        scratch_shapes=[pltpu.VMEM((n_exp, 1), F32), pltpu.VMEM((tm, tm), BF16)],
        compiler_params=_cparams(1), name="xattn_router")(
            x1, g_xattn, w_xq, g_xq, kn, vm, w_xo, g_moe, w_rt, b_r)


def _take(values, idx):
    pick = idx[:, None] == jnp.arange(values.shape[0], dtype=I32)[None, :]
    return jnp.sum(jnp.where(pick, values[None, :], 0), axis=1).astype(I32)


def _route_tables(counts, n_assign, tm):
    n_exp = counts.shape[0]
    ts = 2 * tm
    valid = (counts + tm - 1) // tm * tm
    seg = ((counts + ts - 1) // ts * ts).astype(I32)
    seg_end = jnp.cumsum(seg).astype(I32)
    seg_start = seg_end - seg
    n_steps = n_assign // ts + n_exp
    n_used = seg_end[-1] // ts
    idx = jnp.arange(n_steps, dtype=I32)
    row0 = jnp.minimum(idx, n_used - 1) * ts
    exp = jnp.sum((row0[:, None] >= seg_end[None, :]).astype(I32), axis=1)
    exp = jnp.minimum(exp, n_exp - 1)
    n_valid = jnp.clip((_take(seg_start + valid, exp) - row0) // tm, 1, 2).astype(I32)
    prev = jnp.concatenate([jnp.full((1,), -1, I32), exp[:-1]])
    first = (exp != prev).astype(I32)
    rem = _take(seg_end, exp) // ts - idx
    cand = jnp.where(counts > 0, jnp.arange(n_exp, dtype=I32), n_exp)
    after = jnp.concatenate([lax.cummin(cand, reverse=True)[1:], jnp.full((1,), n_exp, I32)])
    nxt = _take(jnp.where(after >= n_exp, -1, after), exp)
    tables = (exp, first, rem.astype(I32), nxt, n_valid)
    return seg_start, seg_end, seg, tables, n_used.reshape(1).astype(I32)


def _block_tables(tables, n_used):
    exp, first, rem, nxt, n_valid = tables
    half = jnp.tile(jnp.arange(2, dtype=I32), exp.shape[0])
    rep = lambda v: jnp.repeat(v, 2)
    return (rep(exp), rep(first) * (1 - half), 2 * rep(rem) - half, rep(nxt),
            (half < rep(n_valid)).astype(I32)), 2 * n_used


def _dest_kernel(ps_ref, ti_ref, pos_ref, o_ref, *, n_exp):
    ti = ti_ref[...]
    acc = pos_ref[...]
    for e in range(n_exp):
        acc = acc + jnp.where(ti == e, ps_ref[e], 0)
    o_ref[...] = acc


def _dest_rows(pad_start, ti, pos):
    k, t = ti.shape
    kern = functools.partial(_dest_kernel, n_exp=pad_start.shape[0])
    grid_spec = pltpu.PrefetchScalarGridSpec(
        num_scalar_prefetch=1, grid=(1,),
        in_specs=[pl.BlockSpec((k, t), lambda i, ps: (0, 0)),
                  pl.BlockSpec((k, t), lambda i, ps: (0, 0))],
        out_specs=pl.BlockSpec((k, t), lambda i, ps: (0, 0)))
    return pl.pallas_call(kern, grid_spec=grid_spec, out_shape=SDS((k, t), I32),
                          compiler_params=_cparams(1), name="moe_dest")(pad_start, ti, pos)


def _dispatch_kernel(pe_ref, pc_ref, nu_ref, dst_ref, x_ref, xs_ref, buf, zbuf, sem, zsem,
                     *, tc, ts, n_exp, n_steps):
    i = pl.program_id(0)
    n = pl.num_programs(0)
    ns = x_ref.shape[1] // (2 * LANES)

    def zero_copy(row0):
        rows = pl.ds(pl.multiple_of(row0 * ns, ts * ns), ts * ns)
        return pltpu.make_async_copy(zbuf, xs_ref.at[rows], zsem)

    @pl.when(i == 0)
    def _():
        zbuf[...] = jnp.zeros(zbuf.shape, zbuf.dtype)
        for start in (True, False):
            for e in range(n_exp):
                @pl.when(pc_ref[e] > 0)
                def _(e=e):
                    cp = zero_copy(pe_ref[e] - ts)
                    cp.start() if start else cp.wait()

            def tail(b, c):
                cp = zero_copy(b * ts)
                cp.start() if start else cp.wait()
                return c
            lax.fori_loop(nu_ref[0], n_steps, tail, 0)

    slot = i % 2
    x = x_ref[...]
    for s in range(ns):
        lo = x[:, (2 * s) * LANES:(2 * s + 1) * LANES]
        hi = x[:, (2 * s + 1) * LANES:(2 * s + 2) * LANES]
        buf[slot, pl.ds(s, tc, stride=ns), :] = _pack_pair(lo, hi)
    for k in range(TOP_K):
        for r in range(tc):
            dst = pl.ds(pl.multiple_of(dst_ref[0, k, r] * ns, ns), ns)
            pltpu.make_async_copy(buf.at[slot, pl.ds(r * ns, ns)], xs_ref.at[dst],
                                  sem.at[slot]).start(priority=r % 2)

    def wait_slot(s):
        for k in range(TOP_K):
            pltpu.make_async_copy(buf.at[s], xs_ref.at[pl.ds(0, tc * ns)], sem.at[s]).wait()

    @pl.when(i >= 1)
    def _():
        wait_slot(1 - slot)

    @pl.when(i == n - 1)
    def _():
        wait_slot(slot)


def _dispatch(hm, dest, seg_end, seg, n_used, ts, n_steps):
    t, d = hm.shape
    tc = _tile(t, 128)
    n = t // tc
    ns = d // (2 * LANES)
    n_exp = seg_end.shape[0]
    dest3 = dest.reshape(TOP_K, n, tc).transpose(1, 0, 2)
    kern = functools.partial(_dispatch_kernel, tc=tc, ts=ts, n_exp=n_exp, n_steps=n_steps)
    grid_spec = pltpu.PrefetchScalarGridSpec(
        num_scalar_prefetch=3, grid=(n,),
        in_specs=[pl.BlockSpec((1, TOP_K, tc), lambda i, *_: (i, 0, 0), memory_space=pltpu.SMEM),
                  pl.BlockSpec((tc, d), lambda i, *_: (i, 0))],
        out_specs=pl.BlockSpec(memory_space=pl.ANY),
        scratch_shapes=[pltpu.VMEM((2, tc * ns, LANES), U32),
                        pltpu.VMEM((ts * ns, LANES), U32),
                        pltpu.SemaphoreType.DMA((2,)),
                        pltpu.SemaphoreType.DMA(())])
    return pl.pallas_call(
        kern, grid_spec=grid_spec,
        out_shape=SDS((n_steps * ts * ns, LANES), U32),
        compiler_params=_cparams(1), name="moe_dispatch")(seg_end, seg, n_used, dest3, hm)


W_CHUNK = 512


def _stream_weights(i, be_ref, first_ref, rem_ref, nxt_ref, w_hbm, wb_ref, stage_ref, st_ref, sem):
    n_chunks, _, chunk = wb_ref.shape[1:]

    def chunk_copy(e, c, slot):
        cols = pl.ds(pl.multiple_of(c * chunk, chunk), chunk)
        return pltpu.make_async_copy(w_hbm.at[e, :, cols], stage_ref.at[slot], sem.at[slot])

    def start_first_two(e):
        chunk_copy(e, 0, 0).start()
        chunk_copy(e, 1, 1).start()

    def cast_chunks(e, dst, done, issued, k):
        def body(_, carry):
            done, issued = carry
            slot = done % 2
            chunk_copy(e, done, slot).wait()
            wb_ref[dst, done] = stage_ref[slot].astype(BF16)

            @pl.when(issued < n_chunks)
            def _():
                chunk_copy(e, issued, slot).start()
            return done + 1, jnp.minimum(issued + 1, n_chunks)
        return lax.fori_loop(0, k, body, (done, issued))

    @pl.when(i == 0)
    def _():
        start_first_two(be_ref[0])
        cast_chunks(be_ref[0], 0, 0, 2, n_chunks)
        st_ref[0] = 1

    @pl.when(first_ref[i] == 1)
    def _():
        st_ref[0] = 1 - st_ref[0]
        st_ref[1] = 0
        st_ref[2] = 0

        @pl.when(nxt_ref[i] >= 0)
        def _():
            start_first_two(nxt_ref[i])
            st_ref[2] = 2

    cur = st_ref[0]

    @pl.when(nxt_ref[i] >= 0)
    def _():
        done = st_ref[1]
        rem = rem_ref[i]
        lead = (first_ref[i] == 1) & (rem > 1)
        left = jnp.where(first_ref[i] == 1, jnp.maximum(rem - 1, 1), rem)
        k = jnp.where(lead, 0, lax.div(n_chunks - done + left - 1, left))
        done, issued = cast_chunks(nxt_ref[i], 1 - cur, done, st_ref[2], k)
        st_ref[1] = done
        st_ref[2] = issued

    return cur


def _moe_up_kernel(be_ref, first_ref, rem_ref, nxt_ref, ok_ref, nu_ref, x_ref, b_ref, w_hbm, o_ref,
                   wb_ref, stage_ref, st_ref, sem):
    i = pl.program_id(0)
    tm = o_ref.shape[0]
    ns = x_ref.shape[0] // tm
    n_half = wb_ref.shape[1] // 2
    chunk = wb_ref.shape[3]
    f = n_half * chunk

    @pl.when(i < nu_ref[0])
    def _():
        cur = _stream_weights(i, be_ref, first_ref, rem_ref, nxt_ref, w_hbm, wb_ref, stage_ref,
                              st_ref, sem)

        @pl.when(ok_ref[i] == 1)
        def _():
            cols = []
            for s in range(ns):
                lo, hi = _unpack_pair(x_ref[pl.ds(s, tm, stride=ns), :])
                cols += [lo.astype(BF16), hi.astype(BF16)]
            x = jnp.concatenate(cols, axis=1)
            for n in range(n_half):
                lo, hi = n * chunk, (n + 1) * chunk
                gate = jnp.minimum(_dot(x, wb_ref[cur, n]) + b_ref[:, lo:hi], SWIGLU_LIMIT)
                up = jnp.clip(_dot(x, wb_ref[cur, n_half + n]) + b_ref[:, f + lo:f + hi],
                              -SWIGLU_LIMIT, SWIGLU_LIMIT)
                glu = gate * jax.nn.sigmoid(SWIGLU_ALPHA * gate)
                o_ref[:, lo:hi] = ((up + 1.0) * glu).astype(o_ref.dtype)

    @pl.when((i >= nu_ref[0]) | (ok_ref[i] == 0))
    def _():
        o_ref[...] = jnp.zeros(o_ref.shape, o_ref.dtype)


def _moe_up(xs, w_gu, b_gu, tables, n_used, tm):
    d = w_gu.shape[1]
    ns = d // (2 * LANES)
    n_rows = xs.shape[0] // ns
    two_f = w_gu.shape[2]
    chunk = min(W_CHUNK, two_f // 2)
    grid_spec = pltpu.PrefetchScalarGridSpec(
        num_scalar_prefetch=6,
        grid=(n_rows // tm,),
        in_specs=[pl.BlockSpec((tm * ns, LANES),
                               lambda i, be, fi, re, nx, ok, nu: (jnp.minimum(i, nu[0] - 1), 0)),
                  pl.BlockSpec((None, 1, two_f), lambda i, be, fi, re, nx, ok, nu: (be[i], 0, 0)),
                  pl.BlockSpec(memory_space=pl.ANY)],
        out_specs=pl.BlockSpec((tm, two_f // 2), lambda i, *_: (i, 0)),
        scratch_shapes=[pltpu.VMEM((2, two_f // chunk, d, chunk), BF16),
                        pltpu.VMEM((2, d, chunk), w_gu.dtype),
                        pltpu.SMEM((3,), I32),
                        pltpu.SemaphoreType.DMA((2,))])
    return pl.pallas_call(
        _moe_up_kernel, grid_spec=grid_spec,
        out_shape=SDS((n_rows, two_f // 2), BF16),
        compiler_params=_cparams(1), name="moe_up")(*tables, n_used, xs, b_gu, w_gu)


def _moe_down_kernel(be_ref, first_ref, rem_ref, nxt_ref, nv_ref, nu_ref, a_ref, b_ref, w_hbm, o_ref,
                     wb_ref, stage_ref, st_ref, sem):
    i = pl.program_id(0)
    ts = a_ref.shape[0]
    ns = o_ref.shape[0] // ts
    chunk = wb_ref.shape[3]

    def compute(cur, rows):
        a = a_ref[:rows, :]
        for n in range(wb_ref.shape[1]):
            lo, hi = n * chunk, (n + 1) * chunk
            y = _dot(a, wb_ref[cur, n]) + b_ref[:, lo:hi]
            for j in range(chunk // (2 * LANES)):
                s = lo // (2 * LANES) + j
                w = _pack_pair(y[:, (2 * j) * LANES:(2 * j + 1) * LANES],
                               y[:, (2 * j + 1) * LANES:(2 * j + 2) * LANES])
                o_ref[pl.ds(s, rows, stride=ns), :] = w

    @pl.when(i < nu_ref[0])
    def _():
        cur = _stream_weights(i, be_ref, first_ref, rem_ref, nxt_ref, w_hbm, wb_ref, stage_ref,
                              st_ref, sem)

        @pl.when(nv_ref[i] == 2)
        def _():
            compute(cur, ts)

        @pl.when(nv_ref[i] == 1)
        def _():
            compute(cur, ts // 2)
            half = ts // 2 * ns
            o_ref[half:, :] = jnp.zeros((half, LANES), o_ref.dtype)

    @pl.when(i >= nu_ref[0])
    def _():
        o_ref[...] = jnp.zeros(o_ref.shape, o_ref.dtype)


def _moe_down(act, w_dn, b_dn, tables, n_used, ts):
    n_rows, f = act.shape
    d = w_dn.shape[2]
    ns = d // (2 * LANES)
    chunk = max(min(W_CHUNK, d // 2), 2 * LANES)
    grid_spec = pltpu.PrefetchScalarGridSpec(
        num_scalar_prefetch=6,
        grid=(n_rows // ts,),
        in_specs=[pl.BlockSpec((ts, f),
                               lambda i, be, fi, re, nx, nv, nu: (jnp.minimum(i, nu[0] - 1), 0)),
                  pl.BlockSpec((None, 1, d), lambda i, be, fi, re, nx, nv, nu: (be[i], 0, 0)),
                  pl.BlockSpec(memory_space=pl.ANY)],
        out_specs=pl.BlockSpec((ts * ns, LANES), lambda i, *_: (i, 0)),
        scratch_shapes=[pltpu.VMEM((2, d // chunk, f, chunk), BF16),
                        pltpu.VMEM((2, f, chunk), w_dn.dtype),
                        pltpu.SMEM((3,), I32),
                        pltpu.SemaphoreType.DMA((2,))])
    return pl.pallas_call(
        _moe_down_kernel, grid_spec=grid_spec,
        out_shape=SDS((n_rows * ns, LANES), U32),
        compiler_params=_cparams(1), name="moe_down")(*tables, n_used, act, b_dn, w_dn)


def _combine_kernel(dst_ref, x_ref, g_ref, ys_ref, o_ref, buf, sem, *, tc):
    i = pl.program_id(0)
    n = pl.num_programs(0) - 1
    ns = x_ref.shape[1] // (2 * LANES)

    @pl.when(i < n)
    def _():
        slot = i % 2
        for k in range(TOP_K):
            for r in range(tc):
                src = pl.ds(pl.multiple_of(dst_ref[0, k, r] * ns, ns), ns)
                pltpu.make_async_copy(ys_ref.at[src], buf.at[slot, k, pl.ds(r * ns, ns)],
                                      sem.at[slot]).start(priority=r % 2)

    @pl.when(i >= 1)
    def _():
        slot = (i - 1) % 2
        for k in range(TOP_K):
            pltpu.make_async_copy(ys_ref.at[pl.ds(0, tc * ns)], buf.at[slot, k],
                                  sem.at[slot]).wait()
        g = g_ref[...]
        g = jnp.concatenate([g, jnp.zeros((SUBLANES - TOP_K, tc), F32)], axis=0).T
        for s in range(ns):
            c0, c1, c2 = (2 * s) * LANES, (2 * s + 1) * LANES, (2 * s + 2) * LANES
            acc_lo = x_ref[:, c0:c1]
            acc_hi = x_ref[:, c1:c2]
            for k in range(TOP_K):
                lo, hi = _unpack_pair(buf[slot, k, pl.ds(s, tc, stride=ns), :])
                acc_lo = acc_lo + g[:, k:k + 1] * lo
                acc_hi = acc_hi + g[:, k:k + 1] * hi
            o_ref[:, c0:c1] = acc_lo
            o_ref[:, c1:c2] = acc_hi


def _combine(x2, gates, dest, ys):
    t, d = x2.shape
    tc = _tile(t, 128)
    n = t // tc
    ns = d // (2 * LANES)
    dest3 = dest.reshape(TOP_K, n, tc).transpose(1, 0, 2)
    kern = functools.partial(_combine_kernel, tc=tc)
    prev = lambda i: (jnp.maximum(i - 1, 0), 0)
    return pl.pallas_call(
        kern,
        grid=(n + 1,),
        in_specs=[pl.BlockSpec((1, TOP_K, tc), lambda i: (jnp.minimum(i, n - 1), 0, 0),
                               memory_space=pltpu.SMEM),
                  pl.BlockSpec((tc, d), prev),
                  pl.BlockSpec((TOP_K, tc), lambda i: (0, jnp.maximum(i - 1, 0))),
                  pl.BlockSpec(memory_space=pl.ANY)],
        out_specs=pl.BlockSpec((tc, d), prev),
        out_shape=SDS((t, d), F32),
        scratch_shapes=[pltpu.VMEM((2, TOP_K, tc * ns, LANES), U32),
                        pltpu.SemaphoreType.DMA((2,))],
        compiler_params=_cparams(1), name="moe_combine")(dest3, x2, gates, ys)


def _layer(x2d, mem2d, batch, seq, n_mem, p):
    t, d = x2d.shape
    c = p["conv_w"].shape[1]
    fw = p["g_fox_out"].shape[0]
    n_heads = fw // HEAD_DIM
    n_main = 3 * c + 3 * fw
    n_exp = p["w_router"].shape[1]
    assert n_heads <= F_ROWS and TOP_K <= SUBLANES and d % (2 * LANES) == 0

    w_in = p["w_mix_in"].astype(BF16)
    wf_t = jnp.zeros((F_ROWS, d), BF16).at[:n_heads].set(w_in[:, n_main:].T)
    b_col = jnp.zeros((F_ROWS, 1), F32).at[:n_heads, 0].set(p["b_forget"])

    proj, ft = _mix_in(x2d, p["g_mix"][None, :], wf_t, w_in, n_main)
    cdec = _decay(ft, b_col, seq)
    y_fox = _fox_attention(proj, cdec, p["g_q"][None, :], p["g_k"][None, :],
                           batch, seq, n_heads, 3 * c, fw)
    x1 = _mix_out(proj, y_fox, p["conv_w"], p["g_conv_out"][None, :], p["g_fox_out"][None, :],
                  p["w_mix_out"].astype(BF16), x2d, batch, seq)

    kn, vm = _memkv(mem2d, p["g_mem"][None, :], p["w_xkv"].astype(BF16),
                    p["g_xk"][None, :], n_mem)
    x2, hm, ti, tg, pos, cnt = _xattn(x1, p["g_xattn"][None, :], p["w_xq"].astype(BF16),
                            p["g_xq"][None, :], kn, vm, p["w_xo"].astype(BF16),
                            p["g_moe"][None, :], p["w_router"].T.astype(BF16),
                            p["b_router"][:, None], seq, n_mem)

    tm = _tile(TOP_K * t, 512) // 2
    counts = cnt[:, 0].astype(I32)
    seg_start, seg_end, seg, tables, n_used = _route_tables(counts, TOP_K * t, tm)
    dest = _dest_rows(seg_start, ti, pos)
    xs = _dispatch(hm, dest, seg_end, seg, n_used, 2 * tm, tables[0].shape[0])
    act = _moe_up(xs, p["w_gate_up"], p["b_gate_up"][:, None, :], *_block_tables(tables, n_used), tm)
    ys = _moe_down(act, p["w_down"], p["b_down"][:, None, :], tables, n_used, 2 * tm)
    return _combine(x2, tg, dest, ys)


def kernel(x, mem, g_mix, w_mix_in, b_forget, conv_w, g_q, g_k, g_conv_out, g_fox_out, w_mix_out, g_xattn, g_mem, w_xq, w_xkv, g_xq, g_xk, w_xo, g_moe, w_router, b_router, w_gate_up, b_gate_up, w_down, b_down):
    batch, seq, d = x.shape
    n_mem = mem.shape[1]
    params = dict(g_mix=g_mix, w_mix_in=w_mix_in, b_forget=b_forget, conv_w=conv_w, g_q=g_q,
                  g_k=g_k, g_conv_out=g_conv_out, g_fox_out=g_fox_out, w_mix_out=w_mix_out,
                  g_xattn=g_xattn, g_mem=g_mem, w_xq=w_xq, w_xkv=w_xkv, g_xq=g_xq, g_xk=g_xk,
                  w_xo=w_xo, g_moe=g_moe, w_router=w_router, b_router=b_router,
                  w_gate_up=w_gate_up, b_gate_up=b_gate_up, w_down=w_down, b_down=b_down)
    x2d = x.reshape(batch * seq, d)
    mem2d = mem.reshape(batch * n_mem, d)
    for l in range(g_mix.shape[0]):
        x2d = _layer(x2d, mem2d, batch, seq, n_mem, {k: v[l] for k, v in params.items()})
    return x2d.reshape(batch, seq, d)
```

```python
import functools

import jax
import jax.numpy as jnp
from jax import lax
from jax.experimental import pallas as pl
from jax.experimental.pallas import tpu as pltpu

F32 = jnp.float32
BF16 = jnp.bfloat16
I32 = jnp.int32
U32 = jnp.uint32
SDS = jax.ShapeDtypeStruct

EPS = 1e-6
HEAD_DIM = 128
TOP_K = 4
SWIGLU_LIMIT = 7.0
SWIGLU_ALPHA = 1.702
MASK_VALUE = float(jnp.finfo(jnp.float32).min)
V7X_VMEM_LIMIT_BYTES = 56 * 1024 * 1024
F_ROWS = 16
LANES = 128
SUBLANES = 8


def _cparams(n_axes):
    return pltpu.CompilerParams(
        dimension_semantics=("arbitrary",) * n_axes,
        vmem_limit_bytes=V7X_VMEM_LIMIT_BYTES)


def _rms(x, g):
    return x * lax.rsqrt(jnp.mean(x * x, axis=-1, keepdims=True) + EPS) * g


def _dot(a, b):
    return jnp.dot(a, b, preferred_element_type=F32)


def _dot_nt(a, b):
    return lax.dot_general(a, b, (((1,), (1,)), ((), ())), preferred_element_type=F32)


def _tile(n, pref):
    return pref if n % pref == 0 else n


def _pack_pair(lo, hi):
    lo = lax.bitcast_convert_type(lo.astype(BF16).astype(F32), U32) >> 16
    hi = lax.bitcast_convert_type(hi.astype(BF16).astype(F32), U32) & jnp.uint32(0xFFFF0000)
    return hi | lo


def _unpack_pair(w):
    lo = lax.bitcast_convert_type(w << 16, F32)
    hi = lax.bitcast_convert_type(w & jnp.uint32(0xFFFF0000), F32)
    return lo, hi


def _mix_in_kernel(x_ref, g_ref, wf_ref, w_ref, o_ref, ft_ref, h_ref):
    @pl.when(pl.program_id(1) == 0)
    def _():
        hb = _rms(x_ref[...], g_ref[...]).astype(BF16)
        h_ref[...] = hb
        ft_ref[...] = _dot_nt(wf_ref[...], hb)

    o_ref[...] = _dot(h_ref[...], w_ref[...]).astype(o_ref.dtype)


def _mix_in(x2d, g, wf_t, w_in, n_out):
    t, d = x2d.shape
    tm = _tile(t, 1024)
    tn = _tile(n_out, 1024)
    return pl.pallas_call(
        _mix_in_kernel,
        grid=(t // tm, n_out // tn),
        in_specs=[pl.BlockSpec((tm, d), lambda i, j: (i, 0)),
                  pl.BlockSpec((1, d), lambda i, j: (0, 0)),
                  pl.BlockSpec((F_ROWS, d), lambda i, j: (0, 0)),
                  pl.BlockSpec((d, tn), lambda i, j: (0, j))],
        out_specs=[pl.BlockSpec((tm, tn), lambda i, j: (i, j)),
                   pl.BlockSpec((F_ROWS, tm), lambda i, j: (0, i))],
        out_shape=[SDS((t, n_out), BF16), SDS((F_ROWS, t), F32)],
        scratch_shapes=[pltpu.VMEM((tm, d), BF16)],
        compiler_params=_cparams(2), name="mix_in")(x2d, g, wf_t, w_in)


def _decay_kernel(ft_ref, b_ref, c_ref):
    z = ft_ref[...] + b_ref[...]
    lf = jnp.minimum(z, 0.0) - jnp.log(1.0 + jnp.exp(-jnp.abs(z)))
    s = lf.shape[1]
    lane = lax.broadcasted_iota(I32, lf.shape, 1)
    d = 1
    while d < s:
        lf = lf + jnp.where(lane >= d, pltpu.roll(lf, d, 1), 0.0)
        d *= 2
    c_ref[...] = lf


def _decay(ft, b_col, seq):
    rows, t = ft.shape
    return pl.pallas_call(
        _decay_kernel,
        grid=(t // seq,),
        in_specs=[pl.BlockSpec((rows, seq), lambda b: (0, b)),
                  pl.BlockSpec((rows, 1), lambda b: (0, 0))],
        out_specs=pl.BlockSpec((rows, seq), lambda b: (0, b)),
        out_shape=SDS((rows, t), F32),
        compiler_params=_cparams(1), name="decay")(ft, b_col)


def _fox_kernel(q_ref, k_ref, v_ref, c_ref, gq_ref, gk_ref, o_ref, kn_ref, nc_ref, *, tq, hpb):
    g = pl.program_id(1)
    seq = q_ref.shape[0]
    hd = HEAD_DIM
    scale = HEAD_DIM ** -0.5
    for j in range(hpb):
        cs = slice(j * hd, (j + 1) * hd)
        kn_ref[:, cs] = _rms(k_ref[:, cs].astype(F32), gk_ref[...]).astype(BF16)
        nc_ref[j:j + 1, :] = -c_ref[pl.ds(g * hpb + j, 1), :]
    row = lax.broadcasted_iota(I32, (tq, tq), 0)
    col = lax.broadcasted_iota(I32, (tq, tq), 1)
    causal = col <= row
    for qi in range(seq // tq):
        lo, hi = qi * tq, (qi + 1) * tq
        for j in range(hpb):
            cs = slice(j * hd, (j + 1) * hd)
            q = (_rms(q_ref[lo:hi, cs].astype(F32), gq_ref[...]) * scale).astype(BF16)
            s_diag = jnp.where(causal, _dot_nt(q, kn_ref[lo:hi, cs]) + nc_ref[j:j + 1, lo:hi],
                               MASK_VALUE)
            m = jnp.max(s_diag, axis=-1, keepdims=True)
            if qi > 0:
                s_past = _dot_nt(q, kn_ref[:lo, cs]) + nc_ref[j:j + 1, :lo]
                m = jnp.maximum(m, jnp.max(s_past, axis=-1, keepdims=True))
            p_diag = jnp.exp(s_diag - m)
            l = jnp.sum(p_diag, axis=-1, keepdims=True)
            acc = _dot(p_diag.astype(BF16), v_ref[lo:hi, cs])
            if qi > 0:
                p_past = jnp.exp(s_past - m)
                l = l + jnp.sum(p_past, axis=-1, keepdims=True)
                acc = acc + _dot(p_past.astype(BF16), v_ref[:lo, cs])
            o_ref[lo:hi, cs] = (acc / l).astype(o_ref.dtype)


def _fox_attention(proj, c, g_q, g_k, batch, seq, n_heads, col0, width):
    t = proj.shape[0]
    tq = _tile(seq, 256)
    hpb = 2 if n_heads % 2 == 0 else 1
    bw = hpb * HEAD_DIM
    cb = col0 // bw
    wb = width // bw
    kern = functools.partial(_fox_kernel, tq=tq, hpb=hpb)
    return pl.pallas_call(
        kern,
        grid=(batch, n_heads // hpb),
        in_specs=[pl.BlockSpec((seq, bw), lambda b, h: (b, cb + h)),
                  pl.BlockSpec((seq, bw), lambda b, h: (b, cb + wb + h)),
                  pl.BlockSpec((seq, bw), lambda b, h: (b, cb + 2 * wb + h)),
                  pl.BlockSpec((F_ROWS, seq), lambda b, h: (0, b)),
                  pl.BlockSpec((1, HEAD_DIM), lambda b, h: (0, 0)),
                  pl.BlockSpec((1, HEAD_DIM), lambda b, h: (0, 0))],
        out_specs=pl.BlockSpec((seq, bw), lambda b, h: (b, h)),
        out_shape=SDS((t, width), BF16),
        scratch_shapes=[pltpu.VMEM((seq, bw), BF16), pltpu.VMEM((SUBLANES, seq), F32)],
        compiler_params=_cparams(2), name="fox_attention")(proj, proj, proj, c, g_q, g_k)


def _mix_out_kernel(ub_ref, uc_ref, ux_ref, hc_ref, hx_ref, yf_ref, cw_ref, gc_ref, gf_ref,
                    w_ref, x_ref, o_ref):
    si = pl.program_id(1)
    p = uc_ref[...].astype(F32) * ux_ref[...].astype(F32)
    halo = hc_ref[...].astype(F32) * hx_ref[...].astype(F32)
    halo = jnp.where(si > 0, halo, 0.0)
    row = lax.broadcasted_iota(I32, p.shape, 0)
    last, last2 = halo[SUBLANES - 1:SUBLANES, :], halo[SUBLANES - 2:SUBLANES - 1, :]
    p1 = jnp.where(row == 0, last, pltpu.roll(p, 1, 0))
    p2 = jnp.where(row == 0, last2, jnp.where(row == 1, last, pltpu.roll(p, 2, 0)))
    w = cw_ref[...]
    y = ub_ref[...].astype(F32) * (w[0:1, :] * p2 + w[1:2, :] * p1 + w[2:3, :] * p)
    c = y.shape[1]
    y_conv = _rms(y, gc_ref[...]).astype(BF16)
    y_fox = _rms(yf_ref[...].astype(F32), gf_ref[...]).astype(BF16)
    o_ref[...] = x_ref[...] + _dot(y_conv, w_ref[:c, :]) + _dot(y_fox, w_ref[c:, :])


def _mix_out(proj, y_fox, conv_w, g_conv, g_fox, w_out, x2d, batch, seq):
    t, d = x2d.shape
    c = conv_w.shape[1]
    w = y_fox.shape[1]
    ts = _tile(seq, 512)
    ns = seq // ts
    hb = ts // SUBLANES
    tile = lambda col: (lambda b, s: (b * ns + s, col))
    halo = lambda col: (lambda b, s: (jnp.maximum((b * ns + s) * hb - 1, 0), col))
    const = lambda b, s: (0, 0)
    return pl.pallas_call(
        _mix_out_kernel,
        grid=(batch, ns),
        in_specs=[pl.BlockSpec((ts, c), tile(0)),
                  pl.BlockSpec((ts, c), tile(1)),
                  pl.BlockSpec((ts, c), tile(2)),
                  pl.BlockSpec((SUBLANES, c), halo(1)),
                  pl.BlockSpec((SUBLANES, c), halo(2)),
                  pl.BlockSpec((ts, w), tile(0)),
                  pl.BlockSpec((3, c), const),
                  pl.BlockSpec((1, c), const),
                  pl.BlockSpec((1, w), const),
                  pl.BlockSpec((c + w, d), const, pipeline_mode=pl.Buffered(1)),
                  pl.BlockSpec((ts, d), tile(0))],
        out_specs=pl.BlockSpec((ts, d), tile(0)),
        out_shape=SDS((t, d), F32),
        compiler_params=_cparams(2), name="mix_out")(
            proj, proj, proj, proj, proj, y_fox, conv_w, g_conv, g_fox, w_out, x2d)


def _memkv_kernel(mem_ref, g_ref, w_ref, gk_ref, k_ref, v_ref):
    hm = _rms(mem_ref[...], g_ref[...]).astype(BF16)
    kv = _dot(hm, w_ref[...])
    xw = k_ref.shape[1]
    for h in range(xw // HEAD_DIM):
        sl = slice(h * HEAD_DIM, (h + 1) * HEAD_DIM)
        k_ref[:, sl] = _rms(kv[:, sl], gk_ref[...]).astype(BF16)
    v_ref[...] = kv[:, xw:].astype(BF16)


def _memkv(mem2d, g_mem, w_xkv, g_xk, n_mem):
    tm_, d = mem2d.shape
    xw = w_xkv.shape[1] // 2
    return pl.pallas_call(
        _memkv_kernel,
        grid=(tm_ // n_mem,),
        in_specs=[pl.BlockSpec((n_mem, d), lambda b: (b, 0)),
                  pl.BlockSpec((1, d), lambda b: (0, 0)),
                  pl.BlockSpec((d, 2 * xw), lambda b: (0, 0)),
                  pl.BlockSpec((1, HEAD_DIM), lambda b: (0, 0))],
        out_specs=[pl.BlockSpec((n_mem, xw), lambda b: (b, 0)),
                   pl.BlockSpec((n_mem, xw), lambda b: (b, 0))],
        out_shape=[SDS((tm_, xw), BF16), SDS((tm_, xw), BF16)],
        compiler_params=_cparams(1), name="memkv")(mem2d, g_mem, w_xkv, g_xk)


def _xattn_kernel(x_ref, gx_ref, wq_ref, gq_ref, kn_ref, v_ref, wo_ref, gm_ref, wr_ref, br_ref,
                  x2_ref, hm_ref, ti_ref, tg_ref, pos_ref, cnt_ref, run_ref, tri_ref):
    i = pl.program_id(0)
    tm = x_ref.shape[0]

    @pl.when(i == 0)
    def _():
        run_ref[...] = jnp.zeros(run_ref.shape, F32)
        r = lax.broadcasted_iota(I32, (tm, tm), 0)
        c = lax.broadcasted_iota(I32, (tm, tm), 1)
        tri_ref[...] = jnp.where(r <= c, 1.0, 0.0).astype(BF16)

    x = x_ref[...]
    hb = _rms(x, gx_ref[...]).astype(BF16)
    q = _dot(hb, wq_ref[...])
    scale = HEAD_DIM ** -0.5
    outs = []
    for h in range(q.shape[1] // HEAD_DIM):
        sl = slice(h * HEAD_DIM, (h + 1) * HEAD_DIM)
        qh = (_rms(q[:, sl], gq_ref[...]) * scale).astype(BF16)
        s = _dot_nt(qh, kn_ref[:, sl])
        p = jnp.exp(s - jnp.max(s, axis=-1, keepdims=True))
        oh = _dot(p.astype(BF16), v_ref[:, sl]) / jnp.sum(p, axis=-1, keepdims=True)
        outs.append(oh.astype(BF16))
    x2 = x + _dot(jnp.concatenate(outs, axis=-1), wo_ref[...])
    x2_ref[...] = x2
    hm = _rms(x2, gm_ref[...]).astype(BF16)
    hm_ref[...] = hm
    lg = _dot_nt(wr_ref[...], hm) + br_ref[...]
    n_exp = lg.shape[0]
    eidx = lax.broadcasted_iota(I32, lg.shape, 0).astype(F32)
    vals, sel = [], []
    for r in range(TOP_K):
        mx = jnp.max(lg, axis=0, keepdims=True)
        am = jnp.min(jnp.where(lg == mx, eidx, float(n_exp)), axis=0, keepdims=True)
        ti_ref[r:r + 1, :] = am.astype(I32)
        vals.append(mx)
        sel.append(eidx == am)
        lg = jnp.where(sel[r], -jnp.inf, lg)
    ex = [jnp.exp(v - vals[0]) for v in vals]
    den = ex[0]
    for e in ex[1:]:
        den = den + e
    for r in range(TOP_K):
        tg_ref[r:r + 1, :] = ex[r] / den
    onehot = jnp.zeros(lg.shape, F32)
    for r in range(TOP_K):
        onehot = onehot + jnp.where(sel[r], 1.0, 0.0)
    incl = _dot(onehot.astype(BF16), tri_ref[...])
    excl = incl - onehot + run_ref[...]
    for r in range(TOP_K):
        pos_ref[r:r + 1, :] = jnp.sum(jnp.where(sel[r], excl, 0.0), axis=0,
                                      keepdims=True).astype(I32)
    run = run_ref[...] + incl[:, tm - 1:tm]
    run_ref[...] = run
    cnt_ref[...] = jnp.broadcast_to(run, cnt_ref.shape)


def _xattn(x1, g_xattn, w_xq, g_xq, kn, vm, w_xo, g_moe, w_rt, b_r, seq, n_mem):
    t, d = x1.shape
    xw = w_xq.shape[1]
    n_exp = w_rt.shape[0]
    tm = _tile(seq, 512)
    per = seq // tm
    const = lambda i: (0, 0)
    return pl.pallas_call(
        _xattn_kernel,
        grid=(t // tm,),
        in_specs=[pl.BlockSpec((tm, d), lambda i: (i, 0)),
                  pl.BlockSpec((1, d), const),
                  pl.BlockSpec((d, xw), const),
                  pl.BlockSpec((1, HEAD_DIM), const),
                  pl.BlockSpec((n_mem, xw), lambda i: (i // per, 0)),
                  pl.BlockSpec((n_mem, xw), lambda i: (i // per, 0)),
                  pl.BlockSpec((xw, d), const),
                  pl.BlockSpec((1, d), const),
                  pl.BlockSpec((n_exp, d), const),
                  pl.BlockSpec((n_exp, 1), const)],
        out_specs=[pl.BlockSpec((tm, d), lambda i: (i, 0)),
                   pl.BlockSpec((tm, d), lambda i: (i, 0)),
                   pl.BlockSpec((TOP_K, tm), lambda i: (0, i)),
                   pl.BlockSpec((TOP_K, tm), lambda i: (0, i)),
                   pl.BlockSpec((TOP_K, tm), lambda i: (0, i)),
                   pl.BlockSpec((n_exp, 128), const)],
        out_shape=[SDS((t, d), F32), SDS((t, d), BF16),
                   SDS((TOP_K, t), I32), SDS((TOP_K, t), F32), SDS((TOP_K, t), I32),
                   SDS((n_exp, 128), F32)],
        scratch_shapes=[pltpu.VMEM((n_exp, 1), F32), pltpu.VMEM((tm, tm), BF16)],
        compiler_params=_cparams(1), name="xattn_router")(
            x1, g_xattn, w_xq, g_xq, kn, vm, w_xo, g_moe, w_rt, b_r)


def _take(values, idx):
    pick = idx[:, None] == jnp.arange(values.shape[0], dtype=I32)[None, :]
    return jnp.sum(jnp.where(pick, values[None, :], 0), axis=1).astype(I32)


def _route_tables(counts, n_assign, tm):
    n_exp = counts.shape[0]
    ts = 2 * tm
    valid = (counts + tm - 1) // tm * tm
    seg = ((counts + ts - 1) // ts * ts).astype(I32)
    seg_end = jnp.cumsum(seg).astype(I32)
    seg_start = seg_end - seg
    n_steps = n_assign // ts + n_exp
    n_used = seg_end[-1] // ts
    idx = jnp.arange(n_steps, dtype=I32)
    row0 = jnp.minimum(idx, n_used - 1) * ts
    exp = jnp.sum((row0[:, None] >= seg_end[None, :]).astype(I32), axis=1)
    exp = jnp.minimum(exp, n_exp - 1)
    n_valid = jnp.clip((_take(seg_start + valid, exp) - row0) // tm, 1, 2).astype(I32)
    prev = jnp.concatenate([jnp.full((1,), -1, I32), exp[:-1]])
    first = (exp != prev).astype(I32)
    rem = _take(seg_end, exp) // ts - idx
    cand = jnp.where(counts > 0, jnp.arange(n_exp, dtype=I32), n_exp)
    after = jnp.concatenate([lax.cummin(cand, reverse=True)[1:], jnp.full((1,), n_exp, I32)])
    nxt = _take(jnp.where(after >= n_exp, -1, after), exp)
    tables = (exp, first, rem.astype(I32), nxt, n_valid)
    return seg_start, seg_end, seg, tables, n_used.reshape(1).astype(I32)


def _block_tables(tables, n_used):
    exp, first, rem, nxt, n_valid = tables
    half = jnp.tile(jnp.arange(2, dtype=I32), exp.shape[0])
    rep = lambda v: jnp.repeat(v, 2)
    return (rep(exp), rep(first) * (1 - half), 2 * rep(rem) - half, rep(nxt),
            (half < rep(n_valid)).astype(I32)), 2 * n_used


def _dest_kernel(ps_ref, ti_ref, pos_ref, o_ref, *, n_exp):
    ti = ti_ref[...]
    acc = pos_ref[...]
    for e in range(n_exp):
        acc = acc + jnp.where(ti == e, ps_ref[e], 0)
    o_ref[...] = acc


def _dest_rows(pad_start, ti, pos):
    k, t = ti.shape
    kern = functools.partial(_dest_kernel, n_exp=pad_start.shape[0])
    grid_spec = pltpu.PrefetchScalarGridSpec(
        num_scalar_prefetch=1, grid=(1,),
        in_specs=[pl.BlockSpec((k, t), lambda i, ps: (0, 0)),
                  pl.BlockSpec((k, t), lambda i, ps: (0, 0))],
        out_specs=pl.BlockSpec((k, t), lambda i, ps: (0, 0)))
    return pl.pallas_call(kern, grid_spec=grid_spec, out_shape=SDS((k, t), I32),
                          compiler_params=_cparams(1), name="moe_dest")(pad_start, ti, pos)


def _dispatch_kernel(pe_ref, pc_ref, nu_ref, dst_ref, x_ref, xs_ref, buf, zbuf, sem, zsem,
                     *, tc, ts, n_exp, n_steps):
    i = pl.program_id(0)
    n = pl.num_programs(0)
    ns = x_ref.shape[1] // (2 * LANES)

    def zero_copy(row0):
        rows = pl.ds(pl.multiple_of(row0 * ns, ts * ns), ts * ns)
        return pltpu.make_async_copy(zbuf, xs_ref.at[rows], zsem)

    @pl.when(i == 0)
    def _():
        zbuf[...] = jnp.zeros(zbuf.shape, zbuf.dtype)
        for start in (True, False):
            for e in range(n_exp):
                @pl.when(pc_ref[e] > 0)
                def _(e=e):
                    cp = zero_copy(pe_ref[e] - ts)
                    cp.start() if start else cp.wait()

            def tail(b, c):
                cp = zero_copy(b * ts)
                cp.start() if start else cp.wait()
                return c
            lax.fori_loop(nu_ref[0], n_steps, tail, 0)

    slot = i % 2
    x = x_ref[...]
    for s in range(ns):
        lo = x[:, (2 * s) * LANES:(2 * s + 1) * LANES]
        hi = x[:, (2 * s + 1) * LANES:(2 * s + 2) * LANES]
        buf[slot, pl.ds(s, tc, stride=ns), :] = _pack_pair(lo, hi)
    for k in range(TOP_K):
        for r in range(tc):
            dst = pl.ds(pl.multiple_of(dst_ref[0, k, r] * ns, ns), ns)
            pltpu.make_async_copy(buf.at[slot, pl.ds(r * ns, ns)], xs_ref.at[dst],
                                  sem.at[slot]).start(priority=r % 2)

    def wait_slot(s):
        for k in range(TOP_K):
            pltpu.make_async_copy(buf.at[s], xs_ref.at[pl.ds(0, tc * ns)], sem.at[s]).wait()

    @pl.when(i >= 1)
    def _():
        wait_slot(1 - slot)

    @pl.when(i == n - 1)
    def _():
        wait_slot(slot)


def _dispatch(hm, dest, seg_end, seg, n_used, ts, n_steps):
    t, d = hm.shape
    tc = _tile(t, 128)
    n = t // tc
    ns = d // (2 * LANES)
    n_exp = seg_end.shape[0]
    dest3 = dest.reshape(TOP_K, n, tc).transpose(1, 0, 2)
    kern = functools.partial(_dispatch_kernel, tc=tc, ts=ts, n_exp=n_exp, n_steps=n_steps)
    grid_spec = pltpu.PrefetchScalarGridSpec(
        num_scalar_prefetch=3, grid=(n,),
        in_specs=[pl.BlockSpec((1, TOP_K, tc), lambda i, *_: (i, 0, 0), memory_space=pltpu.SMEM),
                  pl.BlockSpec((tc, d), lambda i, *_: (i, 0))],
        out_specs=pl.BlockSpec(memory_space=pl.ANY),
        scratch_shapes=[pltpu.VMEM((2, tc * ns, LANES), U32),
                        pltpu.VMEM((ts * ns, LANES), U32),
                        pltpu.SemaphoreType.DMA((2,)),
                        pltpu.SemaphoreType.DMA(())])
    return pl.pallas_call(
        kern, grid_spec=grid_spec,
        out_shape=SDS((n_steps * ts * ns, LANES), U32),
        compiler_params=_cparams(1), name="moe_dispatch")(seg_end, seg, n_used, dest3, hm)


W_CHUNK = 512


def _stream_weights(i, be_ref, first_ref, rem_ref, nxt_ref, w_hbm, wb_ref, stage_ref, st_ref, sem):
    n_chunks, _, chunk = wb_ref.shape[1:]

    def chunk_copy(e, c, slot):
        cols = pl.ds(pl.multiple_of(c * chunk, chunk), chunk)
        return pltpu.make_async_copy(w_hbm.at[e, :, cols], stage_ref.at[slot], sem.at[slot])

    def start_first_two(e):
        chunk_copy(e, 0, 0).start()
        chunk_copy(e, 1, 1).start()

    def cast_chunks(e, dst, done, issued, k):
        def body(_, carry):
            done, issued = carry
            slot = done % 2
            chunk_copy(e, done, slot).wait()
            wb_ref[dst, done] = stage_ref[slot].astype(BF16)

            @pl.when(issued < n_chunks)
            def _():
                chunk_copy(e, issued, slot).start()
            return done + 1, jnp.minimum(issued + 1, n_chunks)
        return lax.fori_loop(0, k, body, (done, issued))

    @pl.when(i == 0)
    def _():
        start_first_two(be_ref[0])
        cast_chunks(be_ref[0], 0, 0, 2, n_chunks)
        st_ref[0] = 1

    @pl.when(first_ref[i] == 1)
    def _():
        st_ref[0] = 1 - st_ref[0]
        st_ref[1] = 0
        st_ref[2] = 0

        @pl.when(nxt_ref[i] >= 0)
        def _():
            start_first_two(nxt_ref[i])
            st_ref[2] = 2

    cur = st_ref[0]

    @pl.when(nxt_ref[i] >= 0)
    def _():
        done = st_ref[1]
        rem = rem_ref[i]
        lead = (first_ref[i] == 1) & (rem > 1)
        left = jnp.where(first_ref[i] == 1, jnp.maximum(rem - 1, 1), rem)
        k = jnp.where(lead, 0, lax.div(n_chunks - done + left - 1, left))
        done, issued = cast_chunks(nxt_ref[i], 1 - cur, done, st_ref[2], k)
        st_ref[1] = done
        st_ref[2] = issued

    return cur


def _moe_up_kernel(be_ref, first_ref, rem_ref, nxt_ref, ok_ref, nu_ref, x_ref, b_ref, w_hbm, o_ref,
                   wb_ref, stage_ref, st_ref, sem):
    i = pl.program_id(0)
    tm = o_ref.shape[0]
    ns = x_ref.shape[0] // tm
    n_half = wb_ref.shape[1] // 2
    chunk = wb_ref.shape[3]
    f = n_half * chunk

    @pl.when(i < nu_ref[0])
    def _():
        cur = _stream_weights(i, be_ref, first_ref, rem_ref, nxt_ref, w_hbm, wb_ref, stage_ref,
                              st_ref, sem)

        @pl.when(ok_ref[i] == 1)
        def _():
            cols = []
            for s in range(ns):
                lo, hi = _unpack_pair(x_ref[pl.ds(s, tm, stride=ns), :])
                cols += [lo.astype(BF16), hi.astype(BF16)]
            x = jnp.concatenate(cols, axis=1)
            for n in range(n_half):
                lo, hi = n * chunk, (n + 1) * chunk
                gate = jnp.minimum(_dot(x, wb_ref[cur, n]) + b_ref[:, lo:hi], SWIGLU_LIMIT)
                up = jnp.clip(_dot(x, wb_ref[cur, n_half + n]) + b_ref[:, f + lo:f + hi],
                              -SWIGLU_LIMIT, SWIGLU_LIMIT)
                glu = gate * jax.nn.sigmoid(SWIGLU_ALPHA * gate)
                o_ref[:, lo:hi] = ((up + 1.0) * glu).astype(o_ref.dtype)

    @pl.when((i >= nu_ref[0]) | (ok_ref[i] == 0))
    def _():
        o_ref[...] = jnp.zeros(o_ref.shape, o_ref.dtype)


def _moe_up(xs, w_gu, b_gu, tables, n_used, tm):
    d = w_gu.shape[1]
    ns = d // (2 * LANES)
    n_rows = xs.shape[0] // ns
    two_f = w_gu.shape[2]
    chunk = min(W_CHUNK, two_f // 2)
    grid_spec = pltpu.PrefetchScalarGridSpec(
        num_scalar_prefetch=6,
        grid=(n_rows // tm,),
        in_specs=[pl.BlockSpec((tm * ns, LANES),
                               lambda i, be, fi, re, nx, ok, nu: (jnp.minimum(i, nu[0] - 1), 0)),
                  pl.BlockSpec((None, 1, two_f), lambda i, be, fi, re, nx, ok, nu: (be[i], 0, 0)),
                  pl.BlockSpec(memory_space=pl.ANY)],
        out_specs=pl.BlockSpec((tm, two_f // 2), lambda i, *_: (i, 0)),
        scratch_shapes=[pltpu.VMEM((2, two_f // chunk, d, chunk), BF16),
                        pltpu.VMEM((2, d, chunk), w_gu.dtype),
                        pltpu.SMEM((3,), I32),
                        pltpu.SemaphoreType.DMA((2,))])
    return pl.pallas_call(
        _moe_up_kernel, grid_spec=grid_spec,
        out_shape=SDS((n_rows, two_f // 2), BF16),
        compiler_params=_cparams(1), name="moe_up")(*tables, n_used, xs, b_gu, w_gu)


def _moe_down_kernel(be_ref, first_ref, rem_ref, nxt_ref, nv_ref, nu_ref, a_ref, b_ref, w_hbm, o_ref,
                     wf_ref, st_ref, sem):
    i = pl.program_id(0)
    ts = a_ref.shape[0]
    ns = o_ref.shape[0] // ts
    n_chunks, _, chunk = wf_ref.shape[1:]

    def chunk_copy(e, dst, c):
        return pltpu.make_async_copy(w_hbm.at[e, :, pl.ds(c * chunk, chunk)], wf_ref.at[dst, c], sem)

    def compute(cur, rows):
        a = a_ref[:rows, :]
        for n in range(n_chunks):
            lo, hi = n * chunk, (n + 1) * chunk
            y = _dot(a, wf_ref[cur, n].astype(BF16)) + b_ref[:, lo:hi]
            for j in range(chunk // (2 * LANES)):
                s = lo // (2 * LANES) + j
                w = _pack_pair(y[:, (2 * j) * LANES:(2 * j + 1) * LANES],
                               y[:, (2 * j + 1) * LANES:(2 * j + 2) * LANES])
                o_ref[pl.ds(s, rows, stride=ns), :] = w

    @pl.when(i < nu_ref[0])
    def _():
        @pl.when(i == 0)
        def _():
            for c in range(n_chunks):
                chunk_copy(be_ref[0], 0, c).start()
            st_ref[0] = 1
            st_ref[1] = 1

        @pl.when(first_ref[i] == 1)
        def _():
            @pl.when(st_ref[1] == 1)
            def _():
                for c in range(n_chunks):
                    chunk_copy(be_ref[i], 1 - st_ref[0], c).wait()
            st_ref[0] = 1 - st_ref[0]
            st_ref[1] = 0

            @pl.when(nxt_ref[i] >= 0)
            def _():
                for c in range(n_chunks):
                    chunk_copy(nxt_ref[i], 1 - st_ref[0], c).start()
                st_ref[1] = 1

        cur = st_ref[0]

        @pl.when(nv_ref[i] == 2)
        def _():
            compute(cur, ts)

        @pl.when(nv_ref[i] == 1)
        def _():
            compute(cur, ts // 2)
            half = ts // 2 * ns
            o_ref[half:, :] = jnp.zeros((half, LANES), o_ref.dtype)

    @pl.when(i >= nu_ref[0])
    def _():
        o_ref[...] = jnp.zeros(o_ref.shape, o_ref.dtype)


def _moe_down(act, w_dn, b_dn, tables, n_used, ts):
    n_rows, f = act.shape
    d = w_dn.shape[2]
    ns = d // (2 * LANES)
    chunk = max(min(W_CHUNK, d // 2), 2 * LANES)
    grid_spec = pltpu.PrefetchScalarGridSpec(
        num_scalar_prefetch=6,
        grid=(n_rows // ts,),
        in_specs=[pl.BlockSpec((ts, f),
                               lambda i, be, fi, re, nx, nv, nu: (jnp.minimum(i, nu[0] - 1), 0)),
                  pl.BlockSpec((None, 1, d), lambda i, be, fi, re, nx, nv, nu: (be[i], 0, 0)),
                  pl.BlockSpec(memory_space=pl.ANY)],
        out_specs=pl.BlockSpec((ts * ns, LANES), lambda i, *_: (i, 0)),
        scratch_shapes=[pltpu.VMEM((2, d // chunk, f, chunk), w_dn.dtype),
                        pltpu.SMEM((2,), I32),
                        pltpu.SemaphoreType.DMA(())])
    return pl.pallas_call(
        _moe_down_kernel, grid_spec=grid_spec,
        out_shape=SDS((n_rows * ns, LANES), U32),
        compiler_params=_cparams(1), name="moe_down")(*tables, n_used, act, b_dn, w_dn)


def _combine_kernel(dst_ref, x_ref, g_ref, ys_ref, o_ref, buf, sem, *, tc):
    i = pl.program_id(0)
    n = pl.num_programs(0) - 1
    ns = x_ref.shape[1] // (2 * LANES)

    @pl.when(i < n)
    def _():
        slot = i % 2
        for k in range(TOP_K):
            for r in range(tc):
                src = pl.ds(pl.multiple_of(dst_ref[0, k, r] * ns, ns), ns)
                pltpu.make_async_copy(ys_ref.at[src], buf.at[slot, k, pl.ds(r * ns, ns)],
                                      sem.at[slot]).start(priority=r % 2)

    @pl.when(i >= 1)
    def _():
        slot = (i - 1) % 2
        for k in range(TOP_K):
            pltpu.make_async_copy(ys_ref.at[pl.ds(0, tc * ns)], buf.at[slot, k],
                                  sem.at[slot]).wait()
        g = g_ref[...]
        g = jnp.concatenate([g, jnp.zeros((SUBLANES - TOP_K, tc), F32)], axis=0).T
        for s in range(ns):
            c0, c1, c2 = (2 * s) * LANES, (2 * s + 1) * LANES, (2 * s + 2) * LANES
            acc_lo = x_ref[:, c0:c1]
            acc_hi = x_ref[:, c1:c2]
            for k in range(TOP_K):
                lo, hi = _unpack_pair(buf[slot, k, pl.ds(s, tc, stride=ns), :])
                acc_lo = acc_lo + g[:, k:k + 1] * lo
                acc_hi = acc_hi + g[:, k:k + 1] * hi
            o_ref[:, c0:c1] = acc_lo
            o_ref[:, c1:c2] = acc_hi


def _combine(x2, gates, dest, ys):
    t, d = x2.shape
    tc = _tile(t, 128)
    n = t // tc
    ns = d // (2 * LANES)
    dest3 = dest.reshape(TOP_K, n, tc).transpose(1, 0, 2)
    kern = functools.partial(_combine_kernel, tc=tc)
    prev = lambda i: (jnp.maximum(i - 1, 0), 0)
    return pl.pallas_call(
        kern,
        grid=(n + 1,),
        in_specs=[pl.BlockSpec((1, TOP_K, tc), lambda i: (jnp.minimum(i, n - 1), 0, 0),
                               memory_space=pltpu.SMEM),
                  pl.BlockSpec((tc, d), prev),
                  pl.BlockSpec((TOP_K, tc), lambda i: (0, jnp.maximum(i - 1, 0))),
                  pl.BlockSpec(memory_space=pl.ANY)],
        out_specs=pl.BlockSpec((tc, d), prev),
        out_shape=SDS((t, d), F32),
        scratch_shapes=[pltpu.VMEM((2, TOP_K, tc * ns, LANES), U32),
                        pltpu.SemaphoreType.DMA((2,))],
        compiler_params=_cparams(1), name="moe_combine")(dest3, x2, gates, ys)


def _layer(x2d, mem2d, batch, seq, n_mem, p):
    t, d = x2d.shape
    c = p["conv_w"].shape[1]
    fw = p["g_fox_out"].shape[0]
    n_heads = fw // HEAD_DIM
    n_main = 3 * c + 3 * fw
    n_exp = p["w_router"].shape[1]
    assert n_heads <= F_ROWS and TOP_K <= SUBLANES and d % (2 * LANES) == 0

    w_in = p["w_mix_in"].astype(BF16)
    wf_t = jnp.zeros((F_ROWS, d), BF16).at[:n_heads].set(w_in[:, n_main:].T)
    b_col = jnp.zeros((F_ROWS, 1), F32).at[:n_heads, 0].set(p["b_forget"])

    proj, ft = _mix_in(x2d, p["g_mix"][None, :], wf_t, w_in, n_main)
    cdec = _decay(ft, b_col, seq)
    y_fox = _fox_attention(proj, cdec, p["g_q"][None, :], p["g_k"][None, :],
                           batch, seq, n_heads, 3 * c, fw)
    x1 = _mix_out(proj, y_fox, p["conv_w"], p["g_conv_out"][None, :], p["g_fox_out"][None, :],
                  p["w_mix_out"].astype(BF16), x2d, batch, seq)

    kn, vm = _memkv(mem2d, p["g_mem"][None, :], p["w_xkv"].astype(BF16),
                    p["g_xk"][None, :], n_mem)
    x2, hm, ti, tg, pos, cnt = _xattn(x1, p["g_xattn"][None, :], p["w_xq"].astype(BF16),
                            p["g_xq"][None, :], kn, vm, p["w_xo"].astype(BF16),
                            p["g_moe"][None, :], p["w_router"].T.astype(BF16),
                            p["b_router"][:, None], seq, n_mem)

    tm = _tile(TOP_K * t, 512) // 2
    counts = cnt[:, 0].astype(I32)
    seg_start, seg_end, seg, tables, n_used = _route_tables(counts, TOP_K * t, tm)
    dest = _dest_rows(seg_start, ti, pos)
    xs = _dispatch(hm, dest, seg_end, seg, n_used, 2 * tm, tables[0].shape[0])
    act = _moe_up(xs, p["w_gate_up"], p["b_gate_up"][:, None, :], *_block_tables(tables, n_used), tm)
    ys = _moe_down(act, p["w_down"], p["b_down"][:, None, :], tables, n_used, 2 * tm)
    return _combine(x2, tg, dest, ys)


def kernel(x, mem, g_mix, w_mix_in, b_forget, conv_w, g_q, g_k, g_conv_out, g_fox_out, w_mix_out, g_xattn, g_mem, w_xq, w_xkv, g_xq, g_xk, w_xo, g_moe, w_router, b_router, w_gate_up, b_gate_up, w_down, b_down):
    batch, seq, d = x.shape
    n_mem = mem.shape[1]
    params = dict(g_mix=g_mix, w_mix_in=w_mix_in, b_forget=b_forget, conv_w=conv_w, g_q=g_q,
                  g_k=g_k, g_conv_out=g_conv_out, g_fox_out=g_fox_out, w_mix_out=w_mix_out,
                  g_xattn=g_xattn, g_mem=g_mem, w_xq=w_xq, w_xkv=w_xkv, g_xq=g_xq, g_xk=g_xk,
                  w_xo=w_xo, g_moe=g_moe, w_router=w_router, b_router=b_router,
                  w_gate_up=w_gate_up, b_gate_up=b_gate_up, w_down=w_down, b_down=b_down)
    x2d = x.reshape(batch * seq, d)
    mem2d = mem.reshape(batch * n_mem, d)
    for l in range(g_mix.shape[0]):
        x2d = _layer(x2d, mem2d, batch, seq, n_mem, {k: v[l] for k, v in params.items()})
    return x2d.reshape(batch, seq, d)
```

```python
import functools

import jax
import jax.numpy as jnp
from jax import lax
from jax.experimental import pallas as pl
from jax.experimental.pallas import tpu as pltpu

F32 = jnp.float32
BF16 = jnp.bfloat16
I32 = jnp.int32
U32 = jnp.uint32
SDS = jax.ShapeDtypeStruct

EPS = 1e-6
HEAD_DIM = 128
TOP_K = 4
SWIGLU_LIMIT = 7.0
SWIGLU_ALPHA = 1.702
MASK_VALUE = float(jnp.finfo(jnp.float32).min)
V7X_VMEM_LIMIT_BYTES = 56 * 1024 * 1024
F_ROWS = 16
LANES = 128
SUBLANES = 8


def _cparams(n_axes):
    return pltpu.CompilerParams(
        dimension_semantics=("arbitrary",) * n_axes,
        vmem_limit_bytes=V7X_VMEM_LIMIT_BYTES)


def _rms(x, g):
    return x * lax.rsqrt(jnp.mean(x * x, axis=-1, keepdims=True) + EPS) * g


def _dot(a, b):
    return jnp.dot(a, b, preferred_element_type=F32)


def _dot_nt(a, b):
    return lax.dot_general(a, b, (((1,), (1,)), ((), ())), preferred_element_type=F32)


def _tile(n, pref):
    return pref if n % pref == 0 else n


def _pack_pair(lo, hi):
    lo = lax.bitcast_convert_type(lo.astype(BF16).astype(F32), U32) >> 16
    hi = lax.bitcast_convert_type(hi.astype(BF16).astype(F32), U32) & jnp.uint32(0xFFFF0000)
    return hi | lo


def _unpack_pair(w):
    lo = lax.bitcast_convert_type(w << 16, F32)
    hi = lax.bitcast_convert_type(w & jnp.uint32(0xFFFF0000), F32)
    return lo, hi


def _mix_in_kernel(x_ref, g_ref, wf_ref, w_ref, o_ref, ft_ref, h_ref):
    @pl.when(pl.program_id(1) == 0)
    def _():
        hb = _rms(x_ref[...], g_ref[...]).astype(BF16)
        h_ref[...] = hb
        ft_ref[...] = _dot_nt(wf_ref[...], hb)

    o_ref[...] = _dot(h_ref[...], w_ref[...]).astype(o_ref.dtype)


def _mix_in(x2d, g, wf_t, w_in, n_out):
    t, d = x2d.shape
    tm = _tile(t, 1024)
    tn = _tile(n_out, 1024)
    return pl.pallas_call(
        _mix_in_kernel,
        grid=(t // tm, n_out // tn),
        in_specs=[pl.BlockSpec((tm, d), lambda i, j: (i, 0)),
                  pl.BlockSpec((1, d), lambda i, j: (0, 0)),
                  pl.BlockSpec((F_ROWS, d), lambda i, j: (0, 0)),
                  pl.BlockSpec((d, tn), lambda i, j: (0, j))],
        out_specs=[pl.BlockSpec((tm, tn), lambda i, j: (i, j)),
                   pl.BlockSpec((F_ROWS, tm), lambda i, j: (0, i))],
        out_shape=[SDS((t, n_out), BF16), SDS((F_ROWS, t), F32)],
        scratch_shapes=[pltpu.VMEM((tm, d), BF16)],
        compiler_params=_cparams(2), name="mix_in")(x2d, g, wf_t, w_in)


def _decay_kernel(ft_ref, b_ref, c_ref):
    z = ft_ref[...] + b_ref[...]
    lf = jnp.minimum(z, 0.0) - jnp.log(1.0 + jnp.exp(-jnp.abs(z)))
    s = lf.shape[1]
    lane = lax.broadcasted_iota(I32, lf.shape, 1)
    d = 1
    while d < s:
        lf = lf + jnp.where(lane >= d, pltpu.roll(lf, d, 1), 0.0)
        d *= 2
    c_ref[...] = lf


def _decay(ft, b_col, seq):
    rows, t = ft.shape
    return pl.pallas_call(
        _decay_kernel,
        grid=(t // seq,),
        in_specs=[pl.BlockSpec((rows, seq), lambda b: (0, b)),
                  pl.BlockSpec((rows, 1), lambda b: (0, 0))],
        out_specs=pl.BlockSpec((rows, seq), lambda b: (0, b)),
        out_shape=SDS((rows, t), F32),
        compiler_params=_cparams(1), name="decay")(ft, b_col)


def _fox_kernel(q_ref, k_ref, v_ref, c_ref, gq_ref, gk_ref, o_ref, kn_ref, nc_ref, *, tq, hpb):
    g = pl.program_id(1)
    seq = q_ref.shape[0]
    hd = HEAD_DIM
    scale = HEAD_DIM ** -0.5
    for j in range(hpb):
        cs = slice(j * hd, (j + 1) * hd)
        kn_ref[:, cs] = _rms(k_ref[:, cs].astype(F32), gk_ref[...]).astype(BF16)
        nc_ref[j:j + 1, :] = -c_ref[pl.ds(g * hpb + j, 1), :]
    row = lax.broadcasted_iota(I32, (tq, tq), 0)
    col = lax.broadcasted_iota(I32, (tq, tq), 1)
    causal = col <= row
    for qi in range(seq // tq):
        lo, hi = qi * tq, (qi + 1) * tq
        for j in range(hpb):
            cs = slice(j * hd, (j + 1) * hd)
            q = (_rms(q_ref[lo:hi, cs].astype(F32), gq_ref[...]) * scale).astype(BF16)
            s_diag = jnp.where(causal, _dot_nt(q, kn_ref[lo:hi, cs]) + nc_ref[j:j + 1, lo:hi],
                               MASK_VALUE)
            m = jnp.max(s_diag, axis=-1, keepdims=True)
            if qi > 0:
                s_past = _dot_nt(q, kn_ref[:lo, cs]) + nc_ref[j:j + 1, :lo]
                m = jnp.maximum(m, jnp.max(s_past, axis=-1, keepdims=True))
            p_diag = jnp.exp(s_diag - m)
            l = jnp.sum(p_diag, axis=-1, keepdims=True)
            acc = _dot(p_diag.astype(BF16), v_ref[lo:hi, cs])
            if qi > 0:
                p_past = jnp.exp(s_past - m)
                l = l + jnp.sum(p_past, axis=-1, keepdims=True)
                acc = acc + _dot(p_past.astype(BF16), v_ref[:lo, cs])
            o_ref[lo:hi, cs] = (acc / l).astype(o_ref.dtype)


def _fox_attention(proj, c, g_q, g_k, batch, seq, n_heads, col0, width):
    t = proj.shape[0]
    tq = _tile(seq, 256)
    hpb = 2 if n_heads % 2 == 0 else 1
    bw = hpb * HEAD_DIM
    cb = col0 // bw
    wb = width // bw
    kern = functools.partial(_fox_kernel, tq=tq, hpb=hpb)
    return pl.pallas_call(
        kern,
        grid=(batch, n_heads // hpb),
        in_specs=[pl.BlockSpec((seq, bw), lambda b, h: (b, cb + h)),
                  pl.BlockSpec((seq, bw), lambda b, h: (b, cb + wb + h)),
                  pl.BlockSpec((seq, bw), lambda b, h: (b, cb + 2 * wb + h)),
                  pl.BlockSpec((F_ROWS, seq), lambda b, h: (0, b)),
                  pl.BlockSpec((1, HEAD_DIM), lambda b, h: (0, 0)),
                  pl.BlockSpec((1, HEAD_DIM), lambda b, h: (0, 0))],
        out_specs=pl.BlockSpec((seq, bw), lambda b, h: (b, h)),
        out_shape=SDS((t, width), BF16),
        scratch_shapes=[pltpu.VMEM((seq, bw), BF16), pltpu.VMEM((SUBLANES, seq), F32)],
        compiler_params=_cparams(2), name="fox_attention")(proj, proj, proj, c, g_q, g_k)


def _mix_out_kernel(ub_ref, uc_ref, ux_ref, hc_ref, hx_ref, yf_ref, cw_ref, gc_ref, gf_ref,
                    w_ref, x_ref, o_ref):
    si = pl.program_id(1)
    p = uc_ref[...].astype(F32) * ux_ref[...].astype(F32)
    halo = hc_ref[...].astype(F32) * hx_ref[...].astype(F32)
    halo = jnp.where(si > 0, halo, 0.0)
    row = lax.broadcasted_iota(I32, p.shape, 0)
    last, last2 = halo[SUBLANES - 1:SUBLANES, :], halo[SUBLANES - 2:SUBLANES - 1, :]
    p1 = jnp.where(row == 0, last, pltpu.roll(p, 1, 0))
    p2 = jnp.where(row == 0, last2, jnp.where(row == 1, last, pltpu.roll(p, 2, 0)))
    w = cw_ref[...]
    y = ub_ref[...].astype(F32) * (w[0:1, :] * p2 + w[1:2, :] * p1 + w[2:3, :] * p)
    c = y.shape[1]
    y_conv = _rms(y, gc_ref[...]).astype(BF16)
    y_fox = _rms(yf_ref[...].astype(F32), gf_ref[...]).astype(BF16)
    o_ref[...] = (x_ref[...] + _dot(y_conv, w_ref[:c, :].astype(BF16))
                  + _dot(y_fox, w_ref[c:, :].astype(BF16)))


def _mix_out(proj, y_fox, conv_w, g_conv, g_fox, w_out, x2d, batch, seq):
    t, d = x2d.shape
    c = conv_w.shape[1]
    w = y_fox.shape[1]
    ts = _tile(seq, 512)
    ns = seq // ts
    hb = ts // SUBLANES
    tile = lambda col: (lambda b, s: (b * ns + s, col))
    halo = lambda col: (lambda b, s: (jnp.maximum((b * ns + s) * hb - 1, 0), col))
    const = lambda b, s: (0, 0)
    return pl.pallas_call(
        _mix_out_kernel,
        grid=(batch, ns),
        in_specs=[pl.BlockSpec((ts, c), tile(0)),
                  pl.BlockSpec((ts, c), tile(1)),
                  pl.BlockSpec((ts, c), tile(2)),
                  pl.BlockSpec((SUBLANES, c), halo(1)),
                  pl.BlockSpec((SUBLANES, c), halo(2)),
                  pl.BlockSpec((ts, w), tile(0)),
                  pl.BlockSpec((3, c), const),
                  pl.BlockSpec((1, c), const),
                  pl.BlockSpec((1, w), const),
                  pl.BlockSpec((c + w, d), const, pipeline_mode=pl.Buffered(1)),
                  pl.BlockSpec((ts, d), tile(0))],
        out_specs=pl.BlockSpec((ts, d), tile(0)),
        out_shape=SDS((t, d), F32),
        compiler_params=_cparams(2), name="mix_out")(
            proj, proj, proj, proj, proj, y_fox, conv_w, g_conv, g_fox, w_out, x2d)


def _memkv_kernel(mem_ref, g_ref, w_ref, gk_ref, k_ref, v_ref):
    hm = _rms(mem_ref[...], g_ref[...]).astype(BF16)
    kv = _dot(hm, w_ref[...].astype(BF16))
    xw = k_ref.shape[1]
    for h in range(xw // HEAD_DIM):
        sl = slice(h * HEAD_DIM, (h + 1) * HEAD_DIM)
        k_ref[:, sl] = _rms(kv[:, sl], gk_ref[...]).astype(BF16)
    v_ref[...] = kv[:, xw:].astype(BF16)


def _memkv(mem2d, g_mem, w_xkv, g_xk, n_mem):
    tm_, d = mem2d.shape
    xw = w_xkv.shape[1] // 2
    return pl.pallas_call(
        _memkv_kernel,
        grid=(tm_ // n_mem,),
        in_specs=[pl.BlockSpec((n_mem, d), lambda b: (b, 0)),
                  pl.BlockSpec((1, d), lambda b: (0, 0)),
                  pl.BlockSpec((d, 2 * xw), lambda b: (0, 0)),
                  pl.BlockSpec((1, HEAD_DIM), lambda b: (0, 0))],
        out_specs=[pl.BlockSpec((n_mem, xw), lambda b: (b, 0)),
                   pl.BlockSpec((n_mem, xw), lambda b: (b, 0))],
        out_shape=[SDS((tm_, xw), BF16), SDS((tm_, xw), BF16)],
        compiler_params=_cparams(1), name="memkv")(mem2d, g_mem, w_xkv, g_xk)


def _xattn_kernel(x_ref, gx_ref, wq_ref, gq_ref, kn_ref, v_ref, wo_ref, gm_ref, wr_ref, br_ref,
                  x2_ref, hm_ref, ti_ref, tg_ref, pos_ref, cnt_ref, run_ref, tri_ref):
    i = pl.program_id(0)
    tm = x_ref.shape[0]

    @pl.when(i == 0)
    def _():
        run_ref[...] = jnp.zeros(run_ref.shape, F32)
        r = lax.broadcasted_iota(I32, (tm, tm), 0)
        c = lax.broadcasted_iota(I32, (tm, tm), 1)
        tri_ref[...] = jnp.where(r <= c, 1.0, 0.0).astype(BF16)

    x = x_ref[...]
    hb = _rms(x, gx_ref[...]).astype(BF16)
    q = _dot(hb, wq_ref[...].astype(BF16))
    scale = HEAD_DIM ** -0.5
    outs = []
    for h in range(q.shape[1] // HEAD_DIM):
        sl = slice(h * HEAD_DIM, (h + 1) * HEAD_DIM)
        qh = (_rms(q[:, sl], gq_ref[...]) * scale).astype(BF16)
        s = _dot_nt(qh, kn_ref[:, sl])
        p = jnp.exp(s - jnp.max(s, axis=-1, keepdims=True))
        oh = _dot(p.astype(BF16), v_ref[:, sl]) / jnp.sum(p, axis=-1, keepdims=True)
        outs.append(oh.astype(BF16))
    x2 = x + _dot(jnp.concatenate(outs, axis=-1), wo_ref[...].astype(BF16))
    x2_ref[...] = x2
    hm = _rms(x2, gm_ref[...]).astype(BF16)
    hm_ref[...] = hm
    lg = _dot_nt(wr_ref[...], hm) + br_ref[...]
    n_exp = lg.shape[0]
    eidx = lax.broadcasted_iota(I32, lg.shape, 0).astype(F32)
    vals, sel = [], []
    for r in range(TOP_K):
        mx = jnp.max(lg, axis=0, keepdims=True)
        am = jnp.min(jnp.where(lg == mx, eidx, float(n_exp)), axis=0, keepdims=True)
        ti_ref[r:r + 1, :] = am.astype(I32)
        vals.append(mx)
        sel.append(eidx == am)
        lg = jnp.where(sel[r], -jnp.inf, lg)
    ex = [jnp.exp(v - vals[0]) for v in vals]
    den = ex[0]
    for e in ex[1:]:
        den = den + e
    for r in range(TOP_K):
        tg_ref[r:r + 1, :] = ex[r] / den
    onehot = jnp.zeros(lg.shape, F32)
    for r in range(TOP_K):
        onehot = onehot + jnp.where(sel[r], 1.0, 0.0)
    incl = _dot(onehot.astype(BF16), tri_ref[...])
    excl = incl - onehot + run_ref[...]
    for r in range(TOP_K):
        pos_ref[r:r + 1, :] = jnp.sum(jnp.where(sel[r], excl, 0.0), axis=0,
                                      keepdims=True).astype(I32)
    run = run_ref[...] + incl[:, tm - 1:tm]
    run_ref[...] = run
    cnt_ref[...] = jnp.broadcast_to(run, cnt_ref.shape)


def _xattn(x1, g_xattn, w_xq, g_xq, kn, vm, w_xo, g_moe, w_rt, b_r, seq, n_mem):
    t, d = x1.shape
    xw = w_xq.shape[1]
    n_exp = w_rt.shape[0]
    tm = _tile(seq, 512)
    per = seq // tm
    const = lambda i: (0, 0)
    return pl.pallas_call(
        _xattn_kernel,
        grid=(t // tm,),
        in_specs=[pl.BlockSpec((tm, d), lambda i: (i, 0)),
                  pl.BlockSpec((1, d), const),
                  pl.BlockSpec((d, xw), const),
                  pl.BlockSpec((1, HEAD_DIM), const),
                  pl.BlockSpec((n_mem, xw), lambda i: (i // per, 0)),
                  pl.BlockSpec((n_mem, xw), lambda i: (i // per, 0)),
                  pl.BlockSpec((xw, d), const),
                  pl.BlockSpec((1, d), const),
                  pl.BlockSpec((n_exp, d), const),
                  pl.BlockSpec((n_exp, 1), const)],
        out_specs=[pl.BlockSpec((tm, d), lambda i: (i, 0)),
                   pl.BlockSpec((tm, d), lambda i: (i, 0)),
                   pl.BlockSpec((TOP_K, tm), lambda i: (0, i)),
                   pl.BlockSpec((TOP_K, tm), lambda i: (0, i)),
                   pl.BlockSpec((TOP_K, tm), lambda i: (0, i)),
                   pl.BlockSpec((n_exp, 128), const)],
        out_shape=[SDS((t, d), F32), SDS((t, d), BF16),
                   SDS((TOP_K, t), I32), SDS((TOP_K, t), F32), SDS((TOP_K, t), I32),
                   SDS((n_exp, 128), F32)],
        scratch_shapes=[pltpu.VMEM((n_exp, 1), F32), pltpu.VMEM((tm, tm), BF16)],
        compiler_params=_cparams(1), name="xattn_router")(
            x1, g_xattn, w_xq, g_xq, kn, vm, w_xo, g_moe, w_rt, b_r)


def _take(values, idx):
    pick = idx[:, None] == jnp.arange(values.shape[0], dtype=I32)[None, :]
    return jnp.sum(jnp.where(pick, values[None, :], 0), axis=1).astype(I32)


def _route_tables(counts, n_assign, tm):
    n_exp = counts.shape[0]
    ts = 2 * tm
    valid = (counts + tm - 1) // tm * tm
    seg = ((counts + ts - 1) // ts * ts).astype(I32)
    seg_end = jnp.cumsum(seg).astype(I32)
    seg_start = seg_end - seg
    n_steps = n_assign // ts + n_exp
    n_used = seg_end[-1] // ts
    idx = jnp.arange(n_steps, dtype=I32)
    row0 = jnp.minimum(idx, n_used - 1) * ts
    exp = jnp.sum((row0[:, None] >= seg_end[None, :]).astype(I32), axis=1)
    exp = jnp.minimum(exp, n_exp - 1)
    n_valid = jnp.clip((_take(seg_start + valid, exp) - row0) // tm, 1, 2).astype(I32)
    prev = jnp.concatenate([jnp.full((1,), -1, I32), exp[:-1]])
    first = (exp != prev).astype(I32)
    rem = _take(seg_end, exp) // ts - idx
    cand = jnp.where(counts > 0, jnp.arange(n_exp, dtype=I32), n_exp)
    after = jnp.concatenate([lax.cummin(cand, reverse=True)[1:], jnp.full((1,), n_exp, I32)])
    nxt = _take(jnp.where(after >= n_exp, -1, after), exp)
    tables = (exp, first, rem.astype(I32), nxt, n_valid)
    return seg_start, seg_end, seg, tables, n_used.reshape(1).astype(I32)


def _block_tables(tables, n_used):
    exp, first, rem, nxt, n_valid = tables
    half = jnp.tile(jnp.arange(2, dtype=I32), exp.shape[0])
    rep = lambda v: jnp.repeat(v, 2)
    return (rep(exp), rep(first) * (1 - half), 2 * rep(rem) - half, rep(nxt),
            (half < rep(n_valid)).astype(I32)), 2 * n_used


def _dest_kernel(ps_ref, ti_ref, pos_ref, o_ref, *, n_exp):
    ti = ti_ref[...]
    acc = pos_ref[...]
    for e in range(n_exp):
        acc = acc + jnp.where(ti == e, ps_ref[e], 0)
    o_ref[...] = acc


def _dest_rows(pad_start, ti, pos):
    k, t = ti.shape
    kern = functools.partial(_dest_kernel, n_exp=pad_start.shape[0])
    grid_spec = pltpu.PrefetchScalarGridSpec(
        num_scalar_prefetch=1, grid=(1,),
        in_specs=[pl.BlockSpec((k, t), lambda i, ps: (0, 0)),
                  pl.BlockSpec((k, t), lambda i, ps: (0, 0))],
        out_specs=pl.BlockSpec((k, t), lambda i, ps: (0, 0)))
    return pl.pallas_call(kern, grid_spec=grid_spec, out_shape=SDS((k, t), I32),
                          compiler_params=_cparams(1), name="moe_dest")(pad_start, ti, pos)


def _dispatch_kernel(pe_ref, pc_ref, nu_ref, dst_ref, x_ref, xs_ref, buf, zbuf, sem, zsem,
                     *, tc, ts, n_exp, n_steps):
    i = pl.program_id(0)
    n = pl.num_programs(0)
    ns = x_ref.shape[1] // (2 * LANES)

    def zero_copy(row0):
        rows = pl.ds(pl.multiple_of(row0 * ns, ts * ns), ts * ns)
        return pltpu.make_async_copy(zbuf, xs_ref.at[rows], zsem)

    @pl.when(i == 0)
    def _():
        zbuf[...] = jnp.zeros(zbuf.shape, zbuf.dtype)
        for start in (True, False):
            for e in range(n_exp):
                @pl.when(pc_ref[e] > 0)
                def _(e=e):
                    cp = zero_copy(pe_ref[e] - ts)
                    cp.start() if start else cp.wait()

            def tail(b, c):
                cp = zero_copy(b * ts)
                cp.start() if start else cp.wait()
                return c
            lax.fori_loop(nu_ref[0], n_steps, tail, 0)

    slot = i % 2
    x = x_ref[...]
    for s in range(ns):
        lo = x[:, (2 * s) * LANES:(2 * s + 1) * LANES]
        hi = x[:, (2 * s + 1) * LANES:(2 * s + 2) * LANES]
        buf[slot, pl.ds(s, tc, stride=ns), :] = _pack_pair(lo, hi)
    for k in range(TOP_K):
        for r in range(tc):
            dst = pl.ds(pl.multiple_of(dst_ref[0, k, r] * ns, ns), ns)
            pltpu.make_async_copy(buf.at[slot, pl.ds(r * ns, ns)], xs_ref.at[dst],
                                  sem.at[slot]).start(priority=r % 2)

    def wait_slot(s):
        for k in range(TOP_K):
            pltpu.make_async_copy(buf.at[s], xs_ref.at[pl.ds(0, tc * ns)], sem.at[s]).wait()

    @pl.when(i >= 1)
    def _():
        wait_slot(1 - slot)

    @pl.when(i == n - 1)
    def _():
        wait_slot(slot)


def _dispatch(hm, dest, seg_end, seg, n_used, ts, n_steps):
    t, d = hm.shape
    tc = _tile(t, 128)
    n = t // tc
    ns = d // (2 * LANES)
    n_exp = seg_end.shape[0]
    dest3 = dest.reshape(TOP_K, n, tc).transpose(1, 0, 2)
    kern = functools.partial(_dispatch_kernel, tc=tc, ts=ts, n_exp=n_exp, n_steps=n_steps)
    grid_spec = pltpu.PrefetchScalarGridSpec(
        num_scalar_prefetch=3, grid=(n,),
        in_specs=[pl.BlockSpec((1, TOP_K, tc), lambda i, *_: (i, 0, 0), memory_space=pltpu.SMEM),
                  pl.BlockSpec((tc, d), lambda i, *_: (i, 0))],
        out_specs=pl.BlockSpec(memory_space=pl.ANY),
        scratch_shapes=[pltpu.VMEM((2, tc * ns, LANES), U32),
                        pltpu.VMEM((ts * ns, LANES), U32),
                        pltpu.SemaphoreType.DMA((2,)),
                        pltpu.SemaphoreType.DMA(())])
    return pl.pallas_call(
        kern, grid_spec=grid_spec,
        out_shape=SDS((n_steps * ts * ns, LANES), U32),
        compiler_params=_cparams(1), name="moe_dispatch")(seg_end, seg, n_used, dest3, hm)


W_CHUNK = 512


def _stream_weights(i, be_ref, first_ref, rem_ref, nxt_ref, w_hbm, wb_ref, stage_ref, st_ref, sem):
    n_chunks, _, chunk = wb_ref.shape[1:]

    def chunk_copy(e, c, slot):
        cols = pl.ds(pl.multiple_of(c * chunk, chunk), chunk)
        return pltpu.make_async_copy(w_hbm.at[e, :, cols], stage_ref.at[slot], sem.at[slot])

    def start_first_two(e):
        chunk_copy(e, 0, 0).start()
        chunk_copy(e, 1, 1).start()

    def cast_chunks(e, dst, done, issued, k):
        def body(_, carry):
            done, issued = carry
            slot = done % 2
            chunk_copy(e, done, slot).wait()
            wb_ref[dst, done] = stage_ref[slot].astype(BF16)

            @pl.when(issued < n_chunks)
            def _():
                chunk_copy(e, issued, slot).start()
            return done + 1, jnp.minimum(issued + 1, n_chunks)
        return lax.fori_loop(0, k, body, (done, issued))

    @pl.when(i == 0)
    def _():
        start_first_two(be_ref[0])
        cast_chunks(be_ref[0], 0, 0, 2, n_chunks)
        st_ref[0] = 1

    @pl.when(first_ref[i] == 1)
    def _():
        st_ref[0] = 1 - st_ref[0]
        st_ref[1] = 0
        st_ref[2] = 0

        @pl.when(nxt_ref[i] >= 0)
        def _():
            start_first_two(nxt_ref[i])
            st_ref[2] = 2

    cur = st_ref[0]

    @pl.when(nxt_ref[i] >= 0)
    def _():
        done = st_ref[1]
        rem = rem_ref[i]
        lead = (first_ref[i] == 1) & (rem > 1)
        left = jnp.where(first_ref[i] == 1, jnp.maximum(rem - 1, 1), rem)
        k = jnp.where(lead, 0, lax.div(n_chunks - done + left - 1, left))
        done, issued = cast_chunks(nxt_ref[i], 1 - cur, done, st_ref[2], k)
        st_ref[1] = done
        st_ref[2] = issued

    return cur


def _moe_up_kernel(be_ref, first_ref, rem_ref, nxt_ref, ok_ref, nu_ref, x_ref, b_ref, w_hbm, o_ref,
                   wb_ref, stage_ref, st_ref, sem):
    i = pl.program_id(0)
    tm = o_ref.shape[0]
    ns = x_ref.shape[0] // tm
    n_half = wb_ref.shape[1] // 2
    chunk = wb_ref.shape[3]
    f = n_half * chunk

    @pl.when(i < nu_ref[0])
    def _():
        cur = _stream_weights(i, be_ref, first_ref, rem_ref, nxt_ref, w_hbm, wb_ref, stage_ref,
                              st_ref, sem)

        @pl.when(ok_ref[i] == 1)
        def _():
            cols = []
            for s in range(ns):
                lo, hi = _unpack_pair(x_ref[pl.ds(s, tm, stride=ns), :])
                cols += [lo.astype(BF16), hi.astype(BF16)]
            x = jnp.concatenate(cols, axis=1)
            for n in range(n_half):
                lo, hi = n * chunk, (n + 1) * chunk
                gate = jnp.minimum(_dot(x, wb_ref[cur, n]) + b_ref[:, lo:hi], SWIGLU_LIMIT)
                up = jnp.clip(_dot(x, wb_ref[cur, n_half + n]) + b_ref[:, f + lo:f + hi],
                              -SWIGLU_LIMIT, SWIGLU_LIMIT)
                glu = gate * jax.nn.sigmoid(SWIGLU_ALPHA * gate)
                o_ref[:, lo:hi] = ((up + 1.0) * glu).astype(o_ref.dtype)

    @pl.when((i >= nu_ref[0]) | (ok_ref[i] == 0))
    def _():
        o_ref[...] = jnp.zeros(o_ref.shape, o_ref.dtype)


def _moe_up(xs, w_gu, b_gu, tables, n_used, tm):
    d = w_gu.shape[1]
    ns = d // (2 * LANES)
    n_rows = xs.shape[0] // ns
    two_f = w_gu.shape[2]
    chunk = min(W_CHUNK, two_f // 2)
    grid_spec = pltpu.PrefetchScalarGridSpec(
        num_scalar_prefetch=6,
        grid=(n_rows // tm,),
        in_specs=[pl.BlockSpec((tm * ns, LANES),
                               lambda i, be, fi, re, nx, ok, nu: (jnp.minimum(i, nu[0] - 1), 0)),
                  pl.BlockSpec((None, 1, two_f), lambda i, be, fi, re, nx, ok, nu: (be[i], 0, 0)),
                  pl.BlockSpec(memory_space=pl.ANY)],
        out_specs=pl.BlockSpec((tm, two_f // 2), lambda i, *_: (i, 0)),
        scratch_shapes=[pltpu.VMEM((2, two_f // chunk, d, chunk), BF16),
                        pltpu.VMEM((2, d, chunk), w_gu.dtype),
                        pltpu.SMEM((3,), I32),
                        pltpu.SemaphoreType.DMA((2,))])
    return pl.pallas_call(
        _moe_up_kernel, grid_spec=grid_spec,
        out_shape=SDS((n_rows, two_f // 2), BF16),
        compiler_params=_cparams(1), name="moe_up")(*tables, n_used, xs, b_gu, w_gu)


def _moe_down_kernel(be_ref, first_ref, rem_ref, nxt_ref, nv_ref, nu_ref, a_ref, b_ref, w_hbm, o_ref,
                     wf_ref, st_ref, sem):
    i = pl.program_id(0)
    ts = a_ref.shape[0]
    ns = o_ref.shape[0] // ts
    n_chunks, _, chunk = wf_ref.shape[1:]

    def chunk_copy(e, dst, c):
        return pltpu.make_async_copy(w_hbm.at[e, :, pl.ds(c * chunk, chunk)], wf_ref.at[dst, c], sem)

    def compute(cur, rows):
        a = a_ref[:rows, :]
        for n in range(n_chunks):
            lo, hi = n * chunk, (n + 1) * chunk
            y = _dot(a, wf_ref[cur, n].astype(BF16)) + b_ref[:, lo:hi]
            for j in range(chunk // (2 * LANES)):
                s = lo // (2 * LANES) + j
                w = _pack_pair(y[:, (2 * j) * LANES:(2 * j + 1) * LANES],
                               y[:, (2 * j + 1) * LANES:(2 * j + 2) * LANES])
                o_ref[pl.ds(s, rows, stride=ns), :] = w

    @pl.when(i < nu_ref[0])
    def _():
        @pl.when(i == 0)
        def _():
            for c in range(n_chunks):
                chunk_copy(be_ref[0], 0, c).start()
            st_ref[0] = 1
            st_ref[1] = 1

        @pl.when(first_ref[i] == 1)
        def _():
            @pl.when(st_ref[1] == 1)
            def _():
                for c in range(n_chunks):
                    chunk_copy(be_ref[i], 1 - st_ref[0], c).wait()
            st_ref[0] = 1 - st_ref[0]
            st_ref[1] = 0

            @pl.when(nxt_ref[i] >= 0)
            def _():
                for c in range(n_chunks):
                    chunk_copy(nxt_ref[i], 1 - st_ref[0], c).start()
                st_ref[1] = 1

        cur = st_ref[0]

        @pl.when(nv_ref[i] == 2)
        def _():
            compute(cur, ts)

        @pl.when(nv_ref[i] == 1)
        def _():
            compute(cur, ts // 2)
            half = ts // 2 * ns
            o_ref[half:, :] = jnp.zeros((half, LANES), o_ref.dtype)

    @pl.when(i >= nu_ref[0])
    def _():
        o_ref[...] = jnp.zeros(o_ref.shape, o_ref.dtype)


def _moe_down(act, w_dn, b_dn, tables, n_used, ts):
    n_rows, f = act.shape
    d = w_dn.shape[2]
    ns = d // (2 * LANES)
    chunk = max(min(W_CHUNK, d // 2), 2 * LANES)
    grid_spec = pltpu.PrefetchScalarGridSpec(
        num_scalar_prefetch=6,
        grid=(n_rows // ts,),
        in_specs=[pl.BlockSpec((ts, f),
                               lambda i, be, fi, re, nx, nv, nu: (jnp.minimum(i, nu[0] - 1), 0)),
                  pl.BlockSpec((None, 1, d), lambda i, be, fi, re, nx, nv, nu: (be[i], 0, 0)),
                  pl.BlockSpec(memory_space=pl.ANY)],
        out_specs=pl.BlockSpec((ts * ns, LANES), lambda i, *_: (i, 0)),
        scratch_shapes=[pltpu.VMEM((2, d // chunk, f, chunk), w_dn.dtype),
                        pltpu.SMEM((2,), I32),
                        pltpu.SemaphoreType.DMA(())])
    return pl.pallas_call(
        _moe_down_kernel, grid_spec=grid_spec,
        out_shape=SDS((n_rows * ns, LANES), U32),
        compiler_params=_cparams(1), name="moe_down")(*tables, n_used, act, b_dn, w_dn)


def _combine_kernel(dst_ref, x_ref, g_ref, ys_ref, o_ref, buf, sem, *, tc):
    i = pl.program_id(0)
    n = pl.num_programs(0) - 1
    ns = x_ref.shape[1] // (2 * LANES)

    @pl.when(i < n)
    def _():
        slot = i % 2
        for k in range(TOP_K):
            for r in range(tc):
                src = pl.ds(pl.multiple_of(dst_ref[0, k, r] * ns, ns), ns)
                pltpu.make_async_copy(ys_ref.at[src], buf.at[slot, k, pl.ds(r * ns, ns)],
                                      sem.at[slot]).start(priority=r % 2)

    @pl.when(i >= 1)
    def _():
        slot = (i - 1) % 2
        for k in range(TOP_K):
            pltpu.make_async_copy(ys_ref.at[pl.ds(0, tc * ns)], buf.at[slot, k],
                                  sem.at[slot]).wait()
        g = g_ref[...]
        g = jnp.concatenate([g, jnp.zeros((SUBLANES - TOP_K, tc), F32)], axis=0).T
        for s in range(ns):
            c0, c1, c2 = (2 * s) * LANES, (2 * s + 1) * LANES, (2 * s + 2) * LANES
            acc_lo = x_ref[:, c0:c1]
            acc_hi = x_ref[:, c1:c2]
            for k in range(TOP_K):
                lo, hi = _unpack_pair(buf[slot, k, pl.ds(s, tc, stride=ns), :])
                acc_lo = acc_lo + g[:, k:k + 1] * lo
                acc_hi = acc_hi + g[:, k:k + 1] * hi
            o_ref[:, c0:c1] = acc_lo
            o_ref[:, c1:c2] = acc_hi


def _combine(x2, gates, dest, ys):
    t, d = x2.shape
    tc = _tile(t, 128)
    n = t // tc
    ns = d // (2 * LANES)
    dest3 = dest.reshape(TOP_K, n, tc).transpose(1, 0, 2)
    kern = functools.partial(_combine_kernel, tc=tc)
    prev = lambda i: (jnp.maximum(i - 1, 0), 0)
    return pl.pallas_call(
        kern,
        grid=(n + 1,),
        in_specs=[pl.BlockSpec((1, TOP_K, tc), lambda i: (jnp.minimum(i, n - 1), 0, 0),
                               memory_space=pltpu.SMEM),
                  pl.BlockSpec((tc, d), prev),
                  pl.BlockSpec((TOP_K, tc), lambda i: (0, jnp.maximum(i - 1, 0))),
                  pl.BlockSpec(memory_space=pl.ANY)],
        out_specs=pl.BlockSpec((tc, d), prev),
        out_shape=SDS((t, d), F32),
        scratch_shapes=[pltpu.VMEM((2, TOP_K, tc * ns, LANES), U32),
                        pltpu.SemaphoreType.DMA((2,))],
        compiler_params=_cparams(1), name="moe_combine")(dest3, x2, gates, ys)


def _layer(x2d, mem2d, batch, seq, n_mem, p):
    t, d = x2d.shape
    c = p["conv_w"].shape[1]
    fw = p["g_fox_out"].shape[0]
    n_heads = fw // HEAD_DIM
    n_main = 3 * c + 3 * fw
    n_exp = p["w_router"].shape[1]
    assert n_heads <= F_ROWS and TOP_K <= SUBLANES and d % (2 * LANES) == 0

    w_in = p["w_mix_in"].astype(BF16)
    wf_t = jnp.zeros((F_ROWS, d), BF16).at[:n_heads].set(w_in[:, n_main:].T)
    b_col = jnp.zeros((F_ROWS, 1), F32).at[:n_heads, 0].set(p["b_forget"])

    proj, ft = _mix_in(x2d, p["g_mix"][None, :], wf_t, w_in, n_main)
    cdec = _decay(ft, b_col, seq)
    y_fox = _fox_attention(proj, cdec, p["g_q"][None, :], p["g_k"][None, :],
                           batch, seq, n_heads, 3 * c, fw)
    x1 = _mix_out(proj, y_fox, p["conv_w"], p["g_conv_out"][None, :], p["g_fox_out"][None, :],
                  p["w_mix_out"], x2d, batch, seq)

    kn, vm = _memkv(mem2d, p["g_mem"][None, :], p["w_xkv"],
                    p["g_xk"][None, :], n_mem)
    x2, hm, ti, tg, pos, cnt = _xattn(x1, p["g_xattn"][None, :], p["w_xq"],
                            p["g_xq"][None, :], kn, vm, p["w_xo"],
                            p["g_moe"][None, :], p["w_router"].T.astype(BF16),
                            p["b_router"][:, None], seq, n_mem)

    tm = _tile(TOP_K * t, 512) // 2
    counts = cnt[:, 0].astype(I32)
    seg_start, seg_end, seg, tables, n_used = _route_tables(counts, TOP_K * t, tm)
    dest = _dest_rows(seg_start, ti, pos)
    xs = _dispatch(hm, dest, seg_end, seg, n_used, 2 * tm, tables[0].shape[0])
    act = _moe_up(xs, p["w_gate_up"], p["b_gate_up"][:, None, :], *_block_tables(tables, n_used), tm)
    ys = _moe_down(act, p["w_down"], p["b_down"][:, None, :], tables, n_used, 2 * tm)
    return _combine(x2, tg, dest, ys)


def kernel(x, mem, g_mix, w_mix_in, b_forget, conv_w, g_q, g_k, g_conv_out, g_fox_out, w_mix_out, g_xattn, g_mem, w_xq, w_xkv, g_xq, g_xk, w_xo, g_moe, w_router, b_router, w_gate_up, b_gate_up, w_down, b_down):
    batch, seq, d = x.shape
    n_mem = mem.shape[1]
    params = dict(g_mix=g_mix, w_mix_in=w_mix_in, b_forget=b_forget, conv_w=conv_w, g_q=g_q,
                  g_k=g_k, g_conv_out=g_conv_out, g_fox_out=g_fox_out, w_mix_out=w_mix_out,
                  g_xattn=g_xattn, g_mem=g_mem, w_xq=w_xq, w_xkv=w_xkv, g_xq=g_xq, g_xk=g_xk,
                  w_xo=w_xo, g_moe=g_moe, w_router=w_router, b_router=b_router,
                  w_gate_up=w_gate_up, b_gate_up=b_gate_up, w_down=w_down, b_down=b_down)
    x2d = x.reshape(batch * seq, d)
    mem2d = mem.reshape(batch * n_mem, d)
    for l in range(g_mix.shape[0]):
        x2d = _layer(x2d, mem2d, batch, seq, n_mem, {k: v[l] for k, v in params.items()})
    return x2d.reshape(batch, seq, d)
```

```python
import functools

import jax
import jax.numpy as jnp
from jax import lax
from jax.experimental import pallas as pl
from jax.experimental.pallas import tpu as pltpu

F32 = jnp.float32
BF16 = jnp.bfloat16
I32 = jnp.int32
U32 = jnp.uint32
SDS = jax.ShapeDtypeStruct

EPS = 1e-6
HEAD_DIM = 128
TOP_K = 4
SWIGLU_LIMIT = 7.0
SWIGLU_ALPHA = 1.702
MASK_VALUE = float(jnp.finfo(jnp.float32).min)
V7X_VMEM_LIMIT_BYTES = 56 * 1024 * 1024
F_ROWS = 16
LANES = 128
SUBLANES = 8


def _cparams(n_axes):
    return pltpu.CompilerParams(
        dimension_semantics=("arbitrary",) * n_axes,
        vmem_limit_bytes=V7X_VMEM_LIMIT_BYTES)


def _rms(x, g):
    return x * lax.rsqrt(jnp.mean(x * x, axis=-1, keepdims=True) + EPS) * g


def _dot(a, b):
    return jnp.dot(a, b, preferred_element_type=F32)


def _dot_nt(a, b):
    return lax.dot_general(a, b, (((1,), (1,)), ((), ())), preferred_element_type=F32)


def _tile(n, pref):
    return pref if n % pref == 0 else n


def _pack_pair(lo, hi):
    lo = lax.bitcast_convert_type(lo.astype(BF16).astype(F32), U32) >> 16
    hi = lax.bitcast_convert_type(hi.astype(BF16).astype(F32), U32) & jnp.uint32(0xFFFF0000)
    return hi | lo


def _unpack_pair(w):
    lo = lax.bitcast_convert_type(w << 16, F32)
    hi = lax.bitcast_convert_type(w & jnp.uint32(0xFFFF0000), F32)
    return lo, hi


def _mix_in_kernel(x_ref, g_ref, wf_ref, w_ref, o_ref, ft_ref, h_ref):
    @pl.when(pl.program_id(1) == 0)
    def _():
        hb = _rms(x_ref[...], g_ref[...]).astype(BF16)
        h_ref[...] = hb
        ft_ref[...] = _dot_nt(wf_ref[...], hb)

    o_ref[...] = _dot(h_ref[...], w_ref[...]).astype(o_ref.dtype)


def _mix_in(x2d, g, wf_t, w_in, n_out):
    t, d = x2d.shape
    tm = _tile(t, 1024)
    tn = _tile(n_out, 1536)
    return pl.pallas_call(
        _mix_in_kernel,
        grid=(t // tm, n_out // tn),
        in_specs=[pl.BlockSpec((tm, d), lambda i, j: (i, 0)),
                  pl.BlockSpec((1, d), lambda i, j: (0, 0)),
                  pl.BlockSpec((F_ROWS, d), lambda i, j: (0, 0)),
                  pl.BlockSpec((d, tn), lambda i, j: (0, j))],
        out_specs=[pl.BlockSpec((tm, tn), lambda i, j: (i, j)),
                   pl.BlockSpec((F_ROWS, tm), lambda i, j: (0, i))],
        out_shape=[SDS((t, n_out), BF16), SDS((F_ROWS, t), F32)],
        scratch_shapes=[pltpu.VMEM((tm, d), BF16)],
        compiler_params=_cparams(2), name="mix_in")(x2d, g, wf_t, w_in)


def _decay_kernel(ft_ref, b_ref, c_ref):
    z = ft_ref[...] + b_ref[...]
    lf = jnp.minimum(z, 0.0) - jnp.log(1.0 + jnp.exp(-jnp.abs(z)))
    s = lf.shape[1]
    lane = lax.broadcasted_iota(I32, lf.shape, 1)
    d = 1
    while d < s:
        lf = lf + jnp.where(lane >= d, pltpu.roll(lf, d, 1), 0.0)
        d *= 2
    c_ref[...] = lf


def _decay(ft, b_col, seq):
    rows, t = ft.shape
    return pl.pallas_call(
        _decay_kernel,
        grid=(t // seq,),
        in_specs=[pl.BlockSpec((rows, seq), lambda b: (0, b)),
                  pl.BlockSpec((rows, 1), lambda b: (0, 0))],
        out_specs=pl.BlockSpec((rows, seq), lambda b: (0, b)),
        out_shape=SDS((rows, t), F32),
        compiler_params=_cparams(1), name="decay")(ft, b_col)


def _fox_kernel(q_ref, k_ref, v_ref, c_ref, gq_ref, gk_ref, o_ref, kn_ref, nc_ref, *, tq, hpb):
    g = pl.program_id(1)
    seq = q_ref.shape[0]
    hd = HEAD_DIM
    scale = HEAD_DIM ** -0.5
    for j in range(hpb):
        cs = slice(j * hd, (j + 1) * hd)
        kn_ref[:, cs] = _rms(k_ref[:, cs].astype(F32), gk_ref[...]).astype(BF16)
        nc_ref[j:j + 1, :] = -c_ref[pl.ds(g * hpb + j, 1), :]
    row = lax.broadcasted_iota(I32, (tq, tq), 0)
    col = lax.broadcasted_iota(I32, (tq, tq), 1)
    causal = col <= row
    for qi in range(seq // tq):
        lo, hi = qi * tq, (qi + 1) * tq
        for j in range(hpb):
            cs = slice(j * hd, (j + 1) * hd)
            q = (_rms(q_ref[lo:hi, cs].astype(F32), gq_ref[...]) * scale).astype(BF16)
            s_diag = jnp.where(causal, _dot_nt(q, kn_ref[lo:hi, cs]) + nc_ref[j:j + 1, lo:hi],
                               MASK_VALUE)
            m = jnp.max(s_diag, axis=-1, keepdims=True)
            if qi > 0:
                s_past = _dot_nt(q, kn_ref[:lo, cs]) + nc_ref[j:j + 1, :lo]
                m = jnp.maximum(m, jnp.max(s_past, axis=-1, keepdims=True))
            p_diag = jnp.exp(s_diag - m)
            l = jnp.sum(p_diag, axis=-1, keepdims=True)
            acc = _dot(p_diag.astype(BF16), v_ref[lo:hi, cs])
            if qi > 0:
                p_past = jnp.exp(s_past - m)
                l = l + jnp.sum(p_past, axis=-1, keepdims=True)
                acc = acc + _dot(p_past.astype(BF16), v_ref[:lo, cs])
            o_ref[lo:hi, cs] = (acc / l).astype(o_ref.dtype)


def _fox_attention(proj, c, g_q, g_k, batch, seq, n_heads, col0, width):
    t = proj.shape[0]
    tq = _tile(seq, 256)
    hpb = 2 if n_heads % 2 == 0 else 1
    bw = hpb * HEAD_DIM
    cb = col0 // bw
    wb = width // bw
    kern = functools.partial(_fox_kernel, tq=tq, hpb=hpb)
    return pl.pallas_call(
        kern,
        grid=(batch, n_heads // hpb),
        in_specs=[pl.BlockSpec((seq, bw), lambda b, h: (b, cb + h)),
                  pl.BlockSpec((seq, bw), lambda b, h: (b, cb + wb + h)),
                  pl.BlockSpec((seq, bw), lambda b, h: (b, cb + 2 * wb + h)),
                  pl.BlockSpec((F_ROWS, seq), lambda b, h: (0, b)),
                  pl.BlockSpec((1, HEAD_DIM), lambda b, h: (0, 0)),
                  pl.BlockSpec((1, HEAD_DIM), lambda b, h: (0, 0))],
        out_specs=pl.BlockSpec((seq, bw), lambda b, h: (b, h)),
        out_shape=SDS((t, width), BF16),
        scratch_shapes=[pltpu.VMEM((seq, bw), BF16), pltpu.VMEM((SUBLANES, seq), F32)],
        compiler_params=_cparams(2), name="fox_attention")(proj, proj, proj, c, g_q, g_k)


def _mix_out_kernel(ub_ref, uc_ref, ux_ref, hc_ref, hx_ref, yf_ref, cw_ref, gc_ref, gf_ref,
                    w_ref, x_ref, o_ref):
    si = pl.program_id(1)
    p = uc_ref[...].astype(F32) * ux_ref[...].astype(F32)
    halo = hc_ref[...].astype(F32) * hx_ref[...].astype(F32)
    halo = jnp.where(si > 0, halo, 0.0)
    row = lax.broadcasted_iota(I32, p.shape, 0)
    last, last2 = halo[SUBLANES - 1:SUBLANES, :], halo[SUBLANES - 2:SUBLANES - 1, :]
    p1 = jnp.where(row == 0, last, pltpu.roll(p, 1, 0))
    p2 = jnp.where(row == 0, last2, jnp.where(row == 1, last, pltpu.roll(p, 2, 0)))
    w = cw_ref[...]
    y = ub_ref[...].astype(F32) * (w[0:1, :] * p2 + w[1:2, :] * p1 + w[2:3, :] * p)
    c = y.shape[1]
    y_conv = _rms(y, gc_ref[...]).astype(BF16)
    y_fox = _rms(yf_ref[...].astype(F32), gf_ref[...]).astype(BF16)
    o_ref[...] = (x_ref[...] + _dot(y_conv, w_ref[:c, :].astype(BF16))
                  + _dot(y_fox, w_ref[c:, :].astype(BF16)))


def _mix_out(proj, y_fox, conv_w, g_conv, g_fox, w_out, x2d, batch, seq):
    t, d = x2d.shape
    c = conv_w.shape[1]
    w = y_fox.shape[1]
    ts = _tile(seq, 512)
    ns = seq // ts
    hb = ts // SUBLANES
    tile = lambda col: (lambda b, s: (b * ns + s, col))
    halo = lambda col: (lambda b, s: (jnp.maximum((b * ns + s) * hb - 1, 0), col))
    const = lambda b, s: (0, 0)
    return pl.pallas_call(
        _mix_out_kernel,
        grid=(batch, ns),
        in_specs=[pl.BlockSpec((ts, c), tile(0)),
                  pl.BlockSpec((ts, c), tile(1)),
                  pl.BlockSpec((ts, c), tile(2)),
                  pl.BlockSpec((SUBLANES, c), halo(1)),
                  pl.BlockSpec((SUBLANES, c), halo(2)),
                  pl.BlockSpec((ts, w), tile(0)),
                  pl.BlockSpec((3, c), const),
                  pl.BlockSpec((1, c), const),
                  pl.BlockSpec((1, w), const),
                  pl.BlockSpec((c + w, d), const, pipeline_mode=pl.Buffered(1)),
                  pl.BlockSpec((ts, d), tile(0))],
        out_specs=pl.BlockSpec((ts, d), tile(0)),
        out_shape=SDS((t, d), F32),
        compiler_params=_cparams(2), name="mix_out")(
            proj, proj, proj, proj, proj, y_fox, conv_w, g_conv, g_fox, w_out, x2d)


def _memkv_kernel(mem_ref, g_ref, w_ref, gk_ref, k_ref, v_ref):
    hm = _rms(mem_ref[...], g_ref[...]).astype(BF16)
    kv = _dot(hm, w_ref[...].astype(BF16))
    xw = k_ref.shape[1]
    for h in range(xw // HEAD_DIM):
        sl = slice(h * HEAD_DIM, (h + 1) * HEAD_DIM)
        k_ref[:, sl] = _rms(kv[:, sl], gk_ref[...]).astype(BF16)
    v_ref[...] = kv[:, xw:].astype(BF16)


def _memkv(mem2d, g_mem, w_xkv, g_xk, n_mem):
    tm_, d = mem2d.shape
    xw = w_xkv.shape[1] // 2
    return pl.pallas_call(
        _memkv_kernel,
        grid=(tm_ // n_mem,),
        in_specs=[pl.BlockSpec((n_mem, d), lambda b: (b, 0)),
                  pl.BlockSpec((1, d), lambda b: (0, 0)),
                  pl.BlockSpec((d, 2 * xw), lambda b: (0, 0)),
                  pl.BlockSpec((1, HEAD_DIM), lambda b: (0, 0))],
        out_specs=[pl.BlockSpec((n_mem, xw), lambda b: (b, 0)),
                   pl.BlockSpec((n_mem, xw), lambda b: (b, 0))],
        out_shape=[SDS((tm_, xw), BF16), SDS((tm_, xw), BF16)],
        compiler_params=_cparams(1), name="memkv")(mem2d, g_mem, w_xkv, g_xk)


def _xattn_kernel(x_ref, gx_ref, wq_ref, gq_ref, kn_ref, v_ref, wo_ref, gm_ref, wr_ref, br_ref,
                  x2_ref, hm_ref, ti_ref, tg_ref, pos_ref, cnt_ref, run_ref, tri_ref):
    i = pl.program_id(0)
    tm = x_ref.shape[0]

    @pl.when(i == 0)
    def _():
        run_ref[...] = jnp.zeros(run_ref.shape, F32)
        r = lax.broadcasted_iota(I32, (tm, tm), 0)
        c = lax.broadcasted_iota(I32, (tm, tm), 1)
        tri_ref[...] = jnp.where(r <= c, 1.0, 0.0).astype(BF16)

    x = x_ref[...]
    hb = _rms(x, gx_ref[...]).astype(BF16)
    q = _dot(hb, wq_ref[...].astype(BF16))
    scale = HEAD_DIM ** -0.5
    outs = []
    for h in range(q.shape[1] // HEAD_DIM):
        sl = slice(h * HEAD_DIM, (h + 1) * HEAD_DIM)
        qh = (_rms(q[:, sl], gq_ref[...]) * scale).astype(BF16)
        s = _dot_nt(qh, kn_ref[:, sl])
        p = jnp.exp(s - jnp.max(s, axis=-1, keepdims=True))
        oh = _dot(p.astype(BF16), v_ref[:, sl]) / jnp.sum(p, axis=-1, keepdims=True)
        outs.append(oh.astype(BF16))
    x2 = x + _dot(jnp.concatenate(outs, axis=-1), wo_ref[...].astype(BF16))
    x2_ref[...] = x2
    hm = _rms(x2, gm_ref[...]).astype(BF16)
    hm_ref[...] = hm
    lg = _dot_nt(wr_ref[...], hm) + br_ref[...]
    n_exp = lg.shape[0]
    eidx = lax.broadcasted_iota(I32, lg.shape, 0).astype(F32)
    vals, sel = [], []
    for r in range(TOP_K):
        mx = jnp.max(lg, axis=0, keepdims=True)
        am = jnp.min(jnp.where(lg == mx, eidx, float(n_exp)), axis=0, keepdims=True)
        ti_ref[r:r + 1, :] = am.astype(I32)
        vals.append(mx)
        sel.append(eidx == am)
        lg = jnp.where(sel[r], -jnp.inf, lg)
    ex = [jnp.exp(v - vals[0]) for v in vals]
    den = ex[0]
    for e in ex[1:]:
        den = den + e
    for r in range(TOP_K):
        tg_ref[r:r + 1, :] = ex[r] / den
    onehot = jnp.zeros(lg.shape, F32)
    for r in range(TOP_K):
        onehot = onehot + jnp.where(sel[r], 1.0, 0.0)
    incl = _dot(onehot.astype(BF16), tri_ref[...])
    excl = incl - onehot + run_ref[...]
    for r in range(TOP_K):
        pos_ref[r:r + 1, :] = jnp.sum(jnp.where(sel[r], excl, 0.0), axis=0,
                                      keepdims=True).astype(I32)
    run = run_ref[...] + incl[:, tm - 1:tm]
    run_ref[...] = run
    cnt_ref[...] = jnp.broadcast_to(run, cnt_ref.shape)


def _xattn(x1, g_xattn, w_xq, g_xq, kn, vm, w_xo, g_moe, w_rt, b_r, seq, n_mem):
    t, d = x1.shape
    xw = w_xq.shape[1]
    n_exp = w_rt.shape[0]
    tm = _tile(seq, 512)
    per = seq // tm
    const = lambda i: (0, 0)
    return pl.pallas_call(
        _xattn_kernel,
        grid=(t // tm,),
        in_specs=[pl.BlockSpec((tm, d), lambda i: (i, 0)),
                  pl.BlockSpec((1, d), const),
                  pl.BlockSpec((d, xw), const),
                  pl.BlockSpec((1, HEAD_DIM), const),
                  pl.BlockSpec((n_mem, xw), lambda i: (i // per, 0)),
                  pl.BlockSpec((n_mem, xw), lambda i: (i // per, 0)),
                  pl.BlockSpec((xw, d), const),
                  pl.BlockSpec((1, d), const),
                  pl.BlockSpec((n_exp, d), const),
                  pl.BlockSpec((n_exp, 1), const)],
        out_specs=[pl.BlockSpec((tm, d), lambda i: (i, 0)),
                   pl.BlockSpec((tm, d), lambda i: (i, 0)),
                   pl.BlockSpec((TOP_K, tm), lambda i: (0, i)),
                   pl.BlockSpec((TOP_K, tm), lambda i: (0, i)),
                   pl.BlockSpec((TOP_K, tm), lambda i: (0, i)),
                   pl.BlockSpec((n_exp, 128), const)],
        out_shape=[SDS((t, d), F32), SDS((t, d), BF16),
                   SDS((TOP_K, t), I32), SDS((TOP_K, t), F32), SDS((TOP_K, t), I32),
                   SDS((n_exp, 128), F32)],
        scratch_shapes=[pltpu.VMEM((n_exp, 1), F32), pltpu.VMEM((tm, tm), BF16)],
        compiler_params=_cparams(1), name="xattn_router")(
            x1, g_xattn, w_xq, g_xq, kn, vm, w_xo, g_moe, w_rt, b_r)


def _take(values, idx):
    pick = idx[:, None] == jnp.arange(values.shape[0], dtype=I32)[None, :]
    return jnp.sum(jnp.where(pick, values[None, :], 0), axis=1).astype(I32)


def _route_tables(counts, n_assign, tm):
    n_exp = counts.shape[0]
    ts = 2 * tm
    valid = (counts + tm - 1) // tm * tm
    seg = ((counts + ts - 1) // ts * ts).astype(I32)
    seg_end = jnp.cumsum(seg).astype(I32)
    seg_start = seg_end - seg
    n_steps = n_assign // ts + n_exp
    n_used = seg_end[-1] // ts
    idx = jnp.arange(n_steps, dtype=I32)
    row0 = jnp.minimum(idx, n_used - 1) * ts
    exp = jnp.sum((row0[:, None] >= seg_end[None, :]).astype(I32), axis=1)
    exp = jnp.minimum(exp, n_exp - 1)
    n_valid = jnp.clip((_take(seg_start + valid, exp) - row0) // tm, 1, 2).astype(I32)
    prev = jnp.concatenate([jnp.full((1,), -1, I32), exp[:-1]])
    first = (exp != prev).astype(I32)
    rem = _take(seg_end, exp) // ts - idx
    cand = jnp.where(counts > 0, jnp.arange(n_exp, dtype=I32), n_exp)
    after = jnp.concatenate([lax.cummin(cand, reverse=True)[1:], jnp.full((1,), n_exp, I32)])
    nxt = _take(jnp.where(after >= n_exp, -1, after), exp)
    tables = (exp, first, rem.astype(I32), nxt, n_valid)
    return seg_start, seg_end, seg, tables, n_used.reshape(1).astype(I32)


def _block_tables(tables, n_used):
    exp, first, rem, nxt, n_valid = tables
    half = jnp.tile(jnp.arange(2, dtype=I32), exp.shape[0])
    rep = lambda v: jnp.repeat(v, 2)
    return (rep(exp), rep(first) * (1 - half), 2 * rep(rem) - half, rep(nxt),
            (half < rep(n_valid)).astype(I32)), 2 * n_used


def _dest_kernel(ps_ref, ti_ref, pos_ref, o_ref, *, n_exp):
    ti = ti_ref[...]
    acc = pos_ref[...]
    for e in range(n_exp):
        acc = acc + jnp.where(ti == e, ps_ref[e], 0)
    o_ref[...] = acc


def _dest_rows(pad_start, ti, pos):
    k, t = ti.shape
    kern = functools.partial(_dest_kernel, n_exp=pad_start.shape[0])
    grid_spec = pltpu.PrefetchScalarGridSpec(
        num_scalar_prefetch=1, grid=(1,),
        in_specs=[pl.BlockSpec((k, t), lambda i, ps: (0, 0)),
                  pl.BlockSpec((k, t), lambda i, ps: (0, 0))],
        out_specs=pl.BlockSpec((k, t), lambda i, ps: (0, 0)))
    return pl.pallas_call(kern, grid_spec=grid_spec, out_shape=SDS((k, t), I32),
                          compiler_params=_cparams(1), name="moe_dest")(pad_start, ti, pos)


def _dispatch_kernel(pe_ref, pc_ref, nu_ref, dst_ref, x_ref, xs_ref, buf, zbuf, sem, zsem,
                     *, tc, ts, n_exp, n_steps):
    i = pl.program_id(0)
    n = pl.num_programs(0)
    ns = x_ref.shape[1] // (2 * LANES)

    def zero_copy(row0):
        rows = pl.ds(pl.multiple_of(row0 * ns, ts * ns), ts * ns)
        return pltpu.make_async_copy(zbuf, xs_ref.at[rows], zsem)

    @pl.when(i == 0)
    def _():
        zbuf[...] = jnp.zeros(zbuf.shape, zbuf.dtype)
        for start in (True, False):
            for e in range(n_exp):
                @pl.when(pc_ref[e] > 0)
                def _(e=e):
                    cp = zero_copy(pe_ref[e] - ts)
                    cp.start() if start else cp.wait()

            def tail(b, c):
                cp = zero_copy(b * ts)
                cp.start() if start else cp.wait()
                return c
            lax.fori_loop(nu_ref[0], n_steps, tail, 0)

    slot = i % 2
    x = x_ref[...]
    for s in range(ns):
        lo = x[:, (2 * s) * LANES:(2 * s + 1) * LANES]
        hi = x[:, (2 * s + 1) * LANES:(2 * s + 2) * LANES]
        buf[slot, pl.ds(s, tc, stride=ns), :] = _pack_pair(lo, hi)
    for k in range(TOP_K):
        for r in range(tc):
            dst = pl.ds(pl.multiple_of(dst_ref[0, k, r] * ns, ns), ns)
            pltpu.make_async_copy(buf.at[slot, pl.ds(r * ns, ns)], xs_ref.at[dst],
                                  sem.at[slot]).start(priority=r % 2)

    def wait_slot(s):
        for k in range(TOP_K):
            pltpu.make_async_copy(buf.at[s], xs_ref.at[pl.ds(0, tc * ns)], sem.at[s]).wait()

    @pl.when(i >= 1)
    def _():
        wait_slot(1 - slot)

    @pl.when(i == n - 1)
    def _():
        wait_slot(slot)


def _dispatch(hm, dest, seg_end, seg, n_used, ts, n_steps):
    t, d = hm.shape
    tc = _tile(t, 128)
    n = t // tc
    ns = d // (2 * LANES)
    n_exp = seg_end.shape[0]
    dest3 = dest.reshape(TOP_K, n, tc).transpose(1, 0, 2)
    kern = functools.partial(_dispatch_kernel, tc=tc, ts=ts, n_exp=n_exp, n_steps=n_steps)
    grid_spec = pltpu.PrefetchScalarGridSpec(
        num_scalar_prefetch=3, grid=(n,),
        in_specs=[pl.BlockSpec((1, TOP_K, tc), lambda i, *_: (i, 0, 0), memory_space=pltpu.SMEM),
                  pl.BlockSpec((tc, d), lambda i, *_: (i, 0))],
        out_specs=pl.BlockSpec(memory_space=pl.ANY),
        scratch_shapes=[pltpu.VMEM((2, tc * ns, LANES), U32),
                        pltpu.VMEM((ts * ns, LANES), U32),
                        pltpu.SemaphoreType.DMA((2,)),
                        pltpu.SemaphoreType.DMA(())])
    return pl.pallas_call(
        kern, grid_spec=grid_spec,
        out_shape=SDS((n_steps * ts * ns, LANES), U32),
        compiler_params=_cparams(1), name="moe_dispatch")(seg_end, seg, n_used, dest3, hm)


W_CHUNK = 512


def _stream_weights(i, be_ref, first_ref, rem_ref, nxt_ref, w_hbm, wb_ref, stage_ref, st_ref, sem):
    n_chunks, _, chunk = wb_ref.shape[1:]

    def chunk_copy(e, c, slot):
        cols = pl.ds(pl.multiple_of(c * chunk, chunk), chunk)
        return pltpu.make_async_copy(w_hbm.at[e, :, cols], stage_ref.at[slot], sem.at[slot])

    def start_first_two(e):
        chunk_copy(e, 0, 0).start()
        chunk_copy(e, 1, 1).start()

    def cast_chunks(e, dst, done, issued, k):
        def body(_, carry):
            done, issued = carry
            slot = done % 2
            chunk_copy(e, done, slot).wait()
            wb_ref[dst, done] = stage_ref[slot].astype(BF16)

            @pl.when(issued < n_chunks)
            def _():
                chunk_copy(e, issued, slot).start()
            return done + 1, jnp.minimum(issued + 1, n_chunks)
        return lax.fori_loop(0, k, body, (done, issued))

    @pl.when(i == 0)
    def _():
        start_first_two(be_ref[0])
        cast_chunks(be_ref[0], 0, 0, 2, n_chunks)
        st_ref[0] = 1

    @pl.when(first_ref[i] == 1)
    def _():
        st_ref[0] = 1 - st_ref[0]
        st_ref[1] = 0
        st_ref[2] = 0

        @pl.when(nxt_ref[i] >= 0)
        def _():
            start_first_two(nxt_ref[i])
            st_ref[2] = 2

    cur = st_ref[0]

    @pl.when(nxt_ref[i] >= 0)
    def _():
        done = st_ref[1]
        rem = rem_ref[i]
        lead = (first_ref[i] == 1) & (rem > 1)
        left = jnp.where(first_ref[i] == 1, jnp.maximum(rem - 1, 1), rem)
        k = jnp.where(lead, 0, lax.div(n_chunks - done + left - 1, left))
        done, issued = cast_chunks(nxt_ref[i], 1 - cur, done, st_ref[2], k)
        st_ref[1] = done
        st_ref[2] = issued

    return cur


def _moe_up_kernel(be_ref, first_ref, rem_ref, nxt_ref, ok_ref, nu_ref, x_ref, b_ref, w_hbm, o_ref,
                   wb_ref, stage_ref, st_ref, sem):
    i = pl.program_id(0)
    tm = o_ref.shape[0]
    ns = x_ref.shape[0] // tm
    n_half = wb_ref.shape[1] // 2
    chunk = wb_ref.shape[3]
    f = n_half * chunk

    @pl.when(i < nu_ref[0])
    def _():
        cur = _stream_weights(i, be_ref, first_ref, rem_ref, nxt_ref, w_hbm, wb_ref, stage_ref,
                              st_ref, sem)

        @pl.when(ok_ref[i] == 1)
        def _():
            cols = []
            for s in range(ns):
                lo, hi = _unpack_pair(x_ref[pl.ds(s, tm, stride=ns), :])
                cols += [lo.astype(BF16), hi.astype(BF16)]
            x = jnp.concatenate(cols, axis=1)
            for n in range(n_half):
                lo, hi = n * chunk, (n + 1) * chunk
                gate = jnp.minimum(_dot(x, wb_ref[cur, n]) + b_ref[:, lo:hi], SWIGLU_LIMIT)
                up = jnp.clip(_dot(x, wb_ref[cur, n_half + n]) + b_ref[:, f + lo:f + hi],
                              -SWIGLU_LIMIT, SWIGLU_LIMIT)
                glu = gate * jax.nn.sigmoid(SWIGLU_ALPHA * gate)
                o_ref[:, lo:hi] = ((up + 1.0) * glu).astype(o_ref.dtype)

    @pl.when((i >= nu_ref[0]) | (ok_ref[i] == 0))
    def _():
        o_ref[...] = jnp.zeros(o_ref.shape, o_ref.dtype)


def _moe_up(xs, w_gu, b_gu, tables, n_used, tm):
    d = w_gu.shape[1]
    ns = d // (2 * LANES)
    n_rows = xs.shape[0] // ns
    two_f = w_gu.shape[2]
    chunk = min(W_CHUNK, two_f // 2)
    grid_spec = pltpu.PrefetchScalarGridSpec(
        num_scalar_prefetch=6,
        grid=(n_rows // tm,),
        in_specs=[pl.BlockSpec((tm * ns, LANES),
                               lambda i, be, fi, re, nx, ok, nu: (jnp.minimum(i, nu[0] - 1), 0)),
                  pl.BlockSpec((None, 1, two_f), lambda i, be, fi, re, nx, ok, nu: (be[i], 0, 0)),
                  pl.BlockSpec(memory_space=pl.ANY)],
        out_specs=pl.BlockSpec((tm, two_f // 2), lambda i, *_: (i, 0)),
        scratch_shapes=[pltpu.VMEM((2, two_f // chunk, d, chunk), BF16),
                        pltpu.VMEM((2, d, chunk), w_gu.dtype),
                        pltpu.SMEM((3,), I32),
                        pltpu.SemaphoreType.DMA((2,))])
    return pl.pallas_call(
        _moe_up_kernel, grid_spec=grid_spec,
        out_shape=SDS((n_rows, two_f // 2), BF16),
        compiler_params=_cparams(1), name="moe_up")(*tables, n_used, xs, b_gu, w_gu)


def _moe_down_kernel(be_ref, first_ref, rem_ref, nxt_ref, nv_ref, nu_ref, a_ref, b_ref, w_hbm, o_ref,
                     wf_ref, st_ref, sem):
    i = pl.program_id(0)
    ts = a_ref.shape[0]
    ns = o_ref.shape[0] // ts
    n_chunks, _, chunk = wf_ref.shape[1:]

    def chunk_copy(e, dst, c):
        return pltpu.make_async_copy(w_hbm.at[e, :, pl.ds(c * chunk, chunk)], wf_ref.at[dst, c], sem)

    def compute(cur, rows):
        a = a_ref[:rows, :]
        for n in range(n_chunks):
            lo, hi = n * chunk, (n + 1) * chunk
            y = _dot(a, wf_ref[cur, n].astype(BF16)) + b_ref[:, lo:hi]
            for j in range(chunk // (2 * LANES)):
                s = lo // (2 * LANES) + j
                w = _pack_pair(y[:, (2 * j) * LANES:(2 * j + 1) * LANES],
                               y[:, (2 * j + 1) * LANES:(2 * j + 2) * LANES])
                o_ref[pl.ds(s, rows, stride=ns), :] = w

    @pl.when(i < nu_ref[0])
    def _():
        @pl.when(i == 0)
        def _():
            for c in range(n_chunks):
                chunk_copy(be_ref[0], 0, c).start()
            st_ref[0] = 1
            st_ref[1] = 1

        @pl.when(first_ref[i] == 1)
        def _():
            @pl.when(st_ref[1] == 1)
            def _():
                for c in range(n_chunks):
                    chunk_copy(be_ref[i], 1 - st_ref[0], c).wait()
            st_ref[0] = 1 - st_ref[0]
            st_ref[1] = 0

            @pl.when(nxt_ref[i] >= 0)
            def _():
                for c in range(n_chunks):
                    chunk_copy(nxt_ref[i], 1 - st_ref[0], c).start()
                st_ref[1] = 1

        cur = st_ref[0]

        @pl.when(nv_ref[i] == 2)
        def _():
            compute(cur, ts)

        @pl.when(nv_ref[i] == 1)
        def _():
            compute(cur, ts // 2)
            half = ts // 2 * ns
            o_ref[half:, :] = jnp.zeros((half, LANES), o_ref.dtype)

    @pl.when(i >= nu_ref[0])
    def _():
        o_ref[...] = jnp.zeros(o_ref.shape, o_ref.dtype)


def _moe_down(act, w_dn, b_dn, tables, n_used, ts):
    n_rows, f = act.shape
    d = w_dn.shape[2]
    ns = d // (2 * LANES)
    chunk = max(min(W_CHUNK, d // 2), 2 * LANES)
    grid_spec = pltpu.PrefetchScalarGridSpec(
        num_scalar_prefetch=6,
        grid=(n_rows // ts,),
        in_specs=[pl.BlockSpec((ts, f),
                               lambda i, be, fi, re, nx, nv, nu: (jnp.minimum(i, nu[0] - 1), 0)),
                  pl.BlockSpec((None, 1, d), lambda i, be, fi, re, nx, nv, nu: (be[i], 0, 0)),
                  pl.BlockSpec(memory_space=pl.ANY)],
        out_specs=pl.BlockSpec((ts * ns, LANES), lambda i, *_: (i, 0)),
        scratch_shapes=[pltpu.VMEM((2, d // chunk, f, chunk), w_dn.dtype),
                        pltpu.SMEM((2,), I32),
                        pltpu.SemaphoreType.DMA(())])
    return pl.pallas_call(
        _moe_down_kernel, grid_spec=grid_spec,
        out_shape=SDS((n_rows * ns, LANES), U32),
        compiler_params=_cparams(1), name="moe_down")(*tables, n_used, act, b_dn, w_dn)


def _combine_kernel(dst_ref, x_ref, g_ref, ys_ref, o_ref, buf, sem, *, tc):
    i = pl.program_id(0)
    n = pl.num_programs(0) - 1
    ns = x_ref.shape[1] // (2 * LANES)

    @pl.when(i < n)
    def _():
        slot = i % 2
        for k in range(TOP_K):
            for r in range(tc):
                src = pl.ds(pl.multiple_of(dst_ref[0, k, r] * ns, ns), ns)
                pltpu.make_async_copy(ys_ref.at[src], buf.at[slot, k, pl.ds(r * ns, ns)],
                                      sem.at[slot]).start(priority=r % 2)

    @pl.when(i >= 1)
    def _():
        slot = (i - 1) % 2
        for k in range(TOP_K):
            pltpu.make_async_copy(ys_ref.at[pl.ds(0, tc * ns)], buf.at[slot, k],
                                  sem.at[slot]).wait()
        g = g_ref[...]
        g = jnp.concatenate([g, jnp.zeros((SUBLANES - TOP_K, tc), F32)], axis=0).T
        for s in range(ns):
            c0, c1, c2 = (2 * s) * LANES, (2 * s + 1) * LANES, (2 * s + 2) * LANES
            acc_lo = x_ref[:, c0:c1]
            acc_hi = x_ref[:, c1:c2]
            for k in range(TOP_K):
                lo, hi = _unpack_pair(buf[slot, k, pl.ds(s, tc, stride=ns), :])
                acc_lo = acc_lo + g[:, k:k + 1] * lo
                acc_hi = acc_hi + g[:, k:k + 1] * hi
            o_ref[:, c0:c1] = acc_lo
            o_ref[:, c1:c2] = acc_hi


def _combine(x2, gates, dest, ys):
    t, d = x2.shape
    tc = _tile(t, 128)
    n = t // tc
    ns = d // (2 * LANES)
    dest3 = dest.reshape(TOP_K, n, tc).transpose(1, 0, 2)
    kern = functools.partial(_combine_kernel, tc=tc)
    prev = lambda i: (jnp.maximum(i - 1, 0), 0)
    return pl.pallas_call(
        kern,
        grid=(n + 1,),
        in_specs=[pl.BlockSpec((1, TOP_K, tc), lambda i: (jnp.minimum(i, n - 1), 0, 0),
                               memory_space=pltpu.SMEM),
                  pl.BlockSpec((tc, d), prev),
                  pl.BlockSpec((TOP_K, tc), lambda i: (0, jnp.maximum(i - 1, 0))),
                  pl.BlockSpec(memory_space=pl.ANY)],
        out_specs=pl.BlockSpec((tc, d), prev),
        out_shape=SDS((t, d), F32),
        scratch_shapes=[pltpu.VMEM((2, TOP_K, tc * ns, LANES), U32),
                        pltpu.SemaphoreType.DMA((2,))],
        compiler_params=_cparams(1), name="moe_combine")(dest3, x2, gates, ys)


def _layer(x2d, mem2d, batch, seq, n_mem, p):
    t, d = x2d.shape
    c = p["conv_w"].shape[1]
    fw = p["g_fox_out"].shape[0]
    n_heads = fw // HEAD_DIM
    n_main = 3 * c + 3 * fw
    n_exp = p["w_router"].shape[1]
    assert n_heads <= F_ROWS and TOP_K <= SUBLANES and d % (2 * LANES) == 0

    w_in = p["w_mix_in"].astype(BF16)
    wf_t = jnp.zeros((F_ROWS, d), BF16).at[:n_heads].set(w_in[:, n_main:].T)
    b_col = jnp.zeros((F_ROWS, 1), F32).at[:n_heads, 0].set(p["b_forget"])

    proj, ft = _mix_in(x2d, p["g_mix"][None, :], wf_t, w_in, n_main)
    cdec = _decay(ft, b_col, seq)
    y_fox = _fox_attention(proj, cdec, p["g_q"][None, :], p["g_k"][None, :],
                           batch, seq, n_heads, 3 * c, fw)
    x1 = _mix_out(proj, y_fox, p["conv_w"], p["g_conv_out"][None, :], p["g_fox_out"][None, :],
                  p["w_mix_out"], x2d, batch, seq)

    kn, vm = _memkv(mem2d, p["g_mem"][None, :], p["w_xkv"],
                    p["g_xk"][None, :], n_mem)
    x2, hm, ti, tg, pos, cnt = _xattn(x1, p["g_xattn"][None, :], p["w_xq"],
                            p["g_xq"][None, :], kn, vm, p["w_xo"],
                            p["g_moe"][None, :], p["w_router"].T.astype(BF16),
                            p["b_router"][:, None], seq, n_mem)

    tm = _tile(TOP_K * t, 512) // 2
    counts = cnt[:, 0].astype(I32)
    seg_start, seg_end, seg, tables, n_used = _route_tables(counts, TOP_K * t, tm)
    dest = _dest_rows(seg_start, ti, pos)
    xs = _dispatch(hm, dest, seg_end, seg, n_used, 2 * tm, tables[0].shape[0])
    act = _moe_up(xs, p["w_gate_up"], p["b_gate_up"][:, None, :], *_block_tables(tables, n_used), tm)
    ys = _moe_down(act, p["w_down"], p["b_down"][:, None, :], tables, n_used, 2 * tm)
    return _combine(x2, tg, dest, ys)


def kernel(x, mem, g_mix, w_mix_in, b_forget, conv_w, g_q, g_k, g_conv_out, g_fox_out, w_mix_out, g_xattn, g_mem, w_xq, w_xkv, g_xq, g_xk, w_xo, g_moe, w_router, b_router, w_gate_up, b_gate_up, w_down, b_down):
    batch, seq, d = x.shape
    n_mem = mem.shape[1]
    params = dict(g_mix=g_mix, w_mix_in=w_mix_in, b_forget=b_forget, conv_w=conv_w, g_q=g_q,
                  g_k=g_k, g_conv_out=g_conv_out, g_fox_out=g_fox_out, w_mix_out=w_mix_out,
                  g_xattn=g_xattn, g_mem=g_mem, w_xq=w_xq, w_xkv=w_xkv, g_xq=g_xq, g_xk=g_xk,
                  w_xo=w_xo, g_moe=g_moe, w_router=w_router, b_router=b_router,
                  w_gate_up=w_gate_up, b_gate_up=b_gate_up, w_down=w_down, b_down=b_down)
    x2d = x.reshape(batch * seq, d)
    mem2d = mem.reshape(batch * n_mem, d)
    for l in range(g_mix.shape[0]):
        x2d = _layer(x2d, mem2d, batch, seq, n_mem, {k: v[l] for k, v in params.items()})
    return x2d.reshape(batch, seq, d)
```

```python
import functools

import jax
import jax.numpy as jnp
from jax import lax
from jax.experimental import pallas as pl
from jax.experimental.pallas import tpu as pltpu

F32 = jnp.float32
BF16 = jnp.bfloat16
I32 = jnp.int32
U32 = jnp.uint32
SDS = jax.ShapeDtypeStruct

EPS = 1e-6
HEAD_DIM = 128
TOP_K = 4
SWIGLU_LIMIT = 7.0
SWIGLU_ALPHA = 1.702
MASK_VALUE = float(jnp.finfo(jnp.float32).min)
V7X_VMEM_LIMIT_BYTES = 56 * 1024 * 1024
F_ROWS = 16
LANES = 128
SUBLANES = 8


def _cparams(n_axes):
    return pltpu.CompilerParams(
        dimension_semantics=("arbitrary",) * n_axes,
        vmem_limit_bytes=V7X_VMEM_LIMIT_BYTES)


def _rms(x, g):
    return x * lax.rsqrt(jnp.mean(x * x, axis=-1, keepdims=True) + EPS) * g


def _dot(a, b):
    return jnp.dot(a, b, preferred_element_type=F32)


def _dot_nt(a, b):
    return lax.dot_general(a, b, (((1,), (1,)), ((), ())), preferred_element_type=F32)


def _tile(n, pref):
    return pref if n % pref == 0 else n


def _pack_pair(lo, hi):
    lo = lax.bitcast_convert_type(lo.astype(BF16).astype(F32), U32) >> 16
    hi = lax.bitcast_convert_type(hi.astype(BF16).astype(F32), U32) & jnp.uint32(0xFFFF0000)
    return hi | lo


def _unpack_pair(w):
    lo = lax.bitcast_convert_type(w << 16, F32)
    hi = lax.bitcast_convert_type(w & jnp.uint32(0xFFFF0000), F32)
    return lo, hi


def _mix_in_kernel(x_ref, g_ref, wf_ref, w_ref, o_ref, ft_ref, h_ref):
    @pl.when(pl.program_id(1) == 0)
    def _():
        hb = _rms(x_ref[...], g_ref[...]).astype(BF16)
        h_ref[...] = hb
        ft_ref[...] = _dot_nt(wf_ref[...], hb)

    o_ref[...] = _dot(h_ref[...], w_ref[...]).astype(o_ref.dtype)


def _mix_in(x2d, g, wf_t, w_in, n_out):
    t, d = x2d.shape
    tm = _tile(t, 1024)
    tn = _tile(n_out, 2048)
    return pl.pallas_call(
        _mix_in_kernel,
        grid=(t // tm, n_out // tn),
        in_specs=[pl.BlockSpec((tm, d), lambda i, j: (i, 0)),
                  pl.BlockSpec((1, d), lambda i, j: (0, 0)),
                  pl.BlockSpec((F_ROWS, d), lambda i, j: (0, 0)),
                  pl.BlockSpec((d, tn), lambda i, j: (0, j))],
        out_specs=[pl.BlockSpec((tm, tn), lambda i, j: (i, j)),
                   pl.BlockSpec((F_ROWS, tm), lambda i, j: (0, i))],
        out_shape=[SDS((t, n_out), BF16), SDS((F_ROWS, t), F32)],
        scratch_shapes=[pltpu.VMEM((tm, d), BF16)],
        compiler_params=_cparams(2), name="mix_in")(x2d, g, wf_t, w_in)


def _decay_kernel(ft_ref, b_ref, c_ref):
    z = ft_ref[...] + b_ref[...]
    lf = jnp.minimum(z, 0.0) - jnp.log(1.0 + jnp.exp(-jnp.abs(z)))
    s = lf.shape[1]
    lane = lax.broadcasted_iota(I32, lf.shape, 1)
    d = 1
    while d < s:
        lf = lf + jnp.where(lane >= d, pltpu.roll(lf, d, 1), 0.0)
        d *= 2
    c_ref[...] = lf


def _decay(ft, b_col, seq):
    rows, t = ft.shape
    return pl.pallas_call(
        _decay_kernel,
        grid=(t // seq,),
        in_specs=[pl.BlockSpec((rows, seq), lambda b: (0, b)),
                  pl.BlockSpec((rows, 1), lambda b: (0, 0))],
        out_specs=pl.BlockSpec((rows, seq), lambda b: (0, b)),
        out_shape=SDS((rows, t), F32),
        compiler_params=_cparams(1), name="decay")(ft, b_col)


def _fox_kernel(q_ref, k_ref, v_ref, c_ref, gq_ref, gk_ref, o_ref, kn_ref, nc_ref, *, tq, hpb):
    g = pl.program_id(1)
    seq = q_ref.shape[0]
    hd = HEAD_DIM
    scale = HEAD_DIM ** -0.5
    for j in range(hpb):
        cs = slice(j * hd, (j + 1) * hd)
        kn_ref[:, cs] = _rms(k_ref[:, cs].astype(F32), gk_ref[...]).astype(BF16)
        nc_ref[j:j + 1, :] = -c_ref[pl.ds(g * hpb + j, 1), :]
    row = lax.broadcasted_iota(I32, (tq, tq), 0)
    col = lax.broadcasted_iota(I32, (tq, tq), 1)
    causal = col <= row
    for qi in range(seq // tq):
        lo, hi = qi * tq, (qi + 1) * tq
        for j in range(hpb):
            cs = slice(j * hd, (j + 1) * hd)
            q = (_rms(q_ref[lo:hi, cs].astype(F32), gq_ref[...]) * scale).astype(BF16)
            s_diag = jnp.where(causal, _dot_nt(q, kn_ref[lo:hi, cs]) + nc_ref[j:j + 1, lo:hi],
                               MASK_VALUE)
            m = jnp.max(s_diag, axis=-1, keepdims=True)
            if qi > 0:
                s_past = _dot_nt(q, kn_ref[:lo, cs]) + nc_ref[j:j + 1, :lo]
                m = jnp.maximum(m, jnp.max(s_past, axis=-1, keepdims=True))
            p_diag = jnp.exp(s_diag - m)
            l = jnp.sum(p_diag, axis=-1, keepdims=True)
            acc = _dot(p_diag.astype(BF16), v_ref[lo:hi, cs])
            if qi > 0:
                p_past = jnp.exp(s_past - m)
                l = l + jnp.sum(p_past, axis=-1, keepdims=True)
                acc = acc + _dot(p_past.astype(BF16), v_ref[:lo, cs])
            o_ref[lo:hi, cs] = (acc / l).astype(o_ref.dtype)


def _fox_attention(proj, c, g_q, g_k, batch, seq, n_heads, col0, width):
    t = proj.shape[0]
    tq = _tile(seq, 256)
    hpb = 2 if n_heads % 2 == 0 else 1
    bw = hpb * HEAD_DIM
    cb = col0 // bw
    wb = width // bw
    kern = functools.partial(_fox_kernel, tq=tq, hpb=hpb)
    return pl.pallas_call(
        kern,
        grid=(batch, n_heads // hpb),
        in_specs=[pl.BlockSpec((seq, bw), lambda b, h: (b, cb + h)),
                  pl.BlockSpec((seq, bw), lambda b, h: (b, cb + wb + h)),
                  pl.BlockSpec((seq, bw), lambda b, h: (b, cb + 2 * wb + h)),
                  pl.BlockSpec((F_ROWS, seq), lambda b, h: (0, b)),
                  pl.BlockSpec((1, HEAD_DIM), lambda b, h: (0, 0)),
                  pl.BlockSpec((1, HEAD_DIM), lambda b, h: (0, 0))],
        out_specs=pl.BlockSpec((seq, bw), lambda b, h: (b, h)),
        out_shape=SDS((t, width), BF16),
        scratch_shapes=[pltpu.VMEM((seq, bw), BF16), pltpu.VMEM((SUBLANES, seq), F32)],
        compiler_params=_cparams(2), name="fox_attention")(proj, proj, proj, c, g_q, g_k)


def _mix_out_kernel(ub_ref, uc_ref, ux_ref, hc_ref, hx_ref, yf_ref, cw_ref, gc_ref, gf_ref,
                    w_ref, x_ref, o_ref):
    si = pl.program_id(1)
    p = uc_ref[...].astype(F32) * ux_ref[...].astype(F32)
    halo = hc_ref[...].astype(F32) * hx_ref[...].astype(F32)
    halo = jnp.where(si > 0, halo, 0.0)
    row = lax.broadcasted_iota(I32, p.shape, 0)
    last, last2 = halo[SUBLANES - 1:SUBLANES, :], halo[SUBLANES - 2:SUBLANES - 1, :]
    p1 = jnp.where(row == 0, last, pltpu.roll(p, 1, 0))
    p2 = jnp.where(row == 0, last2, jnp.where(row == 1, last, pltpu.roll(p, 2, 0)))
    w = cw_ref[...]
    y = ub_ref[...].astype(F32) * (w[0:1, :] * p2 + w[1:2, :] * p1 + w[2:3, :] * p)
    c = y.shape[1]
    y_conv = _rms(y, gc_ref[...]).astype(BF16)
    y_fox = _rms(yf_ref[...].astype(F32), gf_ref[...]).astype(BF16)
    o_ref[...] = (x_ref[...] + _dot(y_conv, w_ref[:c, :].astype(BF16))
                  + _dot(y_fox, w_ref[c:, :].astype(BF16)))


def _mix_out(proj, y_fox, conv_w, g_conv, g_fox, w_out, x2d, batch, seq):
    t, d = x2d.shape
    c = conv_w.shape[1]
    w = y_fox.shape[1]
    ts = _tile(seq, 512)
    ns = seq // ts
    hb = ts // SUBLANES
    tile = lambda col: (lambda b, s: (b * ns + s, col))
    halo = lambda col: (lambda b, s: (jnp.maximum((b * ns + s) * hb - 1, 0), col))
    const = lambda b, s: (0, 0)
    return pl.pallas_call(
        _mix_out_kernel,
        grid=(batch, ns),
        in_specs=[pl.BlockSpec((ts, c), tile(0)),
                  pl.BlockSpec((ts, c), tile(1)),
                  pl.BlockSpec((ts, c), tile(2)),
                  pl.BlockSpec((SUBLANES, c), halo(1)),
                  pl.BlockSpec((SUBLANES, c), halo(2)),
                  pl.BlockSpec((ts, w), tile(0)),
                  pl.BlockSpec((3, c), const),
                  pl.BlockSpec((1, c), const),
                  pl.BlockSpec((1, w), const),
                  pl.BlockSpec((c + w, d), const, pipeline_mode=pl.Buffered(1)),
                  pl.BlockSpec((ts, d), tile(0))],
        out_specs=pl.BlockSpec((ts, d), tile(0)),
        out_shape=SDS((t, d), F32),
        compiler_params=_cparams(2), name="mix_out")(
            proj, proj, proj, proj, proj, y_fox, conv_w, g_conv, g_fox, w_out, x2d)


def _memkv_kernel(mem_ref, g_ref, w_ref, gk_ref, k_ref, v_ref):
    hm = _rms(mem_ref[...], g_ref[...]).astype(BF16)
    kv = _dot(hm, w_ref[...].astype(BF16))
    xw = k_ref.shape[1]
    for h in range(xw // HEAD_DIM):
        sl = slice(h * HEAD_DIM, (h + 1) * HEAD_DIM)
        k_ref[:, sl] = _rms(kv[:, sl], gk_ref[...]).astype(BF16)
    v_ref[...] = kv[:, xw:].astype(BF16)


def _memkv(mem2d, g_mem, w_xkv, g_xk, n_mem):
    tm_, d = mem2d.shape
    xw = w_xkv.shape[1] // 2
    return pl.pallas_call(
        _memkv_kernel,
        grid=(tm_ // n_mem,),
        in_specs=[pl.BlockSpec((n_mem, d), lambda b: (b, 0)),
                  pl.BlockSpec((1, d), lambda b: (0, 0)),
                  pl.BlockSpec((d, 2 * xw), lambda b: (0, 0)),
                  pl.BlockSpec((1, HEAD_DIM), lambda b: (0, 0))],
        out_specs=[pl.BlockSpec((n_mem, xw), lambda b: (b, 0)),
                   pl.BlockSpec((n_mem, xw), lambda b: (b, 0))],
        out_shape=[SDS((tm_, xw), BF16), SDS((tm_, xw), BF16)],
        compiler_params=_cparams(1), name="memkv")(mem2d, g_mem, w_xkv, g_xk)


def _xattn_kernel(x_ref, gx_ref, wq_ref, gq_ref, kn_ref, v_ref, wo_ref, gm_ref, wr_ref, br_ref,
                  x2_ref, hm_ref, ti_ref, tg_ref, pos_ref, cnt_ref, run_ref, tri_ref):
    i = pl.program_id(0)
    tm = x_ref.shape[0]

    @pl.when(i == 0)
    def _():
        run_ref[...] = jnp.zeros(run_ref.shape, F32)
        r = lax.broadcasted_iota(I32, (tm, tm), 0)
        c = lax.broadcasted_iota(I32, (tm, tm), 1)
        tri_ref[...] = jnp.where(r <= c, 1.0, 0.0).astype(BF16)

    x = x_ref[...]
    hb = _rms(x, gx_ref[...]).astype(BF16)
    q = _dot(hb, wq_ref[...].astype(BF16))
    scale = HEAD_DIM ** -0.5
    outs = []
    for h in range(q.shape[1] // HEAD_DIM):
        sl = slice(h * HEAD_DIM, (h + 1) * HEAD_DIM)
        qh = (_rms(q[:, sl], gq_ref[...]) * scale).astype(BF16)
        s = _dot_nt(qh, kn_ref[:, sl])
        p = jnp.exp(s - jnp.max(s, axis=-1, keepdims=True))
        oh = _dot(p.astype(BF16), v_ref[:, sl]) / jnp.sum(p, axis=-1, keepdims=True)
        outs.append(oh.astype(BF16))
    x2 = x + _dot(jnp.concatenate(outs, axis=-1), wo_ref[...].astype(BF16))
    x2_ref[...] = x2
    hm = _rms(x2, gm_ref[...]).astype(BF16)
    hm_ref[...] = hm
    lg = _dot_nt(wr_ref[...], hm) + br_ref[...]
    n_exp = lg.shape[0]
    eidx = lax.broadcasted_iota(I32, lg.shape, 0).astype(F32)
    vals, sel = [], []
    for r in range(TOP_K):
        mx = jnp.max(lg, axis=0, keepdims=True)
        am = jnp.min(jnp.where(lg == mx, eidx, float(n_exp)), axis=0, keepdims=True)
        ti_ref[r:r + 1, :] = am.astype(I32)
        vals.append(mx)
        sel.append(eidx == am)
        lg = jnp.where(sel[r], -jnp.inf, lg)
    ex = [jnp.exp(v - vals[0]) for v in vals]
    den = ex[0]
    for e in ex[1:]:
        den = den + e
    for r in range(TOP_K):
        tg_ref[r:r + 1, :] = ex[r] / den
    onehot = jnp.zeros(lg.shape, F32)
    for r in range(TOP_K):
        onehot = onehot + jnp.where(sel[r], 1.0, 0.0)
    incl = _dot(onehot.astype(BF16), tri_ref[...])
    excl = incl - onehot + run_ref[...]
    for r in range(TOP_K):
        pos_ref[r:r + 1, :] = jnp.sum(jnp.where(sel[r], excl, 0.0), axis=0,
                                      keepdims=True).astype(I32)
    run = run_ref[...] + incl[:, tm - 1:tm]
    run_ref[...] = run
    cnt_ref[...] = jnp.broadcast_to(run, cnt_ref.shape)


def _xattn(x1, g_xattn, w_xq, g_xq, kn, vm, w_xo, g_moe, w_rt, b_r, seq, n_mem):
    t, d = x1.shape
    xw = w_xq.shape[1]
    n_exp = w_rt.shape[0]
    tm = _tile(seq, 512)
    per = seq // tm
    const = lambda i: (0, 0)
    return pl.pallas_call(
        _xattn_kernel,
        grid=(t // tm,),
        in_specs=[pl.BlockSpec((tm, d), lambda i: (i, 0)),
                  pl.BlockSpec((1, d), const),
                  pl.BlockSpec((d, xw), const),
                  pl.BlockSpec((1, HEAD_DIM), const),
                  pl.BlockSpec((n_mem, xw), lambda i: (i // per, 0)),
                  pl.BlockSpec((n_mem, xw), lambda i: (i // per, 0)),
                  pl.BlockSpec((xw, d), const),
                  pl.BlockSpec((1, d), const),
                  pl.BlockSpec((n_exp, d), const),
                  pl.BlockSpec((n_exp, 1), const)],
        out_specs=[pl.BlockSpec((tm, d), lambda i: (i, 0)),
                   pl.BlockSpec((tm, d), lambda i: (i, 0)),
                   pl.BlockSpec((TOP_K, tm), lambda i: (0, i)),
                   pl.BlockSpec((TOP_K, tm), lambda i: (0, i)),
                   pl.BlockSpec((TOP_K, tm), lambda i: (0, i)),
                   pl.BlockSpec((n_exp, 128), const)],
        out_shape=[SDS((t, d), F32), SDS((t, d), BF16),
                   SDS((TOP_K, t), I32), SDS((TOP_K, t), F32), SDS((TOP_K, t), I32),
                   SDS((n_exp, 128), F32)],
        scratch_shapes=[pltpu.VMEM((n_exp, 1), F32), pltpu.VMEM((tm, tm), BF16)],
        compiler_params=_cparams(1), name="xattn_router")(
            x1, g_xattn, w_xq, g_xq, kn, vm, w_xo, g_moe, w_rt, b_r)


def _take(values, idx):
    pick = idx[:, None] == jnp.arange(values.shape[0], dtype=I32)[None, :]
    return jnp.sum(jnp.where(pick, values[None, :], 0), axis=1).astype(I32)


def _route_tables(counts, n_assign, tm):
    n_exp = counts.shape[0]
    ts = 2 * tm
    valid = (counts + tm - 1) // tm * tm
    seg = ((counts + ts - 1) // ts * ts).astype(I32)
    seg_end = jnp.cumsum(seg).astype(I32)
    seg_start = seg_end - seg
    n_steps = n_assign // ts + n_exp
    n_used = seg_end[-1] // ts
    idx = jnp.arange(n_steps, dtype=I32)
    row0 = jnp.minimum(idx, n_used - 1) * ts
    exp = jnp.sum((row0[:, None] >= seg_end[None, :]).astype(I32), axis=1)
    exp = jnp.minimum(exp, n_exp - 1)
    n_valid = jnp.clip((_take(seg_start + valid, exp) - row0) // tm, 1, 2).astype(I32)
    prev = jnp.concatenate([jnp.full((1,), -1, I32), exp[:-1]])
    first = (exp != prev).astype(I32)
    rem = _take(seg_end, exp) // ts - idx
    cand = jnp.where(counts > 0, jnp.arange(n_exp, dtype=I32), n_exp)
    after = jnp.concatenate([lax.cummin(cand, reverse=True)[1:], jnp.full((1,), n_exp, I32)])
    nxt = _take(jnp.where(after >= n_exp, -1, after), exp)
    tables = (exp, first, rem.astype(I32), nxt, n_valid)
    return seg_start, seg_end, seg, tables, n_used.reshape(1).astype(I32)


def _block_tables(tables, n_used):
    exp, first, rem, nxt, n_valid = tables
    half = jnp.tile(jnp.arange(2, dtype=I32), exp.shape[0])
    rep = lambda v: jnp.repeat(v, 2)
    return (rep(exp), rep(first) * (1 - half), 2 * rep(rem) - half, rep(nxt),
            (half < rep(n_valid)).astype(I32)), 2 * n_used


def _dest_kernel(ps_ref, ti_ref, pos_ref, o_ref, *, n_exp):
    ti = ti_ref[...]
    acc = pos_ref[...]
    for e in range(n_exp):
        acc = acc + jnp.where(ti == e, ps_ref[e], 0)
    o_ref[...] = acc


def _dest_rows(pad_start, ti, pos):
    k, t = ti.shape
    kern = functools.partial(_dest_kernel, n_exp=pad_start.shape[0])
    grid_spec = pltpu.PrefetchScalarGridSpec(
        num_scalar_prefetch=1, grid=(1,),
        in_specs=[pl.BlockSpec((k, t), lambda i, ps: (0, 0)),
                  pl.BlockSpec((k, t), lambda i, ps: (0, 0))],
        out_specs=pl.BlockSpec((k, t), lambda i, ps: (0, 0)))
    return pl.pallas_call(kern, grid_spec=grid_spec, out_shape=SDS((k, t), I32),
                          compiler_params=_cparams(1), name="moe_dest")(pad_start, ti, pos)


def _dispatch_kernel(pe_ref, pc_ref, nu_ref, dst_ref, x_ref, xs_ref, buf, zbuf, sem, zsem,
                     *, tc, ts, n_exp, n_steps):
    i = pl.program_id(0)
    n = pl.num_programs(0)
    ns = x_ref.shape[1] // (2 * LANES)

    def zero_copy(row0):
        rows = pl.ds(pl.multiple_of(row0 * ns, ts * ns), ts * ns)
        return pltpu.make_async_copy(zbuf, xs_ref.at[rows], zsem)

    @pl.when(i == 0)
    def _():
        zbuf[...] = jnp.zeros(zbuf.shape, zbuf.dtype)
        for start in (True, False):
            for e in range(n_exp):
                @pl.when(pc_ref[e] > 0)
                def _(e=e):
                    cp = zero_copy(pe_ref[e] - ts)
                    cp.start() if start else cp.wait()

            def tail(b, c):
                cp = zero_copy(b * ts)
                cp.start() if start else cp.wait()
                return c
            lax.fori_loop(nu_ref[0], n_steps, tail, 0)

    slot = i % 2
    x = x_ref[...]
    for s in range(ns):
        lo = x[:, (2 * s) * LANES:(2 * s + 1) * LANES]
        hi = x[:, (2 * s + 1) * LANES:(2 * s + 2) * LANES]
        buf[slot, pl.ds(s, tc, stride=ns), :] = _pack_pair(lo, hi)
    for k in range(TOP_K):
        for r in range(tc):
            dst = pl.ds(pl.multiple_of(dst_ref[0, k, r] * ns, ns), ns)
            pltpu.make_async_copy(buf.at[slot, pl.ds(r * ns, ns)], xs_ref.at[dst],
                                  sem.at[slot]).start(priority=r % 2)

    def wait_slot(s):
        for k in range(TOP_K):
            pltpu.make_async_copy(buf.at[s], xs_ref.at[pl.ds(0, tc * ns)], sem.at[s]).wait()

    @pl.when(i >= 1)
    def _():
        wait_slot(1 - slot)

    @pl.when(i == n - 1)
    def _():
        wait_slot(slot)


def _dispatch(hm, dest, seg_end, seg, n_used, ts, n_steps):
    t, d = hm.shape
    tc = _tile(t, 128)
    n = t // tc
    ns = d // (2 * LANES)
    n_exp = seg_end.shape[0]
    dest3 = dest.reshape(TOP_K, n, tc).transpose(1, 0, 2)
    kern = functools.partial(_dispatch_kernel, tc=tc, ts=ts, n_exp=n_exp, n_steps=n_steps)
    grid_spec = pltpu.PrefetchScalarGridSpec(
        num_scalar_prefetch=3, grid=(n,),
        in_specs=[pl.BlockSpec((1, TOP_K, tc), lambda i, *_: (i, 0, 0), memory_space=pltpu.SMEM),
                  pl.BlockSpec((tc, d), lambda i, *_: (i, 0))],
        out_specs=pl.BlockSpec(memory_space=pl.ANY),
        scratch_shapes=[pltpu.VMEM((2, tc * ns, LANES), U32),
                        pltpu.VMEM((ts * ns, LANES), U32),
                        pltpu.SemaphoreType.DMA((2,)),
                        pltpu.SemaphoreType.DMA(())])
    return pl.pallas_call(
        kern, grid_spec=grid_spec,
        out_shape=SDS((n_steps * ts * ns, LANES), U32),
        compiler_params=_cparams(1), name="moe_dispatch")(seg_end, seg, n_used, dest3, hm)


W_CHUNK = 512


def _stream_weights(i, be_ref, first_ref, rem_ref, nxt_ref, w_hbm, wb_ref, stage_ref, st_ref, sem):
    n_chunks, _, chunk = wb_ref.shape[1:]

    def chunk_copy(e, c, slot):
        cols = pl.ds(pl.multiple_of(c * chunk, chunk), chunk)
        return pltpu.make_async_copy(w_hbm.at[e, :, cols], stage_ref.at[slot], sem.at[slot])

    def start_first_two(e):
        chunk_copy(e, 0, 0).start()
        chunk_copy(e, 1, 1).start()

    def cast_chunks(e, dst, done, issued, k):
        def body(_, carry):
            done, issued = carry
            slot = done % 2
            chunk_copy(e, done, slot).wait()
            wb_ref[dst, done] = stage_ref[slot].astype(BF16)

            @pl.when(issued < n_chunks)
            def _():
                chunk_copy(e, issued, slot).start()
            return done + 1, jnp.minimum(issued + 1, n_chunks)
        return lax.fori_loop(0, k, body, (done, issued))

    @pl.when(i == 0)
    def _():
        start_first_two(be_ref[0])
        cast_chunks(be_ref[0], 0, 0, 2, n_chunks)
        st_ref[0] = 1

    @pl.when(first_ref[i] == 1)
    def _():
        st_ref[0] = 1 - st_ref[0]
        st_ref[1] = 0
        st_ref[2] = 0

        @pl.when(nxt_ref[i] >= 0)
        def _():
            start_first_two(nxt_ref[i])
            st_ref[2] = 2

    cur = st_ref[0]

    @pl.when(nxt_ref[i] >= 0)
    def _():
        done = st_ref[1]
        rem = rem_ref[i]
        lead = (first_ref[i] == 1) & (rem > 1)
        left = jnp.where(first_ref[i] == 1, jnp.maximum(rem - 1, 1), rem)
        k = jnp.where(lead, 0, lax.div(n_chunks - done + left - 1, left))
        done, issued = cast_chunks(nxt_ref[i], 1 - cur, done, st_ref[2], k)
        st_ref[1] = done
        st_ref[2] = issued

    return cur


def _moe_up_kernel(be_ref, first_ref, rem_ref, nxt_ref, ok_ref, nu_ref, x_ref, b_ref, w_hbm, o_ref,
                   wb_ref, stage_ref, st_ref, sem):
    i = pl.program_id(0)
    tm = o_ref.shape[0]
    ns = x_ref.shape[0] // tm
    n_half = wb_ref.shape[1] // 2
    chunk = wb_ref.shape[3]
    f = n_half * chunk

    @pl.when(i < nu_ref[0])
    def _():
        cur = _stream_weights(i, be_ref, first_ref, rem_ref, nxt_ref, w_hbm, wb_ref, stage_ref,
                              st_ref, sem)

        @pl.when(ok_ref[i] == 1)
        def _():
            cols = []
            for s in range(ns):
                lo, hi = _unpack_pair(x_ref[pl.ds(s, tm, stride=ns), :])
                cols += [lo.astype(BF16), hi.astype(BF16)]
            x = jnp.concatenate(cols, axis=1)
            for n in range(n_half):
                lo, hi = n * chunk, (n + 1) * chunk
                gate = jnp.minimum(_dot(x, wb_ref[cur, n]) + b_ref[:, lo:hi], SWIGLU_LIMIT)
                up = jnp.clip(_dot(x, wb_ref[cur, n_half + n]) + b_ref[:, f + lo:f + hi],
                              -SWIGLU_LIMIT, SWIGLU_LIMIT)
                glu = gate * jax.nn.sigmoid(SWIGLU_ALPHA * gate)
                o_ref[:, lo:hi] = ((up + 1.0) * glu).astype(o_ref.dtype)

    @pl.when((i >= nu_ref[0]) | (ok_ref[i] == 0))
    def _():
        o_ref[...] = jnp.zeros(o_ref.shape, o_ref.dtype)


def _moe_up(xs, w_gu, b_gu, tables, n_used, tm):
    d = w_gu.shape[1]
    ns = d // (2 * LANES)
    n_rows = xs.shape[0] // ns
    two_f = w_gu.shape[2]
    chunk = min(W_CHUNK, two_f // 2)
    grid_spec = pltpu.PrefetchScalarGridSpec(
        num_scalar_prefetch=6,
        grid=(n_rows // tm,),
        in_specs=[pl.BlockSpec((tm * ns, LANES),
                               lambda i, be, fi, re, nx, ok, nu: (jnp.minimum(i, nu[0] - 1), 0)),
                  pl.BlockSpec((None, 1, two_f), lambda i, be, fi, re, nx, ok, nu: (be[i], 0, 0)),
                  pl.BlockSpec(memory_space=pl.ANY)],
        out_specs=pl.BlockSpec((tm, two_f // 2), lambda i, *_: (i, 0)),
        scratch_shapes=[pltpu.VMEM((2, two_f // chunk, d, chunk), BF16),
                        pltpu.VMEM((2, d, chunk), w_gu.dtype),
                        pltpu.SMEM((3,), I32),
                        pltpu.SemaphoreType.DMA((2,))])
    return pl.pallas_call(
        _moe_up_kernel, grid_spec=grid_spec,
        out_shape=SDS((n_rows, two_f // 2), BF16),
        compiler_params=_cparams(1), name="moe_up")(*tables, n_used, xs, b_gu, w_gu)


def _moe_down_kernel(be_ref, first_ref, rem_ref, nxt_ref, nv_ref, nu_ref, a_ref, b_ref, w_hbm, o_ref,
                     wf_ref, st_ref, sem):
    i = pl.program_id(0)
    ts = a_ref.shape[0]
    ns = o_ref.shape[0] // ts
    n_chunks, _, chunk = wf_ref.shape[1:]

    def chunk_copy(e, dst, c):
        return pltpu.make_async_copy(w_hbm.at[e, :, pl.ds(c * chunk, chunk)], wf_ref.at[dst, c], sem)

    def compute(cur, rows):
        a = a_ref[:rows, :]
        for n in range(n_chunks):
            lo, hi = n * chunk, (n + 1) * chunk
            y = _dot(a, wf_ref[cur, n].astype(BF16)) + b_ref[:, lo:hi]
            for j in range(chunk // (2 * LANES)):
                s = lo // (2 * LANES) + j
                w = _pack_pair(y[:, (2 * j) * LANES:(2 * j + 1) * LANES],
                               y[:, (2 * j + 1) * LANES:(2 * j + 2) * LANES])
                o_ref[pl.ds(s, rows, stride=ns), :] = w

    @pl.when(i < nu_ref[0])
    def _():
        @pl.when(i == 0)
        def _():
            for c in range(n_chunks):
                chunk_copy(be_ref[0], 0, c).start()
            st_ref[0] = 1
            st_ref[1] = 1

        @pl.when(first_ref[i] == 1)
        def _():
            @pl.when(st_ref[1] == 1)
            def _():
                for c in range(n_chunks):
                    chunk_copy(be_ref[i], 1 - st_ref[0], c).wait()
            st_ref[0] = 1 - st_ref[0]
            st_ref[1] = 0

            @pl.when(nxt_ref[i] >= 0)
            def _():
                for c in range(n_chunks):
                    chunk_copy(nxt_ref[i], 1 - st_ref[0], c).start()
                st_ref[1] = 1

        cur = st_ref[0]

        @pl.when(nv_ref[i] == 2)
        def _():
            compute(cur, ts)

        @pl.when(nv_ref[i] == 1)
        def _():
            compute(cur, ts // 2)
            half = ts // 2 * ns
            o_ref[half:, :] = jnp.zeros((half, LANES), o_ref.dtype)

    @pl.when(i >= nu_ref[0])
    def _():
        o_ref[...] = jnp.zeros(o_ref.shape, o_ref.dtype)


def _moe_down(act, w_dn, b_dn, tables, n_used, ts):
    n_rows, f = act.shape
    d = w_dn.shape[2]
    ns = d // (2 * LANES)
    chunk = max(min(W_CHUNK, d // 2), 2 * LANES)
    grid_spec = pltpu.PrefetchScalarGridSpec(
        num_scalar_prefetch=6,
        grid=(n_rows // ts,),
        in_specs=[pl.BlockSpec((ts, f),
                               lambda i, be, fi, re, nx, nv, nu: (jnp.minimum(i, nu[0] - 1), 0)),
                  pl.BlockSpec((None, 1, d), lambda i, be, fi, re, nx, nv, nu: (be[i], 0, 0)),
                  pl.BlockSpec(memory_space=pl.ANY)],
        out_specs=pl.BlockSpec((ts * ns, LANES), lambda i, *_: (i, 0)),
        scratch_shapes=[pltpu.VMEM((2, d // chunk, f, chunk), w_dn.dtype),
                        pltpu.SMEM((2,), I32),
                        pltpu.SemaphoreType.DMA(())])
    return pl.pallas_call(
        _moe_down_kernel, grid_spec=grid_spec,
        out_shape=SDS((n_rows * ns, LANES), U32),
        compiler_params=_cparams(1), name="moe_down")(*tables, n_used, act, b_dn, w_dn)


def _combine_kernel(dst_ref, x_ref, g_ref, ys_ref, o_ref, buf, sem, *, tc):
    i = pl.program_id(0)
    n = pl.num_programs(0) - 1
    ns = x_ref.shape[1] // (2 * LANES)

    @pl.when(i < n)
    def _():
        slot = i % 2
        for k in range(TOP_K):
            for r in range(tc):
                src = pl.ds(pl.multiple_of(dst_ref[0, k, r] * ns, ns), ns)
                pltpu.make_async_copy(ys_ref.at[src], buf.at[slot, k, pl.ds(r * ns, ns)],
                                      sem.at[slot]).start(priority=r % 2)

    @pl.when(i >= 1)
    def _():
        slot = (i - 1) % 2
        for k in range(TOP_K):
            pltpu.make_async_copy(ys_ref.at[pl.ds(0, tc * ns)], buf.at[slot, k],
                                  sem.at[slot]).wait()
        g = g_ref[...]
        g = jnp.concatenate([g, jnp.zeros((SUBLANES - TOP_K, tc), F32)], axis=0).T
        for s in range(ns):
            c0, c1, c2 = (2 * s) * LANES, (2 * s + 1) * LANES, (2 * s + 2) * LANES
            acc_lo = x_ref[:, c0:c1]
            acc_hi = x_ref[:, c1:c2]
            for k in range(TOP_K):
                lo, hi = _unpack_pair(buf[slot, k, pl.ds(s, tc, stride=ns), :])
                acc_lo = acc_lo + g[:, k:k + 1] * lo
                acc_hi = acc_hi + g[:, k:k + 1] * hi
            o_ref[:, c0:c1] = acc_lo
            o_ref[:, c1:c2] = acc_hi


def _combine(x2, gates, dest, ys):
    t, d = x2.shape
    tc = _tile(t, 128)
    n = t // tc
    ns = d // (2 * LANES)
    dest3 = dest.reshape(TOP_K, n, tc).transpose(1, 0, 2)
    kern = functools.partial(_combine_kernel, tc=tc)
    prev = lambda i: (jnp.maximum(i - 1, 0), 0)
    return pl.pallas_call(
        kern,
        grid=(n + 1,),
        in_specs=[pl.BlockSpec((1, TOP_K, tc), lambda i: (jnp.minimum(i, n - 1), 0, 0),
                               memory_space=pltpu.SMEM),
                  pl.BlockSpec((tc, d), prev),
                  pl.BlockSpec((TOP_K, tc), lambda i: (0, jnp.maximum(i - 1, 0))),
                  pl.BlockSpec(memory_space=pl.ANY)],
        out_specs=pl.BlockSpec((tc, d), prev),
        out_shape=SDS((t, d), F32),
        scratch_shapes=[pltpu.VMEM((2, TOP_K, tc * ns, LANES), U32),
                        pltpu.SemaphoreType.DMA((2,))],
        compiler_params=_cparams(1), name="moe_combine")(dest3, x2, gates, ys)


def _layer(x2d, mem2d, batch, seq, n_mem, p):
    t, d = x2d.shape
    c = p["conv_w"].shape[1]
    fw = p["g_fox_out"].shape[0]
    n_heads = fw // HEAD_DIM
    n_main = 3 * c + 3 * fw
    n_exp = p["w_router"].shape[1]
    assert n_heads <= F_ROWS and TOP_K <= SUBLANES and d % (2 * LANES) == 0

    w_in = p["w_mix_in"].astype(BF16)
    wf_t = jnp.zeros((F_ROWS, d), BF16).at[:n_heads].set(w_in[:, n_main:].T)
    b_col = jnp.zeros((F_ROWS, 1), F32).at[:n_heads, 0].set(p["b_forget"])

    proj, ft = _mix_in(x2d, p["g_mix"][None, :], wf_t, w_in, n_main)
    cdec = _decay(ft, b_col, seq)
    y_fox = _fox_attention(proj, cdec, p["g_q"][None, :], p["g_k"][None, :],
                           batch, seq, n_heads, 3 * c, fw)
    x1 = _mix_out(proj, y_fox, p["conv_w"], p["g_conv_out"][None, :], p["g_fox_out"][None, :],
                  p["w_mix_out"], x2d, batch, seq)

    kn, vm = _memkv(mem2d, p["g_mem"][None, :], p["w_xkv"],
                    p["g_xk"][None, :], n_mem)
    x2, hm, ti, tg, pos, cnt = _xattn(x1, p["g_xattn"][None, :], p["w_xq"],
                            p["g_xq"][None, :], kn, vm, p["w_xo"],
                            p["g_moe"][None, :], p["w_router"].T.astype(BF16),
                            p["b_router"][:, None], seq, n_mem)

    tm = _tile(TOP_K * t, 512) // 2
    counts = cnt[:, 0].astype(I32)
    seg_start, seg_end, seg, tables, n_used = _route_tables(counts, TOP_K * t, tm)
    dest = _dest_rows(seg_start, ti, pos)
    xs = _dispatch(hm, dest, seg_end, seg, n_used, 2 * tm, tables[0].shape[0])
    act = _moe_up(xs, p["w_gate_up"], p["b_gate_up"][:, None, :], *_block_tables(tables, n_used), tm)
    ys = _moe_down(act, p["w_down"], p["b_down"][:, None, :], tables, n_used, 2 * tm)
    return _combine(x2, tg, dest, ys)


def kernel(x, mem, g_mix, w_mix_in, b_forget, conv_w, g_q, g_k, g_conv_out, g_fox_out, w_mix_out, g_xattn, g_mem, w_xq, w_xkv, g_xq, g_xk, w_xo, g_moe, w_router, b_router, w_gate_up, b_gate_up, w_down, b_down):
    batch, seq, d = x.shape
    n_mem = mem.shape[1]
    params = dict(g_mix=g_mix, w_mix_in=w_mix_in, b_forget=b_forget, conv_w=conv_w, g_q=g_q,
                  g_k=g_k, g_conv_out=g_conv_out, g_fox_out=g_fox_out, w_mix_out=w_mix_out,
                  g_xattn=g_xattn, g_mem=g_mem, w_xq=w_xq, w_xkv=w_xkv, g_xq=g_xq, g_xk=g_xk,
                  w_xo=w_xo, g_moe=g_moe, w_router=w_router, b_router=b_router,
                  w_gate_up=w_gate_up, b_gate_up=b_gate_up, w_down=w_down, b_down=b_down)
    x2d = x.reshape(batch * seq, d)
    mem2d = mem.reshape(batch * n_mem, d)
    for l in range(g_mix.shape[0]):
        x2d = _layer(x2d, mem2d, batch, seq, n_mem, {k: v[l] for k, v in params.items()})
    return x2d.reshape(batch, seq, d)
```

```python
import functools

import jax
import jax.numpy as jnp
from jax import lax
from jax.experimental import pallas as pl
from jax.experimental.pallas import tpu as pltpu

F32 = jnp.float32
BF16 = jnp.bfloat16
I32 = jnp.int32
U32 = jnp.uint32
SDS = jax.ShapeDtypeStruct

EPS = 1e-6
HEAD_DIM = 128
TOP_K = 4
SWIGLU_LIMIT = 7.0
SWIGLU_ALPHA = 1.702
MASK_VALUE = float(jnp.finfo(jnp.float32).min)
V7X_VMEM_LIMIT_BYTES = 56 * 1024 * 1024
F_ROWS = 16
LANES = 128
SUBLANES = 8


def _cparams(n_axes):
    return pltpu.CompilerParams(
        dimension_semantics=("arbitrary",) * n_axes,
        vmem_limit_bytes=V7X_VMEM_LIMIT_BYTES)


def _rms(x, g):
    return x * lax.rsqrt(jnp.mean(x * x, axis=-1, keepdims=True) + EPS) * g


def _dot(a, b):
    return jnp.dot(a, b, preferred_element_type=F32)


def _dot_nt(a, b):
    return lax.dot_general(a, b, (((1,), (1,)), ((), ())), preferred_element_type=F32)


def _tile(n, pref):
    return pref if n % pref == 0 else n


def _pack_pair(lo, hi):
    lo = lax.bitcast_convert_type(lo.astype(BF16).astype(F32), U32) >> 16
    hi = lax.bitcast_convert_type(hi.astype(BF16).astype(F32), U32) & jnp.uint32(0xFFFF0000)
    return hi | lo


def _unpack_pair(w):
    lo = lax.bitcast_convert_type(w << 16, F32)
    hi = lax.bitcast_convert_type(w & jnp.uint32(0xFFFF0000), F32)
    return lo, hi


def _mix_in_kernel(x_ref, g_ref, wf_ref, w_ref, o_ref, ft_ref, h_ref):
    @pl.when(pl.program_id(1) == 0)
    def _():
        hb = _rms(x_ref[...], g_ref[...]).astype(BF16)
        h_ref[...] = hb
        ft_ref[...] = _dot_nt(wf_ref[...], hb)

    o_ref[...] = _dot(h_ref[...], w_ref[...]).astype(o_ref.dtype)


def _mix_in(x2d, g, wf_t, w_in, n_out):
    t, d = x2d.shape
    tm = _tile(t, 1024)
    tn = _tile(n_out, 2048)
    return pl.pallas_call(
        _mix_in_kernel,
        grid=(t // tm, n_out // tn),
        in_specs=[pl.BlockSpec((tm, d), lambda i, j: (i, 0)),
                  pl.BlockSpec((1, d), lambda i, j: (0, 0)),
                  pl.BlockSpec((F_ROWS, d), lambda i, j: (0, 0)),
                  pl.BlockSpec((d, tn), lambda i, j: (0, j))],
        out_specs=[pl.BlockSpec((tm, tn), lambda i, j: (i, j)),
                   pl.BlockSpec((F_ROWS, tm), lambda i, j: (0, i))],
        out_shape=[SDS((t, n_out), BF16), SDS((F_ROWS, t), F32)],
        scratch_shapes=[pltpu.VMEM((tm, d), BF16)],
        compiler_params=_cparams(2), name="mix_in")(x2d, g, wf_t, w_in)


def _decay_kernel(ft_ref, b_ref, c_ref):
    z = ft_ref[...] + b_ref[...]
    lf = jnp.minimum(z, 0.0) - jnp.log(1.0 + jnp.exp(-jnp.abs(z)))
    s = lf.shape[1]
    lane = lax.broadcasted_iota(I32, lf.shape, 1)
    d = 1
    while d < s:
        lf = lf + jnp.where(lane >= d, pltpu.roll(lf, d, 1), 0.0)
        d *= 2
    c_ref[...] = lf


def _decay(ft, b_col, seq):
    rows, t = ft.shape
    return pl.pallas_call(
        _decay_kernel,
        grid=(t // seq,),
        in_specs=[pl.BlockSpec((rows, seq), lambda b: (0, b)),
                  pl.BlockSpec((rows, 1), lambda b: (0, 0))],
        out_specs=pl.BlockSpec((rows, seq), lambda b: (0, b)),
        out_shape=SDS((rows, t), F32),
        compiler_params=_cparams(1), name="decay")(ft, b_col)


def _fox_kernel(q_ref, k_ref, v_ref, c_ref, gq_ref, gk_ref, o_ref, kn_ref, nc_ref, *, tq, hpb):
    g = pl.program_id(1)
    seq = q_ref.shape[0]
    hd = HEAD_DIM
    scale = HEAD_DIM ** -0.5
    for j in range(hpb):
        cs = slice(j * hd, (j + 1) * hd)
        kn_ref[:, cs] = _rms(k_ref[:, cs].astype(F32), gk_ref[...]).astype(BF16)
        nc_ref[j:j + 1, :] = -c_ref[pl.ds(g * hpb + j, 1), :]
    row = lax.broadcasted_iota(I32, (tq, tq), 0)
    col = lax.broadcasted_iota(I32, (tq, tq), 1)
    causal = col <= row
    for qi in range(seq // tq):
        lo, hi = qi * tq, (qi + 1) * tq
        for j in range(hpb):
            cs = slice(j * hd, (j + 1) * hd)
            q = (_rms(q_ref[lo:hi, cs].astype(F32), gq_ref[...]) * scale).astype(BF16)
            s_diag = jnp.where(causal, _dot_nt(q, kn_ref[lo:hi, cs]) + nc_ref[j:j + 1, lo:hi],
                               MASK_VALUE)
            m = jnp.max(s_diag, axis=-1, keepdims=True)
            if qi > 0:
                s_past = _dot_nt(q, kn_ref[:lo, cs]) + nc_ref[j:j + 1, :lo]
                m = jnp.maximum(m, jnp.max(s_past, axis=-1, keepdims=True))
            p_diag = jnp.exp(s_diag - m)
            l = jnp.sum(p_diag, axis=-1, keepdims=True)
            acc = _dot(p_diag.astype(BF16), v_ref[lo:hi, cs])
            if qi > 0:
                p_past = jnp.exp(s_past - m)
                l = l + jnp.sum(p_past, axis=-1, keepdims=True)
                acc = acc + _dot(p_past.astype(BF16), v_ref[:lo, cs])
            o_ref[lo:hi, cs] = (acc / l).astype(o_ref.dtype)


def _fox_attention(proj, c, g_q, g_k, batch, seq, n_heads, col0, width):
    t = proj.shape[0]
    tq = _tile(seq, 256)
    hpb = 2 if n_heads % 2 == 0 else 1
    bw = hpb * HEAD_DIM
    cb = col0 // bw
    wb = width // bw
    kern = functools.partial(_fox_kernel, tq=tq, hpb=hpb)
    return pl.pallas_call(
        kern,
        grid=(batch, n_heads // hpb),
        in_specs=[pl.BlockSpec((seq, bw), lambda b, h: (b, cb + h)),
                  pl.BlockSpec((seq, bw), lambda b, h: (b, cb + wb + h)),
                  pl.BlockSpec((seq, bw), lambda b, h: (b, cb + 2 * wb + h)),
                  pl.BlockSpec((F_ROWS, seq), lambda b, h: (0, b)),
                  pl.BlockSpec((1, HEAD_DIM), lambda b, h: (0, 0)),
                  pl.BlockSpec((1, HEAD_DIM), lambda b, h: (0, 0))],
        out_specs=pl.BlockSpec((seq, bw), lambda b, h: (b, h)),
        out_shape=SDS((t, width), BF16),
        scratch_shapes=[pltpu.VMEM((seq, bw), BF16), pltpu.VMEM((SUBLANES, seq), F32)],
        compiler_params=_cparams(2), name="fox_attention")(proj, proj, proj, c, g_q, g_k)


def _mix_out_kernel(ub_ref, uc_ref, ux_ref, hc_ref, hx_ref, yf_ref, cw_ref, gc_ref, gf_ref,
                    w_ref, x_ref, o_ref):
    si = pl.program_id(1)
    p = uc_ref[...].astype(F32) * ux_ref[...].astype(F32)
    halo = hc_ref[...].astype(F32) * hx_ref[...].astype(F32)
    halo = jnp.where(si > 0, halo, 0.0)
    row = lax.broadcasted_iota(I32, p.shape, 0)
    last, last2 = halo[SUBLANES - 1:SUBLANES, :], halo[SUBLANES - 2:SUBLANES - 1, :]
    p1 = jnp.where(row == 0, last, pltpu.roll(p, 1, 0))
    p2 = jnp.where(row == 0, last2, jnp.where(row == 1, last, pltpu.roll(p, 2, 0)))
    w = cw_ref[...]
    y = ub_ref[...].astype(F32) * (w[0:1, :] * p2 + w[1:2, :] * p1 + w[2:3, :] * p)
    c = y.shape[1]
    y_conv = _rms(y, gc_ref[...]).astype(BF16)
    y_fox = _rms(yf_ref[...].astype(F32), gf_ref[...]).astype(BF16)
    o_ref[...] = (x_ref[...] + _dot(y_conv, w_ref[:c, :].astype(BF16))
                  + _dot(y_fox, w_ref[c:, :].astype(BF16)))


def _mix_out(proj, y_fox, conv_w, g_conv, g_fox, w_out, x2d, batch, seq):
    t, d = x2d.shape
    c = conv_w.shape[1]
    w = y_fox.shape[1]
    ts = _tile(seq, 512)
    ns = seq // ts
    hb = ts // SUBLANES
    tile = lambda col: (lambda b, s: (b * ns + s, col))
    halo = lambda col: (lambda b, s: (jnp.maximum((b * ns + s) * hb - 1, 0), col))
    const = lambda b, s: (0, 0)
    return pl.pallas_call(
        _mix_out_kernel,
        grid=(batch, ns),
        in_specs=[pl.BlockSpec((ts, c), tile(0)),
                  pl.BlockSpec((ts, c), tile(1)),
                  pl.BlockSpec((ts, c), tile(2)),
                  pl.BlockSpec((SUBLANES, c), halo(1)),
                  pl.BlockSpec((SUBLANES, c), halo(2)),
                  pl.BlockSpec((ts, w), tile(0)),
                  pl.BlockSpec((3, c), const),
                  pl.BlockSpec((1, c), const),
                  pl.BlockSpec((1, w), const),
                  pl.BlockSpec((c + w, d), const, pipeline_mode=pl.Buffered(1)),
                  pl.BlockSpec((ts, d), tile(0))],
        out_specs=pl.BlockSpec((ts, d), tile(0)),
        out_shape=SDS((t, d), F32),
        compiler_params=_cparams(2), name="mix_out")(
            proj, proj, proj, proj, proj, y_fox, conv_w, g_conv, g_fox, w_out, x2d)


def _memkv_kernel(mem_ref, g_ref, w_ref, gk_ref, k_ref, v_ref):
    hm = _rms(mem_ref[...], g_ref[...]).astype(BF16)
    kv = _dot(hm, w_ref[...].astype(BF16))
    xw = k_ref.shape[1]
    for h in range(xw // HEAD_DIM):
        sl = slice(h * HEAD_DIM, (h + 1) * HEAD_DIM)
        k_ref[:, sl] = _rms(kv[:, sl], gk_ref[...]).astype(BF16)
    v_ref[...] = kv[:, xw:].astype(BF16)


def _memkv(mem2d, g_mem, w_xkv, g_xk, n_mem):
    tm_, d = mem2d.shape
    xw = w_xkv.shape[1] // 2
    return pl.pallas_call(
        _memkv_kernel,
        grid=(tm_ // n_mem,),
        in_specs=[pl.BlockSpec((n_mem, d), lambda b: (b, 0)),
                  pl.BlockSpec((1, d), lambda b: (0, 0)),
                  pl.BlockSpec((d, 2 * xw), lambda b: (0, 0)),
                  pl.BlockSpec((1, HEAD_DIM), lambda b: (0, 0))],
        out_specs=[pl.BlockSpec((n_mem, xw), lambda b: (b, 0)),
                   pl.BlockSpec((n_mem, xw), lambda b: (b, 0))],
        out_shape=[SDS((tm_, xw), BF16), SDS((tm_, xw), BF16)],
        compiler_params=_cparams(1), name="memkv")(mem2d, g_mem, w_xkv, g_xk)


def _xattn_kernel(x_ref, gx_ref, wq_ref, gq_ref, kn_ref, v_ref, wo_ref, gm_ref, wr_ref, br_ref,
                  x2_ref, hm_ref, ti_ref, tg_ref, pos_ref, cnt_ref, run_ref, tri_ref):
    i = pl.program_id(0)
    tm = x_ref.shape[0]

    @pl.when(i == 0)
    def _():
        run_ref[...] = jnp.zeros(run_ref.shape, F32)
        r = lax.broadcasted_iota(I32, (tm, tm), 0)
        c = lax.broadcasted_iota(I32, (tm, tm), 1)
        tri_ref[...] = jnp.where(r <= c, 1.0, 0.0).astype(BF16)

    x = x_ref[...]
    hb = _rms(x, gx_ref[...]).astype(BF16)
    q = _dot(hb, wq_ref[...].astype(BF16))
    scale = HEAD_DIM ** -0.5
    outs = []
    for h in range(q.shape[1] // HEAD_DIM):
        sl = slice(h * HEAD_DIM, (h + 1) * HEAD_DIM)
        qh = (_rms(q[:, sl], gq_ref[...]) * scale).astype(BF16)
        s = _dot_nt(qh, kn_ref[:, sl])
        p = jnp.exp(s - jnp.max(s, axis=-1, keepdims=True))
        oh = _dot(p.astype(BF16), v_ref[:, sl]) / jnp.sum(p, axis=-1, keepdims=True)
        outs.append(oh.astype(BF16))
    x2 = x + _dot(jnp.concatenate(outs, axis=-1), wo_ref[...].astype(BF16))
    x2_ref[...] = x2
    hm = _rms(x2, gm_ref[...]).astype(BF16)
    hm_ref[...] = hm
    lg = _dot_nt(wr_ref[...], hm) + br_ref[...]
    n_exp = lg.shape[0]
    eidx = lax.broadcasted_iota(I32, lg.shape, 0).astype(F32)
    vals, sel = [], []
    for r in range(TOP_K):
        mx = jnp.max(lg, axis=0, keepdims=True)
        am = jnp.min(jnp.where(lg == mx, eidx, float(n_exp)), axis=0, keepdims=True)
        ti_ref[r:r + 1, :] = am.astype(I32)
        vals.append(mx)
        sel.append(eidx == am)
        lg = jnp.where(sel[r], -jnp.inf, lg)
    ex = [jnp.exp(v - vals[0]) for v in vals]
    den = ex[0]
    for e in ex[1:]:
        den = den + e
    for r in range(TOP_K):
        tg_ref[r:r + 1, :] = ex[r] / den
    onehot = jnp.zeros(lg.shape, F32)
    for r in range(TOP_K):
        onehot = onehot + jnp.where(sel[r], 1.0, 0.0)
    incl = _dot(onehot.astype(BF16), tri_ref[...])
    excl = incl - onehot + run_ref[...]
    for r in range(TOP_K):
        pos_ref[r:r + 1, :] = jnp.sum(jnp.where(sel[r], excl, 0.0), axis=0,
                                      keepdims=True).astype(I32)
    run = run_ref[...] + incl[:, tm - 1:tm]
    run_ref[...] = run
    cnt_ref[...] = jnp.broadcast_to(run, cnt_ref.shape)


def _xattn(x1, g_xattn, w_xq, g_xq, kn, vm, w_xo, g_moe, w_rt, b_r, seq, n_mem):
    t, d = x1.shape
    xw = w_xq.shape[1]
    n_exp = w_rt.shape[0]
    tm = _tile(seq, 512)
    per = seq // tm
    const = lambda i: (0, 0)
    return pl.pallas_call(
        _xattn_kernel,
        grid=(t // tm,),
        in_specs=[pl.BlockSpec((tm, d), lambda i: (i, 0)),
                  pl.BlockSpec((1, d), const),
                  pl.BlockSpec((d, xw), const),
                  pl.BlockSpec((1, HEAD_DIM), const),
                  pl.BlockSpec((n_mem, xw), lambda i: (i // per, 0)),
                  pl.BlockSpec((n_mem, xw), lambda i: (i // per, 0)),
                  pl.BlockSpec((xw, d), const),
                  pl.BlockSpec((1, d), const),
                  pl.BlockSpec((n_exp, d), const),
                  pl.BlockSpec((n_exp, 1), const)],
        out_specs=[pl.BlockSpec((tm, d), lambda i: (i, 0)),
                   pl.BlockSpec((tm, d), lambda i: (i, 0)),
                   pl.BlockSpec((TOP_K, tm), lambda i: (0, i)),
                   pl.BlockSpec((TOP_K, tm), lambda i: (0, i)),
                   pl.BlockSpec((TOP_K, tm), lambda i: (0, i)),
                   pl.BlockSpec((n_exp, 128), const)],
        out_shape=[SDS((t, d), F32), SDS((t, d), BF16),
                   SDS((TOP_K, t), I32), SDS((TOP_K, t), F32), SDS((TOP_K, t), I32),
                   SDS((n_exp, 128), F32)],
        scratch_shapes=[pltpu.VMEM((n_exp, 1), F32), pltpu.VMEM((tm, tm), BF16)],
        compiler_params=_cparams(1), name="xattn_router")(
            x1, g_xattn, w_xq, g_xq, kn, vm, w_xo, g_moe, w_rt, b_r)


def _take(values, idx):
    pick = idx[:, None] == jnp.arange(values.shape[0], dtype=I32)[None, :]
    return jnp.sum(jnp.where(pick, values[None, :], 0), axis=1).astype(I32)


def _route_tables(counts, n_assign, tm):
    n_exp = counts.shape[0]
    ts = 2 * tm
    valid = (counts + tm - 1) // tm * tm
    seg = ((counts + ts - 1) // ts * ts).astype(I32)
    seg_end = jnp.cumsum(seg).astype(I32)
    seg_start = seg_end - seg
    n_steps = n_assign // ts + n_exp
    n_used = seg_end[-1] // ts
    idx = jnp.arange(n_steps, dtype=I32)
    row0 = jnp.minimum(idx, n_used - 1) * ts
    exp = jnp.sum((row0[:, None] >= seg_end[None, :]).astype(I32), axis=1)
    exp = jnp.minimum(exp, n_exp - 1)
    n_valid = jnp.clip((_take(seg_start + valid, exp) - row0) // tm, 1, 2).astype(I32)
    prev = jnp.concatenate([jnp.full((1,), -1, I32), exp[:-1]])
    first = (exp != prev).astype(I32)
    rem = _take(seg_end, exp) // ts - idx
    cand = jnp.where(counts > 0, jnp.arange(n_exp, dtype=I32), n_exp)
    after = jnp.concatenate([lax.cummin(cand, reverse=True)[1:], jnp.full((1,), n_exp, I32)])
    nxt = _take(jnp.where(after >= n_exp, -1, after), exp)
    tables = (exp, first, rem.astype(I32), nxt, n_valid)
    return seg_start, seg_end, seg, tables, n_used.reshape(1).astype(I32)


def _block_tables(tables, n_used):
    exp, first, rem, nxt, n_valid = tables
    half = jnp.tile(jnp.arange(2, dtype=I32), exp.shape[0])
    rep = lambda v: jnp.repeat(v, 2)
    return (rep(exp), rep(first) * (1 - half), 2 * rep(rem) - half, rep(nxt),
            (half < rep(n_valid)).astype(I32)), 2 * n_used


def _dest_kernel(ps_ref, ti_ref, pos_ref, o_ref, *, n_exp):
    ti = ti_ref[...]
    acc = pos_ref[...]
    for e in range(n_exp):
        acc = acc + jnp.where(ti == e, ps_ref[e], 0)
    o_ref[...] = acc


def _dest_rows(pad_start, ti, pos):
    k, t = ti.shape
    kern = functools.partial(_dest_kernel, n_exp=pad_start.shape[0])
    grid_spec = pltpu.PrefetchScalarGridSpec(
        num_scalar_prefetch=1, grid=(1,),
        in_specs=[pl.BlockSpec((k, t), lambda i, ps: (0, 0)),
                  pl.BlockSpec((k, t), lambda i, ps: (0, 0))],
        out_specs=pl.BlockSpec((k, t), lambda i, ps: (0, 0)))
    return pl.pallas_call(kern, grid_spec=grid_spec, out_shape=SDS((k, t), I32),
                          compiler_params=_cparams(1), name="moe_dest")(pad_start, ti, pos)


def _dispatch_kernel(pe_ref, pc_ref, nu_ref, dst_ref, x_ref, xs_ref, buf, zbuf, sem, zsem,
                     *, tc, ts, n_exp, n_steps):
    i = pl.program_id(0)
    n = pl.num_programs(0)
    ns = x_ref.shape[1] // (2 * LANES)

    def zero_copy(row0):
        rows = pl.ds(pl.multiple_of(row0 * ns, ts * ns), ts * ns)
        return pltpu.make_async_copy(zbuf, xs_ref.at[rows], zsem)

    @pl.when(i == 0)
    def _():
        zbuf[...] = jnp.zeros(zbuf.shape, zbuf.dtype)
        for start in (True, False):
            for e in range(n_exp):
                @pl.when(pc_ref[e] > 0)
                def _(e=e):
                    cp = zero_copy(pe_ref[e] - ts)
                    cp.start() if start else cp.wait()

            def tail(b, c):
                cp = zero_copy(b * ts)
                cp.start() if start else cp.wait()
                return c
            lax.fori_loop(nu_ref[0], n_steps, tail, 0)

    slot = i % 2
    x = x_ref[...]
    for s in range(ns):
        lo = x[:, (2 * s) * LANES:(2 * s + 1) * LANES]
        hi = x[:, (2 * s + 1) * LANES:(2 * s + 2) * LANES]
        buf[slot, pl.ds(s, tc, stride=ns), :] = _pack_pair(lo, hi)
    for k in range(TOP_K):
        for r in range(tc):
            dst = pl.ds(pl.multiple_of(dst_ref[0, k, r] * ns, ns), ns)
            pltpu.make_async_copy(buf.at[slot, pl.ds(r * ns, ns)], xs_ref.at[dst],
                                  sem.at[slot]).start(priority=r % 2)

    def wait_slot(s):
        for k in range(TOP_K):
            pltpu.make_async_copy(buf.at[s], xs_ref.at[pl.ds(0, tc * ns)], sem.at[s]).wait()

    @pl.when(i >= 1)
    def _():
        wait_slot(1 - slot)

    @pl.when(i == n - 1)
    def _():
        wait_slot(slot)


def _dispatch(hm, dest, seg_end, seg, n_used, ts, n_steps):
    t, d = hm.shape
    tc = _tile(t, 256)
    n = t // tc
    ns = d // (2 * LANES)
    n_exp = seg_end.shape[0]
    dest3 = dest.reshape(TOP_K, n, tc).transpose(1, 0, 2)
    kern = functools.partial(_dispatch_kernel, tc=tc, ts=ts, n_exp=n_exp, n_steps=n_steps)
    grid_spec = pltpu.PrefetchScalarGridSpec(
        num_scalar_prefetch=3, grid=(n,),
        in_specs=[pl.BlockSpec((1, TOP_K, tc), lambda i, *_: (i, 0, 0), memory_space=pltpu.SMEM),
                  pl.BlockSpec((tc, d), lambda i, *_: (i, 0))],
        out_specs=pl.BlockSpec(memory_space=pl.ANY),
        scratch_shapes=[pltpu.VMEM((2, tc * ns, LANES), U32),
                        pltpu.VMEM((ts * ns, LANES), U32),
                        pltpu.SemaphoreType.DMA((2,)),
                        pltpu.SemaphoreType.DMA(())])
    return pl.pallas_call(
        kern, grid_spec=grid_spec,
        out_shape=SDS((n_steps * ts * ns, LANES), U32),
        compiler_params=_cparams(1), name="moe_dispatch")(seg_end, seg, n_used, dest3, hm)


W_CHUNK = 512


def _stream_weights(i, be_ref, first_ref, rem_ref, nxt_ref, w_hbm, wb_ref, stage_ref, st_ref, sem):
    n_chunks, _, chunk = wb_ref.shape[1:]

    def chunk_copy(e, c, slot):
        cols = pl.ds(pl.multiple_of(c * chunk, chunk), chunk)
        return pltpu.make_async_copy(w_hbm.at[e, :, cols], stage_ref.at[slot], sem.at[slot])

    def start_first_two(e):
        chunk_copy(e, 0, 0).start()
        chunk_copy(e, 1, 1).start()

    def cast_chunks(e, dst, done, issued, k):
        def body(_, carry):
            done, issued = carry
            slot = done % 2
            chunk_copy(e, done, slot).wait()
            wb_ref[dst, done] = stage_ref[slot].astype(BF16)

            @pl.when(issued < n_chunks)
            def _():
                chunk_copy(e, issued, slot).start()
            return done + 1, jnp.minimum(issued + 1, n_chunks)
        return lax.fori_loop(0, k, body, (done, issued))

    @pl.when(i == 0)
    def _():
        start_first_two(be_ref[0])
        cast_chunks(be_ref[0], 0, 0, 2, n_chunks)
        st_ref[0] = 1

    @pl.when(first_ref[i] == 1)
    def _():
        st_ref[0] = 1 - st_ref[0]
        st_ref[1] = 0
        st_ref[2] = 0

        @pl.when(nxt_ref[i] >= 0)
        def _():
            start_first_two(nxt_ref[i])
            st_ref[2] = 2

    cur = st_ref[0]

    @pl.when(nxt_ref[i] >= 0)
    def _():
        done = st_ref[1]
        rem = rem_ref[i]
        lead = (first_ref[i] == 1) & (rem > 1)
        left = jnp.where(first_ref[i] == 1, jnp.maximum(rem - 1, 1), rem)
        k = jnp.where(lead, 0, lax.div(n_chunks - done + left - 1, left))
        done, issued = cast_chunks(nxt_ref[i], 1 - cur, done, st_ref[2], k)
        st_ref[1] = done
        st_ref[2] = issued

    return cur


def _moe_up_kernel(be_ref, first_ref, rem_ref, nxt_ref, ok_ref, nu_ref, x_ref, b_ref, w_hbm, o_ref,
                   wb_ref, stage_ref, st_ref, sem):
    i = pl.program_id(0)
    tm = o_ref.shape[0]
    ns = x_ref.shape[0] // tm
    n_half = wb_ref.shape[1] // 2
    chunk = wb_ref.shape[3]
    f = n_half * chunk

    @pl.when(i < nu_ref[0])
    def _():
        cur = _stream_weights(i, be_ref, first_ref, rem_ref, nxt_ref, w_hbm, wb_ref, stage_ref,
                              st_ref, sem)

        @pl.when(ok_ref[i] == 1)
        def _():
            cols = []
            for s in range(ns):
                lo, hi = _unpack_pair(x_ref[pl.ds(s, tm, stride=ns), :])
                cols += [lo.astype(BF16), hi.astype(BF16)]
            x = jnp.concatenate(cols, axis=1)
            for n in range(n_half):
                lo, hi = n * chunk, (n + 1) * chunk
                gate = jnp.minimum(_dot(x, wb_ref[cur, n]) + b_ref[:, lo:hi], SWIGLU_LIMIT)
                up = jnp.clip(_dot(x, wb_ref[cur, n_half + n]) + b_ref[:, f + lo:f + hi],
                              -SWIGLU_LIMIT, SWIGLU_LIMIT)
                glu = gate * jax.nn.sigmoid(SWIGLU_ALPHA * gate)
                o_ref[:, lo:hi] = ((up + 1.0) * glu).astype(o_ref.dtype)

    @pl.when((i >= nu_ref[0]) | (ok_ref[i] == 0))
    def _():
        o_ref[...] = jnp.zeros(o_ref.shape, o_ref.dtype)


def _moe_up(xs, w_gu, b_gu, tables, n_used, tm):
    d = w_gu.shape[1]
    ns = d // (2 * LANES)
    n_rows = xs.shape[0] // ns
    two_f = w_gu.shape[2]
    chunk = min(W_CHUNK, two_f // 2)
    grid_spec = pltpu.PrefetchScalarGridSpec(
        num_scalar_prefetch=6,
        grid=(n_rows // tm,),
        in_specs=[pl.BlockSpec((tm * ns, LANES),
                               lambda i, be, fi, re, nx, ok, nu: (jnp.minimum(i, nu[0] - 1), 0)),
                  pl.BlockSpec((None, 1, two_f), lambda i, be, fi, re, nx, ok, nu: (be[i], 0, 0)),
                  pl.BlockSpec(memory_space=pl.ANY)],
        out_specs=pl.BlockSpec((tm, two_f // 2), lambda i, *_: (i, 0)),
        scratch_shapes=[pltpu.VMEM((2, two_f // chunk, d, chunk), BF16),
                        pltpu.VMEM((2, d, chunk), w_gu.dtype),
                        pltpu.SMEM((3,), I32),
                        pltpu.SemaphoreType.DMA((2,))])
    return pl.pallas_call(
        _moe_up_kernel, grid_spec=grid_spec,
        out_shape=SDS((n_rows, two_f // 2), BF16),
        compiler_params=_cparams(1), name="moe_up")(*tables, n_used, xs, b_gu, w_gu)


def _moe_down_kernel(be_ref, first_ref, rem_ref, nxt_ref, nv_ref, nu_ref, a_ref, b_ref, w_hbm, o_ref,
                     wf_ref, st_ref, sem):
    i = pl.program_id(0)
    ts = a_ref.shape[0]
    ns = o_ref.shape[0] // ts
    n_chunks, _, chunk = wf_ref.shape[1:]

    def chunk_copy(e, dst, c):
        return pltpu.make_async_copy(w_hbm.at[e, :, pl.ds(c * chunk, chunk)], wf_ref.at[dst, c], sem)

    def compute(cur, rows):
        a = a_ref[:rows, :]
        for n in range(n_chunks):
            lo, hi = n * chunk, (n + 1) * chunk
            y = _dot(a, wf_ref[cur, n].astype(BF16)) + b_ref[:, lo:hi]
            for j in range(chunk // (2 * LANES)):
                s = lo // (2 * LANES) + j
                w = _pack_pair(y[:, (2 * j) * LANES:(2 * j + 1) * LANES],
                               y[:, (2 * j + 1) * LANES:(2 * j + 2) * LANES])
                o_ref[pl.ds(s, rows, stride=ns), :] = w

    @pl.when(i < nu_ref[0])
    def _():
        @pl.when(i == 0)
        def _():
            for c in range(n_chunks):
                chunk_copy(be_ref[0], 0, c).start()
            st_ref[0] = 1
            st_ref[1] = 1

        @pl.when(first_ref[i] == 1)
        def _():
            @pl.when(st_ref[1] == 1)
            def _():
                for c in range(n_chunks):
                    chunk_copy(be_ref[i], 1 - st_ref[0], c).wait()
            st_ref[0] = 1 - st_ref[0]
            st_ref[1] = 0

            @pl.when(nxt_ref[i] >= 0)
            def _():
                for c in range(n_chunks):
                    chunk_copy(nxt_ref[i], 1 - st_ref[0], c).start()
                st_ref[1] = 1

        cur = st_ref[0]

        @pl.when(nv_ref[i] == 2)
        def _():
            compute(cur, ts)

        @pl.when(nv_ref[i] == 1)
        def _():
            compute(cur, ts // 2)
            half = ts // 2 * ns
            o_ref[half:, :] = jnp.zeros((half, LANES), o_ref.dtype)

    @pl.when(i >= nu_ref[0])
    def _():
        o_ref[...] = jnp.zeros(o_ref.shape, o_ref.dtype)


def _moe_down(act, w_dn, b_dn, tables, n_used, ts):
    n_rows, f = act.shape
    d = w_dn.shape[2]
    ns = d // (2 * LANES)
    chunk = max(min(W_CHUNK, d // 2), 2 * LANES)
    grid_spec = pltpu.PrefetchScalarGridSpec(
        num_scalar_prefetch=6,
        grid=(n_rows // ts,),
        in_specs=[pl.BlockSpec((ts, f),
                               lambda i, be, fi, re, nx, nv, nu: (jnp.minimum(i, nu[0] - 1), 0)),
                  pl.BlockSpec((None, 1, d), lambda i, be, fi, re, nx, nv, nu: (be[i], 0, 0)),
                  pl.BlockSpec(memory_space=pl.ANY)],
        out_specs=pl.BlockSpec((ts * ns, LANES), lambda i, *_: (i, 0)),
        scratch_shapes=[pltpu.VMEM((2, d // chunk, f, chunk), w_dn.dtype),
                        pltpu.SMEM((2,), I32),
                        pltpu.SemaphoreType.DMA(())])
    return pl.pallas_call(
        _moe_down_kernel, grid_spec=grid_spec,
        out_shape=SDS((n_rows * ns, LANES), U32),
        compiler_params=_cparams(1), name="moe_down")(*tables, n_used, act, b_dn, w_dn)


def _combine_kernel(dst_ref, x_ref, g_ref, ys_ref, o_ref, buf, sem, *, tc):
    i = pl.program_id(0)
    n = pl.num_programs(0) - 1
    ns = x_ref.shape[1] // (2 * LANES)

    @pl.when(i < n)
    def _():
        slot = i % 2
        for k in range(TOP_K):
            for r in range(tc):
                src = pl.ds(pl.multiple_of(dst_ref[0, k, r] * ns, ns), ns)
                pltpu.make_async_copy(ys_ref.at[src], buf.at[slot, k, pl.ds(r * ns, ns)],
                                      sem.at[slot]).start(priority=r % 2)

    @pl.when(i >= 1)
    def _():
        slot = (i - 1) % 2
        for k in range(TOP_K):
            pltpu.make_async_copy(ys_ref.at[pl.ds(0, tc * ns)], buf.at[slot, k],
                                  sem.at[slot]).wait()
        g = g_ref[...]
        g = jnp.concatenate([g, jnp.zeros((SUBLANES - TOP_K, tc), F32)], axis=0).T
        for s in range(ns):
            c0, c1, c2 = (2 * s) * LANES, (2 * s + 1) * LANES, (2 * s + 2) * LANES
            acc_lo = x_ref[:, c0:c1]
            acc_hi = x_ref[:, c1:c2]
            for k in range(TOP_K):
                lo, hi = _unpack_pair(buf[slot, k, pl.ds(s, tc, stride=ns), :])
                acc_lo = acc_lo + g[:, k:k + 1] * lo
                acc_hi = acc_hi + g[:, k:k + 1] * hi
            o_ref[:, c0:c1] = acc_lo
            o_ref[:, c1:c2] = acc_hi


def _combine(x2, gates, dest, ys):
    t, d = x2.shape
    tc = _tile(t, 256)
    n = t // tc
    ns = d // (2 * LANES)
    dest3 = dest.reshape(TOP_K, n, tc).transpose(1, 0, 2)
    kern = functools.partial(_combine_kernel, tc=tc)
    prev = lambda i: (jnp.maximum(i - 1, 0), 0)
    return pl.pallas_call(
        kern,
        grid=(n + 1,),
        in_specs=[pl.BlockSpec((1, TOP_K, tc), lambda i: (jnp.minimum(i, n - 1), 0, 0),
                               memory_space=pltpu.SMEM),
                  pl.BlockSpec((tc, d), prev),
                  pl.BlockSpec((TOP_K, tc), lambda i: (0, jnp.maximum(i - 1, 0))),
                  pl.BlockSpec(memory_space=pl.ANY)],
        out_specs=pl.BlockSpec((tc, d), prev),
        out_shape=SDS((t, d), F32),
        scratch_shapes=[pltpu.VMEM((2, TOP_K, tc * ns, LANES), U32),
                        pltpu.SemaphoreType.DMA((2,))],
        compiler_params=_cparams(1), name="moe_combine")(dest3, x2, gates, ys)


def _layer(x2d, mem2d, batch, seq, n_mem, p):
    t, d = x2d.shape
    c = p["conv_w"].shape[1]
    fw = p["g_fox_out"].shape[0]
    n_heads = fw // HEAD_DIM
    n_main = 3 * c + 3 * fw
    n_exp = p["w_router"].shape[1]
    assert n_heads <= F_ROWS and TOP_K <= SUBLANES and d % (2 * LANES) == 0

    w_in = p["w_mix_in"].astype(BF16)
    wf_t = jnp.zeros((F_ROWS, d), BF16).at[:n_heads].set(w_in[:, n_main:].T)
    b_col = jnp.zeros((F_ROWS, 1), F32).at[:n_heads, 0].set(p["b_forget"])

    proj, ft = _mix_in(x2d, p["g_mix"][None, :], wf_t, w_in, n_main)
    cdec = _decay(ft, b_col, seq)
    y_fox = _fox_attention(proj, cdec, p["g_q"][None, :], p["g_k"][None, :],
                           batch, seq, n_heads, 3 * c, fw)
    x1 = _mix_out(proj, y_fox, p["conv_w"], p["g_conv_out"][None, :], p["g_fox_out"][None, :],
                  p["w_mix_out"], x2d, batch, seq)

    kn, vm = _memkv(mem2d, p["g_mem"][None, :], p["w_xkv"],
                    p["g_xk"][None, :], n_mem)
    x2, hm, ti, tg, pos, cnt = _xattn(x1, p["g_xattn"][None, :], p["w_xq"],
                            p["g_xq"][None, :], kn, vm, p["w_xo"],
                            p["g_moe"][None, :], p["w_router"].T.astype(BF16),
                            p["b_router"][:, None], seq, n_mem)

    tm = _tile(TOP_K * t, 512) // 2
    counts = cnt[:, 0].astype(I32)
    seg_start, seg_end, seg, tables, n_used = _route_tables(counts, TOP_K * t, tm)
    dest = _dest_rows(seg_start, ti, pos)
    xs = _dispatch(hm, dest, seg_end, seg, n_used, 2 * tm, tables[0].shape[0])
    act = _moe_up(xs, p["w_gate_up"], p["b_gate_up"][:, None, :], *_block_tables(tables, n_used), tm)
    ys = _moe_down(act, p["w_down"], p["b_down"][:, None, :], tables, n_used, 2 * tm)
    return _combine(x2, tg, dest, ys)


def kernel(x, mem, g_mix, w_mix_in, b_forget, conv_w, g_q, g_k, g_conv_out, g_fox_out, w_mix_out, g_xattn, g_mem, w_xq, w_xkv, g_xq, g_xk, w_xo, g_moe, w_router, b_router, w_gate_up, b_gate_up, w_down, b_down):
    batch, seq, d = x.shape
    n_mem = mem.shape[1]
    params = dict(g_mix=g_mix, w_mix_in=w_mix_in, b_forget=b_forget, conv_w=conv_w, g_q=g_q,
                  g_k=g_k, g_conv_out=g_conv_out, g_fox_out=g_fox_out, w_mix_out=w_mix_out,
                  g_xattn=g_xattn, g_mem=g_mem, w_xq=w_xq, w_xkv=w_xkv, g_xq=g_xq, g_xk=g_xk,
                  w_xo=w_xo, g_moe=g_moe, w_router=w_router, b_router=b_router,
                  w_gate_up=w_gate_up, b_gate_up=b_gate_up, w_down=w_down, b_down=b_down)
    x2d = x.reshape(batch * seq, d)
    mem2d = mem.reshape(batch * n_mem, d)
    for l in range(g_mix.shape[0]):
        x2d = _layer(x2d, mem2d, batch, seq, n_mem, {k: v[l] for k, v in params.items()})
    return x2d.reshape(batch, seq, d)
```

```python
import functools

import jax
import jax.numpy as jnp
from jax import lax
from jax.experimental import pallas as pl
from jax.experimental.pallas import tpu as pltpu

F32 = jnp.float32
BF16 = jnp.bfloat16
I32 = jnp.int32
U32 = jnp.uint32
SDS = jax.ShapeDtypeStruct

EPS = 1e-6
HEAD_DIM = 128
TOP_K = 4
SWIGLU_LIMIT = 7.0
SWIGLU_ALPHA = 1.702
MASK_VALUE = float(jnp.finfo(jnp.float32).min)
V7X_VMEM_LIMIT_BYTES = 56 * 1024 * 1024
F_ROWS = 16
LANES = 128
SUBLANES = 8


def _cparams(n_axes):
    return pltpu.CompilerParams(
        dimension_semantics=("arbitrary",) * n_axes,
        vmem_limit_bytes=V7X_VMEM_LIMIT_BYTES)


def _rms(x, g):
    return x * lax.rsqrt(jnp.mean(x * x, axis=-1, keepdims=True) + EPS) * g


def _dot(a, b):
    return jnp.dot(a, b, preferred_element_type=F32)


def _dot_nt(a, b):
    return lax.dot_general(a, b, (((1,), (1,)), ((), ())), preferred_element_type=F32)


def _tile(n, pref):
    return pref if n % pref == 0 else n


def _pack_pair(lo, hi):
    lo = lax.bitcast_convert_type(lo.astype(BF16).astype(F32), U32) >> 16
    hi = lax.bitcast_convert_type(hi.astype(BF16).astype(F32), U32) & jnp.uint32(0xFFFF0000)
    return hi | lo


def _unpack_pair(w):
    lo = lax.bitcast_convert_type(w << 16, F32)
    hi = lax.bitcast_convert_type(w & jnp.uint32(0xFFFF0000), F32)
    return lo, hi


def _mix_in_kernel(x_ref, g_ref, wf_ref, w_ref, o_ref, ft_ref, h_ref):
    @pl.when(pl.program_id(1) == 0)
    def _():
        hb = _rms(x_ref[...], g_ref[...]).astype(BF16)
        h_ref[...] = hb
        ft_ref[...] = _dot_nt(wf_ref[...], hb)

    o_ref[...] = _dot(h_ref[...], w_ref[...]).astype(o_ref.dtype)


def _mix_in(x2d, g, wf_t, w_in, n_out):
    t, d = x2d.shape
    tm = _tile(t, 1024)
    tn = _tile(n_out, 2048)
    return pl.pallas_call(
        _mix_in_kernel,
        grid=(t // tm, n_out // tn),
        in_specs=[pl.BlockSpec((tm, d), lambda i, j: (i, 0)),
                  pl.BlockSpec((1, d), lambda i, j: (0, 0)),
                  pl.BlockSpec((F_ROWS, d), lambda i, j: (0, 0)),
                  pl.BlockSpec((d, tn), lambda i, j: (0, j))],
        out_specs=[pl.BlockSpec((tm, tn), lambda i, j: (i, j)),
                   pl.BlockSpec((F_ROWS, tm), lambda i, j: (0, i))],
        out_shape=[SDS((t, n_out), BF16), SDS((F_ROWS, t), F32)],
        scratch_shapes=[pltpu.VMEM((tm, d), BF16)],
        compiler_params=_cparams(2), name="mix_in")(x2d, g, wf_t, w_in)


def _decay_kernel(ft_ref, b_ref, c_ref):
    z = ft_ref[...] + b_ref[...]
    lf = jnp.minimum(z, 0.0) - jnp.log(1.0 + jnp.exp(-jnp.abs(z)))
    s = lf.shape[1]
    lane = lax.broadcasted_iota(I32, lf.shape, 1)
    d = 1
    while d < s:
        lf = lf + jnp.where(lane >= d, pltpu.roll(lf, d, 1), 0.0)
        d *= 2
    c_ref[...] = lf


def _decay(ft, b_col, seq):
    rows, t = ft.shape
    return pl.pallas_call(
        _decay_kernel,
        grid=(t // seq,),
        in_specs=[pl.BlockSpec((rows, seq), lambda b: (0, b)),
                  pl.BlockSpec((rows, 1), lambda b: (0, 0))],
        out_specs=pl.BlockSpec((rows, seq), lambda b: (0, b)),
        out_shape=SDS((rows, t), F32),
        compiler_params=_cparams(1), name="decay")(ft, b_col)


def _fox_kernel(q_ref, k_ref, v_ref, c_ref, gq_ref, gk_ref, o_ref, kn_ref, nc_ref, *, tq, hpb):
    g = pl.program_id(1)
    seq = q_ref.shape[0]
    hd = HEAD_DIM
    scale = HEAD_DIM ** -0.5
    for j in range(hpb):
        cs = slice(j * hd, (j + 1) * hd)
        kn_ref[:, cs] = _rms(k_ref[:, cs].astype(F32), gk_ref[...]).astype(BF16)
        nc_ref[j:j + 1, :] = -c_ref[pl.ds(g * hpb + j, 1), :]
    row = lax.broadcasted_iota(I32, (tq, tq), 0)
    col = lax.broadcasted_iota(I32, (tq, tq), 1)
    causal = col <= row
    for qi in range(seq // tq):
        lo, hi = qi * tq, (qi + 1) * tq
        for j in range(hpb):
            cs = slice(j * hd, (j + 1) * hd)
            q = (_rms(q_ref[lo:hi, cs].astype(F32), gq_ref[...]) * scale).astype(BF16)
            s_diag = jnp.where(causal, _dot_nt(q, kn_ref[lo:hi, cs]) + nc_ref[j:j + 1, lo:hi],
                               MASK_VALUE)
            m = jnp.max(s_diag, axis=-1, keepdims=True)
            if qi > 0:
                s_past = _dot_nt(q, kn_ref[:lo, cs]) + nc_ref[j:j + 1, :lo]
                m = jnp.maximum(m, jnp.max(s_past, axis=-1, keepdims=True))
            p_diag = jnp.exp(s_diag - m)
            l = jnp.sum(p_diag, axis=-1, keepdims=True)
            acc = _dot(p_diag.astype(BF16), v_ref[lo:hi, cs])
            if qi > 0:
                p_past = jnp.exp(s_past - m)
                l = l + jnp.sum(p_past, axis=-1, keepdims=True)
                acc = acc + _dot(p_past.astype(BF16), v_ref[:lo, cs])
            o_ref[lo:hi, cs] = (acc / l).astype(o_ref.dtype)


def _fox_attention(proj, c, g_q, g_k, batch, seq, n_heads, col0, width):
    t = proj.shape[0]
    tq = _tile(seq, 256)
    hpb = 2 if n_heads % 2 == 0 else 1
    bw = hpb * HEAD_DIM
    cb = col0 // bw
    wb = width // bw
    kern = functools.partial(_fox_kernel, tq=tq, hpb=hpb)
    return pl.pallas_call(
        kern,
        grid=(batch, n_heads // hpb),
        in_specs=[pl.BlockSpec((seq, bw), lambda b, h: (b, cb + h)),
                  pl.BlockSpec((seq, bw), lambda b, h: (b, cb + wb + h)),
                  pl.BlockSpec((seq, bw), lambda b, h: (b, cb + 2 * wb + h)),
                  pl.BlockSpec((F_ROWS, seq), lambda b, h: (0, b)),
                  pl.BlockSpec((1, HEAD_DIM), lambda b, h: (0, 0)),
                  pl.BlockSpec((1, HEAD_DIM), lambda b, h: (0, 0))],
        out_specs=pl.BlockSpec((seq, bw), lambda b, h: (b, h)),
        out_shape=SDS((t, width), BF16),
        scratch_shapes=[pltpu.VMEM((seq, bw), BF16), pltpu.VMEM((SUBLANES, seq), F32)],
        compiler_params=_cparams(2), name="fox_attention")(proj, proj, proj, c, g_q, g_k)


def _mix_out_kernel(ub_ref, uc_ref, ux_ref, hc_ref, hx_ref, yf_ref, cw_ref, gc_ref, gf_ref,
                    w_ref, x_ref, o_ref):
    si = pl.program_id(1)
    p = uc_ref[...].astype(F32) * ux_ref[...].astype(F32)
    halo = hc_ref[...].astype(F32) * hx_ref[...].astype(F32)
    halo = jnp.where(si > 0, halo, 0.0)
    row = lax.broadcasted_iota(I32, p.shape, 0)
    last, last2 = halo[SUBLANES - 1:SUBLANES, :], halo[SUBLANES - 2:SUBLANES - 1, :]
    p1 = jnp.where(row == 0, last, pltpu.roll(p, 1, 0))
    p2 = jnp.where(row == 0, last2, jnp.where(row == 1, last, pltpu.roll(p, 2, 0)))
    w = cw_ref[...]
    y = ub_ref[...].astype(F32) * (w[0:1, :] * p2 + w[1:2, :] * p1 + w[2:3, :] * p)
    c = y.shape[1]
    y_conv = _rms(y, gc_ref[...]).astype(BF16)
    y_fox = _rms(yf_ref[...].astype(F32), gf_ref[...]).astype(BF16)
    o_ref[...] = (x_ref[...] + _dot(y_conv, w_ref[:c, :].astype(BF16))
                  + _dot(y_fox, w_ref[c:, :].astype(BF16)))


def _mix_out(proj, y_fox, conv_w, g_conv, g_fox, w_out, x2d, batch, seq):
    t, d = x2d.shape
    c = conv_w.shape[1]
    w = y_fox.shape[1]
    ts = _tile(seq, 512)
    ns = seq // ts
    hb = ts // SUBLANES
    tile = lambda col: (lambda b, s: (b * ns + s, col))
    halo = lambda col: (lambda b, s: (jnp.maximum((b * ns + s) * hb - 1, 0), col))
    const = lambda b, s: (0, 0)
    return pl.pallas_call(
        _mix_out_kernel,
        grid=(batch, ns),
        in_specs=[pl.BlockSpec((ts, c), tile(0)),
                  pl.BlockSpec((ts, c), tile(1)),
                  pl.BlockSpec((ts, c), tile(2)),
                  pl.BlockSpec((SUBLANES, c), halo(1)),
                  pl.BlockSpec((SUBLANES, c), halo(2)),
                  pl.BlockSpec((ts, w), tile(0)),
                  pl.BlockSpec((3, c), const),
                  pl.BlockSpec((1, c), const),
                  pl.BlockSpec((1, w), const),
                  pl.BlockSpec((c + w, d), const, pipeline_mode=pl.Buffered(1)),
                  pl.BlockSpec((ts, d), tile(0))],
        out_specs=pl.BlockSpec((ts, d), tile(0)),
        out_shape=SDS((t, d), F32),
        compiler_params=_cparams(2), name="mix_out")(
            proj, proj, proj, proj, proj, y_fox, conv_w, g_conv, g_fox, w_out, x2d)


def _memkv_kernel(mem_ref, g_ref, w_ref, gk_ref, k_ref, v_ref):
    hm = _rms(mem_ref[...], g_ref[...]).astype(BF16)
    kv = _dot(hm, w_ref[...].astype(BF16))
    xw = k_ref.shape[1]
    for h in range(xw // HEAD_DIM):
        sl = slice(h * HEAD_DIM, (h + 1) * HEAD_DIM)
        k_ref[:, sl] = _rms(kv[:, sl], gk_ref[...]).astype(BF16)
    v_ref[...] = kv[:, xw:].astype(BF16)


def _memkv(mem2d, g_mem, w_xkv, g_xk, n_mem):
    tm_, d = mem2d.shape
    xw = w_xkv.shape[1] // 2
    return pl.pallas_call(
        _memkv_kernel,
        grid=(tm_ // n_mem,),
        in_specs=[pl.BlockSpec((n_mem, d), lambda b: (b, 0)),
                  pl.BlockSpec((1, d), lambda b: (0, 0)),
                  pl.BlockSpec((d, 2 * xw), lambda b: (0, 0)),
                  pl.BlockSpec((1, HEAD_DIM), lambda b: (0, 0))],
        out_specs=[pl.BlockSpec((n_mem, xw), lambda b: (b, 0)),
                   pl.BlockSpec((n_mem, xw), lambda b: (b, 0))],
        out_shape=[SDS((tm_, xw), BF16), SDS((tm_, xw), BF16)],
        compiler_params=_cparams(1), name="memkv")(mem2d, g_mem, w_xkv, g_xk)


def _xattn_kernel(x_ref, gx_ref, wq_ref, gq_ref, kn_ref, v_ref, wo_ref, gm_ref, wr_ref, br_ref,
                  x2_ref, hm_ref, ti_ref, tg_ref, pos_ref, cnt_ref, run_ref, tri_ref):
    i = pl.program_id(0)
    tm = x_ref.shape[0]

    @pl.when(i == 0)
    def _():
        run_ref[...] = jnp.zeros(run_ref.shape, F32)
        r = lax.broadcasted_iota(I32, (tm, tm), 0)
        c = lax.broadcasted_iota(I32, (tm, tm), 1)
        tri_ref[...] = jnp.where(r <= c, 1.0, 0.0).astype(BF16)

    x = x_ref[...]
    hb = _rms(x, gx_ref[...]).astype(BF16)
    q = _dot(hb, wq_ref[...].astype(BF16))
    scale = HEAD_DIM ** -0.5
    outs = []
    for h in range(q.shape[1] // HEAD_DIM):
        sl = slice(h * HEAD_DIM, (h + 1) * HEAD_DIM)
        qh = (_rms(q[:, sl], gq_ref[...]) * scale).astype(BF16)
        s = _dot_nt(qh, kn_ref[:, sl])
        p = jnp.exp(s - jnp.max(s, axis=-1, keepdims=True))
        oh = _dot(p.astype(BF16), v_ref[:, sl]) / jnp.sum(p, axis=-1, keepdims=True)
        outs.append(oh.astype(BF16))
    x2 = x + _dot(jnp.concatenate(outs, axis=-1), wo_ref[...].astype(BF16))
    x2_ref[...] = x2
    hm = _rms(x2, gm_ref[...]).astype(BF16)
    hm_ref[...] = hm
    lg = _dot_nt(wr_ref[...], hm) + br_ref[...]
    n_exp = lg.shape[0]
    eidx = lax.broadcasted_iota(I32, lg.shape, 0).astype(F32)
    vals, sel = [], []
    for r in range(TOP_K):
        mx = jnp.max(lg, axis=0, keepdims=True)
        am = jnp.min(jnp.where(lg == mx, eidx, float(n_exp)), axis=0, keepdims=True)
        ti_ref[r:r + 1, :] = am.astype(I32)
        vals.append(mx)
        sel.append(eidx == am)
        lg = jnp.where(sel[r], -jnp.inf, lg)
    ex = [jnp.exp(v - vals[0]) for v in vals]
    den = ex[0]
    for e in ex[1:]:
        den = den + e
    for r in range(TOP_K):
        tg_ref[r:r + 1, :] = ex[r] / den
    onehot = jnp.zeros(lg.shape, F32)
    for r in range(TOP_K):
        onehot = onehot + jnp.where(sel[r], 1.0, 0.0)
    incl = _dot(onehot.astype(BF16), tri_ref[...])
    excl = incl - onehot + run_ref[...]
    for r in range(TOP_K):
        pos_ref[r:r + 1, :] = jnp.sum(jnp.where(sel[r], excl, 0.0), axis=0,
                                      keepdims=True).astype(I32)
    run = run_ref[...] + incl[:, tm - 1:tm]
    run_ref[...] = run
    cnt_ref[...] = jnp.broadcast_to(run, cnt_ref.shape)


def _xattn(x1, g_xattn, w_xq, g_xq, kn, vm, w_xo, g_moe, w_rt, b_r, seq, n_mem):
    t, d = x1.shape
    xw = w_xq.shape[1]
    n_exp = w_rt.shape[0]
    tm = _tile(seq, 512)
    per = seq // tm
    const = lambda i: (0, 0)
    return pl.pallas_call(
        _xattn_kernel,
        grid=(t // tm,),
        in_specs=[pl.BlockSpec((tm, d), lambda i: (i, 0)),
                  pl.BlockSpec((1, d), const),
                  pl.BlockSpec((d, xw), const),
                  pl.BlockSpec((1, HEAD_DIM), const),
                  pl.BlockSpec((n_mem, xw), lambda i: (i // per, 0)),
                  pl.BlockSpec((n_mem, xw), lambda i: (i // per, 0)),
                  pl.BlockSpec((xw, d), const),
                  pl.BlockSpec((1, d), const),
                  pl.BlockSpec((n_exp, d), const),
                  pl.BlockSpec((n_exp, 1), const)],
        out_specs=[pl.BlockSpec((tm, d), lambda i: (i, 0)),
                   pl.BlockSpec((tm, d), lambda i: (i, 0)),
                   pl.BlockSpec((TOP_K, tm), lambda i: (0, i)),
                   pl.BlockSpec((TOP_K, tm), lambda i: (0, i)),
                   pl.BlockSpec((TOP_K, tm), lambda i: (0, i)),
                   pl.BlockSpec((n_exp, 128), const)],
        out_shape=[SDS((t, d), F32), SDS((t, d), BF16),
                   SDS((TOP_K, t), I32), SDS((TOP_K, t), F32), SDS((TOP_K, t), I32),
                   SDS((n_exp, 128), F32)],
        scratch_shapes=[pltpu.VMEM((n_exp, 1), F32), pltpu.VMEM((tm, tm), BF16)],
        compiler_params=_cparams(1), name="xattn_router")(
            x1, g_xattn, w_xq, g_xq, kn, vm, w_xo, g_moe, w_rt, b_r)


def _take(values, idx):
    pick = idx[:, None] == jnp.arange(values.shape[0], dtype=I32)[None, :]
    return jnp.sum(jnp.where(pick, values[None, :], 0), axis=1).astype(I32)


def _route_tables(counts, n_assign, tm):
    n_exp = counts.shape[0]
    ts = 2 * tm
    valid = (counts + tm - 1) // tm * tm
    seg = ((counts + ts - 1) // ts * ts).astype(I32)
    seg_end = jnp.cumsum(seg).astype(I32)
    seg_start = seg_end - seg
    n_steps = n_assign // ts + n_exp
    n_used = seg_end[-1] // ts
    idx = jnp.arange(n_steps, dtype=I32)
    row0 = jnp.minimum(idx, n_used - 1) * ts
    exp = jnp.sum((row0[:, None] >= seg_end[None, :]).astype(I32), axis=1)
    exp = jnp.minimum(exp, n_exp - 1)
    n_valid = jnp.clip((_take(seg_start + valid, exp) - row0) // tm, 1, 2).astype(I32)
    prev = jnp.concatenate([jnp.full((1,), -1, I32), exp[:-1]])
    first = (exp != prev).astype(I32)
    rem = _take(seg_end, exp) // ts - idx
    cand = jnp.where(counts > 0, jnp.arange(n_exp, dtype=I32), n_exp)
    after = jnp.concatenate([lax.cummin(cand, reverse=True)[1:], jnp.full((1,), n_exp, I32)])
    nxt = _take(jnp.where(after >= n_exp, -1, after), exp)
    tables = (exp, first, rem.astype(I32), nxt, n_valid)
    return seg_start, seg_end, seg, tables, n_used.reshape(1).astype(I32)


def _block_tables(tables, n_used):
    exp, first, rem, nxt, n_valid = tables
    half = jnp.tile(jnp.arange(2, dtype=I32), exp.shape[0])
    rep = lambda v: jnp.repeat(v, 2)
    return (rep(exp), rep(first) * (1 - half), 2 * rep(rem) - half, rep(nxt),
            (half < rep(n_valid)).astype(I32)), 2 * n_used


def _dest_kernel(ps_ref, ti_ref, pos_ref, o_ref, *, n_exp):
    ti = ti_ref[...]
    acc = pos_ref[...]
    for e in range(n_exp):
        acc = acc + jnp.where(ti == e, ps_ref[e], 0)
    o_ref[...] = acc


def _dest_rows(pad_start, ti, pos):
    k, t = ti.shape
    kern = functools.partial(_dest_kernel, n_exp=pad_start.shape[0])
    grid_spec = pltpu.PrefetchScalarGridSpec(
        num_scalar_prefetch=1, grid=(1,),
        in_specs=[pl.BlockSpec((k, t), lambda i, ps: (0, 0)),
                  pl.BlockSpec((k, t), lambda i, ps: (0, 0))],
        out_specs=pl.BlockSpec((k, t), lambda i, ps: (0, 0)))
    return pl.pallas_call(kern, grid_spec=grid_spec, out_shape=SDS((k, t), I32),
                          compiler_params=_cparams(1), name="moe_dest")(pad_start, ti, pos)


def _dispatch_kernel(pe_ref, pc_ref, nu_ref, dst_ref, x_ref, xs_ref, buf, zbuf, sem, zsem,
                     *, tc, ts, n_exp, n_steps):
    i = pl.program_id(0)
    n = pl.num_programs(0)
    ns = x_ref.shape[1] // (2 * LANES)

    def zero_copy(row0):
        rows = pl.ds(pl.multiple_of(row0 * ns, ts * ns), ts * ns)
        return pltpu.make_async_copy(zbuf, xs_ref.at[rows], zsem)

    @pl.when(i == 0)
    def _():
        zbuf[...] = jnp.zeros(zbuf.shape, zbuf.dtype)
        for start in (True, False):
            for e in range(n_exp):
                @pl.when(pc_ref[e] > 0)
                def _(e=e):
                    cp = zero_copy(pe_ref[e] - ts)
                    cp.start() if start else cp.wait()

            def tail(b, c):
                cp = zero_copy(b * ts)
                cp.start() if start else cp.wait()
                return c
            lax.fori_loop(nu_ref[0], n_steps, tail, 0)

    slot = i % 2
    x = x_ref[...]
    for s in range(ns):
        lo = x[:, (2 * s) * LANES:(2 * s + 1) * LANES]
        hi = x[:, (2 * s + 1) * LANES:(2 * s + 2) * LANES]
        buf[slot, pl.ds(s, tc, stride=ns), :] = _pack_pair(lo, hi)
    for k in range(TOP_K):
        for r in range(tc):
            dst = pl.ds(pl.multiple_of(dst_ref[0, k, r] * ns, ns), ns)
            pltpu.make_async_copy(buf.at[slot, pl.ds(r * ns, ns)], xs_ref.at[dst],
                                  sem.at[slot]).start(priority=r % 2)

    def wait_slot(s):
        for k in range(TOP_K):
            pltpu.make_async_copy(buf.at[s], xs_ref.at[pl.ds(0, tc * ns)], sem.at[s]).wait()

    @pl.when(i >= 1)
    def _():
        wait_slot(1 - slot)

    @pl.when(i == n - 1)
    def _():
        wait_slot(slot)


def _dispatch(hm, dest, seg_end, seg, n_used, ts, n_steps):
    t, d = hm.shape
    tc = _tile(t, 256)
    n = t // tc
    ns = d // (2 * LANES)
    n_exp = seg_end.shape[0]
    dest3 = dest.reshape(TOP_K, n, tc).transpose(1, 0, 2)
    kern = functools.partial(_dispatch_kernel, tc=tc, ts=ts, n_exp=n_exp, n_steps=n_steps)
    grid_spec = pltpu.PrefetchScalarGridSpec(
        num_scalar_prefetch=3, grid=(n,),
        in_specs=[pl.BlockSpec((1, TOP_K, tc), lambda i, *_: (i, 0, 0), memory_space=pltpu.SMEM),
                  pl.BlockSpec((tc, d), lambda i, *_: (i, 0))],
        out_specs=pl.BlockSpec(memory_space=pl.ANY),
        scratch_shapes=[pltpu.VMEM((2, tc * ns, LANES), U32),
                        pltpu.VMEM((ts * ns, LANES), U32),
                        pltpu.SemaphoreType.DMA((2,)),
                        pltpu.SemaphoreType.DMA(())])
    return pl.pallas_call(
        kern, grid_spec=grid_spec,
        out_shape=SDS((n_steps * ts * ns, LANES), U32),
        compiler_params=_cparams(1), name="moe_dispatch")(seg_end, seg, n_used, dest3, hm)


W_CHUNK = 512


def _stream_weights(i, be_ref, first_ref, rem_ref, nxt_ref, w_hbm, wb_ref, stage_ref, st_ref, sem):
    n_chunks, _, chunk = wb_ref.shape[1:]

    def chunk_copy(e, c, slot):
        cols = pl.ds(pl.multiple_of(c * chunk, chunk), chunk)
        return pltpu.make_async_copy(w_hbm.at[e, :, cols], stage_ref.at[slot], sem.at[slot])

    def start_first_two(e):
        chunk_copy(e, 0, 0).start()
        chunk_copy(e, 1, 1).start()

    def cast_chunks(e, dst, done, issued, k):
        def body(_, carry):
            done, issued = carry
            slot = done % 2
            chunk_copy(e, done, slot).wait()
            wb_ref[dst, done] = stage_ref[slot].astype(BF16)

            @pl.when(issued < n_chunks)
            def _():
                chunk_copy(e, issued, slot).start()
            return done + 1, jnp.minimum(issued + 1, n_chunks)
        return lax.fori_loop(0, k, body, (done, issued))

    @pl.when(i == 0)
    def _():
        start_first_two(be_ref[0])
        cast_chunks(be_ref[0], 0, 0, 2, n_chunks)
        st_ref[0] = 1

    @pl.when(first_ref[i] == 1)
    def _():
        st_ref[0] = 1 - st_ref[0]
        st_ref[1] = 0
        st_ref[2] = 0

        @pl.when(nxt_ref[i] >= 0)
        def _():
            start_first_two(nxt_ref[i])
            st_ref[2] = 2

    cur = st_ref[0]

    @pl.when(nxt_ref[i] >= 0)
    def _():
        done = st_ref[1]
        rem = rem_ref[i]
        lead = (first_ref[i] == 1) & (rem > 1)
        left = jnp.where(first_ref[i] == 1, jnp.maximum(rem - 1, 1), rem)
        k = jnp.where(lead, 0, lax.div(n_chunks - done + left - 1, left))
        done, issued = cast_chunks(nxt_ref[i], 1 - cur, done, st_ref[2], k)
        st_ref[1] = done
        st_ref[2] = issued

    return cur


def _moe_up_kernel(be_ref, first_ref, rem_ref, nxt_ref, ok_ref, nu_ref, x_ref, b_ref, w_hbm, o_ref,
                   wb_ref, stage_ref, st_ref, sem):
    i = pl.program_id(0)
    tm = o_ref.shape[0]
    ns = x_ref.shape[0] // tm
    n_half = wb_ref.shape[1] // 2
    chunk = wb_ref.shape[3]
    f = n_half * chunk

    @pl.when(i < nu_ref[0])
    def _():
        cur = _stream_weights(i, be_ref, first_ref, rem_ref, nxt_ref, w_hbm, wb_ref, stage_ref,
                              st_ref, sem)

        @pl.when(ok_ref[i] == 1)
        def _():
            cols = []
            for s in range(ns):
                lo, hi = _unpack_pair(x_ref[pl.ds(s, tm, stride=ns), :])
                cols += [lo.astype(BF16), hi.astype(BF16)]
            x = jnp.concatenate(cols, axis=1)
            for n in range(n_half):
                lo, hi = n * chunk, (n + 1) * chunk
                gate = jnp.minimum(_dot(x, wb_ref[cur, n]) + b_ref[:, lo:hi], SWIGLU_LIMIT)
                up = jnp.clip(_dot(x, wb_ref[cur, n_half + n]) + b_ref[:, f + lo:f + hi],
                              -SWIGLU_LIMIT, SWIGLU_LIMIT)
                glu = gate * jax.nn.sigmoid(SWIGLU_ALPHA * gate)
                o_ref[:, lo:hi] = ((up + 1.0) * glu).astype(o_ref.dtype)

    @pl.when((i >= nu_ref[0]) | (ok_ref[i] == 0))
    def _():
        o_ref[...] = jnp.zeros(o_ref.shape, o_ref.dtype)


def _moe_up(xs, w_gu, b_gu, tables, n_used, tm):
    d = w_gu.shape[1]
    ns = d // (2 * LANES)
    n_rows = xs.shape[0] // ns
    two_f = w_gu.shape[2]
    chunk = min(W_CHUNK, two_f // 2)
    grid_spec = pltpu.PrefetchScalarGridSpec(
        num_scalar_prefetch=6,
        grid=(n_rows // tm,),
        in_specs=[pl.BlockSpec((tm * ns, LANES),
                               lambda i, be, fi, re, nx, ok, nu: (jnp.minimum(i, nu[0] - 1), 0)),
                  pl.BlockSpec((None, 1, two_f), lambda i, be, fi, re, nx, ok, nu: (be[i], 0, 0)),
                  pl.BlockSpec(memory_space=pl.ANY)],
        out_specs=pl.BlockSpec((tm, two_f // 2), lambda i, *_: (i, 0)),
        scratch_shapes=[pltpu.VMEM((2, two_f // chunk, d, chunk), BF16),
                        pltpu.VMEM((2, d, chunk), w_gu.dtype),
                        pltpu.SMEM((3,), I32),
                        pltpu.SemaphoreType.DMA((2,))])
    return pl.pallas_call(
        _moe_up_kernel, grid_spec=grid_spec,
        out_shape=SDS((n_rows, two_f // 2), BF16),
        compiler_params=_cparams(1), name="moe_up")(*tables, n_used, xs, b_gu, w_gu)


def _moe_down_kernel(be_ref, first_ref, rem_ref, nxt_ref, nv_ref, nu_ref, a_ref, b_ref, w_hbm, o_ref,
                     wf_ref, st_ref, sem):
    i = pl.program_id(0)
    ts = a_ref.shape[0]
    ns = o_ref.shape[0] // ts
    n_chunks, _, chunk = wf_ref.shape[1:]

    def chunk_copy(e, dst, c):
        return pltpu.make_async_copy(w_hbm.at[e, :, pl.ds(c * chunk, chunk)], wf_ref.at[dst, c], sem)

    def compute(cur, rows):
        a = a_ref[:rows, :]
        for n in range(n_chunks):
            lo, hi = n * chunk, (n + 1) * chunk
            y = _dot(a, wf_ref[cur, n].astype(BF16)) + b_ref[:, lo:hi]
            for j in range(chunk // (2 * LANES)):
                s = lo // (2 * LANES) + j
                w = _pack_pair(y[:, (2 * j) * LANES:(2 * j + 1) * LANES],
                               y[:, (2 * j + 1) * LANES:(2 * j + 2) * LANES])
                o_ref[pl.ds(s, rows, stride=ns), :] = w

    @pl.when(i < nu_ref[0])
    def _():
        @pl.when(i == 0)
        def _():
            for c in range(n_chunks):
                chunk_copy(be_ref[0], 0, c).start()
            st_ref[0] = 1
            st_ref[1] = 1

        @pl.when(first_ref[i] == 1)
        def _():
            @pl.when(st_ref[1] == 1)
            def _():
                for c in range(n_chunks):
                    chunk_copy(be_ref[i], 1 - st_ref[0], c).wait()
            st_ref[0] = 1 - st_ref[0]
            st_ref[1] = 0

            @pl.when(nxt_ref[i] >= 0)
            def _():
                for c in range(n_chunks):
                    chunk_copy(nxt_ref[i], 1 - st_ref[0], c).start()
                st_ref[1] = 1

        cur = st_ref[0]

        @pl.when(nv_ref[i] == 2)
        def _():
            compute(cur, ts)

        @pl.when(nv_ref[i] == 1)
        def _():
            compute(cur, ts // 2)
            half = ts // 2 * ns
            o_ref[half:, :] = jnp.zeros((half, LANES), o_ref.dtype)

    @pl.when(i >= nu_ref[0])
    def _():
        o_ref[...] = jnp.zeros(o_ref.shape, o_ref.dtype)


def _moe_down(act, w_dn, b_dn, tables, n_used, ts):
    n_rows, f = act.shape
    d = w_dn.shape[2]
    ns = d // (2 * LANES)
    chunk = max(min(W_CHUNK, d // 2), 2 * LANES)
    grid_spec = pltpu.PrefetchScalarGridSpec(
        num_scalar_prefetch=6,
        grid=(n_rows // ts,),
        in_specs=[pl.BlockSpec((ts, f),
                               lambda i, be, fi, re, nx, nv, nu: (jnp.minimum(i, nu[0] - 1), 0)),
                  pl.BlockSpec((None, 1, d), lambda i, be, fi, re, nx, nv, nu: (be[i], 0, 0)),
                  pl.BlockSpec(memory_space=pl.ANY)],
        out_specs=pl.BlockSpec((ts * ns, LANES), lambda i, *_: (i, 0)),
        scratch_shapes=[pltpu.VMEM((2, d // chunk, f, chunk), w_dn.dtype),
                        pltpu.SMEM((2,), I32),
                        pltpu.SemaphoreType.DMA(())])
    return pl.pallas_call(
        _moe_down_kernel, grid_spec=grid_spec,
        out_shape=SDS((n_rows * ns, LANES), U32),
        compiler_params=_cparams(1), name="moe_down")(*tables, n_used, act, b_dn, w_dn)


def _combine_kernel(dst_ref, x_ref, g_ref, ys_ref, o_ref, buf, sem, *, tc):
    i = pl.program_id(0)
    n = pl.num_programs(0) - 1
    ns = x_ref.shape[1] // (2 * LANES)

    @pl.when(i < n)
    def _():
        slot = i % 2
        for k in range(TOP_K):
            for r in range(tc):
                src = pl.ds(pl.multiple_of(dst_ref[0, k, r] * ns, ns), ns)
                pltpu.make_async_copy(ys_ref.at[src], buf.at[slot, k, pl.ds(r * ns, ns)],
                                      sem.at[slot]).start(priority=r % 2)

    @pl.when(i >= 1)
    def _():
        slot = (i - 1) % 2
        for k in range(TOP_K):
            pltpu.make_async_copy(ys_ref.at[pl.ds(0, tc * ns)], buf.at[slot, k],
                                  sem.at[slot]).wait()
        g = g_ref[...]
        g = jnp.concatenate([g, jnp.zeros((SUBLANES - TOP_K, tc), F32)], axis=0).T
        for s in range(ns):
            c0, c1, c2 = (2 * s) * LANES, (2 * s + 1) * LANES, (2 * s + 2) * LANES
            acc_lo = x_ref[:, c0:c1]
            acc_hi = x_ref[:, c1:c2]
            for k in range(TOP_K):
                lo, hi = _unpack_pair(buf[slot, k, pl.ds(s, tc, stride=ns), :])
                acc_lo = acc_lo + g[:, k:k + 1] * lo
                acc_hi = acc_hi + g[:, k:k + 1] * hi
            o_ref[:, c0:c1] = acc_lo
            o_ref[:, c1:c2] = acc_hi


def _combine(x2, gates, dest, ys):
    t, d = x2.shape
    tc = _tile(t, 128)
    n = t // tc
    ns = d // (2 * LANES)
    dest3 = dest.reshape(TOP_K, n, tc).transpose(1, 0, 2)
    kern = functools.partial(_combine_kernel, tc=tc)
    prev = lambda i: (jnp.maximum(i - 1, 0), 0)
    return pl.pallas_call(
        kern,
        grid=(n + 1,),
        in_specs=[pl.BlockSpec((1, TOP_K, tc), lambda i: (jnp.minimum(i, n - 1), 0, 0),
                               memory_space=pltpu.SMEM),
                  pl.BlockSpec((tc, d), prev),
                  pl.BlockSpec((TOP_K, tc), lambda i: (0, jnp.maximum(i - 1, 0))),
                  pl.BlockSpec(memory_space=pl.ANY)],
        out_specs=pl.BlockSpec((tc, d), prev),
        out_shape=SDS((t, d), F32),
        scratch_shapes=[pltpu.VMEM((2, TOP_K, tc * ns, LANES), U32),
                        pltpu.SemaphoreType.DMA((2,))],
        compiler_params=_cparams(1), name="moe_combine")(dest3, x2, gates, ys)


def _layer(x2d, mem2d, batch, seq, n_mem, p):
    t, d = x2d.shape
    c = p["conv_w"].shape[1]
    fw = p["g_fox_out"].shape[0]
    n_heads = fw // HEAD_DIM
    n_main = 3 * c + 3 * fw
    n_exp = p["w_router"].shape[1]
    assert n_heads <= F_ROWS and TOP_K <= SUBLANES and d % (2 * LANES) == 0

    w_in = p["w_mix_in"].astype(BF16)
    wf_t = jnp.zeros((F_ROWS, d), BF16).at[:n_heads].set(w_in[:, n_main:].T)
    b_col = jnp.zeros((F_ROWS, 1), F32).at[:n_heads, 0].set(p["b_forget"])

    proj, ft = _mix_in(x2d, p["g_mix"][None, :], wf_t, w_in, n_main)
    cdec = _decay(ft, b_col, seq)
    y_fox = _fox_attention(proj, cdec, p["g_q"][None, :], p["g_k"][None, :],
                           batch, seq, n_heads, 3 * c, fw)
    x1 = _mix_out(proj, y_fox, p["conv_w"], p["g_conv_out"][None, :], p["g_fox_out"][None, :],
                  p["w_mix_out"], x2d, batch, seq)

    kn, vm = _memkv(mem2d, p["g_mem"][None, :], p["w_xkv"],
                    p["g_xk"][None, :], n_mem)
    x2, hm, ti, tg, pos, cnt = _xattn(x1, p["g_xattn"][None, :], p["w_xq"],
                            p["g_xq"][None, :], kn, vm, p["w_xo"],
                            p["g_moe"][None, :], p["w_router"].T.astype(BF16),
                            p["b_router"][:, None], seq, n_mem)

    tm = _tile(TOP_K * t, 512) // 2
    counts = cnt[:, 0].astype(I32)
    seg_start, seg_end, seg, tables, n_used = _route_tables(counts, TOP_K * t, tm)
    dest = _dest_rows(seg_start, ti, pos)
    xs = _dispatch(hm, dest, seg_end, seg, n_used, 2 * tm, tables[0].shape[0])
    act = _moe_up(xs, p["w_gate_up"], p["b_gate_up"][:, None, :], *_block_tables(tables, n_used), tm)
    ys = _moe_down(act, p["w_down"], p["b_down"][:, None, :], tables, n_used, 2 * tm)
    return _combine(x2, tg, dest, ys)


def kernel(x, mem, g_mix, w_mix_in, b_forget, conv_w, g_q, g_k, g_conv_out, g_fox_out, w_mix_out, g_xattn, g_mem, w_xq, w_xkv, g_xq, g_xk, w_xo, g_moe, w_router, b_router, w_gate_up, b_gate_up, w_down, b_down):
    batch, seq, d = x.shape
    n_mem = mem.shape[1]
    params = dict(g_mix=g_mix, w_mix_in=w_mix_in, b_forget=b_forget, conv_w=conv_w, g_q=g_q,
                  g_k=g_k, g_conv_out=g_conv_out, g_fox_out=g_fox_out, w_mix_out=w_mix_out,
                  g_xattn=g_xattn, g_mem=g_mem, w_xq=w_xq, w_xkv=w_xkv, g_xq=g_xq, g_xk=g_xk,
                  w_xo=w_xo, g_moe=g_moe, w_router=w_router, b_router=b_router,
                  w_gate_up=w_gate_up, b_gate_up=b_gate_up, w_down=w_down, b_down=b_down)
    x2d = x.reshape(batch * seq, d)
    mem2d = mem.reshape(batch * n_mem, d)
    for l in range(g_mix.shape[0]):
        x2d = _layer(x2d, mem2d, batch, seq, n_mem, {k: v[l] for k, v in params.items()})
    return x2d.reshape(batch, seq, d)
```
